```python
import jax, jax.numpy as jnp
from jax import lax
import numpy as np

D_MODEL = 1024
BATCH = 8
SEQ = 2048
DEPTH = 2

MEM_LEN = 256
GROUP_WIDTH = D_MODEL // 2
D_MIX = 3 * GROUP_WIDTH
MOBA_HEAD_DIM = 64
MOBA_HEADS = GROUP_WIDTH // MOBA_HEAD_DIM
MOBA_BLOCK = 256
MOBA_TOPK = 3
MOBA_Q_CHUNK = 16
HGRN_HEAD_DIM = 128
HGRN_HEADS = GROUP_WIDTH // HGRN_HEAD_DIM
HGRN_CHUNK = 64
MEM_HEAD_DIM = 128
MEM_HEADS = GROUP_WIDTH // MEM_HEAD_DIM
ROPE_THETA = 500000.0
ROPE_DIM = MOBA_HEAD_DIM // 4
NORM_EPS = 1e-6
IN_COLS = 3 * GROUP_WIDTH + 3 * GROUP_WIDTH + GROUP_WIDTH + D_MIX

kernel_name = "hymba_moba_hgrn2_memxattn_block"


def rms_norm(x, g):
    xf = x.astype(jnp.float32)
    y = xf * lax.rsqrt(jnp.mean(xf * xf, axis=-1, keepdims=True) + NORM_EPS)
    return (y * g.astype(jnp.float32)).astype(x.dtype)


def partial_rope(x, positions):
    half = ROPE_DIM // 2
    inv_freq = ROPE_THETA ** (-jnp.arange(half, dtype=jnp.float32) / half)
    ang = positions.astype(jnp.float32)[..., None] * inv_freq
    cos = jnp.cos(ang)[:, :, None, :]
    sin = jnp.sin(ang)[:, :, None, :]
    xr = x[..., :ROPE_DIM].astype(jnp.float32)
    x1, x2 = xr[..., :half], xr[..., half:]
    rot = jnp.concatenate([x1 * cos - x2 * sin, x2 * cos + x1 * sin], axis=-1).astype(x.dtype)
    return jnp.concatenate([rot, x[..., ROPE_DIM:]], axis=-1)


def moba_attention(q, k, v):
    B, S, H, D = q.shape
    BLK, QC = MOBA_BLOCK, MOBA_Q_CHUNK
    nb = -(-S // BLK)
    pad = nb * BLK - S
    topk = min(MOBA_TOPK, nb)
    kbh = jnp.pad(k, ((0, 0), (0, pad), (0, 0), (0, 0))).reshape(B, nb, BLK, H, D).transpose(0, 3, 1, 2, 4)
    vbh = jnp.pad(v, ((0, 0), (0, pad), (0, 0), (0, 0))).reshape(B, nb, BLK, H, D).transpose(0, 3, 1, 2, 4)
    kmean = jnp.mean(kbh.astype(jnp.float32), axis=3)
    scale = D ** -0.5
    nqc = S // QC
    qch = q.reshape(B, nqc, QC, H, D).transpose(1, 0, 3, 2, 4)
    bi = jnp.arange(B)[:, None, None, None]
    hi = jnp.arange(H)[None, :, None, None]
    blk_ids = jnp.arange(nb)

    def one_chunk(args):
        ci, qi = args
        start = ci * QC
        own = start // BLK
        qpos = start + jnp.arange(QC)
        gate = jnp.einsum('bhqd,bhnd->bhqn', qi.astype(jnp.float32), kmean)
        gate = jnp.where(blk_ids < own, gate, -jnp.inf)
        _, sel = lax.top_k(gate, topk)
        sel_ok = sel < own
        k_sel = kbh[bi, hi, sel]
        v_sel = vbh[bi, hi, sel]
        s_sel = jnp.einsum('bhqd,bhqkld->bhqkl', qi, k_sel).astype(jnp.float32) * scale
        s_sel = jnp.where(sel_ok[..., None], s_sel, -jnp.inf).reshape(B, H, QC, topk * BLK)
        k_own = lax.dynamic_index_in_dim(kbh, own, axis=2, keepdims=False)
        v_own = lax.dynamic_index_in_dim(vbh, own, axis=2, keepdims=False)
        kpos = own * BLK + jnp.arange(BLK)
        s_own = jnp.einsum('bhqd,bhld->bhql', qi, k_own).astype(jnp.float32) * scale
        s_own = jnp.where(kpos[None, :] <= qpos[:, None], s_own, -jnp.inf)
        p = jax.nn.softmax(jnp.concatenate([s_sel, s_own], axis=-1), axis=-1).astype(v.dtype)
        p_sel = p[..., :topk * BLK].reshape(B, H, QC, topk, BLK)
        p_own = p[..., topk * BLK:]
        return (jnp.einsum('bhqkl,bhqkld->bhqd', p_sel, v_sel)
                + jnp.einsum('bhql,bhld->bhqd', p_own, v_own))

    o = lax.map(one_chunk, (jnp.arange(nqc), qch))
    return o.transpose(1, 0, 3, 2, 4).reshape(B, S, H, D)


def hgrn2_chunkwise(q, f_logit, i, lb):
    B, S, H, DK = q.shape
    DV = i.shape[-1]
    C = HGRN_CHUNK
    NC = S // C
    fl = f_logit.astype(jnp.float32)
    qf = jax.nn.silu(q.astype(jnp.float32))
    log_f = jnp.logaddexp(jnp.log(lb), jnp.log1p(-lb) + jax.nn.log_sigmoid(fl))
    kf = (1.0 - lb) * jax.nn.sigmoid(-fl)
    vf = i.astype(jnp.float32)

    def to_chunks(t):
        return t.reshape(B, NC, C, H, t.shape[-1]).transpose(1, 0, 3, 2, 4)

    causal = jnp.tril(jnp.ones((C, C), dtype=bool))

    def step(state, inp):
        qc, lfc, kc, vc = inp
        A = jnp.cumsum(lfc, axis=2)
        diff = A[:, :, :, None, :] - A[:, :, None, :, :]
        decay = jnp.exp(jnp.where(causal[:, :, None], diff, -jnp.inf))
        scores = jnp.einsum('bhtd,bhtsd,bhsd->bhts', qc, decay, kc)
        o = (jnp.einsum('bhts,bhsv->bhtv', scores, vc)
             + jnp.einsum('bhtd,bhdv->bhtv', qc * jnp.exp(A), state))
        A_end = A[:, :, -1:, :]
        state = (jnp.exp(A_end[:, :, 0, :])[..., None] * state
                 + jnp.einsum('bhsd,bhsv->bhdv', kc * jnp.exp(A_end - A), vc))
        return state, o

    s0 = jnp.zeros((B, H, DK, DV), jnp.float32)
    _, o = lax.scan(step, s0, (to_chunks(qf), to_chunks(log_f), to_chunks(kf), to_chunks(vf)))
    return o.transpose(1, 0, 3, 2, 4).reshape(B, S, H, DV)


def memory_attention(q, k, v):
    scale = q.shape[-1] ** -0.5
    s = jnp.einsum('bshd,bmhd->bhsm', q, k).astype(jnp.float32) * scale
    p = jax.nn.softmax(s, axis=-1).astype(v.dtype)
    return jnp.einsum('bhsm,bmhd->bshd', p, v)


def setup_inputs(seed: int = 0) -> dict:
    key = jax.random.key(seed)
    ks = jax.random.split(key, 14)
    f32 = jnp.float32

    def gain(k, shape):
        return 1.0 + 0.02 * jax.random.normal(k, shape, f32)

    return {
        "x": jax.random.normal(ks[0], (BATCH, SEQ, D_MODEL), f32),
        "mem": jax.random.normal(ks[1], (BATCH, MEM_LEN, D_MODEL), f32),
        "positions": jnp.tile(jnp.arange(SEQ, dtype=jnp.int32)[None, :], (BATCH, 1)),
        "norm_g": gain(ks[2], (DEPTH, D_MODEL)),
        "w_in": jax.random.normal(ks[3], (DEPTH, D_MODEL, IN_COLS), f32) * D_MODEL ** -0.5,
        "w_out": jax.random.normal(ks[4], (DEPTH, D_MIX, D_MODEL), f32) * D_MIX ** -0.5,
        "moba_q_norm": gain(ks[5], (DEPTH, MOBA_HEAD_DIM)),
        "moba_k_norm": gain(ks[6], (DEPTH, MOBA_HEAD_DIM)),
        "hgrn_lb_logits": 0.5 * jax.random.normal(ks[7], (DEPTH, GROUP_WIDTH), f32),
        "hgrn_o_norm": gain(ks[8], (DEPTH, HGRN_HEAD_DIM)),
        "mem_norm_g": gain(ks[9], (DEPTH, D_MODEL)),
        "w_mem_kv": jax.random.normal(ks[10], (DEPTH, D_MODEL, 2 * GROUP_WIDTH), f32) * D_MODEL ** -0.5,
        "mem_q_norm": gain(ks[11], (DEPTH, MEM_HEAD_DIM)),
        "mem_k_norm": gain(ks[12], (DEPTH, MEM_HEAD_DIM)),
    }


def reference(x, mem, positions, norm_g, w_in, w_out, moba_q_norm, moba_k_norm, hgrn_lb_logits,
              hgrn_o_norm, mem_norm_g, w_mem_kv, mem_q_norm, mem_k_norm):
    B, S, _ = x.shape
    M = mem.shape[1]
    G = GROUP_WIDTH
    lb_all = jnp.cumsum(jax.nn.softmax(hgrn_lb_logits.astype(jnp.float32), axis=0), axis=0)
    lb_all = lb_all - lb_all[0:1]
    split_points = [G * n for n in range(1, 8)]
    for l in range(DEPTH):
        h = rms_norm(x, norm_g[l])
        proj = h @ w_in[l]
        q_a, k_a, v_a, q_h, f_h, i_h, q_m, z = jnp.split(proj, split_points, axis=-1)

        qa = partial_rope(rms_norm(q_a.reshape(B, S, MOBA_HEADS, MOBA_HEAD_DIM), moba_q_norm[l]), positions)
        ka = partial_rope(rms_norm(k_a.reshape(B, S, MOBA_HEADS, MOBA_HEAD_DIM), moba_k_norm[l]), positions)
        va = v_a.reshape(B, S, MOBA_HEADS, MOBA_HEAD_DIM)
        o_a = moba_attention(qa, ka, va).reshape(B, S, G)

        o_h = hgrn2_chunkwise(q_h.reshape(B, S, HGRN_HEADS, HGRN_HEAD_DIM),
                              f_h.reshape(B, S, HGRN_HEADS, HGRN_HEAD_DIM),
                              i_h.reshape(B, S, HGRN_HEADS, HGRN_HEAD_DIM),
                              lb_all[l].reshape(HGRN_HEADS, HGRN_HEAD_DIM))
        o_h = rms_norm(o_h, hgrn_o_norm[l]).astype(x.dtype).reshape(B, S, G)

        kv_m = rms_norm(mem, mem_norm_g[l]) @ w_mem_kv[l]
        k_m, v_m = jnp.split(kv_m, 2, axis=-1)
        km = rms_norm(k_m.reshape(B, M, MEM_HEADS, MEM_HEAD_DIM), mem_k_norm[l])
        vm = v_m.reshape(B, M, MEM_HEADS, MEM_HEAD_DIM)
        qm = rms_norm(q_m.reshape(B, S, MEM_HEADS, MEM_HEAD_DIM), mem_q_norm[l])
        o_m = memory_attention(qm, km, vm).reshape(B, S, G)

        y = jnp.concatenate([o_a, o_h, o_m], axis=-1) * jax.nn.silu(z)
        x = x + y @ w_out[l]
    return x
```

```python
import functools

import numpy as np
import jax
import jax.numpy as jnp
from jax import lax
from jax.experimental import pallas as pl
from jax.experimental.pallas import tpu as pltpu

F32 = jnp.float32
BF16 = jnp.bfloat16

D_MODEL = 1024
GROUP = D_MODEL // 2
D_MIX = 3 * GROUP
IN_COLS = 7 * GROUP + D_MIX
MOBA_D = 64
MOBA_BLOCK = 256
MOBA_TOPK = 3
HGRN_D = 128
HGRN_HEADS = GROUP // HGRN_D
HGRN_CHUNK = 64
HGRN_LEVELS = 6
MEM_D = 128
MEM_HEADS = GROUP // MEM_D
ROPE_THETA = 500000.0
ROPE_DIM = MOBA_D // 4
ROPE_HALF = ROPE_DIM // 2
NORM_EPS = 1e-6
LANES = 128
ROW_TILE = 256
MASKED = -1e30
VMEM_LIMIT = 48 * 1024 * 1024


def _nt(a, b):
    return lax.dot_general(a, b, (((1,), (1,)), ((), ())), preferred_element_type=F32)


def _tn(a, b):
    return lax.dot_general(a, b, (((0,), (0,)), ((), ())), preferred_element_type=F32)


def _nn(a, b):
    return jnp.dot(a, b, preferred_element_type=F32)


def _split2(x):
    hi = x.astype(BF16)
    lo = (x - hi.astype(F32)).astype(BF16)
    return hi, lo


def _split3(x):
    hi = x.astype(BF16)
    r = x - hi.astype(F32)
    mid = r.astype(BF16)
    lo = (r - mid.astype(F32)).astype(BF16)
    return hi, mid, lo


def _sigmoid(x):
    return 1.0 / (1.0 + jnp.exp(-x))


def _silu(x):
    return x * _sigmoid(x)


def _rope_table_kernel(pos_ref, invf_ref, sgn_ref, cos_ref, sin_ref):
    ang = pos_ref[...].astype(F32) * invf_ref[...]
    sgn = sgn_ref[...]
    on = sgn != 0.0
    cos_ref[...] = jnp.where(on, jnp.cos(ang), 1.0)
    sin_ref[...] = sgn * jnp.sin(ang)


def _rope_tables(pos):
    n = pos.shape[0]
    tm = min(n, 2048)
    lane = np.arange(LANES) % MOBA_D
    invf = np.where(lane < ROPE_DIM, ROPE_THETA ** (-(lane % ROPE_HALF).astype(np.float64) / ROPE_HALF), 0.0)
    sgn = np.where(lane < ROPE_HALF, -1.0, np.where(lane < ROPE_DIM, 1.0, 0.0))
    invf = jnp.asarray(invf[None, :], F32)
    sgn = jnp.asarray(sgn[None, :], F32)
    return pl.pallas_call(
        _rope_table_kernel,
        grid=(n // tm,),
        in_specs=[pl.BlockSpec((tm, 1), lambda i: (i, 0)),
                  pl.BlockSpec((1, LANES), lambda i: (0, 0)),
                  pl.BlockSpec((1, LANES), lambda i: (0, 0))],
        out_specs=[pl.BlockSpec((tm, LANES), lambda i: (i, 0))] * 2,
        out_shape=[jax.ShapeDtypeStruct((n, LANES), F32)] * 2,
        name="rope_tables",
    )(pos, invf, sgn)


def _in_proj_kernel(x_ref, g_ref, w_ref, cos_ref, sin_ref, gq_ref, gk_ref, seg_ref,
                    qa_ref, ka_ref, va_ref, hg_ref, qm_ref, z_ref):
    x = x_ref[...]
    ms = jnp.mean(x * x, axis=-1, keepdims=True)
    hb = (x * lax.rsqrt(ms + NORM_EPS) * g_ref[...]).astype(BF16)

    def proj(c0, width):
        return _nn(hb, w_ref[:, c0:c0 + width])

    cosf = cos_ref[...]
    sinf = sin_ref[...]
    lane = lax.broadcasted_iota(jnp.int32, (1, LANES), 1) % MOBA_D
    lo_lane = lane < ROPE_HALF
    seg = seg_ref[...]

    def head_norm_rope(p, gain, out_ref):
        for c in range(GROUP // 256):
            pc = p[:, c * 256:(c + 1) * 256]
            ss = _nn((pc * pc).astype(BF16), seg) * (1.0 / MOBA_D)
            pn = pc * lax.rsqrt(ss + NORM_EPS) * gain[:, c * 256:(c + 1) * 256]
            for v in range(2):
                xv = pn[:, v * LANES:(v + 1) * LANES]
                rot = jnp.where(lo_lane, pltpu.roll(xv, LANES - ROPE_HALF, 1), pltpu.roll(xv, ROPE_HALF, 1))
                col = c * 256 + v * LANES
                out_ref[:, col:col + LANES] = xv * cosf + rot * sinf

    head_norm_rope(proj(0, GROUP), gq_ref[...], qa_ref)
    head_norm_rope(proj(GROUP, GROUP), gk_ref[...], ka_ref)
    va_ref[...] = proj(2 * GROUP, GROUP)
    for c in range(3):
        hg_ref[:, c * GROUP:(c + 1) * GROUP] = proj((3 + c) * GROUP, GROUP)
    qm_ref[...] = proj(6 * GROUP, GROUP)
    for c in range(3):
        z_ref[:, c * GROUP:(c + 1) * GROUP] = proj((7 + c) * GROUP, GROUP)


def _in_proj(xf, g, w_bf, cosf, sinf, gq, gk):
    n = xf.shape[0]
    tm = ROW_TILE
    seg = np.arange(256)[:, None] // MOBA_D == np.arange(256)[None, :] // MOBA_D
    seg = jnp.asarray(seg, BF16)
    const = lambda i: (0, 0)
    row = lambda i: (i, 0)
    widths = (GROUP, GROUP, GROUP, 3 * GROUP, GROUP, D_MIX)
    return pl.pallas_call(
        _in_proj_kernel,
        grid=(n // tm,),
        in_specs=[pl.BlockSpec((tm, D_MODEL), row),
                  pl.BlockSpec((1, D_MODEL), const),
                  pl.BlockSpec((D_MODEL, IN_COLS), const, pipeline_mode=pl.Buffered(1)),
                  pl.BlockSpec((tm, LANES), row),
                  pl.BlockSpec((tm, LANES), row),
                  pl.BlockSpec((1, GROUP), const),
                  pl.BlockSpec((1, GROUP), const),
                  pl.BlockSpec((256, 256), const)],
        out_specs=[pl.BlockSpec((tm, w), row) for w in widths],
        out_shape=[jax.ShapeDtypeStruct((n, w), F32) for w in widths],
        compiler_params=pltpu.CompilerParams(dimension_semantics=("parallel",), vmem_limit_bytes=VMEM_LIMIT),
        name="in_proj",
    )(xf, g, w_bf, cosf, sinf, gq, gk, seg)


def _moba_kernel(q_ref, k_ref, v_ref, o_ref, kmean_ref, *, nb):
    i = pl.program_id(2)
    blk = MOBA_BLOCK

    @pl.when(i == 0)
    def _():
        kmean_ref[...] = jnp.zeros_like(kmean_ref)

    q = q_ref[...]
    own0 = pl.multiple_of(i * blk, blk)
    k_own = k_ref[pl.ds(own0, blk), :]
    v_own = v_ref[pl.ds(own0, blk), :]
    kmean_ref[pl.ds(i, 1), :] = jnp.mean(k_own, axis=0, keepdims=True)
    km = kmean_ref[...]

    lane = lax.broadcasted_iota(jnp.int32, (1, LANES), 1)
    blk_row = lax.broadcasted_iota(jnp.int32, (nb, 1), 0)
    past = blk_row < i
    qpos = lax.broadcasted_iota(jnp.int32, (blk, blk), 0)
    kpos = lax.broadcasted_iota(jnp.int32, (blk, blk), 1)
    causal = kpos <= qpos
    k_own_b = k_own.astype(BF16)
    v_own_b = v_own.astype(BF16)

    outs = []
    for h in range(2):
        head = (lane >= h * MOBA_D) & (lane < (h + 1) * MOBA_D)
        spare0 = (1 - h) * MOBA_D
        qh = jnp.where(head, q, 0.0)

        q_hi, q_lo = _split2(qh)
        k_hi, k_lo = _split2(jnp.where(head, km, 0.0))
        gate_t = _nt(k_hi, q_hi) + _nt(k_hi, q_lo) + _nt(k_lo, q_hi)
        gate_t = jnp.where(past, gate_t, -jnp.inf)
        cnt = jnp.zeros((nb, blk), F32)
        for n in range(nb):
            gn = gate_t[n:n + 1, :]
            before = (gn > gate_t) | ((gn == gate_t) & (n < blk_row))
            cnt = cnt + jnp.where(before, 1.0, 0.0)
        sel = past & (cnt < float(MOBA_TOPK))
        bias_t = jnp.where(sel, 0.0, MASKED)
        parts = []
        if spare0:
            parts.append(jnp.zeros((spare0, blk), F32))
        parts.append(bias_t)
        parts.append(jnp.zeros((LANES - spare0 - nb, blk), F32))
        bias = jnp.concatenate(parts, axis=0).T

        qs = qh * (MOBA_D ** -0.5)
        q_own_b = qs.astype(BF16)
        q_aug_b = jnp.where(head, qs, bias).astype(BF16)

        s = _nt(q_own_b, k_own_b)
        s = jnp.where(causal, s, MASKED)
        m0 = jnp.max(s, axis=1, keepdims=True)
        p = jnp.exp(s - m0)
        l0 = jnp.sum(p, axis=1, keepdims=True)
        acc0 = _nn(p.astype(BF16), v_own_b)

        def body(j, carry):
            m, l, acc = carry
            j0 = pl.multiple_of(j * blk, blk)
            kj = k_ref[pl.ds(j0, blk), :]
            vj = v_ref[pl.ds(j0, blk), :]
            k_aug = jnp.where(head, kj, jnp.where(lane == spare0 + j, 1.0, 0.0)).astype(BF16)
            sj = _nt(q_aug_b, k_aug)
            m_new = jnp.maximum(m, jnp.max(sj, axis=1, keepdims=True))
            alpha = jnp.exp(m - m_new)
            pj = jnp.exp(sj - m_new)
            l_new = alpha * l + jnp.sum(pj, axis=1, keepdims=True)
            acc_new = alpha * acc + _nn(pj.astype(BF16), vj.astype(BF16))
            return m_new, l_new, acc_new

        _, l, acc = lax.fori_loop(0, i, body, (m0, l0, acc0))
        outs.append(acc / l)

    o_ref[...] = jnp.where(lane < MOBA_D, outs[0], outs[1])


def _moba(qa, ka, va, batch, seq):
    nb = seq // MOBA_BLOCK
    pairs = GROUP // LANES
    return pl.pallas_call(
        functools.partial(_moba_kernel, nb=nb),
        grid=(batch, pairs, nb),
        in_specs=[pl.BlockSpec((MOBA_BLOCK, LANES), lambda b, p, i: (b * nb + i, p)),
                  pl.BlockSpec((seq, LANES), lambda b, p, i: (b, p)),
                  pl.BlockSpec((seq, LANES), lambda b, p, i: (b, p))],
        out_specs=pl.BlockSpec((MOBA_BLOCK, LANES), lambda b, p, i: (b * nb + i, p)),
        out_shape=jax.ShapeDtypeStruct(qa.shape, F32),
        scratch_shapes=[pltpu.VMEM((nb, LANES), F32)],
        compiler_params=pltpu.CompilerParams(dimension_semantics=("parallel", "parallel", "arbitrary"),
                                             vmem_limit_bytes=VMEM_LIMIT),
        name="moba",
    )(qa, ka, va)


def _hgrn_consts():
    c = HGRN_CHUNK
    t = np.arange(c)
    w_rows, masks = [], []
    for lvl in range(HGRN_LEVELS):
        m = 1 << lvl
        blk = t // m
        odd = (blk % 2) == 1
        start = blk * m
        end = start + m - 1
        u = t[None, :]
        w_odd = (u >= start[:, None]) & (u <= t[:, None])
        w_even = (u > t[:, None]) & (u <= end[:, None])
        w_rows.append(np.where(odd[:, None], w_odd, w_even))
        masks.append(odd[:, None] & (blk[None, :] == blk[:, None] - 1))
    w_rows.append(t[None, :] <= t[:, None])
    w_rows.append(t[None, :] > t[:, None])
    masks.append(t[None, :] == t[:, None])
    w_all = np.concatenate(w_rows, axis=0).astype(np.float32)
    mask_all = np.concatenate(masks, axis=0).astype(np.float32)
    return jnp.asarray(w_all, BF16), jnp.asarray(mask_all, F32)


def _hgrn_kernel(hg_ref, logit_ref, gain_ref, w_ref, mask_ref, o_ref, state_ref, *, layer):
    c = HGRN_CHUNK

    @pl.when(pl.program_id(1) == 0)
    def _():
        state_ref[...] = jnp.zeros_like(state_ref)

    if layer > 0:
        lg = logit_ref[...]
        e = jnp.exp(lg - jnp.max(lg, axis=0, keepdims=True))
        sm = e / jnp.sum(e, axis=0, keepdims=True)
        lb = jnp.sum(sm[1:layer + 1, :], axis=0, keepdims=True)
        log_lb = jnp.log(lb)
        log_1m_lb = jnp.log1p(-lb)
    w_all = w_ref[...]
    gain = gain_ref[...]

    for ci in range(ROW_TILE // c):
        rows = slice(ci * c, (ci + 1) * c)
        qh = hg_ref[rows, 0:GROUP]
        fl = hg_ref[rows, GROUP:2 * GROUP]
        vi = hg_ref[rows, 2 * GROUP:3 * GROUP]
        qf = _silu(qh)
        log_sig = jnp.minimum(fl, 0.0) - jnp.log1p(jnp.exp(-jnp.abs(fl)))
        if layer > 0:
            b = log_1m_lb + log_sig
            lf = jnp.maximum(log_lb, b) + jnp.log1p(jnp.exp(-jnp.abs(log_lb - b)))
            kf = (1.0 - lb) * _sigmoid(-fl)
        else:
            lf = log_sig
            kf = _sigmoid(-fl)

        f1, f2, f3 = _split3(lf)
        e_all = jnp.exp(_nn(w_all, f1) + _nn(w_all, f2) + _nn(w_all, f3))
        e_cum = e_all[HGRN_LEVELS * c:(HGRN_LEVELS + 1) * c, :]
        e_rest = e_all[(HGRN_LEVELS + 1) * c:(HGRN_LEVELS + 2) * c, :]
        e_end = e_all[(HGRN_LEVELS + 1) * c - 1:(HGRN_LEVELS + 1) * c, :]

        for h in range(HGRN_HEADS):
            cols = slice(h * HGRN_D, (h + 1) * HGRN_D)
            qfh, kfh = qf[:, cols], kf[:, cols]
            scores = mask_ref[HGRN_LEVELS * c:(HGRN_LEVELS + 1) * c, :] * _nt(qfh.astype(BF16), kfh.astype(BF16))
            for lvl in range(HGRN_LEVELS):
                el = e_all[lvl * c:(lvl + 1) * c, cols]
                scores = scores + mask_ref[lvl * c:(lvl + 1) * c, :] * _nt((qfh * el).astype(BF16),
                                                                           (kfh * el).astype(BF16))
            vb = vi[:, cols].astype(BF16)
            st = state_ref[:, cols]
            o = _nn(scores.astype(BF16), vb) + _nt((qfh * e_cum[:, cols]).astype(BF16), st.astype(BF16))
            state_ref[:, cols] = st * e_end[:, cols] + _tn(vb, (kfh * e_rest[:, cols]).astype(BF16))
            ms = jnp.mean(o * o, axis=-1, keepdims=True)
            o_ref[rows, cols] = o * lax.rsqrt(ms + NORM_EPS) * gain[:, cols]


def _hgrn(hg, logits, gain, layer, batch, seq):
    w_all, mask_all = _hgrn_consts()
    steps = seq // ROW_TILE
    depth = logits.shape[0]
    const = lambda b, s: (0, 0)
    return pl.pallas_call(
        functools.partial(_hgrn_kernel, layer=layer),
        grid=(batch, steps),
        in_specs=[pl.BlockSpec((ROW_TILE, 3 * GROUP), lambda b, s: (b * steps + s, 0)),
                  pl.BlockSpec((depth, GROUP), const),
                  pl.BlockSpec((1, GROUP), const),
                  pl.BlockSpec(w_all.shape, const),
                  pl.BlockSpec(mask_all.shape, const)],
        out_specs=pl.BlockSpec((ROW_TILE, GROUP), lambda b, s: (b * steps + s, 0)),
        out_shape=jax.ShapeDtypeStruct((hg.shape[0], GROUP), F32),
        scratch_shapes=[pltpu.VMEM((HGRN_D, GROUP), F32)],
        compiler_params=pltpu.CompilerParams(dimension_semantics=("parallel", "arbitrary"),
                                             vmem_limit_bytes=VMEM_LIMIT),
        name="hgrn",
    )(hg, logits, gain, w_all, mask_all)


def _mem_kv_kernel(m_ref, g_ref, w_ref, gk_ref, seg_ref, k_ref, v_ref):
    x = m_ref[...]
    ms = jnp.mean(x * x, axis=-1, keepdims=True)
    hb = (x * lax.rsqrt(ms + NORM_EPS) * g_ref[...]).astype(BF16)
    gk = gk_ref[...]
    seg = seg_ref[...]
    for c in range(GROUP // 256):
        kc = _nn(hb, w_ref[:, c * 256:(c + 1) * 256])
        ss = _nn((kc * kc).astype(BF16), seg) * (1.0 / MEM_D)
        k_ref[:, c * 256:(c + 1) * 256] = kc * lax.rsqrt(ss + NORM_EPS) * gk[:, c * 256:(c + 1) * 256]
    v_ref[...] = _nn(hb, w_ref[:, GROUP:2 * GROUP])


def _mem_kv(memf, g, w_bf, gk, mem_len):
    n = memf.shape[0]
    seg = np.arange(256)[:, None] // MEM_D == np.arange(256)[None, :] // MEM_D
    seg = jnp.asarray(seg, BF16)
    const = lambda i: (0, 0)
    row = lambda i: (i, 0)
    return pl.pallas_call(
        _mem_kv_kernel,
        grid=(n // mem_len,),
        in_specs=[pl.BlockSpec((mem_len, D_MODEL), row),
                  pl.BlockSpec((1, D_MODEL), const),
                  pl.BlockSpec((D_MODEL, 2 * GROUP), const),
                  pl.BlockSpec((1, GROUP), const),
                  pl.BlockSpec((256, 256), const)],
        out_specs=[pl.BlockSpec((mem_len, GROUP), row)] * 2,
        out_shape=[jax.ShapeDtypeStruct((n, GROUP), F32)] * 2,
        compiler_params=pltpu.CompilerParams(dimension_semantics=("parallel",), vmem_limit_bytes=VMEM_LIMIT),
        name="mem_kv",
    )(memf, g, w_bf, gk, seg)


def _out_proj_kernel(x_ref, oa_ref, oh_ref, qm_ref, z_ref, km_ref, vm_ref, gq_ref, w_ref, o_ref):
    out = x_ref[...]
    out = out + _nn((oa_ref[...] * _silu(z_ref[:, 0:GROUP])).astype(BF16), w_ref[0:GROUP, :])
    out = out + _nn((oh_ref[...] * _silu(z_ref[:, GROUP:2 * GROUP])).astype(BF16), w_ref[GROUP:2 * GROUP, :])
    gq = gq_ref[...]
    for h in range(MEM_HEADS):
        cols = slice(h * MEM_D, (h + 1) * MEM_D)
        q = qm_ref[:, cols]
        ms = jnp.mean(q * q, axis=-1, keepdims=True)
        qn = q * lax.rsqrt(ms + NORM_EPS) * gq * (MEM_D ** -0.5)
        s = _nt(qn.astype(BF16), km_ref[:, cols].astype(BF16))
        p = jnp.exp(s - jnp.max(s, axis=1, keepdims=True))
        om = _nn(p.astype(BF16), vm_ref[:, cols].astype(BF16)) / jnp.sum(p, axis=1, keepdims=True)
        zc = slice(2 * GROUP + h * MEM_D, 2 * GROUP + (h + 1) * MEM_D)
        out = out + _nn((om * _silu(z_ref[:, zc])).astype(BF16), w_ref[zc, :])
    o_ref[...] = out


def _out_proj(xf, oa, oh, qm, z, km, vm, gq, w_bf, seq, mem_len):
    n = xf.shape[0]
    tm = ROW_TILE
    steps = seq // tm
    const = lambda i: (0, 0)
    row = lambda i: (i, 0)
    per_batch = lambda i: (i // steps, 0)
    return pl.pallas_call(
        _out_proj_kernel,
        grid=(n // tm,),
        in_specs=[pl.BlockSpec((tm, D_MODEL), row),
                  pl.BlockSpec((tm, GROUP), row),
                  pl.BlockSpec((tm, GROUP), row),
                  pl.BlockSpec((tm, GROUP), row),
                  pl.BlockSpec((tm, D_MIX), row),
                  pl.BlockSpec((mem_len, GROUP), per_batch),
                  pl.BlockSpec((mem_len, GROUP), per_batch),
                  pl.BlockSpec((1, MEM_D), const),
                  pl.BlockSpec((D_MIX, D_MODEL), const)],
        out_specs=pl.BlockSpec((tm, D_MODEL), row),
        out_shape=jax.ShapeDtypeStruct((n, D_MODEL), F32),
        compiler_params=pltpu.CompilerParams(dimension_semantics=("parallel",), vmem_limit_bytes=VMEM_LIMIT),
        name="out_proj",
    )(xf, oa, oh, qm, z, km, vm, gq, w_bf)


def kernel(x, mem, positions, norm_g, w_in, w_out, moba_q_norm, moba_k_norm, hgrn_lb_logits, hgrn_o_norm,
           mem_norm_g, w_mem_kv, mem_q_norm, mem_k_norm):
    batch, seq, d_model = x.shape
    mem_len = mem.shape[1]
    depth = w_in.shape[0]
    assert d_model == D_MODEL and seq % ROW_TILE == 0 and mem_len % 8 == 0
    n = batch * seq
    xf = x.reshape(n, d_model)
    memf = mem.reshape(batch * mem_len, d_model)
    cosf, sinf = _rope_tables(positions.reshape(n, 1))
    for l in range(depth):
        gq = jnp.tile(moba_q_norm[l], GROUP // MOBA_D)[None, :]
        gk = jnp.tile(moba_k_norm[l], GROUP // MOBA_D)[None, :]
        qa, ka, va, hg, qm, z = _in_proj(xf, norm_g[l][None, :], w_in[l].astype(BF16), cosf, sinf, gq, gk)
        oa = _moba(qa, ka, va, batch, seq)
        oh = _hgrn(hg, hgrn_lb_logits, jnp.tile(hgrn_o_norm[l], HGRN_HEADS)[None, :], l, batch, seq)
        km, vm = _mem_kv(memf, mem_norm_g[l][None, :], w_mem_kv[l].astype(BF16),
                         jnp.tile(mem_k_norm[l], MEM_HEADS)[None, :], mem_len)
        xf = _out_proj(xf, oa, oh, qm, z, km, vm, mem_q_norm[l][None, :], w_out[l].astype(BF16), seq, mem_len)
    return xf.reshape(batch, seq, d_model)
```

```python
import functools

import numpy as np
import jax
import jax.numpy as jnp
from jax import lax
from jax.experimental import pallas as pl
from jax.experimental.pallas import tpu as pltpu

F32 = jnp.float32
BF16 = jnp.bfloat16

D_MODEL = 1024
GROUP = D_MODEL // 2
D_MIX = 3 * GROUP
IN_COLS = 7 * GROUP + D_MIX
MOBA_D = 64
MOBA_BLOCK = 256
MOBA_TOPK = 3
HGRN_D = 128
HGRN_HEADS = GROUP // HGRN_D
HGRN_CHUNK = 64
HGRN_LEVELS = 6
MEM_D = 128
MEM_HEADS = GROUP // MEM_D
ROPE_THETA = 500000.0
ROPE_DIM = MOBA_D // 4
ROPE_HALF = ROPE_DIM // 2
NORM_EPS = 1e-6
LANES = 128
ROW_TILE = 256
MASKED = -1e30
VMEM_LIMIT = 48 * 1024 * 1024


def _nt(a, b):
    return lax.dot_general(a, b, (((1,), (1,)), ((), ())), preferred_element_type=F32)


def _tn(a, b):
    return lax.dot_general(a, b, (((0,), (0,)), ((), ())), preferred_element_type=F32)


def _nn(a, b):
    return jnp.dot(a, b, preferred_element_type=F32)


def _split2(x):
    hi = x.astype(BF16)
    lo = (x - hi.astype(F32)).astype(BF16)
    return hi, lo


def _split3(x):
    hi = x.astype(BF16)
    r = x - hi.astype(F32)
    mid = r.astype(BF16)
    lo = (r - mid.astype(F32)).astype(BF16)
    return hi, mid, lo


def _sigmoid(x):
    return 1.0 / (1.0 + jnp.exp(-x))


def _silu(x):
    return x * _sigmoid(x)


def _rope_table_kernel(pos_ref, invf_ref, sgn_ref, cos_ref, sin_ref):
    ang = pos_ref[...].astype(F32) * invf_ref[...]
    sgn = sgn_ref[...]
    on = sgn != 0.0
    cos_ref[...] = jnp.where(on, jnp.cos(ang), 1.0)
    sin_ref[...] = sgn * jnp.sin(ang)


def _rope_tables(pos):
    n = pos.shape[0]
    tm = min(n, 2048)
    lane = np.arange(LANES) % MOBA_D
    invf = np.where(lane < ROPE_DIM, ROPE_THETA ** (-(lane % ROPE_HALF).astype(np.float64) / ROPE_HALF), 0.0)
    sgn = np.where(lane < ROPE_HALF, -1.0, np.where(lane < ROPE_DIM, 1.0, 0.0))
    invf = jnp.asarray(invf[None, :], F32)
    sgn = jnp.asarray(sgn[None, :], F32)
    return pl.pallas_call(
        _rope_table_kernel,
        grid=(n // tm,),
        in_specs=[pl.BlockSpec((tm, 1), lambda i: (i, 0)),
                  pl.BlockSpec((1, LANES), lambda i: (0, 0)),
                  pl.BlockSpec((1, LANES), lambda i: (0, 0))],
        out_specs=[pl.BlockSpec((tm, LANES), lambda i: (i, 0))] * 2,
        out_shape=[jax.ShapeDtypeStruct((n, LANES), F32)] * 2,
        name="rope_tables",
    )(pos, invf, sgn)


def _in_proj_kernel(x_ref, g_ref, w_ref, cos_ref, sin_ref, gq_ref, gk_ref, seg_ref,
                    qa_ref, ka_ref, va_ref, hg_ref, qm_ref, z_ref):
    x = x_ref[...]
    ms = jnp.mean(x * x, axis=-1, keepdims=True)
    hb = (x * lax.rsqrt(ms + NORM_EPS) * g_ref[...]).astype(BF16)

    def proj(c0, width):
        return _nn(hb, w_ref[:, c0:c0 + width])

    cosf = cos_ref[...]
    sinf = sin_ref[...]
    lane = lax.broadcasted_iota(jnp.int32, (1, LANES), 1) % MOBA_D
    lo_lane = lane < ROPE_HALF
    seg = seg_ref[...]

    def head_norm_rope(p, gain, out_ref):
        for c in range(GROUP // 256):
            pc = p[:, c * 256:(c + 1) * 256]
            ss = _nn((pc * pc).astype(BF16), seg) * (1.0 / MOBA_D)
            pn = pc * lax.rsqrt(ss + NORM_EPS) * gain[:, c * 256:(c + 1) * 256]
            for v in range(2):
                xv = pn[:, v * LANES:(v + 1) * LANES]
                rot = jnp.where(lo_lane, pltpu.roll(xv, LANES - ROPE_HALF, 1), pltpu.roll(xv, ROPE_HALF, 1))
                col = c * 256 + v * LANES
                out_ref[:, col:col + LANES] = xv * cosf + rot * sinf

    head_norm_rope(proj(0, GROUP), gq_ref[...], qa_ref)
    head_norm_rope(proj(GROUP, GROUP), gk_ref[...], ka_ref)
    va_ref[...] = proj(2 * GROUP, GROUP)
    for c in range(3):
        hg_ref[:, c * GROUP:(c + 1) * GROUP] = proj((3 + c) * GROUP, GROUP)
    qm_ref[...] = proj(6 * GROUP, GROUP)
    for c in range(3):
        z_ref[:, c * GROUP:(c + 1) * GROUP] = proj((7 + c) * GROUP, GROUP)


def _in_proj(xf, g, w_bf, cosf, sinf, gq, gk):
    n = xf.shape[0]
    tm = ROW_TILE
    seg = np.arange(256)[:, None] // MOBA_D == np.arange(256)[None, :] // MOBA_D
    seg = jnp.asarray(seg, BF16)
    const = lambda i: (0, 0)
    row = lambda i: (i, 0)
    widths = (GROUP, GROUP, GROUP, 3 * GROUP, GROUP, D_MIX)
    return pl.pallas_call(
        _in_proj_kernel,
        grid=(n // tm,),
        in_specs=[pl.BlockSpec((tm, D_MODEL), row),
                  pl.BlockSpec((1, D_MODEL), const),
                  pl.BlockSpec((D_MODEL, IN_COLS), const, pipeline_mode=pl.Buffered(1)),
                  pl.BlockSpec((tm, LANES), row),
                  pl.BlockSpec((tm, LANES), row),
                  pl.BlockSpec((1, GROUP), const),
                  pl.BlockSpec((1, GROUP), const),
                  pl.BlockSpec((256, 256), const)],
        out_specs=[pl.BlockSpec((tm, w), row) for w in widths],
        out_shape=[jax.ShapeDtypeStruct((n, w), F32) for w in widths],
        compiler_params=pltpu.CompilerParams(dimension_semantics=("parallel",), vmem_limit_bytes=VMEM_LIMIT),
        name="in_proj",
    )(xf, g, w_bf, cosf, sinf, gq, gk, seg)


def _moba_kernel(q_ref, k_ref, v_ref, o_ref, kmean_ref, kaug_ref, vaug_ref, s_ref, *, nb):
    i = pl.program_id(2)
    blk = MOBA_BLOCK
    lane = lax.broadcasted_iota(jnp.int32, (1, LANES), 1)
    heads = [(lane >= h * MOBA_D) & (lane < (h + 1) * MOBA_D) for h in range(2)]
    spare = [(1 - h) * MOBA_D for h in range(2)]

    @pl.when(i == 0)
    def _():
        for j in range(nb):
            kj = k_ref[j * blk:(j + 1) * blk, :]
            vj = v_ref[j * blk:(j + 1) * blk, :]
            kmean_ref[j:j + 1, :] = jnp.mean(kj, axis=0, keepdims=True)
            for h in range(2):
                onehot = jnp.where(lane == spare[h] + j, 1.0, 0.0)
                kaug_ref[h, j] = jnp.where(heads[h], kj, onehot).astype(BF16)
                vaug_ref[h, j] = jnp.where(heads[h], vj, 1.0).astype(BF16)

    q = q_ref[...]
    km = kmean_ref[...]
    blk_row = lax.broadcasted_iota(jnp.int32, (nb, 1), 0)
    past = blk_row < i

    q_aug = []
    for h in range(2):
        qh = jnp.where(heads[h], q, 0.0)
        q_hi, q_lo = _split2(qh)
        k_hi, k_lo = _split2(jnp.where(heads[h], km, 0.0))
        gate_t = _nt(k_hi, q_hi) + _nt(k_hi, q_lo) + _nt(k_lo, q_hi)
        gate_t = jnp.where(past, gate_t, -jnp.inf)
        cnt = jnp.zeros((nb, blk), F32)
        for n in range(nb):
            gn = gate_t[n:n + 1, :]
            before = (gn > gate_t) | ((gn == gate_t) & (n < blk_row))
            cnt = cnt + jnp.where(before, 1.0, 0.0)
        sel = (past & (cnt < float(MOBA_TOPK))) | (blk_row == i)
        bias_t = jnp.where(sel, 0.0, MASKED)
        parts = []
        if spare[h]:
            parts.append(jnp.zeros((spare[h], blk), F32))
        parts.append(bias_t)
        parts.append(jnp.zeros((LANES - spare[h] - nb, blk), F32))
        bias = jnp.concatenate(parts, axis=0).T
        q_aug.append(jnp.where(heads[h], qh * (MOBA_D ** -0.5), bias).astype(BF16))

    def attend(c):
        qpos = lax.broadcasted_iota(jnp.int32, (blk, blk), 0)
        kpos = lax.broadcasted_iota(jnp.int32, (blk, blk), 1)
        causal = kpos <= qpos
        outs = []
        for h in range(2):
            mx = None
            for j in range(c + 1):
                s = _nt(q_aug[h], kaug_ref[h, j])
                if j == c:
                    s = jnp.where(causal, s, MASKED)
                s_ref[h, j] = s
                mx = s if mx is None else jnp.maximum(mx, s)
            m = jnp.max(mx, axis=1, keepdims=True)
            acc = None
            for j in range(c + 1):
                p = jnp.exp(s_ref[h, j] - m).astype(BF16)
                pv = _nn(p, vaug_ref[h, j])
                acc = pv if acc is None else acc + pv
            outs.append(acc / pltpu.roll(acc, MOBA_D, 1))
        o_ref[...] = jnp.where(heads[0], outs[0], outs[1])

    for c in range(nb):
        pl.when(i == c)(functools.partial(attend, c))


def _moba(qa, ka, va, batch, seq):
    nb = seq // MOBA_BLOCK
    pairs = GROUP // LANES
    return pl.pallas_call(
        functools.partial(_moba_kernel, nb=nb),
        grid=(batch, pairs, nb),
        in_specs=[pl.BlockSpec((MOBA_BLOCK, LANES), lambda b, p, i: (b * nb + i, p)),
                  pl.BlockSpec((seq, LANES), lambda b, p, i: (b, p)),
                  pl.BlockSpec((seq, LANES), lambda b, p, i: (b, p))],
        out_specs=pl.BlockSpec((MOBA_BLOCK, LANES), lambda b, p, i: (b * nb + i, p)),
        out_shape=jax.ShapeDtypeStruct(qa.shape, F32),
        scratch_shapes=[pltpu.VMEM((nb, LANES), F32),
                        pltpu.VMEM((2, nb, MOBA_BLOCK, LANES), BF16),
                        pltpu.VMEM((2, nb, MOBA_BLOCK, LANES), BF16),
                        pltpu.VMEM((2, nb, MOBA_BLOCK, MOBA_BLOCK), F32)],
        compiler_params=pltpu.CompilerParams(dimension_semantics=("parallel", "parallel", "arbitrary"),
                                             vmem_limit_bytes=VMEM_LIMIT),
        name="moba",
    )(qa, ka, va)


def _hgrn_consts():
    c = HGRN_CHUNK
    t = np.arange(c)
    w_rows, masks = [], []
    for lvl in range(HGRN_LEVELS):
        m = 1 << lvl
        blk = t // m
        odd = (blk % 2) == 1
        start = blk * m
        end = start + m - 1
        u = t[None, :]
        w_odd = (u >= start[:, None]) & (u <= t[:, None])
        w_even = (u > t[:, None]) & (u <= end[:, None])
        w_rows.append(np.where(odd[:, None], w_odd, w_even))
        masks.append(odd[:, None] & (blk[None, :] == blk[:, None] - 1))
    w_rows.append(t[None, :] <= t[:, None])
    w_rows.append(t[None, :] > t[:, None])
    masks.append(t[None, :] == t[:, None])
    w_all = np.concatenate(w_rows, axis=0).astype(np.float32)
    mask_all = np.concatenate(masks, axis=0).astype(np.float32)
    return jnp.asarray(w_all, BF16), jnp.asarray(mask_all, F32)


def _hgrn_kernel(hg_ref, logit_ref, gain_ref, w_ref, mask_ref, o_ref, state_ref, *, layer):
    c = HGRN_CHUNK

    @pl.when(pl.program_id(1) == 0)
    def _():
        state_ref[...] = jnp.zeros_like(state_ref)

    if layer > 0:
        lg = logit_ref[...]
        e = jnp.exp(lg - jnp.max(lg, axis=0, keepdims=True))
        sm = e / jnp.sum(e, axis=0, keepdims=True)
        lb = jnp.sum(sm[1:layer + 1, :], axis=0, keepdims=True)
        log_lb = jnp.log(lb)
        log_1m_lb = jnp.log1p(-lb)
    w_all = w_ref[...]
    gain = gain_ref[...]

    for ci in range(ROW_TILE // c):
        rows = slice(ci * c, (ci + 1) * c)
        qh = hg_ref[rows, 0:GROUP]
        fl = hg_ref[rows, GROUP:2 * GROUP]
        vi = hg_ref[rows, 2 * GROUP:3 * GROUP]
        qf = _silu(qh)
        log_sig = jnp.minimum(fl, 0.0) - jnp.log1p(jnp.exp(-jnp.abs(fl)))
        if layer > 0:
            b = log_1m_lb + log_sig
            lf = jnp.maximum(log_lb, b) + jnp.log1p(jnp.exp(-jnp.abs(log_lb - b)))
            kf = (1.0 - lb) * _sigmoid(-fl)
        else:
            lf = log_sig
            kf = _sigmoid(-fl)

        f1, f2, f3 = _split3(lf)
        e_all = jnp.exp(_nn(w_all, f1) + _nn(w_all, f2) + _nn(w_all, f3))
        e_cum = e_all[HGRN_LEVELS * c:(HGRN_LEVELS + 1) * c, :]
        e_rest = e_all[(HGRN_LEVELS + 1) * c:(HGRN_LEVELS + 2) * c, :]
        e_end = e_all[(HGRN_LEVELS + 1) * c - 1:(HGRN_LEVELS + 1) * c, :]

        for h in range(HGRN_HEADS):
            cols = slice(h * HGRN_D, (h + 1) * HGRN_D)
            qfh, kfh = qf[:, cols], kf[:, cols]
            scores = mask_ref[HGRN_LEVELS * c:(HGRN_LEVELS + 1) * c, :] * _nt(qfh.astype(BF16), kfh.astype(BF16))
            for lvl in range(HGRN_LEVELS):
                el = e_all[lvl * c:(lvl + 1) * c, cols]
                scores = scores + mask_ref[lvl * c:(lvl + 1) * c, :] * _nt((qfh * el).astype(BF16),
                                                                           (kfh * el).astype(BF16))
            vb = vi[:, cols].astype(BF16)
            st = state_ref[:, cols]
            o = _nn(scores.astype(BF16), vb) + _nt((qfh * e_cum[:, cols]).astype(BF16), st.astype(BF16))
            state_ref[:, cols] = st * e_end[:, cols] + _tn(vb, (kfh * e_rest[:, cols]).astype(BF16))
            ms = jnp.mean(o * o, axis=-1, keepdims=True)
            o_ref[rows, cols] = o * lax.rsqrt(ms + NORM_EPS) * gain[:, cols]


def _hgrn(hg, logits, gain, layer, batch, seq):
    w_all, mask_all = _hgrn_consts()
    steps = seq // ROW_TILE
    depth = logits.shape[0]
    const = lambda b, s: (0, 0)
    return pl.pallas_call(
        functools.partial(_hgrn_kernel, layer=layer),
        grid=(batch, steps),
        in_specs=[pl.BlockSpec((ROW_TILE, 3 * GROUP), lambda b, s: (b * steps + s, 0)),
                  pl.BlockSpec((depth, GROUP), const),
                  pl.BlockSpec((1, GROUP), const),
                  pl.BlockSpec(w_all.shape, const),
                  pl.BlockSpec(mask_all.shape, const)],
        out_specs=pl.BlockSpec((ROW_TILE, GROUP), lambda b, s: (b * steps + s, 0)),
        out_shape=jax.ShapeDtypeStruct((hg.shape[0], GROUP), F32),
        scratch_shapes=[pltpu.VMEM((HGRN_D, GROUP), F32)],
        compiler_params=pltpu.CompilerParams(dimension_semantics=("parallel", "arbitrary"),
                                             vmem_limit_bytes=VMEM_LIMIT),
        name="hgrn",
    )(hg, logits, gain, w_all, mask_all)


def _mem_kv_kernel(m_ref, g_ref, w_ref, gk_ref, seg_ref, k_ref, v_ref):
    x = m_ref[...]
    ms = jnp.mean(x * x, axis=-1, keepdims=True)
    hb = (x * lax.rsqrt(ms + NORM_EPS) * g_ref[...]).astype(BF16)
    gk = gk_ref[...]
    seg = seg_ref[...]
    for c in range(GROUP // 256):
        kc = _nn(hb, w_ref[:, c * 256:(c + 1) * 256])
        ss = _nn((kc * kc).astype(BF16), seg) * (1.0 / MEM_D)
        k_ref[:, c * 256:(c + 1) * 256] = kc * lax.rsqrt(ss + NORM_EPS) * gk[:, c * 256:(c + 1) * 256]
    v_ref[...] = _nn(hb, w_ref[:, GROUP:2 * GROUP])


def _mem_kv(memf, g, w_bf, gk, mem_len):
    n = memf.shape[0]
    seg = np.arange(256)[:, None] // MEM_D == np.arange(256)[None, :] // MEM_D
    seg = jnp.asarray(seg, BF16)
    const = lambda i: (0, 0)
    row = lambda i: (i, 0)
    return pl.pallas_call(
        _mem_kv_kernel,
        grid=(n // mem_len,),
        in_specs=[pl.BlockSpec((mem_len, D_MODEL), row),
                  pl.BlockSpec((1, D_MODEL), const),
                  pl.BlockSpec((D_MODEL, 2 * GROUP), const),
                  pl.BlockSpec((1, GROUP), const),
                  pl.BlockSpec((256, 256), const)],
        out_specs=[pl.BlockSpec((mem_len, GROUP), row)] * 2,
        out_shape=[jax.ShapeDtypeStruct((n, GROUP), F32)] * 2,
        compiler_params=pltpu.CompilerParams(dimension_semantics=("parallel",), vmem_limit_bytes=VMEM_LIMIT),
        name="mem_kv",
    )(memf, g, w_bf, gk, seg)


def _out_proj_kernel(x_ref, oa_ref, oh_ref, qm_ref, z_ref, km_ref, vm_ref, gq_ref, w_ref, o_ref):
    out = x_ref[...]
    out = out + _nn((oa_ref[...] * _silu(z_ref[:, 0:GROUP])).astype(BF16), w_ref[0:GROUP, :])
    out = out + _nn((oh_ref[...] * _silu(z_ref[:, GROUP:2 * GROUP])).astype(BF16), w_ref[GROUP:2 * GROUP, :])
    gq = gq_ref[...]
    for h in range(MEM_HEADS):
        cols = slice(h * MEM_D, (h + 1) * MEM_D)
        q = qm_ref[:, cols]
        ms = jnp.mean(q * q, axis=-1, keepdims=True)
        qn = q * lax.rsqrt(ms + NORM_EPS) * gq * (MEM_D ** -0.5)
        s = _nt(qn.astype(BF16), km_ref[:, cols].astype(BF16))
        p = jnp.exp(s - jnp.max(s, axis=1, keepdims=True))
        om = _nn(p.astype(BF16), vm_ref[:, cols].astype(BF16)) / jnp.sum(p, axis=1, keepdims=True)
        zc = slice(2 * GROUP + h * MEM_D, 2 * GROUP + (h + 1) * MEM_D)
        out = out + _nn((om * _silu(z_ref[:, zc])).astype(BF16), w_ref[zc, :])
    o_ref[...] = out


def _out_proj(xf, oa, oh, qm, z, km, vm, gq, w_bf, seq, mem_len):
    n = xf.shape[0]
    tm = ROW_TILE
    steps = seq // tm
    const = lambda i: (0, 0)
    row = lambda i: (i, 0)
    per_batch = lambda i: (i // steps, 0)
    return pl.pallas_call(
        _out_proj_kernel,
        grid=(n // tm,),
        in_specs=[pl.BlockSpec((tm, D_MODEL), row),
                  pl.BlockSpec((tm, GROUP), row),
                  pl.BlockSpec((tm, GROUP), row),
                  pl.BlockSpec((tm, GROUP), row),
                  pl.BlockSpec((tm, D_MIX), row),
                  pl.BlockSpec((mem_len, GROUP), per_batch),
                  pl.BlockSpec((mem_len, GROUP), per_batch),
                  pl.BlockSpec((1, MEM_D), const),
                  pl.BlockSpec((D_MIX, D_MODEL), const)],
        out_specs=pl.BlockSpec((tm, D_MODEL), row),
        out_shape=jax.ShapeDtypeStruct((n, D_MODEL), F32),
        compiler_params=pltpu.CompilerParams(dimension_semantics=("parallel",), vmem_limit_bytes=VMEM_LIMIT),
        name="out_proj",
    )(xf, oa, oh, qm, z, km, vm, gq, w_bf)


def kernel(x, mem, positions, norm_g, w_in, w_out, moba_q_norm, moba_k_norm, hgrn_lb_logits, hgrn_o_norm,
           mem_norm_g, w_mem_kv, mem_q_norm, mem_k_norm):
    batch, seq, d_model = x.shape
    mem_len = mem.shape[1]
    depth = w_in.shape[0]
    assert d_model == D_MODEL and seq % ROW_TILE == 0 and mem_len % 8 == 0
    n = batch * seq
    xf = x.reshape(n, d_model)
    memf = mem.reshape(batch * mem_len, d_model)
    cosf, sinf = _rope_tables(positions.reshape(n, 1))
    for l in range(depth):
        gq = jnp.tile(moba_q_norm[l], GROUP // MOBA_D)[None, :]
        gk = jnp.tile(moba_k_norm[l], GROUP // MOBA_D)[None, :]
        qa, ka, va, hg, qm, z = _in_proj(xf, norm_g[l][None, :], w_in[l].astype(BF16), cosf, sinf, gq, gk)
        oa = _moba(qa, ka, va, batch, seq)
        oh = _hgrn(hg, hgrn_lb_logits, jnp.tile(hgrn_o_norm[l], HGRN_HEADS)[None, :], l, batch, seq)
        km, vm = _mem_kv(memf, mem_norm_g[l][None, :], w_mem_kv[l].astype(BF16),
                         jnp.tile(mem_k_norm[l], MEM_HEADS)[None, :], mem_len)
        xf = _out_proj(xf, oa, oh, qm, z, km, vm, mem_q_norm[l][None, :], w_out[l].astype(BF16), seq, mem_len)
    return xf.reshape(batch, seq, d_model)
```

```python
import functools

import numpy as np
import jax
import jax.numpy as jnp
from jax import lax
from jax.experimental import pallas as pl
from jax.experimental.pallas import tpu as pltpu

F32 = jnp.float32
BF16 = jnp.bfloat16

D_MODEL = 1024
GROUP = D_MODEL // 2
D_MIX = 3 * GROUP
IN_COLS = 7 * GROUP + D_MIX
MOBA_D = 64
MOBA_BLOCK = 256
MOBA_TOPK = 3
HGRN_D = 128
HGRN_HEADS = GROUP // HGRN_D
HGRN_CHUNK = 64
HGRN_LEVELS = 6
HGRN_MM_LEVELS = 3
LOG2E = 1.4426950408889634
MEM_D = 128
MEM_HEADS = GROUP // MEM_D
ROPE_THETA = 500000.0
ROPE_DIM = MOBA_D // 4
ROPE_HALF = ROPE_DIM // 2
NORM_EPS = 1e-6
LANES = 128
ROW_TILE = 256
MASKED = -1e30
VMEM_LIMIT = 48 * 1024 * 1024


def _nt(a, b):
    return lax.dot_general(a, b, (((1,), (1,)), ((), ())), preferred_element_type=F32)


def _tn(a, b):
    return lax.dot_general(a, b, (((0,), (0,)), ((), ())), preferred_element_type=F32)


def _nn(a, b):
    return jnp.dot(a, b, preferred_element_type=F32)


def _split2(x):
    hi = x.astype(BF16)
    lo = (x - hi.astype(F32)).astype(BF16)
    return hi, lo


def _split3(x):
    hi = x.astype(BF16)
    r = x - hi.astype(F32)
    mid = r.astype(BF16)
    lo = (r - mid.astype(F32)).astype(BF16)
    return hi, mid, lo


def _sigmoid(x):
    return 1.0 / (1.0 + jnp.exp(-x))


def _silu(x):
    return x * _sigmoid(x)


def _rope_table_kernel(pos_ref, invf_ref, sgn_ref, cos_ref, sin_ref):
    ang = pos_ref[...].astype(F32) * invf_ref[...]
    sgn = sgn_ref[...]
    on = sgn != 0.0
    cos_ref[...] = jnp.where(on, jnp.cos(ang), 1.0)
    sin_ref[...] = sgn * jnp.sin(ang)


def _rope_tables(pos):
    n = pos.shape[0]
    tm = min(n, 2048)
    lane = np.arange(LANES) % MOBA_D
    invf = np.where(lane < ROPE_DIM, ROPE_THETA ** (-(lane % ROPE_HALF).astype(np.float64) / ROPE_HALF), 0.0)
    sgn = np.where(lane < ROPE_HALF, -1.0, np.where(lane < ROPE_DIM, 1.0, 0.0))
    invf = jnp.asarray(invf[None, :], F32)
    sgn = jnp.asarray(sgn[None, :], F32)
    return pl.pallas_call(
        _rope_table_kernel,
        grid=(n // tm,),
        in_specs=[pl.BlockSpec((tm, 1), lambda i: (i, 0)),
                  pl.BlockSpec((1, LANES), lambda i: (0, 0)),
                  pl.BlockSpec((1, LANES), lambda i: (0, 0))],
        out_specs=[pl.BlockSpec((tm, LANES), lambda i: (i, 0))] * 2,
        out_shape=[jax.ShapeDtypeStruct((n, LANES), F32)] * 2,
        name="rope_tables",
    )(pos, invf, sgn)


def _in_proj_kernel(x_ref, g_ref, w_ref, cos_ref, sin_ref, gq_ref, gk_ref, seg_ref,
                    qa_ref, ka_ref, va_ref, qh_ref, fh_ref, ih_ref, qm_ref, z_ref):
    x = x_ref[...]
    ms = jnp.mean(x * x, axis=-1, keepdims=True)
    hb = (x * lax.rsqrt(ms + NORM_EPS) * g_ref[...]).astype(BF16)

    def proj(c0, width):
        return _nn(hb, w_ref[:, c0:c0 + width])

    cosf = cos_ref[...]
    sinf = sin_ref[...]
    lane = lax.broadcasted_iota(jnp.int32, (1, LANES), 1) % MOBA_D
    lo_lane = lane < ROPE_HALF
    seg = seg_ref[...]

    def head_norm_rope(p, gain, out_ref):
        for c in range(GROUP // 256):
            pc = p[:, c * 256:(c + 1) * 256]
            ss = _nn((pc * pc).astype(BF16), seg) * (1.0 / MOBA_D)
            pn = pc * lax.rsqrt(ss + NORM_EPS) * gain[:, c * 256:(c + 1) * 256]
            for v in range(2):
                xv = pn[:, v * LANES:(v + 1) * LANES]
                rot = jnp.where(lo_lane, pltpu.roll(xv, LANES - ROPE_HALF, 1), pltpu.roll(xv, ROPE_HALF, 1))
                col = c * 256 + v * LANES
                out_ref[:, col:col + LANES] = (xv * cosf + rot * sinf).astype(out_ref.dtype)

    head_norm_rope(proj(0, GROUP), gq_ref[...], qa_ref)
    head_norm_rope(proj(GROUP, GROUP), gk_ref[...], ka_ref)
    va_ref[...] = proj(2 * GROUP, GROUP).astype(BF16)
    qh_ref[...] = proj(3 * GROUP, GROUP).astype(BF16)
    fh_ref[...] = proj(4 * GROUP, GROUP)
    ih_ref[...] = proj(5 * GROUP, GROUP).astype(BF16)
    qm_ref[...] = proj(6 * GROUP, GROUP).astype(BF16)
    for c in range(3):
        z_ref[:, c * GROUP:(c + 1) * GROUP] = proj((7 + c) * GROUP, GROUP).astype(BF16)


def _in_proj(xf, g, w_bf, cosf, sinf, gq, gk):
    n = xf.shape[0]
    tm = ROW_TILE
    seg = np.arange(256)[:, None] // MOBA_D == np.arange(256)[None, :] // MOBA_D
    seg = jnp.asarray(seg, BF16)
    const = lambda i: (0, 0)
    row = lambda i: (i, 0)
    outs = ((GROUP, F32), (GROUP, BF16), (GROUP, BF16), (GROUP, BF16), (GROUP, F32), (GROUP, BF16),
            (GROUP, BF16), (D_MIX, BF16))
    return pl.pallas_call(
        _in_proj_kernel,
        grid=(n // tm,),
        in_specs=[pl.BlockSpec((tm, D_MODEL), row),
                  pl.BlockSpec((1, D_MODEL), const),
                  pl.BlockSpec((D_MODEL, IN_COLS), const, pipeline_mode=pl.Buffered(1)),
                  pl.BlockSpec((tm, LANES), row),
                  pl.BlockSpec((tm, LANES), row),
                  pl.BlockSpec((1, GROUP), const),
                  pl.BlockSpec((1, GROUP), const),
                  pl.BlockSpec((256, 256), const)],
        out_specs=[pl.BlockSpec((tm, w), row) for w, _ in outs],
        out_shape=[jax.ShapeDtypeStruct((n, w), dt) for w, dt in outs],
        compiler_params=pltpu.CompilerParams(dimension_semantics=("parallel",), vmem_limit_bytes=VMEM_LIMIT),
        name="in_proj",
    )(xf, g, w_bf, cosf, sinf, gq, gk, seg)


def _moba_kernel(q_ref, k_ref, v_ref, o_ref, kmean_ref, kaug_ref, vaug_ref, s_ref, *, nb):
    i = pl.program_id(2)
    blk = MOBA_BLOCK
    lane = lax.broadcasted_iota(jnp.int32, (1, LANES), 1)
    heads = [(lane >= h * MOBA_D) & (lane < (h + 1) * MOBA_D) for h in range(2)]
    spare = [(1 - h) * MOBA_D for h in range(2)]

    @pl.when(i == 0)
    def _():
        for j in range(nb):
            kj = k_ref[j * blk:(j + 1) * blk, :]
            vj = v_ref[j * blk:(j + 1) * blk, :]
            kmean_ref[j:j + 1, :] = jnp.mean(kj.astype(F32), axis=0, keepdims=True)
            for h in range(2):
                onehot = jnp.where(lane == spare[h] + j, 1.0, 0.0).astype(BF16)
                kaug_ref[h, j] = jnp.where(heads[h], kj, onehot)
                vaug_ref[h, j] = jnp.where(heads[h], vj, jnp.ones_like(vj))

    q = q_ref[...]
    km = kmean_ref[...]
    blk_row = lax.broadcasted_iota(jnp.int32, (nb, 1), 0)
    past = blk_row < i

    q_aug = []
    for h in range(2):
        qh = jnp.where(heads[h], q, 0.0)
        q_hi, q_lo = _split2(qh)
        k_hi, k_lo = _split2(jnp.where(heads[h], km, 0.0))
        gate_t = _nt(k_hi, q_hi) + _nt(k_hi, q_lo) + _nt(k_lo, q_hi)
        gate_t = jnp.where(past, gate_t, -jnp.inf)
        cnt = jnp.zeros((nb, blk), F32)
        for n in range(nb):
            gn = gate_t[n:n + 1, :]
            before = (gn > gate_t) | ((gn == gate_t) & (n < blk_row))
            cnt = cnt + jnp.where(before, 1.0, 0.0)
        sel = (past & (cnt < float(MOBA_TOPK))) | (blk_row == i)
        bias_t = jnp.where(sel, 0.0, MASKED)
        parts = []
        if spare[h]:
            parts.append(jnp.zeros((spare[h], blk), F32))
        parts.append(bias_t)
        parts.append(jnp.zeros((LANES - spare[h] - nb, blk), F32))
        bias = jnp.concatenate(parts, axis=0).T
        q_aug.append(jnp.where(heads[h], qh * (MOBA_D ** -0.5), bias).astype(BF16))

    def attend(c):
        qpos = lax.broadcasted_iota(jnp.int32, (blk, blk), 0)
        kpos = lax.broadcasted_iota(jnp.int32, (blk, blk), 1)
        causal = kpos <= qpos
        outs = []
        for h in range(2):
            mx = None
            for j in range(c + 1):
                s = _nt(q_aug[h], kaug_ref[h, j])
                if j == c:
                    s = jnp.where(causal, s, MASKED)
                s_ref[h, j] = s
                mx = s if mx is None else jnp.maximum(mx, s)
            m = jnp.max(mx, axis=1, keepdims=True)
            acc = None
            for j in range(c + 1):
                p = jnp.exp(s_ref[h, j] - m).astype(BF16)
                pv = _nn(p, vaug_ref[h, j])
                acc = pv if acc is None else acc + pv
            outs.append(acc / pltpu.roll(acc, MOBA_D, 1))
        o_ref[...] = jnp.where(heads[0], outs[0], outs[1]).astype(BF16)

    for c in range(nb):
        pl.when(i == c)(functools.partial(attend, c))


def _moba(qa, ka, va, batch, seq):
    nb = seq // MOBA_BLOCK
    pairs = GROUP // LANES
    return pl.pallas_call(
        functools.partial(_moba_kernel, nb=nb),
        grid=(batch, pairs, nb),
        in_specs=[pl.BlockSpec((MOBA_BLOCK, LANES), lambda b, p, i: (b * nb + i, p)),
                  pl.BlockSpec((seq, LANES), lambda b, p, i: (b, p)),
                  pl.BlockSpec((seq, LANES), lambda b, p, i: (b, p))],
        out_specs=pl.BlockSpec((MOBA_BLOCK, LANES), lambda b, p, i: (b * nb + i, p)),
        out_shape=jax.ShapeDtypeStruct(qa.shape, BF16),
        scratch_shapes=[pltpu.VMEM((nb, LANES), F32),
                        pltpu.VMEM((2, nb, MOBA_BLOCK, LANES), BF16),
                        pltpu.VMEM((2, nb, MOBA_BLOCK, LANES), BF16),
                        pltpu.VMEM((2, nb, MOBA_BLOCK, MOBA_BLOCK), F32)],
        compiler_params=pltpu.CompilerParams(dimension_semantics=("parallel", "parallel", "arbitrary"),
                                             vmem_limit_bytes=VMEM_LIMIT),
        name="moba",
    )(qa, ka, va)


def _hgrn_consts():
    c = HGRN_CHUNK
    t = np.arange(c)
    w_rows, masks = [], []
    for lvl in range(HGRN_LEVELS):
        m = 1 << lvl
        blk = t // m
        odd = (blk % 2) == 1
        start = blk * m
        end = start + m - 1
        u = t[None, :]
        w_odd = (u >= start[:, None]) & (u <= t[:, None])
        w_even = (u > t[:, None]) & (u <= end[:, None])
        w_rows.append(np.where(odd[:, None], w_odd, w_even))
        masks.append(odd[:, None] & (blk[None, :] == blk[:, None] - 1))
    masks.append(t[None, :] == t[:, None])
    w_mm = w_rows[:HGRN_MM_LEVELS] + [t[None, :] <= t[:, None]]
    w_all = np.concatenate(w_mm, axis=0).astype(np.float32)
    w_all = np.concatenate([w_all, w_all], axis=1)
    mask_cat = np.concatenate([masks[-1]] + masks[:-1] + [np.zeros((c, c), bool)], axis=1).astype(np.float32)
    return jnp.asarray(w_all, BF16), jnp.asarray(mask_cat, F32)


def _hgrn_kernel(qh_ref, fh_ref, ih_ref, logit_ref, gain_ref, w_ref, mask_ref, o_ref, state_ref, *, layer):
    c = HGRN_CHUNK

    @pl.when(pl.program_id(1) == 0)
    def _():
        state_ref[...] = jnp.zeros_like(state_ref)

    if layer > 0:
        lg = logit_ref[...]
        e = jnp.exp(lg - jnp.max(lg, axis=0, keepdims=True))
        sm = e / jnp.sum(e, axis=0, keepdims=True)
        lb = jnp.sum(sm[1:layer + 1, :], axis=0, keepdims=True)
        log_lb = jnp.log(lb)
        log_1m_lb = jnp.log1p(-lb)
    w_all = w_ref[...]
    gain = gain_ref[...]

    for ci in range(ROW_TILE // c):
        rows = slice(ci * c, (ci + 1) * c)
        fl = fh_ref[rows, :]
        vi = ih_ref[rows, :]
        qf = _silu(qh_ref[rows, :].astype(F32))
        e = jnp.exp(-jnp.abs(fl))
        r = 1.0 / (1.0 + e)
        log_sig = jnp.minimum(fl, 0.0) - jnp.log(1.0 + e)
        sig_neg = jnp.where(fl >= 0.0, e * r, r)
        if layer > 0:
            b = log_1m_lb + log_sig
            lf = jnp.maximum(log_lb, b) + jnp.log(1.0 + jnp.exp(-jnp.abs(log_lb - b)))
            kf = (1.0 - lb) * sig_neg
        else:
            lf = log_sig
            kf = sig_neg

        f1, f2 = _split2(lf * LOG2E)
        x_mm = _nn(w_all, jnp.concatenate([f1, f2], axis=0))
        a2 = x_mm[HGRN_MM_LEVELS * c:(HGRN_MM_LEVELS + 1) * c, :]
        e_lvl = [jnp.exp2(x_mm[lvl * c:(lvl + 1) * c, :]) for lvl in range(HGRN_MM_LEVELS)]
        for lvl in range(HGRN_MM_LEVELS, HGRN_LEVELS):
            m = 1 << lvl
            parts = []
            for g in range(c // (2 * m)):
                ref = a2[2 * m * g + m - 1:2 * m * g + m, :]
                parts.append(-jnp.abs(a2[2 * m * g:2 * m * (g + 1), :] - ref))
            e_lvl.append(jnp.exp2(jnp.concatenate(parts, axis=0)))
        e_cum = jnp.exp2(a2)
        e_rest = jnp.exp2(a2[c - 1:c, :] - a2)
        e_end = e_cum[c - 1:c, :]

        for h in range(HGRN_HEADS):
            cols = slice(h * HGRN_D, (h + 1) * HGRN_D)
            qfh, kfh = qf[:, cols], kf[:, cols]
            qb = [qfh.astype(BF16)] + [(qfh * e[:, cols]).astype(BF16) for e in e_lvl]
            kb = [kfh.astype(BF16)] + [(kfh * e[:, cols]).astype(BF16) for e in e_lvl]
            zero = jnp.zeros((c, HGRN_D), BF16)
            s_parts = []
            for a in range(0, HGRN_LEVELS, 2):
                q_cat = jnp.concatenate([qb[a], qb[a + 1]], axis=1)
                k_blk = jnp.concatenate([jnp.concatenate([kb[a], zero], axis=1),
                                         jnp.concatenate([zero, kb[a + 1]], axis=1)], axis=0)
                s_parts.append(_nt(q_cat, k_blk))
            s_parts.append(_nt(qb[HGRN_LEVELS], jnp.concatenate([kb[HGRN_LEVELS], zero], axis=0)))
            p = (jnp.concatenate(s_parts, axis=1) * mask_ref[...]).astype(BF16)
            vb = vi[:, cols]
            st = state_ref[:, cols]
            o = _nn(p, jnp.concatenate([vb] * (HGRN_LEVELS + 2), axis=0))
            o = o + _nt((qfh * e_cum[:, cols]).astype(BF16), st.astype(BF16))
            state_ref[:, cols] = st * e_end[:, cols] + _tn(vb, (kfh * e_rest[:, cols]).astype(BF16))
            ms = jnp.mean(o * o, axis=-1, keepdims=True)
            o_ref[rows, cols] = (o * lax.rsqrt(ms + NORM_EPS) * gain[:, cols]).astype(BF16)


def _hgrn(qh, fh, ih, logits, gain, layer, batch, seq):
    w_all, mask_all = _hgrn_consts()
    steps = seq // ROW_TILE
    depth = logits.shape[0]
    const = lambda b, s: (0, 0)
    return pl.pallas_call(
        functools.partial(_hgrn_kernel, layer=layer),
        grid=(batch, steps),
        in_specs=[pl.BlockSpec((ROW_TILE, GROUP), lambda b, s: (b * steps + s, 0))] * 3 + [
                  pl.BlockSpec((depth, GROUP), const),
                  pl.BlockSpec((1, GROUP), const),
                  pl.BlockSpec(w_all.shape, const),
                  pl.BlockSpec(mask_all.shape, const)],
        out_specs=pl.BlockSpec((ROW_TILE, GROUP), lambda b, s: (b * steps + s, 0)),
        out_shape=jax.ShapeDtypeStruct((fh.shape[0], GROUP), BF16),
        scratch_shapes=[pltpu.VMEM((HGRN_D, GROUP), F32)],
        compiler_params=pltpu.CompilerParams(dimension_semantics=("parallel", "arbitrary"),
                                             vmem_limit_bytes=VMEM_LIMIT),
        name="hgrn",
    )(qh, fh, ih, logits, gain, w_all, mask_all)


def _mem_kv_kernel(m_ref, g_ref, w_ref, gk_ref, seg_ref, k_ref, v_ref):
    x = m_ref[...]
    ms = jnp.mean(x * x, axis=-1, keepdims=True)
    hb = (x * lax.rsqrt(ms + NORM_EPS) * g_ref[...]).astype(BF16)
    gk = gk_ref[...]
    seg = seg_ref[...]
    for c in range(GROUP // 256):
        kc = _nn(hb, w_ref[:, c * 256:(c + 1) * 256])
        ss = _nn((kc * kc).astype(BF16), seg) * (1.0 / MEM_D)
        k_ref[:, c * 256:(c + 1) * 256] = kc * lax.rsqrt(ss + NORM_EPS) * gk[:, c * 256:(c + 1) * 256]
    v_ref[...] = _nn(hb, w_ref[:, GROUP:2 * GROUP])


def _mem_kv(memf, g, w_bf, gk, mem_len):
    n = memf.shape[0]
    seg = np.arange(256)[:, None] // MEM_D == np.arange(256)[None, :] // MEM_D
    seg = jnp.asarray(seg, BF16)
    const = lambda i: (0, 0)
    row = lambda i: (i, 0)
    return pl.pallas_call(
        _mem_kv_kernel,
        grid=(n // mem_len,),
        in_specs=[pl.BlockSpec((mem_len, D_MODEL), row),
                  pl.BlockSpec((1, D_MODEL), const),
                  pl.BlockSpec((D_MODEL, 2 * GROUP), const),
                  pl.BlockSpec((1, GROUP), const),
                  pl.BlockSpec((256, 256), const)],
        out_specs=[pl.BlockSpec((mem_len, GROUP), row)] * 2,
        out_shape=[jax.ShapeDtypeStruct((n, GROUP), F32)] * 2,
        compiler_params=pltpu.CompilerParams(dimension_semantics=("parallel",), vmem_limit_bytes=VMEM_LIMIT),
        name="mem_kv",
    )(memf, g, w_bf, gk, seg)


def _out_proj_kernel(x_ref, oa_ref, oh_ref, qm_ref, z_ref, km_ref, vm_ref, gq_ref, w_ref, o_ref):
    def gated(o, c0, c1):
        return (o * _silu(z_ref[:, c0:c1].astype(F32))).astype(BF16)

    out = x_ref[...]
    out = out + _nn(gated(oa_ref[...].astype(F32), 0, GROUP), w_ref[0:GROUP, :])
    out = out + _nn(gated(oh_ref[...].astype(F32), GROUP, 2 * GROUP), w_ref[GROUP:2 * GROUP, :])
    gq = gq_ref[...]
    for h in range(MEM_HEADS):
        cols = slice(h * MEM_D, (h + 1) * MEM_D)
        q = qm_ref[:, cols].astype(F32)
        ms = jnp.mean(q * q, axis=-1, keepdims=True)
        qn = q * lax.rsqrt(ms + NORM_EPS) * gq * (MEM_D ** -0.5)
        s = _nt(qn.astype(BF16), km_ref[:, cols].astype(BF16))
        p = jnp.exp(s - jnp.max(s, axis=1, keepdims=True))
        om = _nn(p.astype(BF16), vm_ref[:, cols].astype(BF16)) / jnp.sum(p, axis=1, keepdims=True)
        c0 = 2 * GROUP + h * MEM_D
        out = out + _nn(gated(om, c0, c0 + MEM_D), w_ref[c0:c0 + MEM_D, :])
    o_ref[...] = out


def _out_proj(xf, oa, oh, qm, z, km, vm, gq, w_bf, seq, mem_len):
    n = xf.shape[0]
    tm = ROW_TILE
    steps = seq // tm
    const = lambda i: (0, 0)
    row = lambda i: (i, 0)
    per_batch = lambda i: (i // steps, 0)
    return pl.pallas_call(
        _out_proj_kernel,
        grid=(n // tm,),
        in_specs=[pl.BlockSpec((tm, D_MODEL), row),
                  pl.BlockSpec((tm, GROUP), row),
                  pl.BlockSpec((tm, GROUP), row),
                  pl.BlockSpec((tm, GROUP), row),
                  pl.BlockSpec((tm, D_MIX), row),
                  pl.BlockSpec((mem_len, GROUP), per_batch),
                  pl.BlockSpec((mem_len, GROUP), per_batch),
                  pl.BlockSpec((1, MEM_D), const),
                  pl.BlockSpec((D_MIX, D_MODEL), const)],
        out_specs=pl.BlockSpec((tm, D_MODEL), row),
        out_shape=jax.ShapeDtypeStruct((n, D_MODEL), F32),
        compiler_params=pltpu.CompilerParams(dimension_semantics=("parallel",), vmem_limit_bytes=VMEM_LIMIT),
        name="out_proj",
    )(xf, oa, oh, qm, z, km, vm, gq, w_bf)


def kernel(x, mem, positions, norm_g, w_in, w_out, moba_q_norm, moba_k_norm, hgrn_lb_logits, hgrn_o_norm,
           mem_norm_g, w_mem_kv, mem_q_norm, mem_k_norm):
    batch, seq, d_model = x.shape
    mem_len = mem.shape[1]
    depth = w_in.shape[0]
    assert d_model == D_MODEL and seq % ROW_TILE == 0 and mem_len % 8 == 0
    n = batch * seq
    xf = x.reshape(n, d_model)
    memf = mem.reshape(batch * mem_len, d_model)
    cosf, sinf = _rope_tables(positions.reshape(n, 1))
    for l in range(depth):
        gq = jnp.tile(moba_q_norm[l], GROUP // MOBA_D)[None, :]
        gk = jnp.tile(moba_k_norm[l], GROUP // MOBA_D)[None, :]
        qa, ka, va, qh, fh, ih, qm, z = _in_proj(xf, norm_g[l][None, :], w_in[l].astype(BF16), cosf, sinf, gq, gk)
        oa = _moba(qa, ka, va, batch, seq)
        oh = _hgrn(qh, fh, ih, hgrn_lb_logits, jnp.tile(hgrn_o_norm[l], HGRN_HEADS)[None, :], l, batch, seq)
        km, vm = _mem_kv(memf, mem_norm_g[l][None, :], w_mem_kv[l].astype(BF16),
                         jnp.tile(mem_k_norm[l], MEM_HEADS)[None, :], mem_len)
        xf = _out_proj(xf, oa, oh, qm, z, km, vm, mem_q_norm[l][None, :], w_out[l].astype(BF16), seq, mem_len)
    return xf.reshape(batch, seq, d_model)
```

```python
import functools

import numpy as np
import jax
import jax.numpy as jnp
from jax import lax
from jax.experimental import pallas as pl
from jax.experimental.pallas import tpu as pltpu

F32 = jnp.float32
BF16 = jnp.bfloat16

D_MODEL = 1024
GROUP = D_MODEL // 2
D_MIX = 3 * GROUP
IN_COLS = 7 * GROUP + D_MIX
MOBA_D = 64
MOBA_BLOCK = 256
MOBA_TOPK = 3
HGRN_D = 128
HGRN_HEADS = GROUP // HGRN_D
HGRN_CHUNK = 64
HGRN_LEVELS = 6
HGRN_MM_LEVELS = 3
LOG2E = 1.4426950408889634
MEM_D = 128
MEM_HEADS = GROUP // MEM_D
ROPE_THETA = 500000.0
ROPE_DIM = MOBA_D // 4
ROPE_HALF = ROPE_DIM // 2
NORM_EPS = 1e-6
LANES = 128
ROW_TILE = MOBA_BLOCK
MASKED = -1e30
VMEM_LIMIT = 48 * 1024 * 1024


def _nt(a, b):
    return lax.dot_general(a, b, (((1,), (1,)), ((), ())), preferred_element_type=F32)


def _tn(a, b):
    return lax.dot_general(a, b, (((0,), (0,)), ((), ())), preferred_element_type=F32)


def _nn(a, b):
    return jnp.dot(a, b, preferred_element_type=F32)


def _split2(x):
    hi = x.astype(BF16)
    lo = (x - hi.astype(F32)).astype(BF16)
    return hi, lo


def _split3(x):
    hi = x.astype(BF16)
    r = x - hi.astype(F32)
    mid = r.astype(BF16)
    lo = (r - mid.astype(F32)).astype(BF16)
    return hi, mid, lo


def _sigmoid(x):
    return 1.0 / (1.0 + jnp.exp(-x))


def _silu(x):
    return x * _sigmoid(x)


def _rope_table_kernel(pos_ref, invf_ref, sgn_ref, cos_ref, sin_ref):
    ang = pos_ref[...].astype(F32) * invf_ref[...]
    sgn = sgn_ref[...]
    on = sgn != 0.0
    cos_ref[...] = jnp.where(on, jnp.cos(ang), 1.0)
    sin_ref[...] = sgn * jnp.sin(ang)


def _rope_tables(pos):
    n = pos.shape[0]
    tm = min(n, 2048)
    lane = np.arange(LANES) % MOBA_D
    invf = np.where(lane < ROPE_DIM, ROPE_THETA ** (-(lane % ROPE_HALF).astype(np.float64) / ROPE_HALF), 0.0)
    sgn = np.where(lane < ROPE_HALF, -1.0, np.where(lane < ROPE_DIM, 1.0, 0.0))
    invf = jnp.asarray(invf[None, :], F32)
    sgn = jnp.asarray(sgn[None, :], F32)
    return pl.pallas_call(
        _rope_table_kernel,
        grid=(n // tm,),
        in_specs=[pl.BlockSpec((tm, 1), lambda i: (i, 0)),
                  pl.BlockSpec((1, LANES), lambda i: (0, 0)),
                  pl.BlockSpec((1, LANES), lambda i: (0, 0))],
        out_specs=[pl.BlockSpec((tm, LANES), lambda i: (i, 0))] * 2,
        out_shape=[jax.ShapeDtypeStruct((n, LANES), F32)] * 2,
        name="rope_tables",
    )(pos, invf, sgn)


def _in_proj_kernel(x_ref, g_ref, w_ref, cos_ref, sin_ref, gq_ref, gk_ref, seg_ref,
                    qa_ref, ka_ref, va_ref, qh_ref, fh_ref, ih_ref, qm_ref, z_ref):
    x = x_ref[...]
    ms = jnp.mean(x * x, axis=-1, keepdims=True)
    hb = (x * lax.rsqrt(ms + NORM_EPS) * g_ref[...]).astype(BF16)

    def proj(c0, width):
        return _nn(hb, w_ref[:, c0:c0 + width])

    cosf = cos_ref[...]
    sinf = sin_ref[...]
    lane = lax.broadcasted_iota(jnp.int32, (1, LANES), 1) % MOBA_D
    lo_lane = lane < ROPE_HALF
    seg = seg_ref[...]

    def head_norm_rope(p, gain, out_ref):
        for c in range(GROUP // 256):
            pc = p[:, c * 256:(c + 1) * 256]
            ss = _nn((pc * pc).astype(BF16), seg) * (1.0 / MOBA_D)
            pn = pc * lax.rsqrt(ss + NORM_EPS) * gain[:, c * 256:(c + 1) * 256]
            for v in range(2):
                xv = pn[:, v * LANES:(v + 1) * LANES]
                rot = jnp.where(lo_lane, pltpu.roll(xv, LANES - ROPE_HALF, 1), pltpu.roll(xv, ROPE_HALF, 1))
                col = c * 256 + v * LANES
                out_ref[:, col:col + LANES] = (xv * cosf + rot * sinf).astype(out_ref.dtype)

    head_norm_rope(proj(0, GROUP), gq_ref[...], qa_ref)
    head_norm_rope(proj(GROUP, GROUP), gk_ref[...], ka_ref)
    va_ref[...] = proj(2 * GROUP, GROUP).astype(BF16)
    qh_ref[...] = proj(3 * GROUP, GROUP).astype(BF16)
    fh_ref[...] = proj(4 * GROUP, GROUP)
    ih_ref[...] = proj(5 * GROUP, GROUP).astype(BF16)
    qm_ref[...] = proj(6 * GROUP, GROUP).astype(BF16)
    for c in range(3):
        z_ref[:, c * GROUP:(c + 1) * GROUP] = proj((7 + c) * GROUP, GROUP).astype(BF16)


def _in_proj(xf, g, w_bf, cosf, sinf, gq, gk):
    n = xf.shape[0]
    tm = ROW_TILE
    seg = np.arange(256)[:, None] // MOBA_D == np.arange(256)[None, :] // MOBA_D
    seg = jnp.asarray(seg, BF16)
    const = lambda i: (0, 0)
    row = lambda i: (i, 0)
    outs = ((GROUP, F32), (GROUP, BF16), (GROUP, BF16), (GROUP, BF16), (GROUP, F32), (GROUP, BF16),
            (GROUP, BF16), (D_MIX, BF16))
    return pl.pallas_call(
        _in_proj_kernel,
        grid=(n // tm,),
        in_specs=[pl.BlockSpec((tm, D_MODEL), row),
                  pl.BlockSpec((1, D_MODEL), const),
                  pl.BlockSpec((D_MODEL, IN_COLS), const, pipeline_mode=pl.Buffered(1)),
                  pl.BlockSpec((tm, LANES), row),
                  pl.BlockSpec((tm, LANES), row),
                  pl.BlockSpec((1, GROUP), const),
                  pl.BlockSpec((1, GROUP), const),
                  pl.BlockSpec((256, 256), const)],
        out_specs=[pl.BlockSpec((tm, w), row) for w, _ in outs],
        out_shape=[jax.ShapeDtypeStruct((n, w), dt) for w, dt in outs],
        compiler_params=pltpu.CompilerParams(dimension_semantics=("parallel",), vmem_limit_bytes=VMEM_LIMIT),
        name="in_proj",
    )(xf, g, w_bf, cosf, sinf, gq, gk, seg)


def _moba_kernel(qlo_ref, qhi_ref, k_ref, v_ref, o_ref, kmean_ref, kaug_ref, vaug_ref, s_ref, *, nb):
    i = pl.program_id(2)
    blk = MOBA_BLOCK
    lane = lax.broadcasted_iota(jnp.int32, (1, LANES), 1)
    heads = [(lane >= h * MOBA_D) & (lane < (h + 1) * MOBA_D) for h in range(2)]
    spare = [(1 - h) * MOBA_D for h in range(2)]

    @pl.when(i == 0)
    def _():
        for j in range(nb):
            kj = k_ref[j * blk:(j + 1) * blk, :]
            vj = v_ref[j * blk:(j + 1) * blk, :]
            kmean_ref[j:j + 1, :] = jnp.mean(kj.astype(F32), axis=0, keepdims=True)
            for h in range(2):
                onehot = jnp.where(lane == spare[h] + j, 1.0, 0.0).astype(BF16)
                kaug_ref[h, j] = jnp.where(heads[h], kj, onehot)
                vaug_ref[h, j] = jnp.where(heads[h], vj, jnp.ones_like(vj))

    km = kmean_ref[...]
    blk_row = lax.broadcasted_iota(jnp.int32, (nb, 1), 0)

    def augmented_query(q, own, h):
        past = blk_row < own
        qh = jnp.where(heads[h], q, 0.0)
        q_hi, q_lo = _split2(qh)
        k_hi, k_lo = _split2(jnp.where(heads[h], km, 0.0))
        gate_t = _nt(k_hi, q_hi) + _nt(k_hi, q_lo) + _nt(k_lo, q_hi)
        gate_t = jnp.where(past, gate_t, -jnp.inf)
        cnt = jnp.zeros((nb, blk), F32)
        for n in range(nb):
            gn = gate_t[n:n + 1, :]
            before = (gn > gate_t) | ((gn == gate_t) & (n < blk_row))
            cnt = cnt + jnp.where(before, 1.0, 0.0)
        sel = (past & (cnt < float(MOBA_TOPK))) | (blk_row == own)
        bias_t = jnp.where(sel, 0.0, MASKED)
        parts = []
        if spare[h]:
            parts.append(jnp.zeros((spare[h], blk), F32))
        parts.append(bias_t)
        parts.append(jnp.zeros((LANES - spare[h] - nb, blk), F32))
        bias = jnp.concatenate(parts, axis=0).T
        return jnp.where(heads[h], qh * (MOBA_D ** -0.5), bias).astype(BF16)

    q_lo, q_hi = qlo_ref[...], qhi_ref[...]
    own_lo, own_hi = i, nb - 1 - i
    q_aug = [[augmented_query(q_lo, own_lo, h) for h in range(2)],
             [augmented_query(q_hi, own_hi, h) for h in range(2)]]

    def attend(c_lo):
        qpos = lax.broadcasted_iota(jnp.int32, (blk, blk), 0)
        kpos = lax.broadcasted_iota(jnp.int32, (blk, blk), 1)
        causal = kpos <= qpos
        for slot, c in enumerate((c_lo, nb - 1 - c_lo)):
            outs = []
            for h in range(2):
                mx = None
                for j in range(c + 1):
                    s = _nt(q_aug[slot][h], kaug_ref[h, j])
                    if j == c:
                        s = jnp.where(causal, s, MASKED)
                    s_ref[slot, h, j] = s
                    mx = s if mx is None else jnp.maximum(mx, s)
                m = jnp.max(mx, axis=1, keepdims=True)
                acc = None
                for j in range(c + 1):
                    p = jnp.exp(s_ref[slot, h, j] - m).astype(BF16)
                    pv = _nn(p, vaug_ref[h, j])
                    acc = pv if acc is None else acc + pv
                outs.append(acc / pltpu.roll(acc, MOBA_D, 1))
            o_ref[slot] = jnp.where(heads[0], outs[0], outs[1]).astype(BF16)

    for c in range(nb // 2):
        pl.when(i == c)(functools.partial(attend, c))


def _moba(qa, ka, va, batch, seq):
    nb = seq // MOBA_BLOCK
    assert nb % 2 == 0
    pairs = GROUP // LANES
    return pl.pallas_call(
        functools.partial(_moba_kernel, nb=nb),
        grid=(batch, pairs, nb // 2),
        in_specs=[pl.BlockSpec((MOBA_BLOCK, LANES), lambda b, p, t: (b * nb + t, p)),
                  pl.BlockSpec((MOBA_BLOCK, LANES), lambda b, p, t: (b * nb + nb - 1 - t, p)),
                  pl.BlockSpec((seq, LANES), lambda b, p, t: (b, p)),
                  pl.BlockSpec((seq, LANES), lambda b, p, t: (b, p))],
        out_specs=pl.BlockSpec((None, 2, MOBA_BLOCK, LANES), lambda b, p, t: (b, 0, t, p)),
        out_shape=jax.ShapeDtypeStruct((batch, 2, seq // 2, GROUP), BF16),
        scratch_shapes=[pltpu.VMEM((nb, LANES), F32),
                        pltpu.VMEM((2, nb, MOBA_BLOCK, LANES), BF16),
                        pltpu.VMEM((2, nb, MOBA_BLOCK, LANES), BF16),
                        pltpu.VMEM((2, 2, nb, MOBA_BLOCK, MOBA_BLOCK), F32)],
        compiler_params=pltpu.CompilerParams(dimension_semantics=("parallel", "parallel", "arbitrary"),
                                             vmem_limit_bytes=VMEM_LIMIT),
        name="moba",
    )(qa, qa, ka, va)


def _hgrn_consts():
    c = HGRN_CHUNK
    t = np.arange(c)
    w_rows, masks = [], []
    for lvl in range(HGRN_LEVELS):
        m = 1 << lvl
        blk = t // m
        odd = (blk % 2) == 1
        start = blk * m
        end = start + m - 1
        u = t[None, :]
        w_odd = (u >= start[:, None]) & (u <= t[:, None])
        w_even = (u > t[:, None]) & (u <= end[:, None])
        w_rows.append(np.where(odd[:, None], w_odd, w_even))
        masks.append(odd[:, None] & (blk[None, :] == blk[:, None] - 1))
    masks.append(t[None, :] == t[:, None])
    w_mm = w_rows[:HGRN_MM_LEVELS] + [t[None, :] <= t[:, None]]
    w_all = np.concatenate(w_mm, axis=0).astype(np.float32)
    w_all = np.concatenate([w_all, w_all], axis=1)
    mask_cat = np.concatenate([masks[-1]] + masks[:-1] + [np.zeros((c, c), bool)], axis=1).astype(np.float32)
    return jnp.asarray(w_all, BF16), jnp.asarray(mask_cat, F32)


def _hgrn_kernel(qh_ref, fh_ref, ih_ref, logit_ref, gain_ref, w_ref, mask_ref, o_ref, state_ref, *, layer):
    c = HGRN_CHUNK

    @pl.when(pl.program_id(1) == 0)
    def _():
        state_ref[...] = jnp.zeros_like(state_ref)

    if layer > 0:
        lg = logit_ref[...]
        e = jnp.exp(lg - jnp.max(lg, axis=0, keepdims=True))
        sm = e / jnp.sum(e, axis=0, keepdims=True)
        lb = jnp.sum(sm[1:layer + 1, :], axis=0, keepdims=True)
        log_lb = jnp.log(lb)
        log_1m_lb = jnp.log1p(-lb)
    w_all = w_ref[...]
    gain = gain_ref[...]

    for ci in range(ROW_TILE // c):
        rows = slice(ci * c, (ci + 1) * c)
        fl = fh_ref[rows, :]
        vi = ih_ref[rows, :]
        qf = _silu(qh_ref[rows, :].astype(F32))
        e = jnp.exp(-jnp.abs(fl))
        r = 1.0 / (1.0 + e)
        log_sig = jnp.minimum(fl, 0.0) - jnp.log(1.0 + e)
        sig_neg = jnp.where(fl >= 0.0, e * r, r)
        if layer > 0:
            b = log_1m_lb + log_sig
            lf = jnp.maximum(log_lb, b) + jnp.log(1.0 + jnp.exp(-jnp.abs(log_lb - b)))
            kf = (1.0 - lb) * sig_neg
        else:
            lf = log_sig
            kf = sig_neg

        f1, f2 = _split2(lf * LOG2E)
        x_mm = _nn(w_all, jnp.concatenate([f1, f2], axis=0))
        a2 = x_mm[HGRN_MM_LEVELS * c:(HGRN_MM_LEVELS + 1) * c, :]
        e_lvl = [jnp.exp2(x_mm[lvl * c:(lvl + 1) * c, :]) for lvl in range(HGRN_MM_LEVELS)]
        for lvl in range(HGRN_MM_LEVELS, HGRN_LEVELS):
            m = 1 << lvl
            parts = []
            for g in range(c // (2 * m)):
                ref = a2[2 * m * g + m - 1:2 * m * g + m, :]
                parts.append(-jnp.abs(a2[2 * m * g:2 * m * (g + 1), :] - ref))
            e_lvl.append(jnp.exp2(jnp.concatenate(parts, axis=0)))
        e_cum = jnp.exp2(a2)
        e_rest = jnp.exp2(a2[c - 1:c, :] - a2)
        e_end = e_cum[c - 1:c, :]

        for h in range(HGRN_HEADS):
            cols = slice(h * HGRN_D, (h + 1) * HGRN_D)
            qfh, kfh = qf[:, cols], kf[:, cols]
            qb = [qfh.astype(BF16)] + [(qfh * e[:, cols]).astype(BF16) for e in e_lvl]
            kb = [kfh.astype(BF16)] + [(kfh * e[:, cols]).astype(BF16) for e in e_lvl]
            zero = jnp.zeros((c, HGRN_D), BF16)
            s_parts = []
            for a in range(0, HGRN_LEVELS, 2):
                q_cat = jnp.concatenate([qb[a], qb[a + 1]], axis=1)
                k_blk = jnp.concatenate([jnp.concatenate([kb[a], zero], axis=1),
                                         jnp.concatenate([zero, kb[a + 1]], axis=1)], axis=0)
                s_parts.append(_nt(q_cat, k_blk))
            s_parts.append(_nt(qb[HGRN_LEVELS], jnp.concatenate([kb[HGRN_LEVELS], zero], axis=0)))
            p = (jnp.concatenate(s_parts, axis=1) * mask_ref[...]).astype(BF16)
            vb = vi[:, cols]
            st = state_ref[:, cols]
            o = _nn(p, jnp.concatenate([vb] * (HGRN_LEVELS + 2), axis=0))
            o = o + _nt((qfh * e_cum[:, cols]).astype(BF16), st.astype(BF16))
            state_ref[:, cols] = st * e_end[:, cols] + _tn(vb, (kfh * e_rest[:, cols]).astype(BF16))
            ms = jnp.mean(o * o, axis=-1, keepdims=True)
            o_ref[rows, cols] = (o * lax.rsqrt(ms + NORM_EPS) * gain[:, cols]).astype(BF16)


def _hgrn(qh, fh, ih, logits, gain, layer, batch, seq):
    w_all, mask_all = _hgrn_consts()
    steps = seq // ROW_TILE
    depth = logits.shape[0]
    const = lambda b, s: (0, 0)
    return pl.pallas_call(
        functools.partial(_hgrn_kernel, layer=layer),
        grid=(batch, steps),
        in_specs=[pl.BlockSpec((ROW_TILE, GROUP), lambda b, s: (b * steps + s, 0))] * 3 + [
                  pl.BlockSpec((depth, GROUP), const),
                  pl.BlockSpec((1, GROUP), const),
                  pl.BlockSpec(w_all.shape, const),
                  pl.BlockSpec(mask_all.shape, const)],
        out_specs=pl.BlockSpec((ROW_TILE, GROUP), lambda b, s: (b * steps + s, 0)),
        out_shape=jax.ShapeDtypeStruct((fh.shape[0], GROUP), BF16),
        scratch_shapes=[pltpu.VMEM((HGRN_D, GROUP), F32)],
        compiler_params=pltpu.CompilerParams(dimension_semantics=("parallel", "arbitrary"),
                                             vmem_limit_bytes=VMEM_LIMIT),
        name="hgrn",
    )(qh, fh, ih, logits, gain, w_all, mask_all)


def _mem_kv_kernel(m_ref, g_ref, w_ref, gk_ref, seg_ref, k_ref, v_ref):
    x = m_ref[...]
    ms = jnp.mean(x * x, axis=-1, keepdims=True)
    hb = (x * lax.rsqrt(ms + NORM_EPS) * g_ref[...]).astype(BF16)
    gk = gk_ref[...]
    seg = seg_ref[...]
    for c in range(GROUP // 256):
        kc = _nn(hb, w_ref[:, c * 256:(c + 1) * 256])
        ss = _nn((kc * kc).astype(BF16), seg) * (1.0 / MEM_D)
        kn = kc * lax.rsqrt(ss + NORM_EPS) * gk[:, c * 256:(c + 1) * 256]
        k_ref[:, c * 256:(c + 1) * 256] = kn.astype(BF16)
    v_ref[...] = _nn(hb, w_ref[:, GROUP:2 * GROUP]).astype(BF16)


def _mem_kv(memf, g, w_bf, gk, mem_len):
    n = memf.shape[0]
    seg = np.arange(256)[:, None] // MEM_D == np.arange(256)[None, :] // MEM_D
    seg = jnp.asarray(seg, BF16)
    const = lambda i: (0, 0)
    row = lambda i: (i, 0)
    return pl.pallas_call(
        _mem_kv_kernel,
        grid=(n // mem_len,),
        in_specs=[pl.BlockSpec((mem_len, D_MODEL), row),
                  pl.BlockSpec((1, D_MODEL), const),
                  pl.BlockSpec((D_MODEL, 2 * GROUP), const),
                  pl.BlockSpec((1, GROUP), const),
                  pl.BlockSpec((256, 256), const)],
        out_specs=[pl.BlockSpec((mem_len, GROUP), row)] * 2,
        out_shape=[jax.ShapeDtypeStruct((n, GROUP), BF16)] * 2,
        compiler_params=pltpu.CompilerParams(dimension_semantics=("parallel",), vmem_limit_bytes=VMEM_LIMIT),
        name="mem_kv",
    )(memf, g, w_bf, gk, seg)


def _out_proj_kernel(x_ref, oa_ref, oh_ref, qm_ref, z_ref, km_ref, vm_ref, gq_ref, w_ref, o_ref, y_ref):
    def gate(o, c0, c1):
        y_ref[:, c0:c1] = (o * _silu(z_ref[:, c0:c1].astype(F32))).astype(BF16)

    gate(oa_ref[...].astype(F32), 0, GROUP)
    gate(oh_ref[...].astype(F32), GROUP, 2 * GROUP)
    gq = gq_ref[...]
    for h in range(MEM_HEADS):
        cols = slice(h * MEM_D, (h + 1) * MEM_D)
        q = qm_ref[:, cols].astype(F32)
        ms = jnp.mean(q * q, axis=-1, keepdims=True)
        qn = q * lax.rsqrt(ms + NORM_EPS) * gq * (MEM_D ** -0.5)
        s = _nt(qn.astype(BF16), km_ref[:, cols])
        p = jnp.exp(s - jnp.max(s, axis=1, keepdims=True))
        om = _nn(p.astype(BF16), vm_ref[:, cols]) / jnp.sum(p, axis=1, keepdims=True)
        c0 = 2 * GROUP + h * MEM_D
        gate(om, c0, c0 + MEM_D)
    o_ref[...] = x_ref[...] + _nn(y_ref[...], w_ref[...])


def _out_proj(xf, oa, oh, qm, z, km, vm, gq, w_bf, seq, mem_len):
    n = xf.shape[0]
    tm = ROW_TILE
    steps = seq // tm
    const = lambda i: (0, 0)
    row = lambda i: (i, 0)
    per_batch = lambda i: (i // steps, 0)

    def moba_block(i):
        j = i % steps
        second = j >= steps // 2
        return (i // steps, jnp.where(second, 1, 0), jnp.where(second, steps - 1 - j, j), 0)

    return pl.pallas_call(
        _out_proj_kernel,
        grid=(n // tm,),
        in_specs=[pl.BlockSpec((tm, D_MODEL), row),
                  pl.BlockSpec((None, None, tm, GROUP), moba_block),
                  pl.BlockSpec((tm, GROUP), row),
                  pl.BlockSpec((tm, GROUP), row),
                  pl.BlockSpec((tm, D_MIX), row),
                  pl.BlockSpec((mem_len, GROUP), per_batch),
                  pl.BlockSpec((mem_len, GROUP), per_batch),
                  pl.BlockSpec((1, MEM_D), const),
                  pl.BlockSpec((D_MIX, D_MODEL), const)],
        out_specs=pl.BlockSpec((tm, D_MODEL), row),
        out_shape=jax.ShapeDtypeStruct((n, D_MODEL), F32),
        scratch_shapes=[pltpu.VMEM((tm, D_MIX), BF16)],
        compiler_params=pltpu.CompilerParams(dimension_semantics=("parallel",), vmem_limit_bytes=VMEM_LIMIT),
        name="out_proj",
    )(xf, oa, oh, qm, z, km, vm, gq, w_bf)


def kernel(x, mem, positions, norm_g, w_in, w_out, moba_q_norm, moba_k_norm, hgrn_lb_logits, hgrn_o_norm,
           mem_norm_g, w_mem_kv, mem_q_norm, mem_k_norm):
    batch, seq, d_model = x.shape
    mem_len = mem.shape[1]
    depth = w_in.shape[0]
    assert d_model == D_MODEL and seq % ROW_TILE == 0 and mem_len % 8 == 0
    n = batch * seq
    xf = x.reshape(n, d_model)
    memf = mem.reshape(batch * mem_len, d_model)
    cosf, sinf = _rope_tables(positions.reshape(n, 1))
    for l in range(depth):
        gq = jnp.tile(moba_q_norm[l], GROUP // MOBA_D)[None, :]
        gk = jnp.tile(moba_k_norm[l], GROUP // MOBA_D)[None, :]
        qa, ka, va, qh, fh, ih, qm, z = _in_proj(xf, norm_g[l][None, :], w_in[l].astype(BF16), cosf, sinf, gq, gk)
        oa = _moba(qa, ka, va, batch, seq)
        oh = _hgrn(qh, fh, ih, hgrn_lb_logits, jnp.tile(hgrn_o_norm[l], HGRN_HEADS)[None, :], l, batch, seq)
        km, vm = _mem_kv(memf, mem_norm_g[l][None, :], w_mem_kv[l].astype(BF16),
                         jnp.tile(mem_k_norm[l], MEM_HEADS)[None, :], mem_len)
        xf = _out_proj(xf, oa, oh, qm, z, km, vm, mem_q_norm[l][None, :], w_out[l].astype(BF16), seq, mem_len)
    return xf.reshape(batch, seq, d_model)
```

```python
import functools

import numpy as np
import jax
import jax.numpy as jnp
from jax import lax
from jax.experimental import pallas as pl
from jax.experimental.pallas import tpu as pltpu

F32 = jnp.float32
BF16 = jnp.bfloat16

D_MODEL = 1024
GROUP = D_MODEL // 2
D_MIX = 3 * GROUP
IN_COLS = 7 * GROUP + D_MIX
MOBA_D = 64
MOBA_BLOCK = 256
MOBA_TOPK = 3
MOBA_VT_ROWS = MOBA_D + 16
HGRN_D = 128
HGRN_HEADS = GROUP // HGRN_D
HGRN_CHUNK = 64
HGRN_LEVELS = 6
HGRN_MM_LEVELS = 3
LOG2E = 1.4426950408889634
MEM_D = 128
MEM_HEADS = GROUP // MEM_D
ROPE_THETA = 500000.0
ROPE_DIM = MOBA_D // 4
ROPE_HALF = ROPE_DIM // 2
NORM_EPS = 1e-6
LANES = 128
ROW_TILE = MOBA_BLOCK
MASKED = -1e30
VMEM_LIMIT = 48 * 1024 * 1024


def _nt(a, b):
    return lax.dot_general(a, b, (((1,), (1,)), ((), ())), preferred_element_type=F32)


def _tn(a, b):
    return lax.dot_general(a, b, (((0,), (0,)), ((), ())), preferred_element_type=F32)


def _nn(a, b):
    return jnp.dot(a, b, preferred_element_type=F32)


def _split2(x):
    hi = x.astype(BF16)
    lo = (x - hi.astype(F32)).astype(BF16)
    return hi, lo


def _split3(x):
    hi = x.astype(BF16)
    r = x - hi.astype(F32)
    mid = r.astype(BF16)
    lo = (r - mid.astype(F32)).astype(BF16)
    return hi, mid, lo


def _sigmoid(x):
    return 1.0 / (1.0 + jnp.exp(-x))


def _silu(x):
    return x * _sigmoid(x)


def _rope_table_kernel(pos_ref, invf_ref, sgn_ref, cos_ref, sin_ref):
    ang = pos_ref[...].astype(F32) * invf_ref[...]
    sgn = sgn_ref[...]
    on = sgn != 0.0
    cos_ref[...] = jnp.where(on, jnp.cos(ang), 1.0)
    sin_ref[...] = sgn * jnp.sin(ang)


def _rope_tables(pos):
    n = pos.shape[0]
    tm = min(n, 2048)
    lane = np.arange(LANES) % MOBA_D
    invf = np.where(lane < ROPE_DIM, ROPE_THETA ** (-(lane % ROPE_HALF).astype(np.float64) / ROPE_HALF), 0.0)
    sgn = np.where(lane < ROPE_HALF, -1.0, np.where(lane < ROPE_DIM, 1.0, 0.0))
    invf = jnp.asarray(invf[None, :], F32)
    sgn = jnp.asarray(sgn[None, :], F32)
    return pl.pallas_call(
        _rope_table_kernel,
        grid=(n // tm,),
        in_specs=[pl.BlockSpec((tm, 1), lambda i: (i, 0)),
                  pl.BlockSpec((1, LANES), lambda i: (0, 0)),
                  pl.BlockSpec((1, LANES), lambda i: (0, 0))],
        out_specs=[pl.BlockSpec((tm, LANES), lambda i: (i, 0))] * 2,
        out_shape=[jax.ShapeDtypeStruct((n, LANES), F32)] * 2,
        name="rope_tables",
    )(pos, invf, sgn)


def _in_proj_kernel(x_ref, g_ref, w_ref, cos_ref, sin_ref, gq_ref, gk_ref, seg_ref,
                    qa_ref, ka_ref, va_ref, qh_ref, fh_ref, ih_ref, qm_ref, z_ref):
    x = x_ref[...]
    ms = jnp.mean(x * x, axis=-1, keepdims=True)
    hb = (x * lax.rsqrt(ms + NORM_EPS) * g_ref[...]).astype(BF16)

    def proj(c0, width):
        return _nn(hb, w_ref[:, c0:c0 + width])

    cosf = cos_ref[...]
    sinf = sin_ref[...]
    lane = lax.broadcasted_iota(jnp.int32, (1, LANES), 1) % MOBA_D
    lo_lane = lane < ROPE_HALF
    seg = seg_ref[...]

    def head_norm_rope(p, gain, out_ref):
        for c in range(GROUP // 256):
            pc = p[:, c * 256:(c + 1) * 256]
            ss = _nn((pc * pc).astype(BF16), seg) * (1.0 / MOBA_D)
            pn = pc * lax.rsqrt(ss + NORM_EPS) * gain[:, c * 256:(c + 1) * 256]
            for v in range(2):
                xv = pn[:, v * LANES:(v + 1) * LANES]
                rot = jnp.where(lo_lane, pltpu.roll(xv, LANES - ROPE_HALF, 1), pltpu.roll(xv, ROPE_HALF, 1))
                col = c * 256 + v * LANES
                out_ref[:, col:col + LANES] = (xv * cosf + rot * sinf).astype(out_ref.dtype)

    head_norm_rope(proj(0, GROUP), gq_ref[...], qa_ref)
    head_norm_rope(proj(GROUP, GROUP), gk_ref[...], ka_ref)
    va_ref[...] = proj(2 * GROUP, GROUP).astype(BF16)
    qh_ref[...] = proj(3 * GROUP, GROUP).astype(BF16)
    fh_ref[...] = proj(4 * GROUP, GROUP)
    ih_ref[...] = proj(5 * GROUP, GROUP).astype(BF16)
    qm_ref[...] = proj(6 * GROUP, GROUP).astype(BF16)
    for c in range(3):
        z_ref[:, c * GROUP:(c + 1) * GROUP] = proj((7 + c) * GROUP, GROUP).astype(BF16)


def _in_proj(xf, g, w_bf, cosf, sinf, gq, gk):
    n = xf.shape[0]
    tm = ROW_TILE
    seg = np.arange(256)[:, None] // MOBA_D == np.arange(256)[None, :] // MOBA_D
    seg = jnp.asarray(seg, BF16)
    const = lambda i: (0, 0)
    row = lambda i: (i, 0)
    outs = ((GROUP, F32), (GROUP, BF16), (GROUP, BF16), (GROUP, BF16), (GROUP, F32), (GROUP, BF16),
            (GROUP, BF16), (D_MIX, BF16))
    return pl.pallas_call(
        _in_proj_kernel,
        grid=(n // tm,),
        in_specs=[pl.BlockSpec((tm, D_MODEL), row),
                  pl.BlockSpec((1, D_MODEL), const),
                  pl.BlockSpec((D_MODEL, IN_COLS), const, pipeline_mode=pl.Buffered(1)),
                  pl.BlockSpec((tm, LANES), row),
                  pl.BlockSpec((tm, LANES), row),
                  pl.BlockSpec((1, GROUP), const),
                  pl.BlockSpec((1, GROUP), const),
                  pl.BlockSpec((256, 256), const)],
        out_specs=[pl.BlockSpec((tm, w), row) for w, _ in outs],
        out_shape=[jax.ShapeDtypeStruct((n, w), dt) for w, dt in outs],
        compiler_params=pltpu.CompilerParams(dimension_semantics=("parallel",), vmem_limit_bytes=VMEM_LIMIT),
        name="in_proj",
    )(xf, g, w_bf, cosf, sinf, gq, gk, seg)


def _moba_kernel(qlo_ref, qhi_ref, k_ref, v_ref, o_ref, kmean_ref, kaug_ref, vt_ref, s_ref, *, nb):
    i = pl.program_id(2)
    blk = MOBA_BLOCK
    lane = lax.broadcasted_iota(jnp.int32, (1, LANES), 1)
    heads = [(lane >= h * MOBA_D) & (lane < (h + 1) * MOBA_D) for h in range(2)]
    spare = [(1 - h) * MOBA_D for h in range(2)]

    @pl.when(i == 0)
    def _():
        ones = jnp.ones((MOBA_VT_ROWS - MOBA_D, blk), F32)
        for j in range(nb):
            rows = slice(j * blk, (j + 1) * blk)
            kj = k_ref[rows, :]
            kmean_ref[j:j + 1, :] = jnp.mean(kj.astype(F32), axis=0, keepdims=True)
            v_t = v_ref[rows, :].astype(F32).T
            for h in range(2):
                onehot = jnp.where(lane == spare[h] + j, 1.0, 0.0).astype(BF16)
                kaug_ref[h, rows, :] = jnp.where(heads[h], kj, onehot)
                vt_ref[h, :, rows] = jnp.concatenate([v_t[h * MOBA_D:(h + 1) * MOBA_D], ones], axis=0).astype(BF16)

    km = kmean_ref[...]
    blk_row = lax.broadcasted_iota(jnp.int32, (nb, 1), 0)

    def augmented_query(q, own, h):
        past = blk_row < own
        qh = jnp.where(heads[h], q, 0.0)
        q_hi, q_lo = _split2(qh)
        k_hi, k_lo = _split2(jnp.where(heads[h], km, 0.0))
        gate_t = _nt(k_hi, q_hi) + _nt(k_hi, q_lo) + _nt(k_lo, q_hi)
        gate_t = jnp.where(past, gate_t, -jnp.inf)
        cnt = jnp.zeros((nb, blk), F32)
        for n in range(nb):
            gn = gate_t[n:n + 1, :]
            before = (gn > gate_t) | ((gn == gate_t) & (n < blk_row))
            cnt = cnt + jnp.where(before, 1.0, 0.0)
        sel = (past & (cnt < float(MOBA_TOPK))) | (blk_row == own)
        bias_t = jnp.where(sel, 0.0, MASKED)
        parts = []
        if spare[h]:
            parts.append(jnp.zeros((spare[h], blk), F32))
        parts.append(bias_t)
        parts.append(jnp.zeros((LANES - spare[h] - nb, blk), F32))
        bias = jnp.concatenate(parts, axis=0).T
        return jnp.where(heads[h], qh * (MOBA_D ** -0.5 * LOG2E), bias)

    q_lo, q_hi = qlo_ref[...], qhi_ref[...]
    own_lo, own_hi = i, nb - 1 - i
    q_aug = [[augmented_query(q_lo, own_lo, h) for h in range(2)],
             [augmented_query(q_hi, own_hi, h) for h in range(2)]]

    def attend(c_lo):
        kpos = lax.broadcasted_iota(jnp.int32, (blk, blk), 0)
        qpos = lax.broadcasted_iota(jnp.int32, (blk, blk), 1)
        causal_t = kpos <= qpos
        token = None
        outs = {}
        for slot, c in ((1, nb - 1 - c_lo), (0, c_lo)):
            for h in range(2):
                qa = q_aug[slot][h]
                if token is not None:
                    qa = qa + token
                qa = qa.astype(BF16)
                mx = None
                for j in range(c + 1):
                    rows = slice(j * blk, (j + 1) * blk)
                    s = _nt(kaug_ref[h, rows, :], qa)
                    if j == c:
                        s = jnp.where(causal_t, s, MASKED)
                    s_ref[slot, h, rows, :] = s
                    mx = s if mx is None else jnp.maximum(mx, s)
                    if j == 0:
                        bits = pltpu.bitcast(s[0:1, 0:LANES], jnp.uint32)
                        zero = lax.shift_right_logical(lax.shift_right_logical(bits, jnp.uint32(16)), jnp.uint32(16))
                        token = zero.astype(F32)
                m = jnp.max(mx, axis=0, keepdims=True)
                acc = None
                for j in range(c + 1):
                    rows = slice(j * blk, (j + 1) * blk)
                    p = jnp.exp2(s_ref[slot, h, rows, :] - m).astype(BF16)
                    pv = _nn(vt_ref[h, :, rows], p)
                    acc = pv if acc is None else acc + pv
                outs[slot, h] = acc[0:MOBA_D] / acc[MOBA_D:MOBA_D + 1]
        for slot in range(2):
            o_ref[slot] = jnp.concatenate([outs[slot, 0], outs[slot, 1]], axis=0).T.astype(BF16)

    for c in range(nb // 2):
        pl.when(i == c)(functools.partial(attend, c))


def _moba(qa, ka, va, batch, seq):
    nb = seq // MOBA_BLOCK
    assert nb % 2 == 0
    pairs = GROUP // LANES
    return pl.pallas_call(
        functools.partial(_moba_kernel, nb=nb),
        grid=(batch, pairs, nb // 2),
        in_specs=[pl.BlockSpec((MOBA_BLOCK, LANES), lambda b, p, t: (b * nb + t, p)),
                  pl.BlockSpec((MOBA_BLOCK, LANES), lambda b, p, t: (b * nb + nb - 1 - t, p)),
                  pl.BlockSpec((seq, LANES), lambda b, p, t: (b, p)),
                  pl.BlockSpec((seq, LANES), lambda b, p, t: (b, p))],
        out_specs=pl.BlockSpec((None, 2, MOBA_BLOCK, LANES), lambda b, p, t: (b, 0, t, p)),
        out_shape=jax.ShapeDtypeStruct((batch, 2, seq // 2, GROUP), BF16),
        scratch_shapes=[pltpu.VMEM((nb, LANES), F32),
                        pltpu.VMEM((2, seq, LANES), BF16),
                        pltpu.VMEM((2, MOBA_VT_ROWS, seq), BF16),
                        pltpu.VMEM((2, 2, seq, MOBA_BLOCK), F32)],
        compiler_params=pltpu.CompilerParams(dimension_semantics=("parallel", "parallel", "arbitrary"),
                                             vmem_limit_bytes=VMEM_LIMIT),
        name="moba",
    )(qa, qa, ka, va)


def _hgrn_consts():
    c = HGRN_CHUNK
    t = np.arange(c)
    w_rows, masks = [], []
    for lvl in range(HGRN_LEVELS):
        m = 1 << lvl
        blk = t // m
        odd = (blk % 2) == 1
        start = blk * m
        end = start + m - 1
        u = t[None, :]
        w_odd = (u >= start[:, None]) & (u <= t[:, None])
        w_even = (u > t[:, None]) & (u <= end[:, None])
        w_rows.append(np.where(odd[:, None], w_odd, w_even))
        masks.append(odd[:, None] & (blk[None, :] == blk[:, None] - 1))
    masks.append(t[None, :] == t[:, None])
    w_mm = w_rows[:HGRN_MM_LEVELS] + [t[None, :] <= t[:, None]]
    w_all = np.concatenate(w_mm, axis=0).astype(np.float32)
    w_all = np.concatenate([w_all, w_all], axis=1)
    mask_cat = np.concatenate([masks[-1]] + masks[:-1] + [np.zeros((c, c), bool)], axis=1).astype(np.float32)
    return jnp.asarray(w_all, BF16), jnp.asarray(mask_cat, F32)


def _hgrn_kernel(qh_ref, fh_ref, ih_ref, logit_ref, gain_ref, w_ref, mask_ref, o_ref, state_ref, *, layer):
    c = HGRN_CHUNK

    @pl.when(pl.program_id(1) == 0)
    def _():
        state_ref[...] = jnp.zeros_like(state_ref)

    if layer > 0:
        lg = logit_ref[...]
        e = jnp.exp(lg - jnp.max(lg, axis=0, keepdims=True))
        sm = e / jnp.sum(e, axis=0, keepdims=True)
        lb = jnp.sum(sm[1:layer + 1, :], axis=0, keepdims=True)
        log_lb = jnp.log(lb)
        log_1m_lb = jnp.log1p(-lb)
    w_all = w_ref[...]
    gain = gain_ref[...]

    for ci in range(ROW_TILE // c):
        rows = slice(ci * c, (ci + 1) * c)
        fl = fh_ref[rows, :]
        vi = ih_ref[rows, :]
        qf = _silu(qh_ref[rows, :].astype(F32))
        e = jnp.exp(-jnp.abs(fl))
        r = 1.0 / (1.0 + e)
        log_sig = jnp.minimum(fl, 0.0) - jnp.log(1.0 + e)
        sig_neg = jnp.where(fl >= 0.0, e * r, r)
        if layer > 0:
            b = log_1m_lb + log_sig
            lf = jnp.maximum(log_lb, b) + jnp.log(1.0 + jnp.exp(-jnp.abs(log_lb - b)))
            kf = (1.0 - lb) * sig_neg
        else:
            lf = log_sig
            kf = sig_neg

        f1, f2 = _split2(lf * LOG2E)
        x_mm = _nn(w_all, jnp.concatenate([f1, f2], axis=0))
        a2 = x_mm[HGRN_MM_LEVELS * c:(HGRN_MM_LEVELS + 1) * c, :]
        e_lvl = [jnp.exp2(x_mm[lvl * c:(lvl + 1) * c, :]) for lvl in range(HGRN_MM_LEVELS)]
        for lvl in range(HGRN_MM_LEVELS, HGRN_LEVELS):
            m = 1 << lvl
            parts = []
            for g in range(c // (2 * m)):
                ref = a2[2 * m * g + m - 1:2 * m * g + m, :]
                parts.append(-jnp.abs(a2[2 * m * g:2 * m * (g + 1), :] - ref))
            e_lvl.append(jnp.exp2(jnp.concatenate(parts, axis=0)))
        e_cum = jnp.exp2(a2)
        e_rest = jnp.exp2(a2[c - 1:c, :] - a2)
        e_end = e_cum[c - 1:c, :]

        for h in range(HGRN_HEADS):
            cols = slice(h * HGRN_D, (h + 1) * HGRN_D)
            qfh, kfh = qf[:, cols], kf[:, cols]
            qb = [qfh.astype(BF16)] + [(qfh * e[:, cols]).astype(BF16) for e in e_lvl]
            kb = [kfh.astype(BF16)] + [(kfh * e[:, cols]).astype(BF16) for e in e_lvl]
            zero = jnp.zeros((c, HGRN_D), BF16)
            s_parts = []
            for a in range(0, HGRN_LEVELS, 2):
                q_cat = jnp.concatenate([qb[a], qb[a + 1]], axis=1)
                k_blk = jnp.concatenate([jnp.concatenate([kb[a], zero], axis=1),
                                         jnp.concatenate([zero, kb[a + 1]], axis=1)], axis=0)
                s_parts.append(_nt(q_cat, k_blk))
            s_parts.append(_nt(qb[HGRN_LEVELS], jnp.concatenate([kb[HGRN_LEVELS], zero], axis=0)))
            p = (jnp.concatenate(s_parts, axis=1) * mask_ref[...]).astype(BF16)
            vb = vi[:, cols]
            st = state_ref[:, cols]
            o = _nn(p, jnp.concatenate([vb] * (HGRN_LEVELS + 2), axis=0))
            o = o + _nt((qfh * e_cum[:, cols]).astype(BF16), st.astype(BF16))
            state_ref[:, cols] = st * e_end[:, cols] + _tn(vb, (kfh * e_rest[:, cols]).astype(BF16))
            ms = jnp.mean(o * o, axis=-1, keepdims=True)
            o_ref[rows, cols] = (o * lax.rsqrt(ms + NORM_EPS) * gain[:, cols]).astype(BF16)


def _hgrn(qh, fh, ih, logits, gain, layer, batch, seq):
    w_all, mask_all = _hgrn_consts()
    steps = seq // ROW_TILE
    depth = logits.shape[0]
    const = lambda b, s: (0, 0)
    return pl.pallas_call(
        functools.partial(_hgrn_kernel, layer=layer),
        grid=(batch, steps),
        in_specs=[pl.BlockSpec((ROW_TILE, GROUP), lambda b, s: (b * steps + s, 0))] * 3 + [
                  pl.BlockSpec((depth, GROUP), const),
                  pl.BlockSpec((1, GROUP), const),
                  pl.BlockSpec(w_all.shape, const),
                  pl.BlockSpec(mask_all.shape, const)],
        out_specs=pl.BlockSpec((ROW_TILE, GROUP), lambda b, s: (b * steps + s, 0)),
        out_shape=jax.ShapeDtypeStruct((fh.shape[0], GROUP), BF16),
        scratch_shapes=[pltpu.VMEM((HGRN_D, GROUP), F32)],
        compiler_params=pltpu.CompilerParams(dimension_semantics=("parallel", "arbitrary"),
                                             vmem_limit_bytes=VMEM_LIMIT),
        name="hgrn",
    )(qh, fh, ih, logits, gain, w_all, mask_all)


def _mem_kv_kernel(m_ref, g_ref, w_ref, gk_ref, seg_ref, k_ref, v_ref):
    x = m_ref[...]
    ms = jnp.mean(x * x, axis=-1, keepdims=True)
    hb = (x * lax.rsqrt(ms + NORM_EPS) * g_ref[...]).astype(BF16)
    gk = gk_ref[...]
    seg = seg_ref[...]
    for c in range(GROUP // 256):
        kc = _nn(hb, w_ref[:, c * 256:(c + 1) * 256])
        ss = _nn((kc * kc).astype(BF16), seg) * (1.0 / MEM_D)
        kn = kc * lax.rsqrt(ss + NORM_EPS) * gk[:, c * 256:(c + 1) * 256]
        k_ref[:, c * 256:(c + 1) * 256] = kn.astype(BF16)
    v_ref[...] = _nn(hb, w_ref[:, GROUP:2 * GROUP]).astype(BF16)


def _mem_kv(memf, g, w_bf, gk, mem_len):
    n = memf.shape[0]
    seg = np.arange(256)[:, None] // MEM_D == np.arange(256)[None, :] // MEM_D
    seg = jnp.asarray(seg, BF16)
    const = lambda i: (0, 0)
    row = lambda i: (i, 0)
    return pl.pallas_call(
        _mem_kv_kernel,
        grid=(n // mem_len,),
        in_specs=[pl.BlockSpec((mem_len, D_MODEL), row),
                  pl.BlockSpec((1, D_MODEL), const),
                  pl.BlockSpec((D_MODEL, 2 * GROUP), const),
                  pl.BlockSpec((1, GROUP), const),
                  pl.BlockSpec((256, 256), const)],
        out_specs=[pl.BlockSpec((mem_len, GROUP), row)] * 2,
        out_shape=[jax.ShapeDtypeStruct((n, GROUP), BF16)] * 2,
        compiler_params=pltpu.CompilerParams(dimension_semantics=("parallel",), vmem_limit_bytes=VMEM_LIMIT),
        name="mem_kv",
    )(memf, g, w_bf, gk, seg)


def _out_proj_kernel(x_ref, oa_ref, oh_ref, qm_ref, z_ref, km_ref, vm_ref, gq_ref, w_ref, o_ref, y_ref):
    def gate(o, c0, c1):
        y_ref[:, c0:c1] = (o * _silu(z_ref[:, c0:c1].astype(F32))).astype(BF16)

    gate(oa_ref[...].astype(F32), 0, GROUP)
    gate(oh_ref[...].astype(F32), GROUP, 2 * GROUP)
    gq = gq_ref[...]
    for h in range(MEM_HEADS):
        cols = slice(h * MEM_D, (h + 1) * MEM_D)
        q = qm_ref[:, cols].astype(F32)
        ms = jnp.mean(q * q, axis=-1, keepdims=True)
        qn = q * lax.rsqrt(ms + NORM_EPS) * gq * (MEM_D ** -0.5)
        s = _nt(qn.astype(BF16), km_ref[:, cols])
        p = jnp.exp(s - jnp.max(s, axis=1, keepdims=True))
        om = _nn(p.astype(BF16), vm_ref[:, cols]) / jnp.sum(p, axis=1, keepdims=True)
        c0 = 2 * GROUP + h * MEM_D
        gate(om, c0, c0 + MEM_D)
    o_ref[...] = x_ref[...] + _nn(y_ref[...], w_ref[...])


def _out_proj(xf, oa, oh, qm, z, km, vm, gq, w_bf, seq, mem_len):
    n = xf.shape[0]
    tm = ROW_TILE
    steps = seq // tm
    const = lambda i: (0, 0)
    row = lambda i: (i, 0)
    per_batch = lambda i: (i // steps, 0)

    def moba_block(i):
        j = i % steps
        second = j >= steps // 2
        return (i // steps, jnp.where(second, 1, 0), jnp.where(second, steps - 1 - j, j), 0)

    return pl.pallas_call(
        _out_proj_kernel,
        grid=(n // tm,),
        in_specs=[pl.BlockSpec((tm, D_MODEL), row),
                  pl.BlockSpec((None, None, tm, GROUP), moba_block),
                  pl.BlockSpec((tm, GROUP), row),
                  pl.BlockSpec((tm, GROUP), row),
                  pl.BlockSpec((tm, D_MIX), row),
                  pl.BlockSpec((mem_len, GROUP), per_batch),
                  pl.BlockSpec((mem_len, GROUP), per_batch),
                  pl.BlockSpec((1, MEM_D), const),
                  pl.BlockSpec((D_MIX, D_MODEL), const)],
        out_specs=pl.BlockSpec((tm, D_MODEL), row),
        out_shape=jax.ShapeDtypeStruct((n, D_MODEL), F32),
        scratch_shapes=[pltpu.VMEM((tm, D_MIX), BF16)],
        compiler_params=pltpu.CompilerParams(dimension_semantics=("parallel",), vmem_limit_bytes=VMEM_LIMIT),
        name="out_proj",
    )(xf, oa, oh, qm, z, km, vm, gq, w_bf)


def kernel(x, mem, positions, norm_g, w_in, w_out, moba_q_norm, moba_k_norm, hgrn_lb_logits, hgrn_o_norm,
           mem_norm_g, w_mem_kv, mem_q_norm, mem_k_norm):
    batch, seq, d_model = x.shape
    mem_len = mem.shape[1]
    depth = w_in.shape[0]
    assert d_model == D_MODEL and seq % ROW_TILE == 0 and mem_len % 8 == 0
    n = batch * seq
    xf = x.reshape(n, d_model)
    memf = mem.reshape(batch * mem_len, d_model)
    cosf, sinf = _rope_tables(positions.reshape(n, 1))
    for l in range(depth):
        gq = jnp.tile(moba_q_norm[l], GROUP // MOBA_D)[None, :]
        gk = jnp.tile(moba_k_norm[l], GROUP // MOBA_D)[None, :]
        qa, ka, va, qh, fh, ih, qm, z = _in_proj(xf, norm_g[l][None, :], w_in[l].astype(BF16), cosf, sinf, gq, gk)
        oa = _moba(qa, ka, va, batch, seq)
        oh = _hgrn(qh, fh, ih, hgrn_lb_logits, jnp.tile(hgrn_o_norm[l], HGRN_HEADS)[None, :], l, batch, seq)
        km, vm = _mem_kv(memf, mem_norm_g[l][None, :], w_mem_kv[l].astype(BF16),
                         jnp.tile(mem_k_norm[l], MEM_HEADS)[None, :], mem_len)
        xf = _out_proj(xf, oa, oh, qm, z, km, vm, mem_q_norm[l][None, :], w_out[l].astype(BF16), seq, mem_len)
    return xf.reshape(batch, seq, d_model)
```

```python
import functools

import numpy as np
import jax
import jax.numpy as jnp
from jax import lax
from jax.experimental import pallas as pl
from jax.experimental.pallas import tpu as pltpu

F32 = jnp.float32
BF16 = jnp.bfloat16

D_MODEL = 1024
GROUP = D_MODEL // 2
D_MIX = 3 * GROUP
IN_COLS = 7 * GROUP + D_MIX
MOBA_D = 64
MOBA_BLOCK = 256
MOBA_TOPK = 3
MOBA_VT_ROWS = MOBA_D + 16
HGRN_D = 128
HGRN_HEADS = GROUP // HGRN_D
HGRN_CHUNK = 64
HGRN_LEVELS = 6
HGRN_MM_LEVELS = 3
LOG2E = 1.4426950408889634
MEM_D = 128
MEM_HEADS = GROUP // MEM_D
ROPE_THETA = 500000.0
ROPE_DIM = MOBA_D // 4
ROPE_HALF = ROPE_DIM // 2
NORM_EPS = 1e-6
LANES = 128
ROW_TILE = 512
HGRN_TILE = 256
MASKED = -1e30
VMEM_LIMIT = 48 * 1024 * 1024


def _nt(a, b):
    return lax.dot_general(a, b, (((1,), (1,)), ((), ())), preferred_element_type=F32)


def _tn(a, b):
    return lax.dot_general(a, b, (((0,), (0,)), ((), ())), preferred_element_type=F32)


def _nn(a, b):
    return jnp.dot(a, b, preferred_element_type=F32)


def _split2(x):
    hi = x.astype(BF16)
    lo = (x - hi.astype(F32)).astype(BF16)
    return hi, lo


def _split3(x):
    hi = x.astype(BF16)
    r = x - hi.astype(F32)
    mid = r.astype(BF16)
    lo = (r - mid.astype(F32)).astype(BF16)
    return hi, mid, lo


def _sigmoid(x):
    return 1.0 / (1.0 + jnp.exp(-x))


def _silu(x):
    return x * _sigmoid(x)


def _rope_table_kernel(pos_ref, invf_ref, sgn_ref, cos_ref, sin_ref):
    ang = pos_ref[...].astype(F32) * invf_ref[...]
    sgn = sgn_ref[...]
    on = sgn != 0.0
    cos_ref[...] = jnp.where(on, jnp.cos(ang), 1.0)
    sin_ref[...] = sgn * jnp.sin(ang)


def _rope_tables(pos):
    n = pos.shape[0]
    tm = min(n, 2048)
    lane = np.arange(LANES) % MOBA_D
    invf = np.where(lane < ROPE_DIM, ROPE_THETA ** (-(lane % ROPE_HALF).astype(np.float64) / ROPE_HALF), 0.0)
    sgn = np.where(lane < ROPE_HALF, -1.0, np.where(lane < ROPE_DIM, 1.0, 0.0))
    invf = jnp.asarray(invf[None, :], F32)
    sgn = jnp.asarray(sgn[None, :], F32)
    return pl.pallas_call(
        _rope_table_kernel,
        grid=(n // tm,),
        in_specs=[pl.BlockSpec((tm, 1), lambda i: (i, 0)),
                  pl.BlockSpec((1, LANES), lambda i: (0, 0)),
                  pl.BlockSpec((1, LANES), lambda i: (0, 0))],
        out_specs=[pl.BlockSpec((tm, LANES), lambda i: (i, 0))] * 2,
        out_shape=[jax.ShapeDtypeStruct((n, LANES), F32)] * 2,
        name="rope_tables",
    )(pos, invf, sgn)


def _in_proj_kernel(x_ref, g_ref, w_ref, cos_ref, sin_ref, gq_ref, gk_ref, seg_ref,
                    qa_ref, ka_ref, va_ref, qh_ref, fh_ref, ih_ref, qm_ref, z_ref):
    x = x_ref[...]
    ms = jnp.mean(x * x, axis=-1, keepdims=True)
    hb = (x * lax.rsqrt(ms + NORM_EPS) * g_ref[...]).astype(BF16)

    def proj(c0, width):
        return _nn(hb, w_ref[:, c0:c0 + width])

    cosf = cos_ref[...]
    sinf = sin_ref[...]
    lane = lax.broadcasted_iota(jnp.int32, (1, LANES), 1) % MOBA_D
    lo_lane = lane < ROPE_HALF
    seg = seg_ref[...]

    def head_norm_rope(p, gain, out_ref):
        for c in range(GROUP // 256):
            pc = p[:, c * 256:(c + 1) * 256]
            ss = _nn((pc * pc).astype(BF16), seg) * (1.0 / MOBA_D)
            pn = pc * lax.rsqrt(ss + NORM_EPS) * gain[:, c * 256:(c + 1) * 256]
            for v in range(2):
                xv = pn[:, v * LANES:(v + 1) * LANES]
                rot = jnp.where(lo_lane, pltpu.roll(xv, LANES - ROPE_HALF, 1), pltpu.roll(xv, ROPE_HALF, 1))
                col = c * 256 + v * LANES
                out_ref[:, col:col + LANES] = (xv * cosf + rot * sinf).astype(out_ref.dtype)

    head_norm_rope(proj(0, GROUP), gq_ref[...], qa_ref)
    head_norm_rope(proj(GROUP, GROUP), gk_ref[...], ka_ref)
    va_ref[...] = proj(2 * GROUP, GROUP).astype(BF16)
    qh_ref[...] = proj(3 * GROUP, GROUP).astype(BF16)
    fh_ref[...] = proj(4 * GROUP, GROUP)
    ih_ref[...] = proj(5 * GROUP, GROUP).astype(BF16)
    qm_ref[...] = proj(6 * GROUP, GROUP).astype(BF16)
    for c in range(3):
        z_ref[:, c * GROUP:(c + 1) * GROUP] = proj((7 + c) * GROUP, GROUP).astype(BF16)


def _in_proj(xf, g, w_bf, cosf, sinf, gq, gk):
    n = xf.shape[0]
    tm = ROW_TILE
    seg = np.arange(256)[:, None] // MOBA_D == np.arange(256)[None, :] // MOBA_D
    seg = jnp.asarray(seg, BF16)
    const = lambda i: (0, 0)
    row = lambda i: (i, 0)
    outs = ((GROUP, F32), (GROUP, BF16), (GROUP, BF16), (GROUP, BF16), (GROUP, F32), (GROUP, BF16),
            (GROUP, BF16), (D_MIX, BF16))
    return pl.pallas_call(
        _in_proj_kernel,
        grid=(n // tm,),
        in_specs=[pl.BlockSpec((tm, D_MODEL), row),
                  pl.BlockSpec((1, D_MODEL), const),
                  pl.BlockSpec((D_MODEL, IN_COLS), const, pipeline_mode=pl.Buffered(1)),
                  pl.BlockSpec((tm, LANES), row),
                  pl.BlockSpec((tm, LANES), row),
                  pl.BlockSpec((1, GROUP), const),
                  pl.BlockSpec((1, GROUP), const),
                  pl.BlockSpec((256, 256), const)],
        out_specs=[pl.BlockSpec((tm, w), row) for w, _ in outs],
        out_shape=[jax.ShapeDtypeStruct((n, w), dt) for w, dt in outs],
        compiler_params=pltpu.CompilerParams(dimension_semantics=("parallel",), vmem_limit_bytes=VMEM_LIMIT),
        name="in_proj",
    )(xf, g, w_bf, cosf, sinf, gq, gk, seg)


def _moba_kernel(q_ref, k_ref, v_ref, o_ref, kmean_ref, kaug_ref, vt_ref, s_ref, *, nb):
    blk = MOBA_BLOCK
    lane = lax.broadcasted_iota(jnp.int32, (1, LANES), 1)
    heads = [(lane >= h * MOBA_D) & (lane < (h + 1) * MOBA_D) for h in range(2)]
    spare = [(1 - h) * MOBA_D for h in range(2)]

    ones = jnp.ones((MOBA_VT_ROWS - MOBA_D, blk), F32)
    for j in range(nb):
        rows = slice(j * blk, (j + 1) * blk)
        kj = k_ref[rows, :]
        kmean_ref[j:j + 1, :] = jnp.mean(kj.astype(F32), axis=0, keepdims=True)
        v_t = v_ref[rows, :].astype(F32).T
        for h in range(2):
            onehot = jnp.where(lane == spare[h] + j, 1.0, 0.0).astype(BF16)
            kaug_ref[h, rows, :] = jnp.where(heads[h], kj, onehot)
            vt_ref[h, :, rows] = jnp.concatenate([v_t[h * MOBA_D:(h + 1) * MOBA_D], ones], axis=0).astype(BF16)

    km = kmean_ref[...]
    blk_row = lax.broadcasted_iota(jnp.int32, (nb, 1), 0)

    def augmented_query(q, own, h):
        past = blk_row < own
        qh = jnp.where(heads[h], q, 0.0)
        q_hi, q_lo = _split2(qh)
        k_hi, k_lo = _split2(jnp.where(heads[h], km, 0.0))
        gate_t = _nt(k_hi, q_hi) + _nt(k_hi, q_lo) + _nt(k_lo, q_hi)
        gate_t = jnp.where(past, gate_t, -jnp.inf)
        cnt = jnp.zeros((nb, blk), F32)
        for n in range(own):
            gn = gate_t[n:n + 1, :]
            before = (gn > gate_t) | ((gn == gate_t) & (n < blk_row))
            cnt = cnt + jnp.where(before, 1.0, 0.0)
        sel = (past & (cnt < float(MOBA_TOPK))) | (blk_row == own)
        bias_t = jnp.where(sel, 0.0, MASKED)
        parts = []
        if spare[h]:
            parts.append(jnp.zeros((spare[h], blk), F32))
        parts.append(bias_t)
        parts.append(jnp.zeros((LANES - spare[h] - nb, blk), F32))
        bias = jnp.concatenate(parts, axis=0).T
        return jnp.where(heads[h], qh * (MOBA_D ** -0.5 * LOG2E), bias)

    kpos = lax.broadcasted_iota(jnp.int32, (blk, blk), 0)
    qpos = lax.broadcasted_iota(jnp.int32, (blk, blk), 1)
    causal_t = kpos <= qpos

    token = None
    for t in range(nb // 2):
        buf = t % 2
        base = 0
        for c in (nb - 1 - t, t):
            q = q_ref[c * blk:(c + 1) * blk, :]
            outs = []
            for h in range(2):
                qa = augmented_query(q, c, h)
                if token is not None:
                    qa = qa + token
                qa = qa.astype(BF16)
                mx = None
                for j in range(c + 1):
                    rows = slice(j * blk, (j + 1) * blk)
                    s = _nt(kaug_ref[h, rows, :], qa)
                    if j == c:
                        s = jnp.where(causal_t, s, MASKED)
                    s_ref[buf, h, base + j * blk:base + (j + 1) * blk, :] = s
                    mx = s if mx is None else jnp.maximum(mx, s)
                    if j == 0:
                        bits = pltpu.bitcast(s[0:1, 0:LANES], jnp.uint32)
                        zero = lax.shift_right_logical(lax.shift_right_logical(bits, jnp.uint32(16)), jnp.uint32(16))
                        token = zero.astype(F32)
                m = jnp.max(mx, axis=0, keepdims=True)
                acc = None
                for j in range(c + 1):
                    rows = slice(j * blk, (j + 1) * blk)
                    p = jnp.exp2(s_ref[buf, h, base + j * blk:base + (j + 1) * blk, :] - m).astype(BF16)
                    pv = _nn(vt_ref[h, :, rows], p)
                    acc = pv if acc is None else acc + pv
                outs.append(acc[0:MOBA_D] / acc[MOBA_D:MOBA_D + 1])
            o_ref[c * blk:(c + 1) * blk, :] = jnp.concatenate(outs, axis=0).T.astype(BF16)
            base += (c + 1) * blk


def _moba(qa, ka, va, batch, seq):
    nb = seq // MOBA_BLOCK
    assert nb % 2 == 0
    pairs = GROUP // LANES
    whole = pl.BlockSpec((seq, LANES), lambda b, p: (b, p))
    return pl.pallas_call(
        functools.partial(_moba_kernel, nb=nb),
        grid=(batch, pairs),
        in_specs=[whole, whole, whole],
        out_specs=whole,
        out_shape=jax.ShapeDtypeStruct(qa.shape, BF16),
        scratch_shapes=[pltpu.VMEM((nb, LANES), F32),
                        pltpu.VMEM((2, seq, LANES), BF16),
                        pltpu.VMEM((2, MOBA_VT_ROWS, seq), BF16),
                        pltpu.VMEM((2, 2, (nb + 1) * MOBA_BLOCK, MOBA_BLOCK), F32)],
        compiler_params=pltpu.CompilerParams(dimension_semantics=("parallel", "parallel"),
                                             vmem_limit_bytes=VMEM_LIMIT),
        name="moba",
    )(qa, ka, va)


def _hgrn_consts():
    c = HGRN_CHUNK
    t = np.arange(c)
    w_rows, masks = [], []
    for lvl in range(HGRN_LEVELS):
        m = 1 << lvl
        blk = t // m
        odd = (blk % 2) == 1
        start = blk * m
        end = start + m - 1
        u = t[None, :]
        w_odd = (u >= start[:, None]) & (u <= t[:, None])
        w_even = (u > t[:, None]) & (u <= end[:, None])
        w_rows.append(np.where(odd[:, None], w_odd, w_even))
        masks.append(odd[:, None] & (blk[None, :] == blk[:, None] - 1))
    masks.append(t[None, :] == t[:, None])
    w_mm = w_rows[:HGRN_MM_LEVELS] + [t[None, :] <= t[:, None]]
    w_all = np.concatenate(w_mm, axis=0).astype(np.float32)
    w_all = np.concatenate([w_all, w_all], axis=1)
    mask_cat = np.concatenate([masks[-1]] + masks[:-1] + [np.zeros((c, c), bool)], axis=1).astype(np.float32)
    return jnp.asarray(w_all, BF16), jnp.asarray(mask_cat, F32)


def _hgrn_kernel(qh_ref, fh_ref, ih_ref, logit_ref, gain_ref, w_ref, mask_ref, o_ref, state_ref, *, layer):
    c = HGRN_CHUNK
    state_ref[...] = jnp.zeros_like(state_ref)

    if layer > 0:
        lg = logit_ref[...]
        e = jnp.exp(lg - jnp.max(lg, axis=0, keepdims=True))
        sm = e / jnp.sum(e, axis=0, keepdims=True)
        lb = jnp.sum(sm[1:layer + 1, :], axis=0, keepdims=True)
        log_lb = jnp.log(lb)
        log_1m_lb = jnp.log1p(-lb)
    w_all = w_ref[...]
    gain = gain_ref[...]

    def prepare(base, ci):
        rows = pl.ds(base + ci * c, c)
        fl = fh_ref[rows, :]
        vi = ih_ref[rows, :]
        qf = _silu(qh_ref[rows, :].astype(F32))
        e = jnp.exp(-jnp.abs(fl))
        r = 1.0 / (1.0 + e)
        log_sig = jnp.minimum(fl, 0.0) - jnp.log(1.0 + e)
        sig_neg = jnp.where(fl >= 0.0, e * r, r)
        if layer > 0:
            b = log_1m_lb + log_sig
            lf = jnp.maximum(log_lb, b) + jnp.log(1.0 + jnp.exp(-jnp.abs(log_lb - b)))
            kf = (1.0 - lb) * sig_neg
        else:
            lf = log_sig
            kf = sig_neg

        f1, f2 = _split2(lf * LOG2E)
        x_mm = _nn(w_all, jnp.concatenate([f1, f2], axis=0))
        a2 = x_mm[HGRN_MM_LEVELS * c:(HGRN_MM_LEVELS + 1) * c, :]
        e_lvl = [jnp.exp2(x_mm[lvl * c:(lvl + 1) * c, :]) for lvl in range(HGRN_MM_LEVELS)]
        for lvl in range(HGRN_MM_LEVELS, HGRN_LEVELS):
            m = 1 << lvl
            parts = []
            for g in range(c // (2 * m)):
                ref = a2[2 * m * g + m - 1:2 * m * g + m, :]
                parts.append(-jnp.abs(a2[2 * m * g:2 * m * (g + 1), :] - ref))
            e_lvl.append(jnp.exp2(jnp.concatenate(parts, axis=0)))
        e_cum = jnp.exp2(a2)
        e_rest = jnp.exp2(a2[c - 1:c, :] - a2)
        e_end = e_cum[c - 1:c, :]
        return qf, kf, vi, e_lvl, e_cum, e_rest, e_end

    def recur(base, ci, prepared):
        rows = pl.ds(base + ci * c, c)
        qf, kf, vi, e_lvl, e_cum, e_rest, e_end = prepared
        for h in range(HGRN_HEADS):
            cols = slice(h * HGRN_D, (h + 1) * HGRN_D)
            qfh, kfh = qf[:, cols], kf[:, cols]
            qb = [qfh.astype(BF16)] + [(qfh * e[:, cols]).astype(BF16) for e in e_lvl]
            kb = [kfh.astype(BF16)] + [(kfh * e[:, cols]).astype(BF16) for e in e_lvl]
            zero = jnp.zeros((c, HGRN_D), BF16)
            s_parts = []
            for a in range(0, HGRN_LEVELS, 2):
                q_cat = jnp.concatenate([qb[a], qb[a + 1]], axis=1)
                k_blk = jnp.concatenate([jnp.concatenate([kb[a], zero], axis=1),
                                         jnp.concatenate([zero, kb[a + 1]], axis=1)], axis=0)
                s_parts.append(_nt(q_cat, k_blk))
            s_parts.append(_nt(qb[HGRN_LEVELS], jnp.concatenate([kb[HGRN_LEVELS], zero], axis=0)))
            p = (jnp.concatenate(s_parts, axis=1) * mask_ref[...]).astype(BF16)
            vb = vi[:, cols]
            st = state_ref[:, cols]
            o = _nn(p, jnp.concatenate([vb] * (HGRN_LEVELS + 2), axis=0))
            o = o + _nt((qfh * e_cum[:, cols]).astype(BF16), st.astype(BF16))
            state_ref[:, cols] = st * e_end[:, cols] + _tn(vb, (kfh * e_rest[:, cols]).astype(BF16))
            ms = jnp.mean(o * o, axis=-1, keepdims=True)
            o_ref[rows, cols] = (o * lax.rsqrt(ms + NORM_EPS) * gain[:, cols]).astype(BF16)

    n_chunks = HGRN_TILE // c

    def group(g, carry):
        base = pl.multiple_of(g * HGRN_TILE, HGRN_TILE)
        prepared = prepare(base, 0)
        for ci in range(n_chunks):
            nxt = prepare(base, ci + 1) if ci + 1 < n_chunks else None
            recur(base, ci, prepared)
            prepared = nxt
        return carry

    lax.fori_loop(0, qh_ref.shape[0] // HGRN_TILE, group, 0)


def _hgrn(qh, fh, ih, logits, gain, layer, batch, seq):
    w_all, mask_all = _hgrn_consts()
    assert seq % HGRN_TILE == 0
    depth = logits.shape[0]
    const = lambda b: (0, 0)
    return pl.pallas_call(
        functools.partial(_hgrn_kernel, layer=layer),
        grid=(batch,),
        in_specs=[pl.BlockSpec((seq, GROUP), lambda b: (b, 0))] * 3 + [
                  pl.BlockSpec((depth, GROUP), const),
                  pl.BlockSpec((1, GROUP), const),
                  pl.BlockSpec(w_all.shape, const),
                  pl.BlockSpec(mask_all.shape, const)],
        out_specs=pl.BlockSpec((seq, GROUP), lambda b: (b, 0)),
        out_shape=jax.ShapeDtypeStruct((fh.shape[0], GROUP), BF16),
        scratch_shapes=[pltpu.VMEM((HGRN_D, GROUP), F32)],
        compiler_params=pltpu.CompilerParams(dimension_semantics=("parallel",),
                                             vmem_limit_bytes=VMEM_LIMIT),
        name="hgrn",
    )(qh, fh, ih, logits, gain, w_all, mask_all)


def _mem_kv_kernel(m_ref, g_ref, w_ref, gk_ref, seg_ref, k_ref, v_ref):
    x = m_ref[...]
    ms = jnp.mean(x * x, axis=-1, keepdims=True)
    hb = (x * lax.rsqrt(ms + NORM_EPS) * g_ref[...]).astype(BF16)
    gk = gk_ref[...]
    seg = seg_ref[...]
    for c in range(GROUP // 256):
        kc = _nn(hb, w_ref[:, c * 256:(c + 1) * 256])
        ss = _nn((kc * kc).astype(BF16), seg) * (1.0 / MEM_D)
        kn = kc * lax.rsqrt(ss + NORM_EPS) * gk[:, c * 256:(c + 1) * 256]
        k_ref[:, c * 256:(c + 1) * 256] = kn.astype(BF16)
    v_ref[...] = _nn(hb, w_ref[:, GROUP:2 * GROUP]).astype(BF16)


def _mem_kv(memf, g, w_bf, gk, mem_len):
    n = memf.shape[0]
    seg = np.arange(256)[:, None] // MEM_D == np.arange(256)[None, :] // MEM_D
    seg = jnp.asarray(seg, BF16)
    const = lambda i: (0, 0)
    row = lambda i: (i, 0)
    return pl.pallas_call(
        _mem_kv_kernel,
        grid=(n // mem_len,),
        in_specs=[pl.BlockSpec((mem_len, D_MODEL), row),
                  pl.BlockSpec((1, D_MODEL), const),
                  pl.BlockSpec((D_MODEL, 2 * GROUP), const),
                  pl.BlockSpec((1, GROUP), const),
                  pl.BlockSpec((256, 256), const)],
        out_specs=[pl.BlockSpec((mem_len, GROUP), row)] * 2,
        out_shape=[jax.ShapeDtypeStruct((n, GROUP), BF16)] * 2,
        compiler_params=pltpu.CompilerParams(dimension_semantics=("parallel",), vmem_limit_bytes=VMEM_LIMIT),
        name="mem_kv",
    )(memf, g, w_bf, gk, seg)


def _out_proj_kernel(x_ref, oa_ref, oh_ref, qm_ref, z_ref, km_ref, vm_ref, gq_ref, w_ref, o_ref, y_ref):
    def gate(o, c0, c1):
        y_ref[:, c0:c1] = (o * _silu(z_ref[:, c0:c1].astype(F32))).astype(BF16)

    gate(oa_ref[...].astype(F32), 0, GROUP)
    gate(oh_ref[...].astype(F32), GROUP, 2 * GROUP)
    gq = gq_ref[...]
    for h in range(MEM_HEADS):
        cols = slice(h * MEM_D, (h + 1) * MEM_D)
        q = qm_ref[:, cols].astype(F32)
        ms = jnp.mean(q * q, axis=-1, keepdims=True)
        qn = q * lax.rsqrt(ms + NORM_EPS) * gq * (MEM_D ** -0.5)
        s = _nt(qn.astype(BF16), km_ref[:, cols])
        p = jnp.exp(s - jnp.max(s, axis=1, keepdims=True))
        om = _nn(p.astype(BF16), vm_ref[:, cols]) / jnp.sum(p, axis=1, keepdims=True)
        c0 = 2 * GROUP + h * MEM_D
        gate(om, c0, c0 + MEM_D)
    o_ref[...] = x_ref[...] + _nn(y_ref[...], w_ref[...])


def _out_proj(xf, oa, oh, qm, z, km, vm, gq, w_bf, seq, mem_len):
    n = xf.shape[0]
    tm = ROW_TILE
    steps = seq // tm
    const = lambda i: (0, 0)
    row = lambda i: (i, 0)
    per_batch = lambda i: (i // steps, 0)
    return pl.pallas_call(
        _out_proj_kernel,
        grid=(n // tm,),
        in_specs=[pl.BlockSpec((tm, D_MODEL), row),
                  pl.BlockSpec((tm, GROUP), row),
                  pl.BlockSpec((tm, GROUP), row),
                  pl.BlockSpec((tm, GROUP), row),
                  pl.BlockSpec((tm, D_MIX), row),
                  pl.BlockSpec((mem_len, GROUP), per_batch),
                  pl.BlockSpec((mem_len, GROUP), per_batch),
                  pl.BlockSpec((1, MEM_D), const),
                  pl.BlockSpec((D_MIX, D_MODEL), const)],
        out_specs=pl.BlockSpec((tm, D_MODEL), row),
        out_shape=jax.ShapeDtypeStruct((n, D_MODEL), F32),
        scratch_shapes=[pltpu.VMEM((tm, D_MIX), BF16)],
        compiler_params=pltpu.CompilerParams(dimension_semantics=("parallel",), vmem_limit_bytes=VMEM_LIMIT),
        name="out_proj",
    )(xf, oa, oh, qm, z, km, vm, gq, w_bf)


def kernel(x, mem, positions, norm_g, w_in, w_out, moba_q_norm, moba_k_norm, hgrn_lb_logits, hgrn_o_norm,
           mem_norm_g, w_mem_kv, mem_q_norm, mem_k_norm):
    batch, seq, d_model = x.shape
    mem_len = mem.shape[1]
    depth = w_in.shape[0]
    assert d_model == D_MODEL and seq % ROW_TILE == 0 and mem_len % 8 == 0
    n = batch * seq
    xf = x.reshape(n, d_model)
    memf = mem.reshape(batch * mem_len, d_model)
    cosf, sinf = _rope_tables(positions.reshape(n, 1))
    for l in range(depth):
        gq = jnp.tile(moba_q_norm[l], GROUP // MOBA_D)[None, :]
        gk = jnp.tile(moba_k_norm[l], GROUP // MOBA_D)[None, :]
        qa, ka, va, qh, fh, ih, qm, z = _in_proj(xf, norm_g[l][None, :], w_in[l].astype(BF16), cosf, sinf, gq, gk)
        oa = _moba(qa, ka, va, batch, seq)
        oh = _hgrn(qh, fh, ih, hgrn_lb_logits, jnp.tile(hgrn_o_norm[l], HGRN_HEADS)[None, :], l, batch, seq)
        km, vm = _mem_kv(memf, mem_norm_g[l][None, :], w_mem_kv[l].astype(BF16),
                         jnp.tile(mem_k_norm[l], MEM_HEADS)[None, :], mem_len)
        xf = _out_proj(xf, oa, oh, qm, z, km, vm, mem_q_norm[l][None, :], w_out[l].astype(BF16), seq, mem_len)
    return xf.reshape(batch, seq, d_model)
```

```python
import functools

import numpy as np
import jax
import jax.numpy as jnp
from jax import lax
from jax.experimental import pallas as pl
from jax.experimental.pallas import tpu as pltpu

F32 = jnp.float32
BF16 = jnp.bfloat16

D_MODEL = 1024
GROUP = D_MODEL // 2
D_MIX = 3 * GROUP
IN_COLS = 7 * GROUP + D_MIX
MOBA_D = 64
MOBA_BLOCK = 256
MOBA_TOPK = 3
MOBA_VT_ROWS = MOBA_D + 16
HGRN_D = 128
HGRN_HEADS = GROUP // HGRN_D
HGRN_CHUNK = 64
HGRN_LEVELS = 6
HGRN_MM_LEVELS = 3
LOG2E = 1.4426950408889634
MEM_D = 128
MEM_HEADS = GROUP // MEM_D
ROPE_THETA = 500000.0
ROPE_DIM = MOBA_D // 4
ROPE_HALF = ROPE_DIM // 2
NORM_EPS = 1e-6
LANES = 128
ROW_TILE = 512
HGRN_TILE = 256
MASKED = -1e30
VMEM_LIMIT = 48 * 1024 * 1024


def _nt(a, b):
    return lax.dot_general(a, b, (((1,), (1,)), ((), ())), preferred_element_type=F32)


def _tn(a, b):
    return lax.dot_general(a, b, (((0,), (0,)), ((), ())), preferred_element_type=F32)


def _nn(a, b):
    return jnp.dot(a, b, preferred_element_type=F32)


def _split2(x):
    hi = x.astype(BF16)
    lo = (x - hi.astype(F32)).astype(BF16)
    return hi, lo


def _sigmoid(x):
    return 1.0 / (1.0 + jnp.exp(-x))


def _silu(x):
    return x * _sigmoid(x)


def _rope_table_kernel(pos_ref, invf_ref, sgn_ref, cos_ref, sin_ref):
    ang = pos_ref[...].astype(F32) * invf_ref[...]
    sgn = sgn_ref[...]
    on = sgn != 0.0
    cos_ref[...] = jnp.where(on, jnp.cos(ang), 1.0)
    sin_ref[...] = sgn * jnp.sin(ang)


def _rope_tables(pos):
    n = pos.shape[0]
    tm = min(n, 2048)
    lane = np.arange(LANES) % MOBA_D
    invf = np.where(lane < ROPE_DIM, ROPE_THETA ** (-(lane % ROPE_HALF).astype(np.float64) / ROPE_HALF), 0.0)
    sgn = np.where(lane < ROPE_HALF, -1.0, np.where(lane < ROPE_DIM, 1.0, 0.0))
    invf = jnp.asarray(invf[None, :], F32)
    sgn = jnp.asarray(sgn[None, :], F32)
    return pl.pallas_call(
        _rope_table_kernel,
        grid=(n // tm,),
        in_specs=[pl.BlockSpec((tm, 1), lambda i: (i, 0)),
                  pl.BlockSpec((1, LANES), lambda i: (0, 0)),
                  pl.BlockSpec((1, LANES), lambda i: (0, 0))],
        out_specs=[pl.BlockSpec((tm, LANES), lambda i: (i, 0))] * 2,
        out_shape=[jax.ShapeDtypeStruct((n, LANES), F32)] * 2,
        name="rope_tables",
    )(pos, invf, sgn)


def _in_proj_kernel(x_ref, g_ref, w_ref, cos_ref, sin_ref, gq_ref, gk_ref, seg_ref,
                    qa_ref, ka_ref, va_ref, qh_ref, fh_ref, ih_ref, qm_ref, z_ref):
    x = x_ref[...]
    ms = jnp.mean(x * x, axis=-1, keepdims=True)
    hb = (x * lax.rsqrt(ms + NORM_EPS) * g_ref[...]).astype(BF16)

    def proj(c0, width):
        return _nn(hb, w_ref[:, c0:c0 + width])

    cosf = cos_ref[...]
    sinf = sin_ref[...]
    lane = lax.broadcasted_iota(jnp.int32, (1, LANES), 1) % MOBA_D
    lo_lane = lane < ROPE_HALF
    seg = seg_ref[...]

    def head_norm_rope(p, gain, out_ref):
        for c in range(GROUP // 256):
            pc = p[:, c * 256:(c + 1) * 256]
            ss = _nn((pc * pc).astype(BF16), seg) * (1.0 / MOBA_D)
            pn = pc * lax.rsqrt(ss + NORM_EPS) * gain[:, c * 256:(c + 1) * 256]
            for v in range(2):
                xv = pn[:, v * LANES:(v + 1) * LANES]
                rot = jnp.where(lo_lane, pltpu.roll(xv, LANES - ROPE_HALF, 1), pltpu.roll(xv, ROPE_HALF, 1))
                col = c * 256 + v * LANES
                out_ref[:, col:col + LANES] = (xv * cosf + rot * sinf).astype(out_ref.dtype)

    head_norm_rope(proj(0, GROUP), gq_ref[...], qa_ref)
    head_norm_rope(proj(GROUP, GROUP), gk_ref[...], ka_ref)
    va_ref[...] = proj(2 * GROUP, GROUP).astype(BF16)
    qh_ref[...] = proj(3 * GROUP, GROUP).astype(BF16)
    fh_ref[...] = proj(4 * GROUP, GROUP)
    ih_ref[...] = proj(5 * GROUP, GROUP).astype(BF16)
    qm_ref[...] = proj(6 * GROUP, GROUP).astype(BF16)
    for c in range(3):
        z_ref[:, c * GROUP:(c + 1) * GROUP] = proj((7 + c) * GROUP, GROUP).astype(BF16)


def _in_proj(xf, g, w_bf, cosf, sinf, gq, gk):
    n = xf.shape[0]
    tm = ROW_TILE
    seg = np.arange(256)[:, None] // MOBA_D == np.arange(256)[None, :] // MOBA_D
    seg = jnp.asarray(seg, BF16)
    const = lambda i: (0, 0)
    row = lambda i: (i, 0)
    outs = ((GROUP, F32), (GROUP, BF16), (GROUP, BF16), (GROUP, BF16), (GROUP, F32), (GROUP, BF16),
            (GROUP, BF16), (D_MIX, BF16))
    return pl.pallas_call(
        _in_proj_kernel,
        grid=(n // tm,),
        in_specs=[pl.BlockSpec((tm, D_MODEL), row),
                  pl.BlockSpec((1, D_MODEL), const),
                  pl.BlockSpec((D_MODEL, IN_COLS), const, pipeline_mode=pl.Buffered(1)),
                  pl.BlockSpec((tm, LANES), row),
                  pl.BlockSpec((tm, LANES), row),
                  pl.BlockSpec((1, GROUP), const),
                  pl.BlockSpec((1, GROUP), const),
                  pl.BlockSpec((256, 256), const)],
        out_specs=[pl.BlockSpec((tm, w), row) for w, _ in outs],
        out_shape=[jax.ShapeDtypeStruct((n, w), dt) for w, dt in outs],
        compiler_params=pltpu.CompilerParams(dimension_semantics=("parallel",), vmem_limit_bytes=VMEM_LIMIT),
        name="in_proj",
    )(xf, g, w_bf, cosf, sinf, gq, gk, seg)


def _moba_kernel(q_ref, k_ref, v_ref, o_ref, kmean_ref, kaug_ref, vt_ref, qaug_ref, s_ref, *, nb):
    blk = MOBA_BLOCK
    seq = nb * blk
    lane = lax.broadcasted_iota(jnp.int32, (1, LANES), 1)
    heads = [(lane >= h * MOBA_D) & (lane < (h + 1) * MOBA_D) for h in range(2)]
    spare = [(1 - h) * MOBA_D for h in range(2)]

    ones = jnp.ones((MOBA_VT_ROWS - MOBA_D, blk), F32)
    for j in range(nb):
        rows = slice(j * blk, (j + 1) * blk)
        kj = k_ref[rows, :]
        kmean_ref[j:j + 1, :] = jnp.mean(kj.astype(F32), axis=0, keepdims=True)
        v_t = v_ref[rows, :].astype(F32).T
        for h in range(2):
            onehot = jnp.where(lane == spare[h] + j, 1.0, 0.0).astype(BF16)
            kaug_ref[h, rows, :] = jnp.where(heads[h], kj, onehot)
            vt_ref[h, :, rows] = jnp.concatenate([v_t[h * MOBA_D:(h + 1) * MOBA_D], ones], axis=0).astype(BF16)

    km = kmean_ref[...]
    q_hi, q_lo = _split2(q_ref[...])
    km_parts = [_split2(jnp.where(heads[h], km, 0.0)) for h in range(2)]
    g_hi = _nt(jnp.concatenate([km_parts[0][0], km_parts[0][1], km_parts[1][0], km_parts[1][1]], axis=0), q_hi)
    g_lo = _nt(jnp.concatenate([km_parts[0][0], km_parts[1][0]], axis=0), q_lo)
    blk_row = lax.broadcasted_iota(jnp.int32, (nb, 1), 0)
    q_blk = lax.broadcasted_iota(jnp.int32, (1, seq), 1) // blk
    past = blk_row < q_blk
    for h in range(2):
        gate_t = g_hi[2 * h * nb:(2 * h + 1) * nb] + g_hi[(2 * h + 1) * nb:(2 * h + 2) * nb] + g_lo[h * nb:(h + 1) * nb]
        gate_t = jnp.where(past, gate_t, -jnp.inf)
        cnt = jnp.zeros((nb, seq), F32)
        for n in range(nb - 1):
            gn = gate_t[n:n + 1, :]
            before = (gn > gate_t) | ((gn == gate_t) & (n < blk_row))
            cnt = cnt + jnp.where(before, 1.0, 0.0)
        sel = (past & (cnt < float(MOBA_TOPK))) | (blk_row == q_blk)
        bias_t = jnp.where(sel, 0.0, MASKED)
        for c in range(nb):
            parts = []
            if spare[h]:
                parts.append(jnp.zeros((spare[h], blk), F32))
            parts.append(bias_t[:, c * blk:(c + 1) * blk])
            parts.append(jnp.zeros((LANES - spare[h] - nb, blk), F32))
            bias = jnp.concatenate(parts, axis=0).T
            qs = q_ref[c * blk:(c + 1) * blk, :] * (MOBA_D ** -0.5 * LOG2E)
            qaug_ref[c, h] = jnp.where(heads[h], qs, bias).astype(BF16)

    kpos = lax.broadcasted_iota(jnp.int32, (blk, blk), 0)
    qpos = lax.broadcasted_iota(jnp.int32, (blk, blk), 1)
    causal_t = kpos <= qpos

    token = None
    for t in range(nb // 2):
        buf = t % 2
        base = 0
        for c in (nb - 1 - t, t):
            outs = []
            for h in range(2):
                qa = qaug_ref[c, h]
                if token is not None:
                    qa = (qa.astype(F32) + token).astype(BF16)
                st = s_ref.at[buf, h]
                mx = None
                for j in range(c + 1):
                    s = _nt(kaug_ref[h, j * blk:(j + 1) * blk, :], qa)
                    if j == 0:
                        bits = pltpu.bitcast(s[0:1, 0:LANES], jnp.uint32)
                        zero = lax.shift_right_logical(lax.shift_right_logical(bits, jnp.uint32(16)), jnp.uint32(16))
                        token = zero.astype(F32)
                    if j == c:
                        s = jnp.where(causal_t, s, MASKED)
                    st[base + j * blk:base + (j + 1) * blk, :] = s
                    mx = s if mx is None else jnp.maximum(mx, s)
                m = jnp.max(mx, axis=0, keepdims=True)
                acc = None
                for j in range(c + 1):
                    rows = slice(j * blk, (j + 1) * blk)
                    p = jnp.exp2(st[base + j * blk:base + (j + 1) * blk, :] - m).astype(BF16)
                    pv = _nn(vt_ref[h, :, rows], p)
                    acc = pv if acc is None else acc + pv
                outs.append(acc[0:MOBA_D] / acc[MOBA_D:MOBA_D + 1])
            o_ref[c * blk:(c + 1) * blk, :] = jnp.concatenate(outs, axis=0).T.astype(BF16)
            base += (c + 1) * blk


def _moba(qa, ka, va, batch, seq):
    nb = seq // MOBA_BLOCK
    assert nb % 2 == 0 and nb <= MOBA_D
    pairs = GROUP // LANES
    whole = pl.BlockSpec((seq, LANES), lambda b, p: (b, p))
    return pl.pallas_call(
        functools.partial(_moba_kernel, nb=nb),
        grid=(batch, pairs),
        in_specs=[whole, whole, whole],
        out_specs=whole,
        out_shape=jax.ShapeDtypeStruct(qa.shape, BF16),
        scratch_shapes=[pltpu.VMEM((nb, LANES), F32),
                        pltpu.VMEM((2, seq, LANES), BF16),
                        pltpu.VMEM((2, MOBA_VT_ROWS, seq), BF16),
                        pltpu.VMEM((nb, 2, MOBA_BLOCK, LANES), BF16),
                        pltpu.VMEM((2, 2, (nb + 1) * MOBA_BLOCK, MOBA_BLOCK), F32)],
        compiler_params=pltpu.CompilerParams(dimension_semantics=("parallel", "parallel"),
                                             vmem_limit_bytes=VMEM_LIMIT),
        name="moba",
    )(qa, ka, va)


def _hgrn_consts():
    c = HGRN_CHUNK
    t = np.arange(c)
    w_rows, masks = [], []
    for lvl in range(HGRN_LEVELS):
        m = 1 << lvl
        blk = t // m
        odd = (blk % 2) == 1
        start = blk * m
        end = start + m - 1
        u = t[None, :]
        w_odd = (u >= start[:, None]) & (u <= t[:, None])
        w_even = (u > t[:, None]) & (u <= end[:, None])
        w_rows.append(np.where(odd[:, None], w_odd, w_even))
        masks.append(odd[:, None] & (blk[None, :] == blk[:, None] - 1))
    masks.append(t[None, :] == t[:, None])
    w_mm = w_rows[:HGRN_MM_LEVELS] + [t[None, :] <= t[:, None]]
    w_all = np.concatenate(w_mm, axis=0).astype(np.float32)
    w_all = np.concatenate([w_all, w_all], axis=1)
    mask_cat = np.concatenate([masks[-1]] + masks[:-1] + [np.zeros((c, c), bool)], axis=1).astype(np.float32)
    return jnp.asarray(w_all, BF16), jnp.asarray(mask_cat, F32)


def _hgrn_kernel(qh_ref, fh_ref, ih_ref, logit_ref, gain_ref, w_ref, mask_ref, o_ref, state_ref, *, layer):
    c = HGRN_CHUNK
    state_ref[...] = jnp.zeros_like(state_ref)

    if layer > 0:
        lg = logit_ref[...]
        e = jnp.exp(lg - jnp.max(lg, axis=0, keepdims=True))
        sm = e / jnp.sum(e, axis=0, keepdims=True)
        lb = jnp.sum(sm[1:layer + 1, :], axis=0, keepdims=True)
        log_lb = jnp.log(lb)
        log_1m_lb = jnp.log1p(-lb)
    w_all = w_ref[...]
    gain = gain_ref[...]

    def prepare(base, ci):
        rows = pl.ds(base + ci * c, c)
        fl = fh_ref[rows, :]
        vi = ih_ref[rows, :]
        qf = _silu(qh_ref[rows, :].astype(F32))
        e = jnp.exp(-jnp.abs(fl))
        r = 1.0 / (1.0 + e)
        log_sig = jnp.minimum(fl, 0.0) - jnp.log(1.0 + e)
        sig_neg = jnp.where(fl >= 0.0, e * r, r)
        if layer > 0:
            b = log_1m_lb + log_sig
            lf = jnp.maximum(log_lb, b) + jnp.log(1.0 + jnp.exp(-jnp.abs(log_lb - b)))
            kf = (1.0 - lb) * sig_neg
        else:
            lf = log_sig
            kf = sig_neg

        f1, f2 = _split2(lf * LOG2E)
        x_mm = _nn(w_all, jnp.concatenate([f1, f2], axis=0))
        a2 = x_mm[HGRN_MM_LEVELS * c:(HGRN_MM_LEVELS + 1) * c, :]
        e_lvl = [jnp.exp2(x_mm[lvl * c:(lvl + 1) * c, :]) for lvl in range(HGRN_MM_LEVELS)]
        for lvl in range(HGRN_MM_LEVELS, HGRN_LEVELS):
            m = 1 << lvl
            parts = []
            for g in range(c // (2 * m)):
                ref = a2[2 * m * g + m - 1:2 * m * g + m, :]
                parts.append(-jnp.abs(a2[2 * m * g:2 * m * (g + 1), :] - ref))
            e_lvl.append(jnp.exp2(jnp.concatenate(parts, axis=0)))
        e_cum = jnp.exp2(a2)
        e_rest = jnp.exp2(a2[c - 1:c, :] - a2)
        e_end = e_cum[c - 1:c, :]
        return qf, kf, vi, e_lvl, e_cum, e_rest, e_end

    def recur(base, ci, prepared):
        rows = pl.ds(base + ci * c, c)
        qf, kf, vi, e_lvl, e_cum, e_rest, e_end = prepared
        for h in range(HGRN_HEADS):
            cols = slice(h * HGRN_D, (h + 1) * HGRN_D)
            qfh, kfh = qf[:, cols], kf[:, cols]
            qb = [qfh.astype(BF16)] + [(qfh * e[:, cols]).astype(BF16) for e in e_lvl]
            kb = [kfh.astype(BF16)] + [(kfh * e[:, cols]).astype(BF16) for e in e_lvl]
            zero = jnp.zeros((c, HGRN_D), BF16)
            s_parts = []
            for a in range(0, HGRN_LEVELS, 2):
                q_cat = jnp.concatenate([qb[a], qb[a + 1]], axis=1)
                k_blk = jnp.concatenate([jnp.concatenate([kb[a], zero], axis=1),
                                         jnp.concatenate([zero, kb[a + 1]], axis=1)], axis=0)
                s_parts.append(_nt(q_cat, k_blk))
            s_parts.append(_nt(qb[HGRN_LEVELS], jnp.concatenate([kb[HGRN_LEVELS], zero], axis=0)))
            p = (jnp.concatenate(s_parts, axis=1) * mask_ref[...]).astype(BF16)
            vb = vi[:, cols]
            st = state_ref[:, cols]
            o = _nn(p, jnp.concatenate([vb] * (HGRN_LEVELS + 2), axis=0))
            o = o + _nt((qfh * e_cum[:, cols]).astype(BF16), st.astype(BF16))
            state_ref[:, cols] = st * e_end[:, cols] + _tn(vb, (kfh * e_rest[:, cols]).astype(BF16))
            ms = jnp.mean(o * o, axis=-1, keepdims=True)
            o_ref[rows, cols] = (o * lax.rsqrt(ms + NORM_EPS) * gain[:, cols]).astype(BF16)

    n_chunks = HGRN_TILE // c

    def group(g, carry):
        base = pl.multiple_of(g * HGRN_TILE, HGRN_TILE)
        prepared = prepare(base, 0)
        for ci in range(n_chunks):
            nxt = prepare(base, ci + 1) if ci + 1 < n_chunks else None
            recur(base, ci, prepared)
            prepared = nxt
        return carry

    lax.fori_loop(0, qh_ref.shape[0] // HGRN_TILE, group, 0)


def _hgrn(qh, fh, ih, logits, gain, layer, batch, seq):
    w_all, mask_all = _hgrn_consts()
    assert seq % HGRN_TILE == 0
    depth = logits.shape[0]
    const = lambda b: (0, 0)
    return pl.pallas_call(
        functools.partial(_hgrn_kernel, layer=layer),
        grid=(batch,),
        in_specs=[pl.BlockSpec((seq, GROUP), lambda b: (b, 0))] * 3 + [
                  pl.BlockSpec((depth, GROUP), const),
                  pl.BlockSpec((1, GROUP), const),
                  pl.BlockSpec(w_all.shape, const),
                  pl.BlockSpec(mask_all.shape, const)],
        out_specs=pl.BlockSpec((seq, GROUP), lambda b: (b, 0)),
        out_shape=jax.ShapeDtypeStruct((fh.shape[0], GROUP), BF16),
        scratch_shapes=[pltpu.VMEM((HGRN_D, GROUP), F32)],
        compiler_params=pltpu.CompilerParams(dimension_semantics=("parallel",),
                                             vmem_limit_bytes=VMEM_LIMIT),
        name="hgrn",
    )(qh, fh, ih, logits, gain, w_all, mask_all)


def _mem_kv_kernel(m_ref, g_ref, w_ref, gk_ref, seg_ref, k_ref, v_ref):
    x = m_ref[...]
    ms = jnp.mean(x * x, axis=-1, keepdims=True)
    hb = (x * lax.rsqrt(ms + NORM_EPS) * g_ref[...]).astype(BF16)
    gk = gk_ref[...]
    seg = seg_ref[...]
    for c in range(GROUP // 256):
        kc = _nn(hb, w_ref[:, c * 256:(c + 1) * 256])
        ss = _nn((kc * kc).astype(BF16), seg) * (1.0 / MEM_D)
        kn = kc * lax.rsqrt(ss + NORM_EPS) * gk[:, c * 256:(c + 1) * 256]
        k_ref[:, c * 256:(c + 1) * 256] = kn.astype(BF16)
    v_ref[...] = _nn(hb, w_ref[:, GROUP:2 * GROUP]).astype(BF16)


def _mem_kv(memf, g, w_bf, gk, mem_len):
    n = memf.shape[0]
    seg = np.arange(256)[:, None] // MEM_D == np.arange(256)[None, :] // MEM_D
    seg = jnp.asarray(seg, BF16)
    const = lambda i: (0, 0)
    row = lambda i: (i, 0)
    return pl.pallas_call(
        _mem_kv_kernel,
        grid=(n // mem_len,),
        in_specs=[pl.BlockSpec((mem_len, D_MODEL), row),
                  pl.BlockSpec((1, D_MODEL), const),
                  pl.BlockSpec((D_MODEL, 2 * GROUP), const),
                  pl.BlockSpec((1, GROUP), const),
                  pl.BlockSpec((256, 256), const)],
        out_specs=[pl.BlockSpec((mem_len, GROUP), row)] * 2,
        out_shape=[jax.ShapeDtypeStruct((n, GROUP), BF16)] * 2,
        compiler_params=pltpu.CompilerParams(dimension_semantics=("parallel",), vmem_limit_bytes=VMEM_LIMIT),
        name="mem_kv",
    )(memf, g, w_bf, gk, seg)


def _out_proj_kernel(x_ref, oa_ref, oh_ref, qm_ref, z_ref, km_ref, vm_ref, gq_ref, w_ref, o_ref, y_ref):
    def gate(o, c0, c1):
        y_ref[:, c0:c1] = (o * _silu(z_ref[:, c0:c1].astype(F32))).astype(BF16)

    gate(oa_ref[...].astype(F32), 0, GROUP)
    gate(oh_ref[...].astype(F32), GROUP, 2 * GROUP)
    gq = gq_ref[...]
    for h in range(MEM_HEADS):
        cols = slice(h * MEM_D, (h + 1) * MEM_D)
        q = qm_ref[:, cols].astype(F32)
        ms = jnp.mean(q * q, axis=-1, keepdims=True)
        qn = q * lax.rsqrt(ms + NORM_EPS) * gq * (MEM_D ** -0.5)
        s = _nt(qn.astype(BF16), km_ref[:, cols])
        p = jnp.exp(s - jnp.max(s, axis=1, keepdims=True))
        om = _nn(p.astype(BF16), vm_ref[:, cols]) / jnp.sum(p, axis=1, keepdims=True)
        c0 = 2 * GROUP + h * MEM_D
        gate(om, c0, c0 + MEM_D)
    o_ref[...] = x_ref[...] + _nn(y_ref[...], w_ref[...])


def _out_proj(xf, oa, oh, qm, z, km, vm, gq, w_bf, seq, mem_len):
    n = xf.shape[0]
    tm = ROW_TILE
    steps = seq // tm
    const = lambda i: (0, 0)
    row = lambda i: (i, 0)
    per_batch = lambda i: (i // steps, 0)
    return pl.pallas_call(
        _out_proj_kernel,
        grid=(n // tm,),
        in_specs=[pl.BlockSpec((tm, D_MODEL), row),
                  pl.BlockSpec((tm, GROUP), row),
                  pl.BlockSpec((tm, GROUP), row),
                  pl.BlockSpec((tm, GROUP), row),
                  pl.BlockSpec((tm, D_MIX), row),
                  pl.BlockSpec((mem_len, GROUP), per_batch),
                  pl.BlockSpec((mem_len, GROUP), per_batch),
                  pl.BlockSpec((1, MEM_D), const),
                  pl.BlockSpec((D_MIX, D_MODEL), const)],
        out_specs=pl.BlockSpec((tm, D_MODEL), row),
        out_shape=jax.ShapeDtypeStruct((n, D_MODEL), F32),
        scratch_shapes=[pltpu.VMEM((tm, D_MIX), BF16)],
        compiler_params=pltpu.CompilerParams(dimension_semantics=("parallel",), vmem_limit_bytes=VMEM_LIMIT),
        name="out_proj",
    )(xf, oa, oh, qm, z, km, vm, gq, w_bf)


def kernel(x, mem, positions, norm_g, w_in, w_out, moba_q_norm, moba_k_norm, hgrn_lb_logits, hgrn_o_norm,
           mem_norm_g, w_mem_kv, mem_q_norm, mem_k_norm):
    batch, seq, d_model = x.shape
    mem_len = mem.shape[1]
    depth = w_in.shape[0]
    assert d_model == D_MODEL and seq % ROW_TILE == 0 and mem_len % 8 == 0
    n = batch * seq
    xf = x.reshape(n, d_model)
    memf = mem.reshape(batch * mem_len, d_model)
    cosf, sinf = _rope_tables(positions.reshape(n, 1))
    for l in range(depth):
        gq = jnp.tile(moba_q_norm[l], GROUP // MOBA_D)[None, :]
        gk = jnp.tile(moba_k_norm[l], GROUP // MOBA_D)[None, :]
        qa, ka, va, qh, fh, ih, qm, z = _in_proj(xf, norm_g[l][None, :], w_in[l].astype(BF16), cosf, sinf, gq, gk)
        oa = _moba(qa, ka, va, batch, seq)
        oh = _hgrn(qh, fh, ih, hgrn_lb_logits, jnp.tile(hgrn_o_norm[l], HGRN_HEADS)[None, :], l, batch, seq)
        km, vm = _mem_kv(memf, mem_norm_g[l][None, :], w_mem_kv[l].astype(BF16),
                         jnp.tile(mem_k_norm[l], MEM_HEADS)[None, :], mem_len)
        xf = _out_proj(xf, oa, oh, qm, z, km, vm, mem_q_norm[l][None, :], w_out[l].astype(BF16), seq, mem_len)
    return xf.reshape(batch, seq, d_model)
```

```python
import functools

import numpy as np
import jax
import jax.numpy as jnp
from jax import lax
from jax.experimental import pallas as pl
from jax.experimental.pallas import tpu as pltpu

F32 = jnp.float32
BF16 = jnp.bfloat16

D_MODEL = 1024
GROUP = D_MODEL // 2
D_MIX = 3 * GROUP
IN_COLS = 7 * GROUP + D_MIX
MOBA_D = 64
MOBA_BLOCK = 256
MOBA_TOPK = 3
MOBA_VT_ROWS = MOBA_D + 16
HGRN_D = 128
HGRN_HEADS = GROUP // HGRN_D
HGRN_CHUNK = 64
HGRN_LEVELS = 6
HGRN_MM_LEVELS = 3
LOG2E = 1.4426950408889634
MEM_D = 128
MEM_HEADS = GROUP // MEM_D
ROPE_THETA = 500000.0
ROPE_DIM = MOBA_D // 4
ROPE_HALF = ROPE_DIM // 2
NORM_EPS = 1e-6
LANES = 128
ROW_TILE = 512
HGRN_TILE = 256
MASKED = -1e30
VMEM_LIMIT = 48 * 1024 * 1024


def _nt(a, b):
    return lax.dot_general(a, b, (((1,), (1,)), ((), ())), preferred_element_type=F32)


def _tn(a, b):
    return lax.dot_general(a, b, (((0,), (0,)), ((), ())), preferred_element_type=F32)


def _nn(a, b):
    return jnp.dot(a, b, preferred_element_type=F32)


def _split2(x):
    hi = x.astype(BF16)
    lo = (x - hi.astype(F32)).astype(BF16)
    return hi, lo


def _sigmoid(x):
    return 1.0 / (1.0 + jnp.exp(-x))


def _silu(x):
    return x * _sigmoid(x)


def _rope_table_kernel(pos_ref, invf_ref, sgn_ref, cos_ref, sin_ref):
    ang = pos_ref[...].astype(F32) * invf_ref[...]
    sgn = sgn_ref[...]
    on = sgn != 0.0
    cos_ref[...] = jnp.where(on, jnp.cos(ang), 1.0)
    sin_ref[...] = sgn * jnp.sin(ang)


def _rope_tables(pos):
    n = pos.shape[0]
    tm = min(n, 2048)
    lane = np.arange(LANES) % MOBA_D
    invf = np.where(lane < ROPE_DIM, ROPE_THETA ** (-(lane % ROPE_HALF).astype(np.float64) / ROPE_HALF), 0.0)
    sgn = np.where(lane < ROPE_HALF, -1.0, np.where(lane < ROPE_DIM, 1.0, 0.0))
    invf = jnp.asarray(invf[None, :], F32)
    sgn = jnp.asarray(sgn[None, :], F32)
    return pl.pallas_call(
        _rope_table_kernel,
        grid=(n // tm,),
        in_specs=[pl.BlockSpec((tm, 1), lambda i: (i, 0)),
                  pl.BlockSpec((1, LANES), lambda i: (0, 0)),
                  pl.BlockSpec((1, LANES), lambda i: (0, 0))],
        out_specs=[pl.BlockSpec((tm, LANES), lambda i: (i, 0))] * 2,
        out_shape=[jax.ShapeDtypeStruct((n, LANES), F32)] * 2,
        name="rope_tables",
    )(pos, invf, sgn)


def _in_proj_kernel(x_ref, g_ref, w_ref, cos_ref, sin_ref, gq_ref, gk_ref, seg_ref,
                    qa_ref, ka_ref, va_ref, qh_ref, fh_ref, ih_ref, qm_ref, z_ref):
    x = x_ref[...]
    ms = jnp.mean(x * x, axis=-1, keepdims=True)
    hb = (x * lax.rsqrt(ms + NORM_EPS) * g_ref[...]).astype(BF16)

    def proj(c0, width):
        return _nn(hb, w_ref[:, c0:c0 + width].astype(BF16))

    cosf = cos_ref[...]
    sinf = sin_ref[...]
    lane = lax.broadcasted_iota(jnp.int32, (1, LANES), 1) % MOBA_D
    lo_lane = lane < ROPE_HALF
    seg = seg_ref[...]

    def head_norm_rope(p, gain, out_ref):
        for c in range(GROUP // 256):
            pc = p[:, c * 256:(c + 1) * 256]
            ss = _nn((pc * pc).astype(BF16), seg) * (1.0 / MOBA_D)
            pn = pc * lax.rsqrt(ss + NORM_EPS) * gain[:, c * 256:(c + 1) * 256]
            for v in range(2):
                xv = pn[:, v * LANES:(v + 1) * LANES]
                rot = jnp.where(lo_lane, pltpu.roll(xv, LANES - ROPE_HALF, 1), pltpu.roll(xv, ROPE_HALF, 1))
                col = c * 256 + v * LANES
                out_ref[:, col:col + LANES] = (xv * cosf + rot * sinf).astype(out_ref.dtype)

    head_norm_rope(proj(0, GROUP), gq_ref[...], qa_ref)
    head_norm_rope(proj(GROUP, GROUP), gk_ref[...], ka_ref)
    va_ref[...] = proj(2 * GROUP, GROUP).astype(BF16)
    qh_ref[...] = proj(3 * GROUP, GROUP).astype(BF16)
    fh_ref[...] = proj(4 * GROUP, GROUP)
    ih_ref[...] = proj(5 * GROUP, GROUP).astype(BF16)
    qm_ref[...] = proj(6 * GROUP, GROUP).astype(BF16)
    for c in range(3):
        z_ref[:, c * GROUP:(c + 1) * GROUP] = proj((7 + c) * GROUP, GROUP).astype(BF16)


def _in_proj(xf, g, w, cosf, sinf, gq, gk, layer):
    n = xf.shape[0]
    tm = ROW_TILE
    seg = np.arange(256)[:, None] // MOBA_D == np.arange(256)[None, :] // MOBA_D
    seg = jnp.asarray(seg, BF16)
    const = lambda i: (0, 0)
    row = lambda i: (i, 0)
    of_layer = lambda i: (layer, 0, 0)
    outs = ((GROUP, F32), (GROUP, BF16), (GROUP, BF16), (GROUP, BF16), (GROUP, F32), (GROUP, BF16),
            (GROUP, BF16), (D_MIX, BF16))
    return pl.pallas_call(
        _in_proj_kernel,
        grid=(n // tm,),
        in_specs=[pl.BlockSpec((tm, D_MODEL), row),
                  pl.BlockSpec((None, 1, D_MODEL), of_layer),
                  pl.BlockSpec((None, D_MODEL, IN_COLS), of_layer, pipeline_mode=pl.Buffered(1)),
                  pl.BlockSpec((tm, LANES), row),
                  pl.BlockSpec((tm, LANES), row),
                  pl.BlockSpec((None, 1, GROUP), of_layer),
                  pl.BlockSpec((None, 1, GROUP), of_layer),
                  pl.BlockSpec((256, 256), const)],
        out_specs=[pl.BlockSpec((tm, w), row) for w, _ in outs],
        out_shape=[jax.ShapeDtypeStruct((n, w), dt) for w, dt in outs],
        compiler_params=pltpu.CompilerParams(dimension_semantics=("parallel",), vmem_limit_bytes=VMEM_LIMIT),
        name="in_proj",
    )(xf, g, w, cosf, sinf, gq, gk, seg)


def _moba_kernel(q_ref, k_ref, v_ref, o_ref, kmean_ref, kaug_ref, vt_ref, qaug_ref, s_ref, *, nb):
    blk = MOBA_BLOCK
    seq = nb * blk
    lane = lax.broadcasted_iota(jnp.int32, (1, LANES), 1)
    heads = [(lane >= h * MOBA_D) & (lane < (h + 1) * MOBA_D) for h in range(2)]
    spare = [(1 - h) * MOBA_D for h in range(2)]

    ones = jnp.ones((MOBA_VT_ROWS - MOBA_D, blk), F32)
    for j in range(nb):
        rows = slice(j * blk, (j + 1) * blk)
        kj = k_ref[rows, :]
        kmean_ref[j:j + 1, :] = jnp.mean(kj.astype(F32), axis=0, keepdims=True)
        v_t = v_ref[rows, :].astype(F32).T
        for h in range(2):
            onehot = jnp.where(lane == spare[h] + j, 1.0, 0.0).astype(BF16)
            kaug_ref[h, rows, :] = jnp.where(heads[h], kj, onehot)
            vt_ref[h, :, rows] = jnp.concatenate([v_t[h * MOBA_D:(h + 1) * MOBA_D], ones], axis=0).astype(BF16)

    km = kmean_ref[...]
    q_hi, q_lo = _split2(q_ref[...])
    km_parts = [_split2(jnp.where(heads[h], km, 0.0)) for h in range(2)]
    g_hi = _nt(jnp.concatenate([km_parts[0][0], km_parts[0][1], km_parts[1][0], km_parts[1][1]], axis=0), q_hi)
    g_lo = _nt(jnp.concatenate([km_parts[0][0], km_parts[1][0]], axis=0), q_lo)
    blk_row = lax.broadcasted_iota(jnp.int32, (nb, 1), 0)
    q_blk = lax.broadcasted_iota(jnp.int32, (1, seq), 1) // blk
    past = blk_row < q_blk
    for h in range(2):
        gate_t = g_hi[2 * h * nb:(2 * h + 1) * nb] + g_hi[(2 * h + 1) * nb:(2 * h + 2) * nb] + g_lo[h * nb:(h + 1) * nb]
        gate_t = jnp.where(past, gate_t, -jnp.inf)
        cnt = jnp.zeros((nb, seq), F32)
        for n in range(nb - 1):
            gn = gate_t[n:n + 1, :]
            before = (gn > gate_t) | ((gn == gate_t) & (n < blk_row))
            cnt = cnt + jnp.where(before, 1.0, 0.0)
        sel = (past & (cnt < float(MOBA_TOPK))) | (blk_row == q_blk)
        bias_t = jnp.where(sel, 0.0, MASKED)
        for c in range(nb):
            parts = []
            if spare[h]:
                parts.append(jnp.zeros((spare[h], blk), F32))
            parts.append(bias_t[:, c * blk:(c + 1) * blk])
            parts.append(jnp.zeros((LANES - spare[h] - nb, blk), F32))
            bias = jnp.concatenate(parts, axis=0).T
            qs = q_ref[c * blk:(c + 1) * blk, :] * (MOBA_D ** -0.5 * LOG2E)
            qaug_ref[c, h] = jnp.where(heads[h], qs, bias).astype(BF16)

    kpos = lax.broadcasted_iota(jnp.int32, (blk, blk), 0)
    qpos = lax.broadcasted_iota(jnp.int32, (blk, blk), 1)
    causal_t = kpos <= qpos

    token = None
    for t in range(nb // 2):
        buf = t % 2
        base = 0
        for c in (nb - 1 - t, t):
            outs = []
            for h in range(2):
                qa = qaug_ref[c, h]
                if token is not None:
                    qa = (qa.astype(F32) + token).astype(BF16)
                st = s_ref.at[buf, h]
                mx = None
                for j in range(c + 1):
                    s = _nt(kaug_ref[h, j * blk:(j + 1) * blk, :], qa)
                    if j == 0:
                        bits = pltpu.bitcast(s[0:1, 0:LANES], jnp.uint32)
                        zero = lax.shift_right_logical(lax.shift_right_logical(bits, jnp.uint32(16)), jnp.uint32(16))
                        token = zero.astype(F32)
                    if j == c:
                        s = jnp.where(causal_t, s, MASKED)
                    st[base + j * blk:base + (j + 1) * blk, :] = s
                    mx = s if mx is None else jnp.maximum(mx, s)
                m = jnp.max(mx, axis=0, keepdims=True)
                acc = None
                for j in range(c + 1):
                    rows = slice(j * blk, (j + 1) * blk)
                    p = jnp.exp2(st[base + j * blk:base + (j + 1) * blk, :] - m).astype(BF16)
                    pv = _nn(vt_ref[h, :, rows], p)
                    acc = pv if acc is None else acc + pv
                outs.append(acc[0:MOBA_D] / acc[MOBA_D:MOBA_D + 1])
            o_ref[c * blk:(c + 1) * blk, :] = jnp.concatenate(outs, axis=0).T.astype(BF16)
            base += (c + 1) * blk


def _moba(qa, ka, va, batch, seq):
    nb = seq // MOBA_BLOCK
    assert nb % 2 == 0 and nb <= MOBA_D
    pairs = GROUP // LANES
    whole = pl.BlockSpec((seq, LANES), lambda b, p: (b, p))
    return pl.pallas_call(
        functools.partial(_moba_kernel, nb=nb),
        grid=(batch, pairs),
        in_specs=[whole, whole, whole],
        out_specs=whole,
        out_shape=jax.ShapeDtypeStruct(qa.shape, BF16),
        scratch_shapes=[pltpu.VMEM((nb, LANES), F32),
                        pltpu.VMEM((2, seq, LANES), BF16),
                        pltpu.VMEM((2, MOBA_VT_ROWS, seq), BF16),
                        pltpu.VMEM((nb, 2, MOBA_BLOCK, LANES), BF16),
                        pltpu.VMEM((2, 2, (nb + 1) * MOBA_BLOCK, MOBA_BLOCK), F32)],
        compiler_params=pltpu.CompilerParams(dimension_semantics=("parallel", "parallel"),
                                             vmem_limit_bytes=VMEM_LIMIT),
        name="moba",
    )(qa, ka, va)


def _hgrn_consts():
    c = HGRN_CHUNK
    t = np.arange(c)
    w_rows, masks = [], []
    for lvl in range(HGRN_LEVELS):
        m = 1 << lvl
        blk = t // m
        odd = (blk % 2) == 1
        start = blk * m
        end = start + m - 1
        u = t[None, :]
        w_odd = (u >= start[:, None]) & (u <= t[:, None])
        w_even = (u > t[:, None]) & (u <= end[:, None])
        w_rows.append(np.where(odd[:, None], w_odd, w_even))
        masks.append(odd[:, None] & (blk[None, :] == blk[:, None] - 1))
    masks.append(t[None, :] == t[:, None])
    w_mm = w_rows[:HGRN_MM_LEVELS] + [t[None, :] <= t[:, None]]
    w_all = np.concatenate(w_mm, axis=0).astype(np.float32)
    w_all = np.concatenate([w_all, w_all], axis=1)
    mask_cat = np.concatenate([masks[-1]] + masks[:-1] + [np.zeros((c, c), bool)], axis=1).astype(np.float32)
    return jnp.asarray(w_all, BF16), jnp.asarray(mask_cat, F32)


def _hgrn_kernel(qh_ref, fh_ref, ih_ref, logit_ref, gain_ref, w_ref, mask_ref, o_ref, state_ref, *, layer):
    c = HGRN_CHUNK
    state_ref[...] = jnp.zeros_like(state_ref)

    if layer > 0:
        lg = logit_ref[...]
        e = jnp.exp(lg - jnp.max(lg, axis=0, keepdims=True))
        sm = e / jnp.sum(e, axis=0, keepdims=True)
        lb = jnp.sum(sm[1:layer + 1, :], axis=0, keepdims=True)
        log_lb = jnp.log(lb)
        log_1m_lb = jnp.log1p(-lb)
    w_all = w_ref[...]
    gain = gain_ref[...]

    def prepare(base, ci):
        rows = pl.ds(base + ci * c, c)
        fl = fh_ref[rows, :]
        vi = ih_ref[rows, :]
        qf = _silu(qh_ref[rows, :].astype(F32))
        e = jnp.exp(-jnp.abs(fl))
        r = 1.0 / (1.0 + e)
        log_sig = jnp.minimum(fl, 0.0) - jnp.log(1.0 + e)
        sig_neg = jnp.where(fl >= 0.0, e * r, r)
        if layer > 0:
            b = log_1m_lb + log_sig
            lf = jnp.maximum(log_lb, b) + jnp.log(1.0 + jnp.exp(-jnp.abs(log_lb - b)))
            kf = (1.0 - lb) * sig_neg
        else:
            lf = log_sig
            kf = sig_neg

        f1, f2 = _split2(lf * LOG2E)
        x_mm = _nn(w_all, jnp.concatenate([f1, f2], axis=0))
        a2 = x_mm[HGRN_MM_LEVELS * c:(HGRN_MM_LEVELS + 1) * c, :]
        e_lvl = [jnp.exp2(x_mm[lvl * c:(lvl + 1) * c, :]) for lvl in range(HGRN_MM_LEVELS)]
        for lvl in range(HGRN_MM_LEVELS, HGRN_LEVELS):
            m = 1 << lvl
            parts = []
            for g in range(c // (2 * m)):
                ref = a2[2 * m * g + m - 1:2 * m * g + m, :]
                parts.append(-jnp.abs(a2[2 * m * g:2 * m * (g + 1), :] - ref))
            e_lvl.append(jnp.exp2(jnp.concatenate(parts, axis=0)))
        e_cum = jnp.exp2(a2)
        e_rest = jnp.exp2(a2[c - 1:c, :] - a2)
        e_end = e_cum[c - 1:c, :]
        return qf, kf, vi, e_lvl, e_cum, e_rest, e_end

    def recur(base, ci, prepared):
        rows = pl.ds(base + ci * c, c)
        qf, kf, vi, e_lvl, e_cum, e_rest, e_end = prepared
        for h in range(HGRN_HEADS):
            cols = slice(h * HGRN_D, (h + 1) * HGRN_D)
            qfh, kfh = qf[:, cols], kf[:, cols]
            qb = [qfh.astype(BF16)] + [(qfh * e[:, cols]).astype(BF16) for e in e_lvl]
            kb = [kfh.astype(BF16)] + [(kfh * e[:, cols]).astype(BF16) for e in e_lvl]
            zero = jnp.zeros((c, HGRN_D), BF16)
            s_parts = []
            for a in range(0, HGRN_LEVELS, 2):
                q_cat = jnp.concatenate([qb[a], qb[a + 1]], axis=1)
                k_blk = jnp.concatenate([jnp.concatenate([kb[a], zero], axis=1),
                                         jnp.concatenate([zero, kb[a + 1]], axis=1)], axis=0)
                s_parts.append(_nt(q_cat, k_blk))
            s_parts.append(_nt(qb[HGRN_LEVELS], jnp.concatenate([kb[HGRN_LEVELS], zero], axis=0)))
            p = (jnp.concatenate(s_parts, axis=1) * mask_ref[...]).astype(BF16)
            vb = vi[:, cols]
            st = state_ref[:, cols]
            o = _nn(p, jnp.concatenate([vb] * (HGRN_LEVELS + 2), axis=0))
            o = o + _nt((qfh * e_cum[:, cols]).astype(BF16), st.astype(BF16))
            state_ref[:, cols] = st * e_end[:, cols] + _tn(vb, (kfh * e_rest[:, cols]).astype(BF16))
            ms = jnp.mean(o * o, axis=-1, keepdims=True)
            o_ref[rows, cols] = (o * lax.rsqrt(ms + NORM_EPS) * gain[:, cols]).astype(BF16)

    n_chunks = HGRN_TILE // c

    def group(g, carry):
        base = pl.multiple_of(g * HGRN_TILE, HGRN_TILE)
        prepared = prepare(base, 0)
        for ci in range(n_chunks):
            nxt = prepare(base, ci + 1) if ci + 1 < n_chunks else None
            recur(base, ci, prepared)
            prepared = nxt
        return carry

    lax.fori_loop(0, qh_ref.shape[0] // HGRN_TILE, group, 0)


def _hgrn(qh, fh, ih, logits, gain, layer, batch, seq):
    w_all, mask_all = _hgrn_consts()
    assert seq % HGRN_TILE == 0
    depth = logits.shape[0]
    const = lambda b: (0, 0)
    return pl.pallas_call(
        functools.partial(_hgrn_kernel, layer=layer),
        grid=(batch,),
        in_specs=[pl.BlockSpec((seq, GROUP), lambda b: (b, 0))] * 3 + [
                  pl.BlockSpec((depth, GROUP), const),
                  pl.BlockSpec((None, 1, GROUP), lambda b: (layer, 0, 0)),
                  pl.BlockSpec(w_all.shape, const),
                  pl.BlockSpec(mask_all.shape, const)],
        out_specs=pl.BlockSpec((seq, GROUP), lambda b: (b, 0)),
        out_shape=jax.ShapeDtypeStruct((fh.shape[0], GROUP), BF16),
        scratch_shapes=[pltpu.VMEM((HGRN_D, GROUP), F32)],
        compiler_params=pltpu.CompilerParams(dimension_semantics=("parallel",),
                                             vmem_limit_bytes=VMEM_LIMIT),
        name="hgrn",
    )(qh, fh, ih, logits, gain, w_all, mask_all)


def _mem_kv_kernel(m_ref, g_ref, w_ref, gk_ref, seg_ref, k_ref, v_ref):
    x = m_ref[...]
    ms = jnp.mean(x * x, axis=-1, keepdims=True)
    hb = (x * lax.rsqrt(ms + NORM_EPS) * g_ref[...]).astype(BF16)
    gk = gk_ref[...]
    seg = seg_ref[...]
    for c in range(GROUP // 256):
        kc = _nn(hb, w_ref[:, c * 256:(c + 1) * 256].astype(BF16))
        ss = _nn((kc * kc).astype(BF16), seg) * (1.0 / MEM_D)
        kn = kc * lax.rsqrt(ss + NORM_EPS) * gk[:, c * 256:(c + 1) * 256]
        k_ref[:, c * 256:(c + 1) * 256] = kn.astype(BF16)
    v_ref[...] = _nn(hb, w_ref[:, GROUP:2 * GROUP].astype(BF16)).astype(BF16)


def _mem_kv(memf, g, w, gk, mem_len, layer):
    n = memf.shape[0]
    seg = np.arange(256)[:, None] // MEM_D == np.arange(256)[None, :] // MEM_D
    seg = jnp.asarray(seg, BF16)
    const = lambda i: (0, 0)
    row = lambda i: (i, 0)
    of_layer = lambda i: (layer, 0, 0)
    return pl.pallas_call(
        _mem_kv_kernel,
        grid=(n // mem_len,),
        in_specs=[pl.BlockSpec((mem_len, D_MODEL), row),
                  pl.BlockSpec((None, 1, D_MODEL), of_layer),
                  pl.BlockSpec((None, D_MODEL, 2 * GROUP), of_layer),
                  pl.BlockSpec((None, 1, GROUP), of_layer),
                  pl.BlockSpec((256, 256), const)],
        out_specs=[pl.BlockSpec((mem_len, GROUP), row)] * 2,
        out_shape=[jax.ShapeDtypeStruct((n, GROUP), BF16)] * 2,
        compiler_params=pltpu.CompilerParams(dimension_semantics=("parallel",), vmem_limit_bytes=VMEM_LIMIT),
        name="mem_kv",
    )(memf, g, w, gk, seg)


def _out_proj_kernel(x_ref, oa_ref, oh_ref, qm_ref, z_ref, km_ref, vm_ref, gq_ref, w_ref, o_ref, y_ref):
    def gate(o, c0, c1):
        y_ref[:, c0:c1] = (o * _silu(z_ref[:, c0:c1].astype(F32))).astype(BF16)

    gate(oa_ref[...].astype(F32), 0, GROUP)
    gate(oh_ref[...].astype(F32), GROUP, 2 * GROUP)
    gq = gq_ref[...]
    for h in range(MEM_HEADS):
        cols = slice(h * MEM_D, (h + 1) * MEM_D)
        q = qm_ref[:, cols].astype(F32)
        ms = jnp.mean(q * q, axis=-1, keepdims=True)
        qn = q * lax.rsqrt(ms + NORM_EPS) * gq * (MEM_D ** -0.5)
        s = _nt(qn.astype(BF16), km_ref[:, cols])
        p = jnp.exp(s - jnp.max(s, axis=1, keepdims=True))
        om = _nn(p.astype(BF16), vm_ref[:, cols]) / jnp.sum(p, axis=1, keepdims=True)
        c0 = 2 * GROUP + h * MEM_D
        gate(om, c0, c0 + MEM_D)
    o_ref[...] = x_ref[...] + _nn(y_ref[...], w_ref[...].astype(BF16))


def _out_proj(xf, oa, oh, qm, z, km, vm, gq, w, seq, mem_len, layer):
    n = xf.shape[0]
    tm = ROW_TILE
    steps = seq // tm
    const = lambda i: (0, 0)
    row = lambda i: (i, 0)
    per_batch = lambda i: (i // steps, 0)
    of_layer = lambda i: (layer, 0, 0)
    return pl.pallas_call(
        _out_proj_kernel,
        grid=(n // tm,),
        in_specs=[pl.BlockSpec((tm, D_MODEL), row),
                  pl.BlockSpec((tm, GROUP), row),
                  pl.BlockSpec((tm, GROUP), row),
                  pl.BlockSpec((tm, GROUP), row),
                  pl.BlockSpec((tm, D_MIX), row),
                  pl.BlockSpec((mem_len, GROUP), per_batch),
                  pl.BlockSpec((mem_len, GROUP), per_batch),
                  pl.BlockSpec((None, 1, MEM_D), of_layer),
                  pl.BlockSpec((None, D_MIX, D_MODEL), of_layer, pipeline_mode=pl.Buffered(1))],
        out_specs=pl.BlockSpec((tm, D_MODEL), row),
        out_shape=jax.ShapeDtypeStruct((n, D_MODEL), F32),
        scratch_shapes=[pltpu.VMEM((tm, D_MIX), BF16)],
        compiler_params=pltpu.CompilerParams(dimension_semantics=("parallel",), vmem_limit_bytes=VMEM_LIMIT),
        name="out_proj",
    )(xf, oa, oh, qm, z, km, vm, gq, w)


def kernel(x, mem, positions, norm_g, w_in, w_out, moba_q_norm, moba_k_norm, hgrn_lb_logits, hgrn_o_norm,
           mem_norm_g, w_mem_kv, mem_q_norm, mem_k_norm):
    batch, seq, d_model = x.shape
    mem_len = mem.shape[1]
    depth = w_in.shape[0]
    assert d_model == D_MODEL and seq % ROW_TILE == 0 and mem_len % 8 == 0
    n = batch * seq
    xf = x.reshape(n, d_model)
    memf = mem.reshape(batch * mem_len, d_model)
    cosf, sinf = _rope_tables(positions.reshape(n, 1))
    per_head = lambda p, reps: jnp.tile(p, (1, reps))[:, None, :]
    norm_g, mem_norm_g, mem_q_norm = norm_g[:, None, :], mem_norm_g[:, None, :], mem_q_norm[:, None, :]
    gq, gk = per_head(moba_q_norm, GROUP // MOBA_D), per_head(moba_k_norm, GROUP // MOBA_D)
    g_hgrn, g_memk = per_head(hgrn_o_norm, HGRN_HEADS), per_head(mem_k_norm, MEM_HEADS)
    for l in range(depth):
        qa, ka, va, qh, fh, ih, qm, z = _in_proj(xf, norm_g, w_in, cosf, sinf, gq, gk, l)
        oa = _moba(qa, ka, va, batch, seq)
        oh = _hgrn(qh, fh, ih, hgrn_lb_logits, g_hgrn, l, batch, seq)
        km, vm = _mem_kv(memf, mem_norm_g, w_mem_kv, g_memk, mem_len, l)
        xf = _out_proj(xf, oa, oh, qm, z, km, vm, mem_q_norm, w_out, seq, mem_len, l)
    return xf.reshape(batch, seq, d_model)
```

```python
import functools

import numpy as np
import jax
import jax.numpy as jnp
from jax import lax
from jax.experimental import pallas as pl
from jax.experimental.pallas import tpu as pltpu

F32 = jnp.float32
BF16 = jnp.bfloat16

D_MODEL = 1024
GROUP = D_MODEL // 2
D_MIX = 3 * GROUP
IN_COLS = 7 * GROUP + D_MIX
MOBA_D = 64
MOBA_BLOCK = 256
MOBA_TOPK = 3
MOBA_VT_ROWS = MOBA_D + 16
HGRN_D = 128
HGRN_HEADS = GROUP // HGRN_D
HGRN_CHUNK = 64
HGRN_LEVELS = 6
HGRN_MM_LEVELS = 3
LOG2E = 1.4426950408889634
MEM_D = 128
MEM_HEADS = GROUP // MEM_D
ROPE_THETA = 500000.0
ROPE_DIM = MOBA_D // 4
ROPE_HALF = ROPE_DIM // 2
NORM_EPS = 1e-6
LANES = 128
ROW_TILE = 512
OUT_TILE = 1024
HGRN_TILE = 256
MASKED = -1e30
VMEM_LIMIT = 48 * 1024 * 1024


def _nt(a, b):
    return lax.dot_general(a, b, (((1,), (1,)), ((), ())), preferred_element_type=F32)


def _tn(a, b):
    return lax.dot_general(a, b, (((0,), (0,)), ((), ())), preferred_element_type=F32)


def _nn(a, b):
    return jnp.dot(a, b, preferred_element_type=F32)


def _split2(x):
    hi = x.astype(BF16)
    lo = (x - hi.astype(F32)).astype(BF16)
    return hi, lo


def _sigmoid(x):
    return 1.0 / (1.0 + jnp.exp(-x))


def _silu(x):
    return x * _sigmoid(x)


def _rope_table_kernel(pos_ref, invf_ref, sgn_ref, cos_ref, sin_ref):
    ang = pos_ref[...].astype(F32) * invf_ref[...]
    sgn = sgn_ref[...]
    on = sgn != 0.0
    cos_ref[...] = jnp.where(on, jnp.cos(ang), 1.0)
    sin_ref[...] = sgn * jnp.sin(ang)


def _rope_tables(pos):
    n = pos.shape[0]
    tm = min(n, 2048)
    lane = np.arange(LANES) % MOBA_D
    invf = np.where(lane < ROPE_DIM, ROPE_THETA ** (-(lane % ROPE_HALF).astype(np.float64) / ROPE_HALF), 0.0)
    sgn = np.where(lane < ROPE_HALF, -1.0, np.where(lane < ROPE_DIM, 1.0, 0.0))
    invf = jnp.asarray(invf[None, :], F32)
    sgn = jnp.asarray(sgn[None, :], F32)
    return pl.pallas_call(
        _rope_table_kernel,
        grid=(n // tm,),
        in_specs=[pl.BlockSpec((tm, 1), lambda i: (i, 0)),
                  pl.BlockSpec((1, LANES), lambda i: (0, 0)),
                  pl.BlockSpec((1, LANES), lambda i: (0, 0))],
        out_specs=[pl.BlockSpec((tm, LANES), lambda i: (i, 0))] * 2,
        out_shape=[jax.ShapeDtypeStruct((n, LANES), F32)] * 2,
        name="rope_tables",
    )(pos, invf, sgn)


def _in_proj_kernel(x_ref, g_ref, w_ref, cos_ref, sin_ref, gq_ref, gk_ref, seg_ref,
                    qa_ref, ka_ref, va_ref, qh_ref, fh_ref, ih_ref, qm_ref, z_ref):
    x = x_ref[...]
    ms = jnp.mean(x * x, axis=-1, keepdims=True)
    hb = (x * lax.rsqrt(ms + NORM_EPS) * g_ref[...]).astype(BF16)

    def proj(c0, width):
        return _nn(hb, w_ref[:, c0:c0 + width].astype(BF16))

    cosf = cos_ref[...]
    sinf = sin_ref[...]
    lane = lax.broadcasted_iota(jnp.int32, (1, LANES), 1) % MOBA_D
    lo_lane = lane < ROPE_HALF
    seg = seg_ref[...]

    def head_norm_rope(p, gain, out_ref):
        for c in range(GROUP // 256):
            pc = p[:, c * 256:(c + 1) * 256]
            ss = _nn((pc * pc).astype(BF16), seg) * (1.0 / MOBA_D)
            pn = pc * lax.rsqrt(ss + NORM_EPS) * gain[:, c * 256:(c + 1) * 256]
            for v in range(2):
                xv = pn[:, v * LANES:(v + 1) * LANES]
                rot = jnp.where(lo_lane, pltpu.roll(xv, LANES - ROPE_HALF, 1), pltpu.roll(xv, ROPE_HALF, 1))
                col = c * 256 + v * LANES
                out_ref[:, col:col + LANES] = (xv * cosf + rot * sinf).astype(out_ref.dtype)

    head_norm_rope(proj(0, GROUP), gq_ref[...], qa_ref)
    head_norm_rope(proj(GROUP, GROUP), gk_ref[...], ka_ref)
    va_ref[...] = proj(2 * GROUP, GROUP).astype(BF16)
    qh_ref[...] = proj(3 * GROUP, GROUP).astype(BF16)
    fh_ref[...] = proj(4 * GROUP, GROUP)
    ih_ref[...] = proj(5 * GROUP, GROUP).astype(BF16)
    qm_ref[...] = proj(6 * GROUP, GROUP).astype(BF16)
    for c in range(3):
        z_ref[:, c * GROUP:(c + 1) * GROUP] = proj((7 + c) * GROUP, GROUP).astype(BF16)


def _in_proj(xf, g, w, cosf, sinf, gq, gk, layer):
    n = xf.shape[0]
    tm = ROW_TILE
    seg = np.arange(256)[:, None] // MOBA_D == np.arange(256)[None, :] // MOBA_D
    seg = jnp.asarray(seg, BF16)
    const = lambda i: (0, 0)
    row = lambda i: (i, 0)
    of_layer = lambda i: (layer, 0, 0)
    outs = ((GROUP, F32), (GROUP, BF16), (GROUP, BF16), (GROUP, BF16), (GROUP, F32), (GROUP, BF16),
            (GROUP, BF16), (D_MIX, BF16))
    return pl.pallas_call(
        _in_proj_kernel,
        grid=(n // tm,),
        in_specs=[pl.BlockSpec((tm, D_MODEL), row),
                  pl.BlockSpec((None, 1, D_MODEL), of_layer),
                  pl.BlockSpec((None, D_MODEL, IN_COLS), of_layer, pipeline_mode=pl.Buffered(1)),
                  pl.BlockSpec((tm, LANES), row),
                  pl.BlockSpec((tm, LANES), row),
                  pl.BlockSpec((None, 1, GROUP), of_layer),
                  pl.BlockSpec((None, 1, GROUP), of_layer),
                  pl.BlockSpec((256, 256), const)],
        out_specs=[pl.BlockSpec((tm, w), row) for w, _ in outs],
        out_shape=[jax.ShapeDtypeStruct((n, w), dt) for w, dt in outs],
        compiler_params=pltpu.CompilerParams(dimension_semantics=("parallel",), vmem_limit_bytes=VMEM_LIMIT),
        name="in_proj",
    )(xf, g, w, cosf, sinf, gq, gk, seg)


def _moba_kernel(q_ref, k_ref, v_ref, o_ref, kmean_ref, kaug_ref, vt_ref, qaug_ref, s_ref, *, nb):
    blk = MOBA_BLOCK
    seq = nb * blk
    lane = lax.broadcasted_iota(jnp.int32, (1, LANES), 1)
    heads = [(lane >= h * MOBA_D) & (lane < (h + 1) * MOBA_D) for h in range(2)]
    spare = [(1 - h) * MOBA_D for h in range(2)]

    ones = jnp.ones((MOBA_VT_ROWS - MOBA_D, blk), F32)
    for j in range(nb):
        rows = slice(j * blk, (j + 1) * blk)
        kj = k_ref[rows, :]
        kmean_ref[j:j + 1, :] = jnp.mean(kj.astype(F32), axis=0, keepdims=True)
        v_t = v_ref[rows, :].astype(F32).T
        for h in range(2):
            onehot = jnp.where(lane == spare[h] + j, 1.0, 0.0).astype(BF16)
            kaug_ref[h, rows, :] = jnp.where(heads[h], kj, onehot)
            vt_ref[h, :, rows] = jnp.concatenate([v_t[h * MOBA_D:(h + 1) * MOBA_D], ones], axis=0).astype(BF16)

    km = kmean_ref[...]
    q_hi, q_lo = _split2(q_ref[...])
    km_parts = [_split2(jnp.where(heads[h], km, 0.0)) for h in range(2)]
    g_hi = _nt(jnp.concatenate([km_parts[0][0], km_parts[0][1], km_parts[1][0], km_parts[1][1]], axis=0), q_hi)
    g_lo = _nt(jnp.concatenate([km_parts[0][0], km_parts[1][0]], axis=0), q_lo)
    blk_row = lax.broadcasted_iota(jnp.int32, (nb, 1), 0)
    q_blk = lax.broadcasted_iota(jnp.int32, (1, seq), 1) // blk
    past = blk_row < q_blk
    for h in range(2):
        gate_t = g_hi[2 * h * nb:(2 * h + 1) * nb] + g_hi[(2 * h + 1) * nb:(2 * h + 2) * nb] + g_lo[h * nb:(h + 1) * nb]
        gate_t = jnp.where(past, gate_t, -jnp.inf)
        cnt = jnp.zeros((nb, seq), F32)
        for n in range(nb - 1):
            gn = gate_t[n:n + 1, :]
            before = (gn > gate_t) | ((gn == gate_t) & (n < blk_row))
            cnt = cnt + jnp.where(before, 1.0, 0.0)
        sel = (past & (cnt < float(MOBA_TOPK))) | (blk_row == q_blk)
        bias_t = jnp.where(sel, 0.0, MASKED)
        for c in range(nb):
            parts = []
            if spare[h]:
                parts.append(jnp.zeros((spare[h], blk), F32))
            parts.append(bias_t[:, c * blk:(c + 1) * blk])
            parts.append(jnp.zeros((LANES - spare[h] - nb, blk), F32))
            bias = jnp.concatenate(parts, axis=0).T
            qs = q_ref[c * blk:(c + 1) * blk, :] * (MOBA_D ** -0.5 * LOG2E)
            qaug_ref[c, h] = jnp.where(heads[h], qs, bias).astype(BF16)

    kpos = lax.broadcasted_iota(jnp.int32, (blk, blk), 0)
    qpos = lax.broadcasted_iota(jnp.int32, (blk, blk), 1)
    causal_t = kpos <= qpos

    token = None
    for t in range(nb // 2):
        buf = t % 2
        base = 0
        for c in (nb - 1 - t, t):
            outs = []
            for h in range(2):
                qa = qaug_ref[c, h]
                if token is not None:
                    qa = (qa.astype(F32) + token).astype(BF16)
                st = s_ref.at[buf, h]
                mx = None
                for j in range(c + 1):
                    s = _nt(kaug_ref[h, j * blk:(j + 1) * blk, :], qa)
                    if j == 0:
                        bits = pltpu.bitcast(s[0:1, 0:LANES], jnp.uint32)
                        zero = lax.shift_right_logical(lax.shift_right_logical(bits, jnp.uint32(16)), jnp.uint32(16))
                        token = zero.astype(F32)
                    if j == c:
                        s = jnp.where(causal_t, s, MASKED)
                    st[base + j * blk:base + (j + 1) * blk, :] = s
                    mx = s if mx is None else jnp.maximum(mx, s)
                m = jnp.max(mx, axis=0, keepdims=True)
                acc = None
                for j in range(c + 1):
                    rows = slice(j * blk, (j + 1) * blk)
                    p = jnp.exp2(st[base + j * blk:base + (j + 1) * blk, :] - m).astype(BF16)
                    pv = _nn(vt_ref[h, :, rows], p)
                    acc = pv if acc is None else acc + pv
                outs.append(acc[0:MOBA_D] / acc[MOBA_D:MOBA_D + 1])
            o_ref[c * blk:(c + 1) * blk, :] = jnp.concatenate(outs, axis=0).T.astype(BF16)
            base += (c + 1) * blk


def _moba(qa, ka, va, batch, seq):
    nb = seq // MOBA_BLOCK
    assert nb % 2 == 0 and nb <= MOBA_D
    pairs = GROUP // LANES
    whole = pl.BlockSpec((seq, LANES), lambda b, p: (b, p))
    return pl.pallas_call(
        functools.partial(_moba_kernel, nb=nb),
        grid=(batch, pairs),
        in_specs=[whole, whole, whole],
        out_specs=whole,
        out_shape=jax.ShapeDtypeStruct(qa.shape, BF16),
        scratch_shapes=[pltpu.VMEM((nb, LANES), F32),
                        pltpu.VMEM((2, seq, LANES), BF16),
                        pltpu.VMEM((2, MOBA_VT_ROWS, seq), BF16),
                        pltpu.VMEM((nb, 2, MOBA_BLOCK, LANES), BF16),
                        pltpu.VMEM((2, 2, (nb + 1) * MOBA_BLOCK, MOBA_BLOCK), F32)],
        compiler_params=pltpu.CompilerParams(dimension_semantics=("parallel", "parallel"),
                                             vmem_limit_bytes=VMEM_LIMIT),
        name="moba",
    )(qa, ka, va)


def _hgrn_consts():
    c = HGRN_CHUNK
    t = np.arange(c)
    w_rows, masks = [], []
    for lvl in range(HGRN_LEVELS):
        m = 1 << lvl
        blk = t // m
        odd = (blk % 2) == 1
        start = blk * m
        end = start + m - 1
        u = t[None, :]
        w_odd = (u >= start[:, None]) & (u <= t[:, None])
        w_even = (u > t[:, None]) & (u <= end[:, None])
        w_rows.append(np.where(odd[:, None], w_odd, w_even))
        masks.append(odd[:, None] & (blk[None, :] == blk[:, None] - 1))
    masks.append(t[None, :] == t[:, None])
    w_mm = w_rows[:HGRN_MM_LEVELS] + [t[None, :] <= t[:, None]]
    w_all = np.concatenate(w_mm, axis=0).astype(np.float32)
    w_all = np.concatenate([w_all, w_all], axis=1)
    mask_cat = np.concatenate([masks[-1]] + masks[:-1] + [np.zeros((c, c), bool)], axis=1).astype(np.float32)
    return jnp.asarray(w_all, BF16), jnp.asarray(mask_cat, F32)


def _hgrn_kernel(qh_ref, fh_ref, ih_ref, logit_ref, gain_ref, w_ref, mask_ref, o_ref, state_ref, *, layer):
    c = HGRN_CHUNK
    state_ref[...] = jnp.zeros_like(state_ref)

    if layer > 0:
        lg = logit_ref[...]
        e = jnp.exp(lg - jnp.max(lg, axis=0, keepdims=True))
        sm = e / jnp.sum(e, axis=0, keepdims=True)
        lb = jnp.sum(sm[1:layer + 1, :], axis=0, keepdims=True)
        log_lb = jnp.log(lb)
        log_1m_lb = jnp.log1p(-lb)
    w_all = w_ref[...]
    gain = gain_ref[...]

    def prepare(base, ci):
        rows = pl.ds(base + ci * c, c)
        fl = fh_ref[rows, :]
        vi = ih_ref[rows, :]
        qf = _silu(qh_ref[rows, :].astype(F32))
        e = jnp.exp(-jnp.abs(fl))
        r = 1.0 / (1.0 + e)
        log_sig = jnp.minimum(fl, 0.0) - jnp.log(1.0 + e)
        sig_neg = jnp.where(fl >= 0.0, e * r, r)
        if layer > 0:
            b = log_1m_lb + log_sig
            lf = jnp.maximum(log_lb, b) + jnp.log(1.0 + jnp.exp(-jnp.abs(log_lb - b)))
            kf = (1.0 - lb) * sig_neg
        else:
            lf = log_sig
            kf = sig_neg

        f1, f2 = _split2(lf * LOG2E)
        x_mm = _nn(w_all, jnp.concatenate([f1, f2], axis=0))
        a2 = x_mm[HGRN_MM_LEVELS * c:(HGRN_MM_LEVELS + 1) * c, :]
        e_lvl = [jnp.exp2(x_mm[lvl * c:(lvl + 1) * c, :]) for lvl in range(HGRN_MM_LEVELS)]
        for lvl in range(HGRN_MM_LEVELS, HGRN_LEVELS):
            m = 1 << lvl
            parts = []
            for g in range(c // (2 * m)):
                ref = a2[2 * m * g + m - 1:2 * m * g + m, :]
                parts.append(-jnp.abs(a2[2 * m * g:2 * m * (g + 1), :] - ref))
            e_lvl.append(jnp.exp2(jnp.concatenate(parts, axis=0)))
        e_cum = jnp.exp2(a2)
        e_rest = jnp.exp2(a2[c - 1:c, :] - a2)
        e_end = e_cum[c - 1:c, :]
        return qf, kf, vi, e_lvl, e_cum, e_rest, e_end

    def recur(base, ci, prepared):
        rows = pl.ds(base + ci * c, c)
        qf, kf, vi, e_lvl, e_cum, e_rest, e_end = prepared
        for h in range(HGRN_HEADS):
            cols = slice(h * HGRN_D, (h + 1) * HGRN_D)
            qfh, kfh = qf[:, cols], kf[:, cols]
            qb = [qfh.astype(BF16)] + [(qfh * e[:, cols]).astype(BF16) for e in e_lvl]
            kb = [kfh.astype(BF16)] + [(kfh * e[:, cols]).astype(BF16) for e in e_lvl]
            zero = jnp.zeros((c, HGRN_D), BF16)
            s_parts = []
            for a in range(0, HGRN_LEVELS, 2):
                q_cat = jnp.concatenate([qb[a], qb[a + 1]], axis=1)
                k_blk = jnp.concatenate([jnp.concatenate([kb[a], zero], axis=1),
                                         jnp.concatenate([zero, kb[a + 1]], axis=1)], axis=0)
                s_parts.append(_nt(q_cat, k_blk))
            s_parts.append(_nt(qb[HGRN_LEVELS], jnp.concatenate([kb[HGRN_LEVELS], zero], axis=0)))
            p = None
            for a, part in enumerate(s_parts):
                part = part * mask_ref[:, 2 * a * c:(2 * a + 2) * c]
                p = part if p is None else p + part
            vb = vi[:, cols]
            st = state_ref[:, cols]
            o = _nn(p.astype(BF16), jnp.concatenate([vb, vb], axis=0))
            o = o + _nt((qfh * e_cum[:, cols]).astype(BF16), st.astype(BF16))
            state_ref[:, cols] = st * e_end[:, cols] + _tn(vb, (kfh * e_rest[:, cols]).astype(BF16))
            ms = jnp.mean(o * o, axis=-1, keepdims=True)
            o_ref[rows, cols] = (o * lax.rsqrt(ms + NORM_EPS) * gain[:, cols]).astype(BF16)

    n_chunks = HGRN_TILE // c

    def group(g, carry):
        base = pl.multiple_of(g * HGRN_TILE, HGRN_TILE)
        prepared = prepare(base, 0)
        for ci in range(n_chunks):
            nxt = prepare(base, ci + 1) if ci + 1 < n_chunks else None
            recur(base, ci, prepared)
            prepared = nxt
        return carry

    lax.fori_loop(0, qh_ref.shape[0] // HGRN_TILE, group, 0)


def _hgrn(qh, fh, ih, logits, gain, layer, batch, seq):
    w_all, mask_all = _hgrn_consts()
    assert seq % HGRN_TILE == 0
    depth = logits.shape[0]
    const = lambda b: (0, 0)
    return pl.pallas_call(
        functools.partial(_hgrn_kernel, layer=layer),
        grid=(batch,),
        in_specs=[pl.BlockSpec((seq, GROUP), lambda b: (b, 0))] * 3 + [
                  pl.BlockSpec((depth, GROUP), const),
                  pl.BlockSpec((None, 1, GROUP), lambda b: (layer, 0, 0)),
                  pl.BlockSpec(w_all.shape, const),
                  pl.BlockSpec(mask_all.shape, const)],
        out_specs=pl.BlockSpec((seq, GROUP), lambda b: (b, 0)),
        out_shape=jax.ShapeDtypeStruct((fh.shape[0], GROUP), BF16),
        scratch_shapes=[pltpu.VMEM((HGRN_D, GROUP), F32)],
        compiler_params=pltpu.CompilerParams(dimension_semantics=("parallel",),
                                             vmem_limit_bytes=VMEM_LIMIT),
        name="hgrn",
    )(qh, fh, ih, logits, gain, w_all, mask_all)


def _mem_kv_kernel(m_ref, g_ref, w_ref, gk_ref, seg_ref, k_ref, v_ref):
    x = m_ref[...]
    ms = jnp.mean(x * x, axis=-1, keepdims=True)
    hb = (x * lax.rsqrt(ms + NORM_EPS) * g_ref[...]).astype(BF16)
    gk = gk_ref[...]
    seg = seg_ref[...]
    for c in range(GROUP // 256):
        kc = _nn(hb, w_ref[:, c * 256:(c + 1) * 256].astype(BF16))
        ss = _nn((kc * kc).astype(BF16), seg) * (1.0 / MEM_D)
        kn = kc * lax.rsqrt(ss + NORM_EPS) * gk[:, c * 256:(c + 1) * 256]
        k_ref[:, c * 256:(c + 1) * 256] = kn.astype(BF16)
    v_ref[...] = _nn(hb, w_ref[:, GROUP:2 * GROUP].astype(BF16)).astype(BF16)


def _mem_kv(memf, g, w, gk, mem_len, layer):
    n = memf.shape[0]
    seg = np.arange(256)[:, None] // MEM_D == np.arange(256)[None, :] // MEM_D
    seg = jnp.asarray(seg, BF16)
    const = lambda i: (0, 0)
    row = lambda i: (i, 0)
    of_layer = lambda i: (layer, 0, 0)
    return pl.pallas_call(
        _mem_kv_kernel,
        grid=(n // mem_len,),
        in_specs=[pl.BlockSpec((mem_len, D_MODEL), row),
                  pl.BlockSpec((None, 1, D_MODEL), of_layer),
                  pl.BlockSpec((None, D_MODEL, 2 * GROUP), of_layer),
                  pl.BlockSpec((None, 1, GROUP), of_layer),
                  pl.BlockSpec((256, 256), const)],
        out_specs=[pl.BlockSpec((mem_len, GROUP), row)] * 2,
        out_shape=[jax.ShapeDtypeStruct((n, GROUP), BF16)] * 2,
        compiler_params=pltpu.CompilerParams(dimension_semantics=("parallel",), vmem_limit_bytes=VMEM_LIMIT),
        name="mem_kv",
    )(memf, g, w, gk, seg)


def _out_proj_kernel(x_ref, oa_ref, oh_ref, qm_ref, z_ref, km_ref, vm_ref, gq_ref, w_ref, o_ref, y_ref):
    def gate(o, c0, c1):
        y_ref[:, c0:c1] = (o * _silu(z_ref[:, c0:c1].astype(F32))).astype(BF16)

    gate(oa_ref[...].astype(F32), 0, GROUP)
    gate(oh_ref[...].astype(F32), GROUP, 2 * GROUP)
    gq = gq_ref[...]
    for h in range(MEM_HEADS):
        cols = slice(h * MEM_D, (h + 1) * MEM_D)
        q = qm_ref[:, cols].astype(F32)
        ms = jnp.mean(q * q, axis=-1, keepdims=True)
        qn = q * lax.rsqrt(ms + NORM_EPS) * gq * (MEM_D ** -0.5)
        s = _nt(qn.astype(BF16), km_ref[:, cols])
        p = jnp.exp(s - jnp.max(s, axis=1, keepdims=True))
        v_aug = jnp.concatenate([vm_ref[:, cols], jnp.ones((vm_ref.shape[0], MEM_D), BF16)], axis=1)
        pv = _nn(p.astype(BF16), v_aug)
        om = pv[:, 0:MEM_D] / pv[:, MEM_D:2 * MEM_D]
        c0 = 2 * GROUP + h * MEM_D
        gate(om, c0, c0 + MEM_D)
    o_ref[...] = x_ref[...] + _nn(y_ref[...], w_ref[...].astype(BF16))


def _out_proj(xf, oa, oh, qm, z, km, vm, gq, w, seq, mem_len, layer):
    n = xf.shape[0]
    tm = OUT_TILE
    steps = seq // tm
    const = lambda i: (0, 0)
    row = lambda i: (i, 0)
    per_batch = lambda i: (i // steps, 0)
    of_layer = lambda i: (layer, 0, 0)
    return pl.pallas_call(
        _out_proj_kernel,
        grid=(n // tm,),
        in_specs=[pl.BlockSpec((tm, D_MODEL), row),
                  pl.BlockSpec((tm, GROUP), row),
                  pl.BlockSpec((tm, GROUP), row),
                  pl.BlockSpec((tm, GROUP), row),
                  pl.BlockSpec((tm, D_MIX), row),
                  pl.BlockSpec((mem_len, GROUP), per_batch),
                  pl.BlockSpec((mem_len, GROUP), per_batch),
                  pl.BlockSpec((None, 1, MEM_D), of_layer),
                  pl.BlockSpec((None, D_MIX, D_MODEL), of_layer, pipeline_mode=pl.Buffered(1))],
        out_specs=pl.BlockSpec((tm, D_MODEL), row),
        out_shape=jax.ShapeDtypeStruct((n, D_MODEL), F32),
        scratch_shapes=[pltpu.VMEM((tm, D_MIX), BF16)],
        compiler_params=pltpu.CompilerParams(dimension_semantics=("parallel",), vmem_limit_bytes=VMEM_LIMIT),
        name="out_proj",
    )(xf, oa, oh, qm, z, km, vm, gq, w)


def kernel(x, mem, positions, norm_g, w_in, w_out, moba_q_norm, moba_k_norm, hgrn_lb_logits, hgrn_o_norm,
           mem_norm_g, w_mem_kv, mem_q_norm, mem_k_norm):
    batch, seq, d_model = x.shape
    mem_len = mem.shape[1]
    depth = w_in.shape[0]
    assert d_model == D_MODEL and seq % ROW_TILE == 0 and seq % OUT_TILE == 0 and mem_len % 8 == 0
    n = batch * seq
    xf = x.reshape(n, d_model)
    memf = mem.reshape(batch * mem_len, d_model)
    cosf, sinf = _rope_tables(positions.reshape(n, 1))
    per_head = lambda p, reps: jnp.tile(p, (1, reps))[:, None, :]
    norm_g, mem_norm_g, mem_q_norm = norm_g[:, None, :], mem_norm_g[:, None, :], mem_q_norm[:, None, :]
    gq, gk = per_head(moba_q_norm, GROUP // MOBA_D), per_head(moba_k_norm, GROUP // MOBA_D)
    g_hgrn, g_memk = per_head(hgrn_o_norm, HGRN_HEADS), per_head(mem_k_norm, MEM_HEADS)
    for l in range(depth):
        qa, ka, va, qh, fh, ih, qm, z = _in_proj(xf, norm_g, w_in, cosf, sinf, gq, gk, l)
        oa = _moba(qa, ka, va, batch, seq)
        oh = _hgrn(qh, fh, ih, hgrn_lb_logits, g_hgrn, l, batch, seq)
        km, vm = _mem_kv(memf, mem_norm_g, w_mem_kv, g_memk, mem_len, l)
        xf = _out_proj(xf, oa, oh, qm, z, km, vm, mem_q_norm, w_out, seq, mem_len, l)
    return xf.reshape(batch, seq, d_model)
```

```python
import functools

import numpy as np
import jax
import jax.numpy as jnp
from jax import lax
from jax.experimental import pallas as pl
from jax.experimental.pallas import tpu as pltpu

F32 = jnp.float32
BF16 = jnp.bfloat16

D_MODEL = 1024
GROUP = D_MODEL // 2
D_MIX = 3 * GROUP
IN_COLS = 7 * GROUP + D_MIX
MOBA_D = 64
MOBA_BLOCK = 256
MOBA_TOPK = 3
MOBA_VT_ROWS = MOBA_D + 16
HGRN_D = 128
HGRN_HEADS = GROUP // HGRN_D
HGRN_CHUNK = 64
HGRN_LEVELS = 6
HGRN_MM_LEVELS = 3
LOG2E = 1.4426950408889634
MEM_D = 128
MEM_HEADS = GROUP // MEM_D
ROPE_THETA = 500000.0
ROPE_DIM = MOBA_D // 4
ROPE_HALF = ROPE_DIM // 2
NORM_EPS = 1e-6
LANES = 128
ROW_TILE = 512
OUT_TILE = 1024
HGRN_TILE = 1024
MASKED = -1e30
VMEM_LIMIT = 48 * 1024 * 1024


def _nt(a, b):
    return lax.dot_general(a, b, (((1,), (1,)), ((), ())), preferred_element_type=F32)


def _tn(a, b):
    return lax.dot_general(a, b, (((0,), (0,)), ((), ())), preferred_element_type=F32)


def _nn(a, b):
    return jnp.dot(a, b, preferred_element_type=F32)


def _split2(x):
    hi = x.astype(BF16)
    lo = (x - hi.astype(F32)).astype(BF16)
    return hi, lo


def _sigmoid(x):
    return 1.0 / (1.0 + jnp.exp(-x))


def _silu(x):
    return x * _sigmoid(x)


def _rope_table_kernel(pos_ref, invf_ref, sgn_ref, cos_ref, sin_ref):
    ang = pos_ref[...].astype(F32) * invf_ref[...]
    sgn = sgn_ref[...]
    on = sgn != 0.0
    cos_ref[...] = jnp.where(on, jnp.cos(ang), 1.0)
    sin_ref[...] = sgn * jnp.sin(ang)


def _rope_tables(pos):
    n = pos.shape[0]
    tm = min(n, 2048)
    lane = np.arange(LANES) % MOBA_D
    invf = np.where(lane < ROPE_DIM, ROPE_THETA ** (-(lane % ROPE_HALF).astype(np.float64) / ROPE_HALF), 0.0)
    sgn = np.where(lane < ROPE_HALF, -1.0, np.where(lane < ROPE_DIM, 1.0, 0.0))
    invf = jnp.asarray(invf[None, :], F32)
    sgn = jnp.asarray(sgn[None, :], F32)
    return pl.pallas_call(
        _rope_table_kernel,
        grid=(n // tm,),
        in_specs=[pl.BlockSpec((tm, 1), lambda i: (i, 0)),
                  pl.BlockSpec((1, LANES), lambda i: (0, 0)),
                  pl.BlockSpec((1, LANES), lambda i: (0, 0))],
        out_specs=[pl.BlockSpec((tm, LANES), lambda i: (i, 0))] * 2,
        out_shape=[jax.ShapeDtypeStruct((n, LANES), F32)] * 2,
        name="rope_tables",
    )(pos, invf, sgn)


def _in_proj_kernel(x_ref, g_ref, w_ref, cos_ref, sin_ref, gq_ref, gk_ref, seg_ref,
                    qa_ref, ka_ref, va_ref, qh_ref, fh_ref, ih_ref, qm_ref, z_ref):
    x = x_ref[...]
    ms = jnp.mean(x * x, axis=-1, keepdims=True)
    hb = (x * lax.rsqrt(ms + NORM_EPS) * g_ref[...]).astype(BF16)

    def proj(c0, width):
        return _nn(hb, w_ref[:, c0:c0 + width].astype(BF16))

    cosf = cos_ref[...]
    sinf = sin_ref[...]
    lane = lax.broadcasted_iota(jnp.int32, (1, LANES), 1) % MOBA_D
    lo_lane = lane < ROPE_HALF
    seg = seg_ref[...]

    def head_norm_rope(p, gain, out_ref):
        for c in range(GROUP // 256):
            pc = p[:, c * 256:(c + 1) * 256]
            ss = _nn((pc * pc).astype(BF16), seg) * (1.0 / MOBA_D)
            pn = pc * lax.rsqrt(ss + NORM_EPS) * gain[:, c * 256:(c + 1) * 256]
            for v in range(2):
                xv = pn[:, v * LANES:(v + 1) * LANES]
                rot = jnp.where(lo_lane, pltpu.roll(xv, LANES - ROPE_HALF, 1), pltpu.roll(xv, ROPE_HALF, 1))
                col = c * 256 + v * LANES
                out_ref[:, col:col + LANES] = (xv * cosf + rot * sinf).astype(out_ref.dtype)

    head_norm_rope(proj(0, GROUP), gq_ref[...], qa_ref)
    head_norm_rope(proj(GROUP, GROUP), gk_ref[...], ka_ref)
    va_ref[...] = proj(2 * GROUP, GROUP).astype(BF16)
    qh_ref[...] = proj(3 * GROUP, GROUP).astype(BF16)
    fh_ref[...] = proj(4 * GROUP, GROUP)
    ih_ref[...] = proj(5 * GROUP, GROUP).astype(BF16)
    qm_ref[...] = proj(6 * GROUP, GROUP).astype(BF16)
    for c in range(3):
        z_ref[:, c * GROUP:(c + 1) * GROUP] = proj((7 + c) * GROUP, GROUP).astype(BF16)


def _in_proj(xf, g, w, cosf, sinf, gq, gk, layer):
    n = xf.shape[0]
    tm = ROW_TILE
    seg = np.arange(256)[:, None] // MOBA_D == np.arange(256)[None, :] // MOBA_D
    seg = jnp.asarray(seg, BF16)
    const = lambda i: (0, 0)
    row = lambda i: (i, 0)
    of_layer = lambda i: (layer, 0, 0)
    outs = ((GROUP, F32), (GROUP, BF16), (GROUP, BF16), (GROUP, BF16), (GROUP, F32), (GROUP, BF16),
            (GROUP, BF16), (D_MIX, BF16))
    return pl.pallas_call(
        _in_proj_kernel,
        grid=(n // tm,),
        in_specs=[pl.BlockSpec((tm, D_MODEL), row),
                  pl.BlockSpec((None, 1, D_MODEL), of_layer),
                  pl.BlockSpec((None, D_MODEL, IN_COLS), of_layer, pipeline_mode=pl.Buffered(1)),
                  pl.BlockSpec((tm, LANES), row),
                  pl.BlockSpec((tm, LANES), row),
                  pl.BlockSpec((None, 1, GROUP), of_layer),
                  pl.BlockSpec((None, 1, GROUP), of_layer),
                  pl.BlockSpec((256, 256), const)],
        out_specs=[pl.BlockSpec((tm, w), row) for w, _ in outs],
        out_shape=[jax.ShapeDtypeStruct((n, w), dt) for w, dt in outs],
        compiler_params=pltpu.CompilerParams(dimension_semantics=("parallel",), vmem_limit_bytes=VMEM_LIMIT),
        name="in_proj",
    )(xf, g, w, cosf, sinf, gq, gk, seg)


def _moba_kernel(q_ref, k_ref, v_ref, o_ref, kmean_ref, kaug_ref, vt_ref, qaug_ref, s_ref, *, nb):
    blk = MOBA_BLOCK
    seq = nb * blk
    lane = lax.broadcasted_iota(jnp.int32, (1, LANES), 1)
    heads = [(lane >= h * MOBA_D) & (lane < (h + 1) * MOBA_D) for h in range(2)]
    spare = [(1 - h) * MOBA_D for h in range(2)]

    ones = jnp.ones((MOBA_VT_ROWS - MOBA_D, blk), F32)
    for j in range(nb):
        rows = slice(j * blk, (j + 1) * blk)
        kj = k_ref[rows, :]
        kmean_ref[j:j + 1, :] = jnp.mean(kj.astype(F32), axis=0, keepdims=True)
        v_t = v_ref[rows, :].astype(F32).T
        for h in range(2):
            onehot = jnp.where(lane == spare[h] + j, 1.0, 0.0).astype(BF16)
            kaug_ref[h, rows, :] = jnp.where(heads[h], kj, onehot)
            vt_ref[h, :, rows] = jnp.concatenate([v_t[h * MOBA_D:(h + 1) * MOBA_D], ones], axis=0).astype(BF16)

    km = kmean_ref[...]
    q_hi, q_lo = _split2(q_ref[...])
    km_parts = [_split2(jnp.where(heads[h], km, 0.0)) for h in range(2)]
    g_hi = _nt(jnp.concatenate([km_parts[0][0], km_parts[0][1], km_parts[1][0], km_parts[1][1]], axis=0), q_hi)
    g_lo = _nt(jnp.concatenate([km_parts[0][0], km_parts[1][0]], axis=0), q_lo)
    blk_row = lax.broadcasted_iota(jnp.int32, (nb, 1), 0)
    q_blk = lax.broadcasted_iota(jnp.int32, (1, seq), 1) // blk
    past = blk_row < q_blk
    for h in range(2):
        gate_t = g_hi[2 * h * nb:(2 * h + 1) * nb] + g_hi[(2 * h + 1) * nb:(2 * h + 2) * nb] + g_lo[h * nb:(h + 1) * nb]
        gate_t = jnp.where(past, gate_t, -jnp.inf)
        cnt = jnp.zeros((nb, seq), F32)
        for n in range(nb - 1):
            gn = gate_t[n:n + 1, :]
            before = (gn > gate_t) | ((gn == gate_t) & (n < blk_row))
            cnt = cnt + jnp.where(before, 1.0, 0.0)
        sel = (past & (cnt < float(MOBA_TOPK))) | (blk_row == q_blk)
        bias_t = jnp.where(sel, 0.0, MASKED)
        for c in range(nb):
            parts = []
            if spare[h]:
                parts.append(jnp.zeros((spare[h], blk), F32))
            parts.append(bias_t[:, c * blk:(c + 1) * blk])
            parts.append(jnp.zeros((LANES - spare[h] - nb, blk), F32))
            bias = jnp.concatenate(parts, axis=0).T
            qs = q_ref[c * blk:(c + 1) * blk, :] * (MOBA_D ** -0.5 * LOG2E)
            qaug_ref[c, h] = jnp.where(heads[h], qs, bias).astype(BF16)

    kpos = lax.broadcasted_iota(jnp.int32, (blk, blk), 0)
    qpos = lax.broadcasted_iota(jnp.int32, (blk, blk), 1)
    causal_t = kpos <= qpos

    token = None
    for t in range(nb // 2):
        buf = t % 2
        base = 0
        for c in (nb - 1 - t, t):
            outs = []
            for h in range(2):
                qa = qaug_ref[c, h]
                if token is not None:
                    qa = (qa.astype(F32) + token).astype(BF16)
                st = s_ref.at[buf, h]
                mx = None
                for j in range(c + 1):
                    s = _nt(kaug_ref[h, j * blk:(j + 1) * blk, :], qa)
                    if j == 0:
                        bits = pltpu.bitcast(s[0:1, 0:LANES], jnp.uint32)
                        zero = lax.shift_right_logical(lax.shift_right_logical(bits, jnp.uint32(16)), jnp.uint32(16))
                        token = zero.astype(F32)
                    if j == c:
                        s = jnp.where(causal_t, s, MASKED)
                    st[base + j * blk:base + (j + 1) * blk, :] = s
                    mx = s if mx is None else jnp.maximum(mx, s)
                m = jnp.max(mx, axis=0, keepdims=True)
                acc = None
                for j in range(c + 1):
                    rows = slice(j * blk, (j + 1) * blk)
                    p = jnp.exp2(st[base + j * blk:base + (j + 1) * blk, :] - m).astype(BF16)
                    pv = _nn(vt_ref[h, :, rows], p)
                    acc = pv if acc is None else acc + pv
                outs.append(acc[0:MOBA_D] / acc[MOBA_D:MOBA_D + 1])
            o_ref[c * blk:(c + 1) * blk, :] = jnp.concatenate(outs, axis=0).T.astype(BF16)
            base += (c + 1) * blk


def _moba(qa, ka, va, batch, seq):
    nb = seq // MOBA_BLOCK
    assert nb % 2 == 0 and nb <= MOBA_D
    pairs = GROUP // LANES
    whole = pl.BlockSpec((seq, LANES), lambda b, p: (b, p))
    return pl.pallas_call(
        functools.partial(_moba_kernel, nb=nb),
        grid=(batch, pairs),
        in_specs=[whole, whole, whole],
        out_specs=whole,
        out_shape=jax.ShapeDtypeStruct(qa.shape, BF16),
        scratch_shapes=[pltpu.VMEM((nb, LANES), F32),
                        pltpu.VMEM((2, seq, LANES), BF16),
                        pltpu.VMEM((2, MOBA_VT_ROWS, seq), BF16),
                        pltpu.VMEM((nb, 2, MOBA_BLOCK, LANES), BF16),
                        pltpu.VMEM((2, 2, (nb + 1) * MOBA_BLOCK, MOBA_BLOCK), F32)],
        compiler_params=pltpu.CompilerParams(dimension_semantics=("parallel", "parallel"),
                                             vmem_limit_bytes=VMEM_LIMIT),
        name="moba",
    )(qa, ka, va)


def _hgrn_consts():
    c = HGRN_CHUNK
    t = np.arange(c)
    w_rows, masks = [], []
    for lvl in range(HGRN_LEVELS):
        m = 1 << lvl
        blk = t // m
        odd = (blk % 2) == 1
        start = blk * m
        end = start + m - 1
        u = t[None, :]
        w_odd = (u >= start[:, None]) & (u <= t[:, None])
        w_even = (u > t[:, None]) & (u <= end[:, None])
        w_rows.append(np.where(odd[:, None], w_odd, w_even))
        masks.append(odd[:, None] & (blk[None, :] == blk[:, None] - 1))
    masks.append(t[None, :] == t[:, None])
    w_mm = w_rows[:HGRN_MM_LEVELS] + [t[None, :] <= t[:, None]]
    w_all = np.concatenate(w_mm, axis=0).astype(np.float32)
    w_all = np.concatenate([w_all, w_all], axis=1)
    mask_cat = np.concatenate([masks[-1]] + masks[:-1] + [np.zeros((c, c), bool)], axis=1).astype(np.float32)
    return jnp.asarray(w_all, BF16), jnp.asarray(mask_cat, F32)


def _hgrn_kernel(qh_ref, fh_ref, ih_ref, logit_ref, gain_ref, w_ref, mask_ref, o_ref, state_ref, *, layer):
    c = HGRN_CHUNK
    state_ref[...] = jnp.zeros_like(state_ref)

    if layer > 0:
        lg = logit_ref[...]
        e = jnp.exp(lg - jnp.max(lg, axis=0, keepdims=True))
        sm = e / jnp.sum(e, axis=0, keepdims=True)
        lb = jnp.sum(sm[1:layer + 1, :], axis=0, keepdims=True)
        log_lb = jnp.log(lb)
        log_1m_lb = jnp.log1p(-lb)
    w_all = w_ref[...]
    gain = gain_ref[...]

    def prepare(base, ci):
        rows = pl.ds(base + ci * c, c)
        fl = fh_ref[rows, :]
        vi = ih_ref[rows, :]
        qf = _silu(qh_ref[rows, :].astype(F32))
        e = jnp.exp(-jnp.abs(fl))
        r = 1.0 / (1.0 + e)
        log_sig = jnp.minimum(fl, 0.0) - jnp.log(1.0 + e)
        sig_neg = jnp.where(fl >= 0.0, e * r, r)
        if layer > 0:
            b = log_1m_lb + log_sig
            lf = jnp.maximum(log_lb, b) + jnp.log(1.0 + jnp.exp(-jnp.abs(log_lb - b)))
            kf = (1.0 - lb) * sig_neg
        else:
            lf = log_sig
            kf = sig_neg

        f1, f2 = _split2(lf * LOG2E)
        x_mm = _nn(w_all, jnp.concatenate([f1, f2], axis=0))
        a2 = x_mm[HGRN_MM_LEVELS * c:(HGRN_MM_LEVELS + 1) * c, :]
        e_lvl = [jnp.exp2(x_mm[lvl * c:(lvl + 1) * c, :]) for lvl in range(HGRN_MM_LEVELS)]
        for lvl in range(HGRN_MM_LEVELS, HGRN_LEVELS):
            m = 1 << lvl
            parts = []
            for g in range(c // (2 * m)):
                ref = a2[2 * m * g + m - 1:2 * m * g + m, :]
                parts.append(-jnp.abs(a2[2 * m * g:2 * m * (g + 1), :] - ref))
            e_lvl.append(jnp.exp2(jnp.concatenate(parts, axis=0)))
        e_cum = jnp.exp2(a2)
        e_rest = jnp.exp2(a2[c - 1:c, :] - a2)
        e_end = e_cum[c - 1:c, :]
        return qf, kf, vi, e_lvl, e_cum, e_rest, e_end

    def recur(base, ci, prepared):
        rows = pl.ds(base + ci * c, c)
        qf, kf, vi, e_lvl, e_cum, e_rest, e_end = prepared
        for h in range(HGRN_HEADS):
            cols = slice(h * HGRN_D, (h + 1) * HGRN_D)
            qfh, kfh = qf[:, cols], kf[:, cols]
            qb = [qfh.astype(BF16)] + [(qfh * e[:, cols]).astype(BF16) for e in e_lvl]
            kb = [kfh.astype(BF16)] + [(kfh * e[:, cols]).astype(BF16) for e in e_lvl]
            zero = jnp.zeros((c, HGRN_D), BF16)
            s_parts = []
            for a in range(0, HGRN_LEVELS, 2):
                q_cat = jnp.concatenate([qb[a], qb[a + 1]], axis=1)
                k_blk = jnp.concatenate([jnp.concatenate([kb[a], zero], axis=1),
                                         jnp.concatenate([zero, kb[a + 1]], axis=1)], axis=0)
                s_parts.append(_nt(q_cat, k_blk))
            s_parts.append(_nt(qb[HGRN_LEVELS], jnp.concatenate([kb[HGRN_LEVELS], zero], axis=0)))
            p = None
            for a, part in enumerate(s_parts):
                part = part * mask_ref[:, 2 * a * c:(2 * a + 2) * c]
                p = part if p is None else p + part
            vb = vi[:, cols]
            st = state_ref[:, cols]
            o = _nn(p.astype(BF16), jnp.concatenate([vb, vb], axis=0))
            o = o + _nt((qfh * e_cum[:, cols]).astype(BF16), st.astype(BF16))
            state_ref[:, cols] = st * e_end[:, cols] + _tn(vb, (kfh * e_rest[:, cols]).astype(BF16))
            ms = jnp.mean(o * o, axis=-1, keepdims=True)
            o_ref[rows, cols] = (o * lax.rsqrt(ms + NORM_EPS) * gain[:, cols]).astype(BF16)

    n_chunks = HGRN_TILE // c

    def group(g, carry):
        base = pl.multiple_of(g * HGRN_TILE, HGRN_TILE)
        prepared = prepare(base, 0)
        for ci in range(n_chunks):
            nxt = prepare(base, ci + 1) if ci + 1 < n_chunks else None
            recur(base, ci, prepared)
            prepared = nxt
        return carry

    lax.fori_loop(0, qh_ref.shape[0] // HGRN_TILE, group, 0)


def _hgrn(qh, fh, ih, logits, gain, layer, batch, seq):
    w_all, mask_all = _hgrn_consts()
    assert seq % HGRN_TILE == 0
    depth = logits.shape[0]
    const = lambda b: (0, 0)
    return pl.pallas_call(
        functools.partial(_hgrn_kernel, layer=layer),
        grid=(batch,),
        in_specs=[pl.BlockSpec((seq, GROUP), lambda b: (b, 0))] * 3 + [
                  pl.BlockSpec((depth, GROUP), const),
                  pl.BlockSpec((None, 1, GROUP), lambda b: (layer, 0, 0)),
                  pl.BlockSpec(w_all.shape, const),
                  pl.BlockSpec(mask_all.shape, const)],
        out_specs=pl.BlockSpec((seq, GROUP), lambda b: (b, 0)),
        out_shape=jax.ShapeDtypeStruct((fh.shape[0], GROUP), BF16),
        scratch_shapes=[pltpu.VMEM((HGRN_D, GROUP), F32)],
        compiler_params=pltpu.CompilerParams(dimension_semantics=("parallel",),
                                             vmem_limit_bytes=VMEM_LIMIT),
        name="hgrn",
    )(qh, fh, ih, logits, gain, w_all, mask_all)


def _mem_kv_kernel(m_ref, g_ref, w_ref, gk_ref, seg_ref, k_ref, v_ref):
    x = m_ref[...]
    ms = jnp.mean(x * x, axis=-1, keepdims=True)
    hb = (x * lax.rsqrt(ms + NORM_EPS) * g_ref[...]).astype(BF16)
    gk = gk_ref[...]
    seg = seg_ref[...]
    for c in range(GROUP // 256):
        kc = _nn(hb, w_ref[:, c * 256:(c + 1) * 256].astype(BF16))
        ss = _nn((kc * kc).astype(BF16), seg) * (1.0 / MEM_D)
        kn = kc * lax.rsqrt(ss + NORM_EPS) * gk[:, c * 256:(c + 1) * 256]
        k_ref[:, c * 256:(c + 1) * 256] = kn.astype(BF16)
    v_ref[...] = _nn(hb, w_ref[:, GROUP:2 * GROUP].astype(BF16)).astype(BF16)


def _mem_kv(memf, g, w, gk, mem_len, layer):
    n = memf.shape[0]
    seg = np.arange(256)[:, None] // MEM_D == np.arange(256)[None, :] // MEM_D
    seg = jnp.asarray(seg, BF16)
    const = lambda i: (0, 0)
    row = lambda i: (i, 0)
    of_layer = lambda i: (layer, 0, 0)
    return pl.pallas_call(
        _mem_kv_kernel,
        grid=(n // mem_len,),
        in_specs=[pl.BlockSpec((mem_len, D_MODEL), row),
                  pl.BlockSpec((None, 1, D_MODEL), of_layer),
                  pl.BlockSpec((None, D_MODEL, 2 * GROUP), of_layer),
                  pl.BlockSpec((None, 1, GROUP), of_layer),
                  pl.BlockSpec((256, 256), const)],
        out_specs=[pl.BlockSpec((mem_len, GROUP), row)] * 2,
        out_shape=[jax.ShapeDtypeStruct((n, GROUP), BF16)] * 2,
        compiler_params=pltpu.CompilerParams(dimension_semantics=("parallel",), vmem_limit_bytes=VMEM_LIMIT),
        name="mem_kv",
    )(memf, g, w, gk, seg)


def _out_proj_kernel(x_ref, oa_ref, oh_ref, qm_ref, z_ref, km_ref, vm_ref, gq_ref, w_ref, o_ref, y_ref):
    def gate(o, c0, c1):
        y_ref[:, c0:c1] = (o * _silu(z_ref[:, c0:c1].astype(F32))).astype(BF16)

    gate(oa_ref[...].astype(F32), 0, GROUP)
    gate(oh_ref[...].astype(F32), GROUP, 2 * GROUP)
    gq = gq_ref[...]
    for h in range(MEM_HEADS):
        cols = slice(h * MEM_D, (h + 1) * MEM_D)
        q = qm_ref[:, cols].astype(F32)
        ms = jnp.mean(q * q, axis=-1, keepdims=True)
        qn = q * lax.rsqrt(ms + NORM_EPS) * gq * (MEM_D ** -0.5)
        s = _nt(qn.astype(BF16), km_ref[:, cols])
        p = jnp.exp(s - jnp.max(s, axis=1, keepdims=True))
        v_aug = jnp.concatenate([vm_ref[:, cols], jnp.ones((vm_ref.shape[0], MEM_D), BF16)], axis=1)
        pv = _nn(p.astype(BF16), v_aug)
        om = pv[:, 0:MEM_D] / pv[:, MEM_D:2 * MEM_D]
        c0 = 2 * GROUP + h * MEM_D
        gate(om, c0, c0 + MEM_D)
    o_ref[...] = x_ref[...] + _nn(y_ref[...], w_ref[...].astype(BF16))


def _out_proj(xf, oa, oh, qm, z, km, vm, gq, w, seq, mem_len, layer):
    n = xf.shape[0]
    tm = OUT_TILE
    steps = seq // tm
    const = lambda i: (0, 0)
    row = lambda i: (i, 0)
    per_batch = lambda i: (i // steps, 0)
    of_layer = lambda i: (layer, 0, 0)
    return pl.pallas_call(
        _out_proj_kernel,
        grid=(n // tm,),
        in_specs=[pl.BlockSpec((tm, D_MODEL), row),
                  pl.BlockSpec((tm, GROUP), row),
                  pl.BlockSpec((tm, GROUP), row),
                  pl.BlockSpec((tm, GROUP), row),
                  pl.BlockSpec((tm, D_MIX), row),
                  pl.BlockSpec((mem_len, GROUP), per_batch),
                  pl.BlockSpec((mem_len, GROUP), per_batch),
                  pl.BlockSpec((None, 1, MEM_D), of_layer),
                  pl.BlockSpec((None, D_MIX, D_MODEL), of_layer, pipeline_mode=pl.Buffered(1))],
        out_specs=pl.BlockSpec((tm, D_MODEL), row),
        out_shape=jax.ShapeDtypeStruct((n, D_MODEL), F32),
        scratch_shapes=[pltpu.VMEM((tm, D_MIX), BF16)],
        compiler_params=pltpu.CompilerParams(dimension_semantics=("parallel",), vmem_limit_bytes=VMEM_LIMIT),
        name="out_proj",
    )(xf, oa, oh, qm, z, km, vm, gq, w)


def kernel(x, mem, positions, norm_g, w_in, w_out, moba_q_norm, moba_k_norm, hgrn_lb_logits, hgrn_o_norm,
           mem_norm_g, w_mem_kv, mem_q_norm, mem_k_norm):
    batch, seq, d_model = x.shape
    mem_len = mem.shape[1]
    depth = w_in.shape[0]
    assert d_model == D_MODEL and seq % ROW_TILE == 0 and seq % OUT_TILE == 0 and mem_len % 8 == 0
    n = batch * seq
    xf = x.reshape(n, d_model)
    memf = mem.reshape(batch * mem_len, d_model)
    cosf, sinf = _rope_tables(positions.reshape(n, 1))
    per_head = lambda p, reps: jnp.tile(p, (1, reps))[:, None, :]
    norm_g, mem_norm_g, mem_q_norm = norm_g[:, None, :], mem_norm_g[:, None, :], mem_q_norm[:, None, :]
    gq, gk = per_head(moba_q_norm, GROUP // MOBA_D), per_head(moba_k_norm, GROUP // MOBA_D)
    g_hgrn, g_memk = per_head(hgrn_o_norm, HGRN_HEADS), per_head(mem_k_norm, MEM_HEADS)
    for l in range(depth):
        qa, ka, va, qh, fh, ih, qm, z = _in_proj(xf, norm_g, w_in, cosf, sinf, gq, gk, l)
        oa = _moba(qa, ka, va, batch, seq)
        oh = _hgrn(qh, fh, ih, hgrn_lb_logits, g_hgrn, l, batch, seq)
        km, vm = _mem_kv(memf, mem_norm_g, w_mem_kv, g_memk, mem_len, l)
        xf = _out_proj(xf, oa, oh, qm, z, km, vm, mem_q_norm, w_out, seq, mem_len, l)
    return xf.reshape(batch, seq, d_model)
```

```python
import functools

import numpy as np
import jax
import jax.numpy as jnp
from jax import lax
from jax.experimental import pallas as pl
from jax.experimental.pallas import tpu as pltpu

F32 = jnp.float32
BF16 = jnp.bfloat16

D_MODEL = 1024
GROUP = D_MODEL // 2
D_MIX = 3 * GROUP
IN_COLS = 7 * GROUP + D_MIX
MOBA_D = 64
MOBA_BLOCK = 256
MOBA_TOPK = 3
MOBA_VT_ROWS = MOBA_D + 16
HGRN_D = 128
HGRN_HEADS = GROUP // HGRN_D
HGRN_CHUNK = 64
HGRN_LEVELS = 6
HGRN_MM_LEVELS = 3
LOG2E = 1.4426950408889634
MEM_D = 128
MEM_HEADS = GROUP // MEM_D
ROPE_THETA = 500000.0
ROPE_DIM = MOBA_D // 4
ROPE_HALF = ROPE_DIM // 2
NORM_EPS = 1e-6
LANES = 128
ROW_TILE = 512
OUT_TILE = 1024
HGRN_TILE = 1024
MASKED = -1e30
VMEM_LIMIT = 48 * 1024 * 1024


def _nt(a, b):
    return lax.dot_general(a, b, (((1,), (1,)), ((), ())), preferred_element_type=F32)


def _tn(a, b):
    return lax.dot_general(a, b, (((0,), (0,)), ((), ())), preferred_element_type=F32)


def _nn(a, b):
    return jnp.dot(a, b, preferred_element_type=F32)


def _split2(x):
    hi = x.astype(BF16)
    lo = (x - hi.astype(F32)).astype(BF16)
    return hi, lo


def _split3(x):
    hi = x.astype(BF16)
    r = x - hi.astype(F32)
    mid = r.astype(BF16)
    lo = (r - mid.astype(F32)).astype(BF16)
    return hi, mid, lo


def _sigmoid(x):
    return 1.0 / (1.0 + jnp.exp(-x))


def _silu(x):
    return x * _sigmoid(x)


def _rope_table_kernel(pos_ref, invf_ref, expand_ref, base_ref, cos_ref, sin_ref):
    ang = invf_ref[...] * pos_ref[...].astype(F32)
    tm = ang.shape[1]
    table = jnp.concatenate([jnp.cos(ang), jnp.sin(ang), jnp.zeros((LANES - ROPE_DIM, tm), F32)], axis=0)
    hi, mid, lo = _split3(table.T)
    e = expand_ref[...]
    both = _nn(hi, e) + _nn(mid, e) + _nn(lo, e)
    cos_ref[...] = both[:, 0:LANES] + base_ref[...]
    sin_ref[...] = both[:, LANES:2 * LANES]


def _rope_tables(pos):
    n = pos.shape[1]
    tm = min(n, 2048)
    lane = np.arange(LANES) % MOBA_D
    freq = lane % ROPE_HALF
    rope = lane < ROPE_DIM
    sgn = np.where(lane < ROPE_HALF, -1.0, 1.0)
    expand = np.zeros((LANES, 2 * LANES), np.float32)
    for l in np.nonzero(rope)[0]:
        expand[freq[l], l] = 1.0
        expand[ROPE_HALF + freq[l], LANES + l] = sgn[l]
    base = np.where(rope, 0.0, 1.0).astype(np.float32)[None, :]
    invf = (ROPE_THETA ** (-np.arange(ROPE_HALF, dtype=np.float64) / ROPE_HALF)).astype(np.float32)[:, None]
    const = lambda i: (0, 0)
    return pl.pallas_call(
        _rope_table_kernel,
        grid=(n // tm,),
        in_specs=[pl.BlockSpec((1, tm), lambda i: (0, i)),
                  pl.BlockSpec((ROPE_HALF, 1), const),
                  pl.BlockSpec((LANES, 2 * LANES), const),
                  pl.BlockSpec((1, LANES), const)],
        out_specs=[pl.BlockSpec((tm, LANES), lambda i: (i, 0))] * 2,
        out_shape=[jax.ShapeDtypeStruct((n, LANES), F32)] * 2,
        name="rope_tables",
    )(pos, jnp.asarray(invf), jnp.asarray(expand, BF16), jnp.asarray(base))


def _in_proj_kernel(x_ref, g_ref, w_ref, cos_ref, sin_ref, gq_ref, gk_ref, seg_ref,
                    qa_ref, ka_ref, va_ref, qh_ref, fh_ref, ih_ref, qm_ref, z_ref):
    x = x_ref[...]
    ms = jnp.mean(x * x, axis=-1, keepdims=True)
    hb = (x * lax.rsqrt(ms + NORM_EPS) * g_ref[...]).astype(BF16)

    def proj(c0, width):
        return _nn(hb, w_ref[:, c0:c0 + width].astype(BF16))

    cosf = cos_ref[...]
    sinf = sin_ref[...]
    lane = lax.broadcasted_iota(jnp.int32, (1, LANES), 1) % MOBA_D
    lo_lane = lane < ROPE_HALF
    seg = seg_ref[...]

    def head_norm_rope(p, gain, out_ref):
        for c in range(GROUP // 256):
            pc = p[:, c * 256:(c + 1) * 256]
            ss = _nn((pc * pc).astype(BF16), seg) * (1.0 / MOBA_D)
            pn = pc * lax.rsqrt(ss + NORM_EPS) * gain[:, c * 256:(c + 1) * 256]
            for v in range(2):
                xv = pn[:, v * LANES:(v + 1) * LANES]
                rot = jnp.where(lo_lane, pltpu.roll(xv, LANES - ROPE_HALF, 1), pltpu.roll(xv, ROPE_HALF, 1))
                col = c * 256 + v * LANES
                out_ref[:, col:col + LANES] = (xv * cosf + rot * sinf).astype(out_ref.dtype)

    head_norm_rope(proj(0, GROUP), gq_ref[...], qa_ref)
    head_norm_rope(proj(GROUP, GROUP), gk_ref[...], ka_ref)
    va_ref[...] = proj(2 * GROUP, GROUP).astype(BF16)
    qh_ref[...] = proj(3 * GROUP, GROUP).astype(BF16)
    fh_ref[...] = proj(4 * GROUP, GROUP)
    ih_ref[...] = proj(5 * GROUP, GROUP).astype(BF16)
    qm_ref[...] = proj(6 * GROUP, GROUP).astype(BF16)
    for c in range(3):
        z_ref[:, c * GROUP:(c + 1) * GROUP] = proj((7 + c) * GROUP, GROUP).astype(BF16)


def _in_proj(xf, g, w, cosf, sinf, gq, gk, layer):
    n = xf.shape[0]
    tm = ROW_TILE
    seg = np.arange(256)[:, None] // MOBA_D == np.arange(256)[None, :] // MOBA_D
    seg = jnp.asarray(seg, BF16)
    const = lambda i: (0, 0)
    row = lambda i: (i, 0)
    of_layer = lambda i: (layer, 0, 0)
    outs = ((GROUP, F32), (GROUP, BF16), (GROUP, BF16), (GROUP, BF16), (GROUP, F32), (GROUP, BF16),
            (GROUP, BF16), (D_MIX, BF16))
    return pl.pallas_call(
        _in_proj_kernel,
        grid=(n // tm,),
        in_specs=[pl.BlockSpec((tm, D_MODEL), row),
                  pl.BlockSpec((None, 1, D_MODEL), of_layer),
                  pl.BlockSpec((None, D_MODEL, IN_COLS), of_layer, pipeline_mode=pl.Buffered(1)),
                  pl.BlockSpec((tm, LANES), row),
                  pl.BlockSpec((tm, LANES), row),
                  pl.BlockSpec((None, 1, GROUP), of_layer),
                  pl.BlockSpec((None, 1, GROUP), of_layer),
                  pl.BlockSpec((256, 256), const)],
        out_specs=[pl.BlockSpec((tm, w), row) for w, _ in outs],
        out_shape=[jax.ShapeDtypeStruct((n, w), dt) for w, dt in outs],
        compiler_params=pltpu.CompilerParams(dimension_semantics=("parallel",), vmem_limit_bytes=VMEM_LIMIT),
        name="in_proj",
    )(xf, g, w, cosf, sinf, gq, gk, seg)


def _moba_kernel(q_ref, k_ref, v_ref, o_ref, kmean_ref, kaug_ref, vt_ref, qaug_ref, s_ref, *, nb):
    blk = MOBA_BLOCK
    seq = nb * blk
    lane = lax.broadcasted_iota(jnp.int32, (1, LANES), 1)
    heads = [(lane >= h * MOBA_D) & (lane < (h + 1) * MOBA_D) for h in range(2)]
    spare = [(1 - h) * MOBA_D for h in range(2)]

    ones = jnp.ones((MOBA_VT_ROWS - MOBA_D, blk), F32)
    for j in range(nb):
        rows = slice(j * blk, (j + 1) * blk)
        kj = k_ref[rows, :]
        kmean_ref[j:j + 1, :] = jnp.mean(kj.astype(F32), axis=0, keepdims=True)
        v_t = v_ref[rows, :].astype(F32).T
        for h in range(2):
            onehot = jnp.where(lane == spare[h] + j, 1.0, 0.0).astype(BF16)
            kaug_ref[h, rows, :] = jnp.where(heads[h], kj, onehot)
            vt_ref[h, :, rows] = jnp.concatenate([v_t[h * MOBA_D:(h + 1) * MOBA_D], ones], axis=0).astype(BF16)

    km = kmean_ref[...]
    q_hi, q_lo = _split2(q_ref[...])
    km_parts = [_split2(jnp.where(heads[h], km, 0.0)) for h in range(2)]
    g_hi = _nt(jnp.concatenate([km_parts[0][0], km_parts[0][1], km_parts[1][0], km_parts[1][1]], axis=0), q_hi)
    g_lo = _nt(jnp.concatenate([km_parts[0][0], km_parts[1][0]], axis=0), q_lo)
    blk_row = lax.broadcasted_iota(jnp.int32, (nb, 1), 0)
    q_blk = lax.broadcasted_iota(jnp.int32, (1, seq), 1) // blk
    past = blk_row < q_blk
    for h in range(2):
        gate_t = g_hi[2 * h * nb:(2 * h + 1) * nb] + g_hi[(2 * h + 1) * nb:(2 * h + 2) * nb] + g_lo[h * nb:(h + 1) * nb]
        gate_t = jnp.where(past, gate_t, -jnp.inf)
        cnt = jnp.zeros((nb, seq), F32)
        for n in range(nb - 1):
            gn = gate_t[n:n + 1, :]
            before = (gn > gate_t) | ((gn == gate_t) & (n < blk_row))
            cnt = cnt + jnp.where(before, 1.0, 0.0)
        sel = (past & (cnt < float(MOBA_TOPK))) | (blk_row == q_blk)
        bias_t = jnp.where(sel, 0.0, MASKED)
        for c in range(nb):
            parts = []
            if spare[h]:
                parts.append(jnp.zeros((spare[h], blk), F32))
            parts.append(bias_t[:, c * blk:(c + 1) * blk])
            parts.append(jnp.zeros((LANES - spare[h] - nb, blk), F32))
            bias = jnp.concatenate(parts, axis=0).T
            qs = q_ref[c * blk:(c + 1) * blk, :] * (MOBA_D ** -0.5 * LOG2E)
            qaug_ref[c, h] = jnp.where(heads[h], qs, bias).astype(BF16)

    kpos = lax.broadcasted_iota(jnp.int32, (blk, blk), 0)
    qpos = lax.broadcasted_iota(jnp.int32, (blk, blk), 1)
    causal_t = kpos <= qpos

    token = None
    for t in range(nb // 2):
        buf = t % 2
        base = 0
        for c in (nb - 1 - t, t):
            outs = []
            for h in range(2):
                qa = qaug_ref[c, h]
                if token is not None:
                    qa = (qa.astype(F32) + token).astype(BF16)
                st = s_ref.at[buf, h]
                mx = None
                for j in range(c + 1):
                    s = _nt(kaug_ref[h, j * blk:(j + 1) * blk, :], qa)
                    if j == 0:
                        bits = pltpu.bitcast(s[0:1, 0:LANES], jnp.uint32)
                        zero = lax.shift_right_logical(lax.shift_right_logical(bits, jnp.uint32(16)), jnp.uint32(16))
                        token = zero.astype(F32)
                    if j == c:
                        s = jnp.where(causal_t, s, MASKED)
                    st[base + j * blk:base + (j + 1) * blk, :] = s
                    mx = s if mx is None else jnp.maximum(mx, s)
                m = jnp.max(mx, axis=0, keepdims=True)
                acc = None
                for j in range(c + 1):
                    rows = slice(j * blk, (j + 1) * blk)
                    p = jnp.exp2(st[base + j * blk:base + (j + 1) * blk, :] - m).astype(BF16)
                    pv = _nn(vt_ref[h, :, rows], p)
                    acc = pv if acc is None else acc + pv
                outs.append(acc[0:MOBA_D] / acc[MOBA_D:MOBA_D + 1])
            o_ref[c * blk:(c + 1) * blk, :] = jnp.concatenate(outs, axis=0).T.astype(BF16)
            base += (c + 1) * blk


def _moba(qa, ka, va, batch, seq):
    nb = seq // MOBA_BLOCK
    assert nb % 2 == 0 and nb <= MOBA_D
    pairs = GROUP // LANES
    whole = pl.BlockSpec((seq, LANES), lambda b, p: (b, p))
    return pl.pallas_call(
        functools.partial(_moba_kernel, nb=nb),
        grid=(batch, pairs),
        in_specs=[whole, whole, whole],
        out_specs=whole,
        out_shape=jax.ShapeDtypeStruct(qa.shape, BF16),
        scratch_shapes=[pltpu.VMEM((nb, LANES), F32),
                        pltpu.VMEM((2, seq, LANES), BF16),
                        pltpu.VMEM((2, MOBA_VT_ROWS, seq), BF16),
                        pltpu.VMEM((nb, 2, MOBA_BLOCK, LANES), BF16),
                        pltpu.VMEM((2, 2, (nb + 1) * MOBA_BLOCK, MOBA_BLOCK), F32)],
        compiler_params=pltpu.CompilerParams(dimension_semantics=("parallel", "parallel"),
                                             vmem_limit_bytes=VMEM_LIMIT),
        name="moba",
    )(qa, ka, va)


def _hgrn_consts():
    c = HGRN_CHUNK
    t = np.arange(c)
    w_rows, masks = [], []
    for lvl in range(HGRN_LEVELS):
        m = 1 << lvl
        blk = t // m
        odd = (blk % 2) == 1
        start = blk * m
        end = start + m - 1
        u = t[None, :]
        w_odd = (u >= start[:, None]) & (u <= t[:, None])
        w_even = (u > t[:, None]) & (u <= end[:, None])
        w_rows.append(np.where(odd[:, None], w_odd, w_even))
        masks.append(odd[:, None] & (blk[None, :] == blk[:, None] - 1))
    masks.append(t[None, :] == t[:, None])
    w_mm = w_rows[:HGRN_MM_LEVELS] + [t[None, :] <= t[:, None]]
    w_all = np.concatenate(w_mm, axis=0).astype(np.float32)
    w_all = np.concatenate([w_all, w_all], axis=1)
    mask_cat = np.concatenate([masks[-1]] + masks[:-1] + [np.zeros((c, c), bool)], axis=1).astype(np.float32)
    return jnp.asarray(w_all, BF16), jnp.asarray(mask_cat, F32)


def _hgrn_kernel(qh_ref, fh_ref, ih_ref, logit_ref, gain_ref, w_ref, mask_ref, o_ref, state_ref, *, layer):
    c = HGRN_CHUNK
    state_ref[...] = jnp.zeros_like(state_ref)

    if layer > 0:
        lg = logit_ref[...]
        e = jnp.exp(lg - jnp.max(lg, axis=0, keepdims=True))
        sm = e / jnp.sum(e, axis=0, keepdims=True)
        lb = jnp.sum(sm[1:layer + 1, :], axis=0, keepdims=True)
        log_lb = jnp.log(lb)
        log_1m_lb = jnp.log1p(-lb)
    w_all = w_ref[...]
    gain = gain_ref[...]

    def prepare(base, ci):
        rows = pl.ds(base + ci * c, c)
        fl = fh_ref[rows, :]
        vi = ih_ref[rows, :]
        qf = _silu(qh_ref[rows, :].astype(F32))
        e = jnp.exp(-jnp.abs(fl))
        r = 1.0 / (1.0 + e)
        log_sig = jnp.minimum(fl, 0.0) - jnp.log(1.0 + e)
        sig_neg = jnp.where(fl >= 0.0, e * r, r)
        if layer > 0:
            b = log_1m_lb + log_sig
            lf = jnp.maximum(log_lb, b) + jnp.log(1.0 + jnp.exp(-jnp.abs(log_lb - b)))
            kf = (1.0 - lb) * sig_neg
        else:
            lf = log_sig
            kf = sig_neg

        f1, f2 = _split2(lf * LOG2E)
        x_mm = _nn(w_all, jnp.concatenate([f1, f2], axis=0))
        a2 = x_mm[HGRN_MM_LEVELS * c:(HGRN_MM_LEVELS + 1) * c, :]
        e_lvl = [jnp.exp2(x_mm[lvl * c:(lvl + 1) * c, :]) for lvl in range(HGRN_MM_LEVELS)]
        for lvl in range(HGRN_MM_LEVELS, HGRN_LEVELS):
            m = 1 << lvl
            parts = []
            for g in range(c // (2 * m)):
                ref = a2[2 * m * g + m - 1:2 * m * g + m, :]
                parts.append(-jnp.abs(a2[2 * m * g:2 * m * (g + 1), :] - ref))
            e_lvl.append(jnp.exp2(jnp.concatenate(parts, axis=0)))
        e_cum = jnp.exp2(a2)
        e_rest = jnp.exp2(a2[c - 1:c, :] - a2)
        e_end = e_cum[c - 1:c, :]
        return qf, kf, vi, e_lvl, e_cum, e_rest, e_end

    def recur(base, ci, prepared):
        rows = pl.ds(base + ci * c, c)
        qf, kf, vi, e_lvl, e_cum, e_rest, e_end = prepared
        for h in range(HGRN_HEADS):
            cols = slice(h * HGRN_D, (h + 1) * HGRN_D)
            qfh, kfh = qf[:, cols], kf[:, cols]
            qb = [qfh.astype(BF16)] + [(qfh * e[:, cols]).astype(BF16) for e in e_lvl]
            kb = [kfh.astype(BF16)] + [(kfh * e[:, cols]).astype(BF16) for e in e_lvl]
            zero = jnp.zeros((c, HGRN_D), BF16)
            s_parts = []
            for a in range(0, HGRN_LEVELS, 2):
                q_cat = jnp.concatenate([qb[a], qb[a + 1]], axis=1)
                k_blk = jnp.concatenate([jnp.concatenate([kb[a], zero], axis=1),
                                         jnp.concatenate([zero, kb[a + 1]], axis=1)], axis=0)
                s_parts.append(_nt(q_cat, k_blk))
            s_parts.append(_nt(qb[HGRN_LEVELS], jnp.concatenate([kb[HGRN_LEVELS], zero], axis=0)))
            p = None
            for a, part in enumerate(s_parts):
                part = part * mask_ref[:, 2 * a * c:(2 * a + 2) * c]
                p = part if p is None else p + part
            vb = vi[:, cols]
            st = state_ref[:, cols]
            o = _nn(p.astype(BF16), jnp.concatenate([vb, vb], axis=0))
            o = o + _nt((qfh * e_cum[:, cols]).astype(BF16), st.astype(BF16))
            state_ref[:, cols] = st * e_end[:, cols] + _tn(vb, (kfh * e_rest[:, cols]).astype(BF16))
            ms = jnp.mean(o * o, axis=-1, keepdims=True)
            o_ref[rows, cols] = (o * lax.rsqrt(ms + NORM_EPS) * gain[:, cols]).astype(BF16)

    n_chunks = HGRN_TILE // c

    def group(g, carry):
        base = pl.multiple_of(g * HGRN_TILE, HGRN_TILE)
        prepared = prepare(base, 0)
        for ci in range(n_chunks):
            nxt = prepare(base, ci + 1) if ci + 1 < n_chunks else None
            recur(base, ci, prepared)
            prepared = nxt
        return carry

    lax.fori_loop(0, qh_ref.shape[0] // HGRN_TILE, group, 0)


def _hgrn(qh, fh, ih, logits, gain, layer, batch, seq):
    w_all, mask_all = _hgrn_consts()
    assert seq % HGRN_TILE == 0
    depth = logits.shape[0]
    const = lambda b: (0, 0)
    return pl.pallas_call(
        functools.partial(_hgrn_kernel, layer=layer),
        grid=(batch,),
        in_specs=[pl.BlockSpec((seq, GROUP), lambda b: (b, 0))] * 3 + [
                  pl.BlockSpec((depth, GROUP), const),
                  pl.BlockSpec((None, 1, GROUP), lambda b: (layer, 0, 0)),
                  pl.BlockSpec(w_all.shape, const),
                  pl.BlockSpec(mask_all.shape, const)],
        out_specs=pl.BlockSpec((seq, GROUP), lambda b: (b, 0)),
        out_shape=jax.ShapeDtypeStruct((fh.shape[0], GROUP), BF16),
        scratch_shapes=[pltpu.VMEM((HGRN_D, GROUP), F32)],
        compiler_params=pltpu.CompilerParams(dimension_semantics=("parallel",),
                                             vmem_limit_bytes=VMEM_LIMIT),
        name="hgrn",
    )(qh, fh, ih, logits, gain, w_all, mask_all)


def _mem_kv_kernel(m_ref, g_ref, w_ref, gk_ref, seg_ref, k_ref, v_ref):
    x = m_ref[...]
    ms = jnp.mean(x * x, axis=-1, keepdims=True)
    hb = (x * lax.rsqrt(ms + NORM_EPS) * g_ref[...]).astype(BF16)
    gk = gk_ref[...]
    seg = seg_ref[...]
    for c in range(GROUP // 256):
        kc = _nn(hb, w_ref[:, c * 256:(c + 1) * 256].astype(BF16))
        ss = _nn((kc * kc).astype(BF16), seg) * (1.0 / MEM_D)
        kn = kc * lax.rsqrt(ss + NORM_EPS) * gk[:, c * 256:(c + 1) * 256]
        k_ref[:, c * 256:(c + 1) * 256] = kn.astype(BF16)
    v_ref[...] = _nn(hb, w_ref[:, GROUP:2 * GROUP].astype(BF16)).astype(BF16)


def _mem_kv(memf, g, w, gk, mem_len, layer):
    n = memf.shape[0]
    seg = np.arange(256)[:, None] // MEM_D == np.arange(256)[None, :] // MEM_D
    seg = jnp.asarray(seg, BF16)
    const = lambda i: (0, 0)
    row = lambda i: (i, 0)
    of_layer = lambda i: (layer, 0, 0)
    return pl.pallas_call(
        _mem_kv_kernel,
        grid=(n // mem_len,),
        in_specs=[pl.BlockSpec((mem_len, D_MODEL), row),
                  pl.BlockSpec((None, 1, D_MODEL), of_layer),
                  pl.BlockSpec((None, D_MODEL, 2 * GROUP), of_layer),
                  pl.BlockSpec((None, 1, GROUP), of_layer),
                  pl.BlockSpec((256, 256), const)],
        out_specs=[pl.BlockSpec((mem_len, GROUP), row)] * 2,
        out_shape=[jax.ShapeDtypeStruct((n, GROUP), BF16)] * 2,
        compiler_params=pltpu.CompilerParams(dimension_semantics=("parallel",), vmem_limit_bytes=VMEM_LIMIT),
        name="mem_kv",
    )(memf, g, w, gk, seg)


def _out_proj_kernel(x_ref, oa_ref, oh_ref, qm_ref, z_ref, km_ref, vm_ref, gq_ref, w_ref, o_ref, y_ref):
    def gate(o, c0, c1):
        y_ref[:, c0:c1] = (o * _silu(z_ref[:, c0:c1].astype(F32))).astype(BF16)

    gate(oa_ref[...].astype(F32), 0, GROUP)
    gate(oh_ref[...].astype(F32), GROUP, 2 * GROUP)
    gq = gq_ref[...]
    for h in range(MEM_HEADS):
        cols = slice(h * MEM_D, (h + 1) * MEM_D)
        q = qm_ref[:, cols].astype(F32)
        ms = jnp.mean(q * q, axis=-1, keepdims=True)
        qn = q * lax.rsqrt(ms + NORM_EPS) * gq * (MEM_D ** -0.5)
        s = _nt(qn.astype(BF16), km_ref[:, cols])
        p = jnp.exp(s - jnp.max(s, axis=1, keepdims=True))
        v_aug = jnp.concatenate([vm_ref[:, cols], jnp.ones((vm_ref.shape[0], MEM_D), BF16)], axis=1)
        pv = _nn(p.astype(BF16), v_aug)
        om = pv[:, 0:MEM_D] / pv[:, MEM_D:2 * MEM_D]
        c0 = 2 * GROUP + h * MEM_D
        gate(om, c0, c0 + MEM_D)
    o_ref[...] = x_ref[...] + _nn(y_ref[...], w_ref[...].astype(BF16))


def _out_proj(xf, oa, oh, qm, z, km, vm, gq, w, seq, mem_len, layer):
    n = xf.shape[0]
    tm = OUT_TILE
    steps = seq // tm
    const = lambda i: (0, 0)
    row = lambda i: (i, 0)
    per_batch = lambda i: (i // steps, 0)
    of_layer = lambda i: (layer, 0, 0)
    return pl.pallas_call(
        _out_proj_kernel,
        grid=(n // tm,),
        in_specs=[pl.BlockSpec((tm, D_MODEL), row),
                  pl.BlockSpec((tm, GROUP), row),
                  pl.BlockSpec((tm, GROUP), row),
                  pl.BlockSpec((tm, GROUP), row),
                  pl.BlockSpec((tm, D_MIX), row),
                  pl.BlockSpec((mem_len, GROUP), per_batch),
                  pl.BlockSpec((mem_len, GROUP), per_batch),
                  pl.BlockSpec((None, 1, MEM_D), of_layer),
                  pl.BlockSpec((None, D_MIX, D_MODEL), of_layer, pipeline_mode=pl.Buffered(1))],
        out_specs=pl.BlockSpec((tm, D_MODEL), row),
        out_shape=jax.ShapeDtypeStruct((n, D_MODEL), F32),
        scratch_shapes=[pltpu.VMEM((tm, D_MIX), BF16)],
        compiler_params=pltpu.CompilerParams(dimension_semantics=("parallel",), vmem_limit_bytes=VMEM_LIMIT),
        name="out_proj",
    )(xf, oa, oh, qm, z, km, vm, gq, w)


def kernel(x, mem, positions, norm_g, w_in, w_out, moba_q_norm, moba_k_norm, hgrn_lb_logits, hgrn_o_norm,
           mem_norm_g, w_mem_kv, mem_q_norm, mem_k_norm):
    batch, seq, d_model = x.shape
    mem_len = mem.shape[1]
    depth = w_in.shape[0]
    assert d_model == D_MODEL and seq % ROW_TILE == 0 and seq % OUT_TILE == 0 and mem_len % 8 == 0
    n = batch * seq
    xf = x.reshape(n, d_model)
    memf = mem.reshape(batch * mem_len, d_model)
    cosf, sinf = _rope_tables(positions.reshape(1, n))
    per_head = lambda p, reps: jnp.tile(p, (1, reps))[:, None, :]
    norm_g, mem_norm_g, mem_q_norm = norm_g[:, None, :], mem_norm_g[:, None, :], mem_q_norm[:, None, :]
    gq, gk = per_head(moba_q_norm, GROUP // MOBA_D), per_head(moba_k_norm, GROUP // MOBA_D)
    g_hgrn, g_memk = per_head(hgrn_o_norm, HGRN_HEADS), per_head(mem_k_norm, MEM_HEADS)
    for l in range(depth):
        qa, ka, va, qh, fh, ih, qm, z = _in_proj(xf, norm_g, w_in, cosf, sinf, gq, gk, l)
        oa = _moba(qa, ka, va, batch, seq)
        oh = _hgrn(qh, fh, ih, hgrn_lb_logits, g_hgrn, l, batch, seq)
        km, vm = _mem_kv(memf, mem_norm_g, w_mem_kv, g_memk, mem_len, l)
        xf = _out_proj(xf, oa, oh, qm, z, km, vm, mem_q_norm, w_out, seq, mem_len, l)
    return xf.reshape(batch, seq, d_model)
```

```python
import functools

import numpy as np
import jax
import jax.numpy as jnp
from jax import lax
from jax.experimental import pallas as pl
from jax.experimental.pallas import tpu as pltpu

F32 = jnp.float32
BF16 = jnp.bfloat16

D_MODEL = 1024
GROUP = D_MODEL // 2
D_MIX = 3 * GROUP
IN_COLS = 7 * GROUP + D_MIX
MOBA_D = 64
MOBA_BLOCK = 256
MOBA_TOPK = 3
MOBA_VT_ROWS = MOBA_D + 16
HGRN_D = 128
HGRN_HEADS = GROUP // HGRN_D
HGRN_CHUNK = 64
HGRN_LEVELS = 6
HGRN_MM_LEVELS = 3
LOG2E = 1.4426950408889634
MEM_D = 128
MEM_HEADS = GROUP // MEM_D
ROPE_THETA = 500000.0
ROPE_DIM = MOBA_D // 4
ROPE_HALF = ROPE_DIM // 2
NORM_EPS = 1e-6
LANES = 128
ROW_TILE = 512
OUT_TILE = 1024
HGRN_TILE = 1024
MASKED = -1e30
VMEM_LIMIT = 48 * 1024 * 1024


def _nt(a, b):
    return lax.dot_general(a, b, (((1,), (1,)), ((), ())), preferred_element_type=F32)


def _tn(a, b):
    return lax.dot_general(a, b, (((0,), (0,)), ((), ())), preferred_element_type=F32)


def _nn(a, b):
    return jnp.dot(a, b, preferred_element_type=F32)


def _split2(x):
    hi = x.astype(BF16)
    lo = (x - hi.astype(F32)).astype(BF16)
    return hi, lo


def _split3(x):
    hi = x.astype(BF16)
    r = x - hi.astype(F32)
    mid = r.astype(BF16)
    lo = (r - mid.astype(F32)).astype(BF16)
    return hi, mid, lo


def _sigmoid(x):
    return 1.0 / (1.0 + jnp.exp(-x))


def _silu(x):
    return x * _sigmoid(x)


def _rope_table_kernel(pos_ref, invf_ref, expand_ref, base_ref, cos_ref, sin_ref):
    ang = invf_ref[...] * pos_ref[...].astype(F32)
    tm = ang.shape[1]
    table = jnp.concatenate([jnp.cos(ang), jnp.sin(ang), jnp.zeros((LANES - ROPE_DIM, tm), F32)], axis=0)
    hi, mid, lo = _split3(table.T)
    e = expand_ref[...]
    both = _nn(hi, e) + _nn(mid, e) + _nn(lo, e)
    cos_ref[...] = both[:, 0:LANES] + base_ref[...]
    sin_ref[...] = both[:, LANES:2 * LANES]


def _rope_tables(pos):
    n = pos.shape[1]
    tm = min(n, 2048)
    lane = np.arange(LANES) % MOBA_D
    freq = lane % ROPE_HALF
    rope = lane < ROPE_DIM
    sgn = np.where(lane < ROPE_HALF, -1.0, 1.0)
    expand = np.zeros((LANES, 2 * LANES), np.float32)
    for l in np.nonzero(rope)[0]:
        expand[freq[l], l] = 1.0
        expand[ROPE_HALF + freq[l], LANES + l] = sgn[l]
    base = np.where(rope, 0.0, 1.0).astype(np.float32)[None, :]
    invf = (ROPE_THETA ** (-np.arange(ROPE_HALF, dtype=np.float64) / ROPE_HALF)).astype(np.float32)[:, None]
    const = lambda i: (0, 0)
    return pl.pallas_call(
        _rope_table_kernel,
        grid=(n // tm,),
        in_specs=[pl.BlockSpec((1, tm), lambda i: (0, i)),
                  pl.BlockSpec((ROPE_HALF, 1), const),
                  pl.BlockSpec((LANES, 2 * LANES), const),
                  pl.BlockSpec((1, LANES), const)],
        out_specs=[pl.BlockSpec((tm, LANES), lambda i: (i, 0))] * 2,
        out_shape=[jax.ShapeDtypeStruct((n, LANES), F32)] * 2,
        name="rope_tables",
    )(pos, jnp.asarray(invf), jnp.asarray(expand, BF16), jnp.asarray(base))


def _in_proj_kernel(x_ref, g_ref, w_ref, cos_ref, sin_ref, gq_ref, gk_ref, seg_ref,
                    qa_ref, ka_ref, va_ref, qh_ref, fh_ref, ih_ref, qm_ref, z_ref):
    x = x_ref[...]
    ms = jnp.mean(x * x, axis=-1, keepdims=True)
    hb = (x * lax.rsqrt(ms + NORM_EPS) * g_ref[...]).astype(BF16)

    def proj(c0, width):
        return _nn(hb, w_ref[:, c0:c0 + width].astype(BF16))

    cosf = cos_ref[...]
    sinf = sin_ref[...]
    lane = lax.broadcasted_iota(jnp.int32, (1, LANES), 1) % MOBA_D
    lo_lane = lane < ROPE_HALF
    seg = seg_ref[...]

    def head_norm_rope(p, gain, out_ref):
        for c in range(GROUP // 256):
            pc = p[:, c * 256:(c + 1) * 256]
            ss = _nn((pc * pc).astype(BF16), seg) * (1.0 / MOBA_D)
            pn = pc * lax.rsqrt(ss + NORM_EPS) * gain[:, c * 256:(c + 1) * 256]
            for v in range(2):
                xv = pn[:, v * LANES:(v + 1) * LANES]
                rot = jnp.where(lo_lane, pltpu.roll(xv, LANES - ROPE_HALF, 1), pltpu.roll(xv, ROPE_HALF, 1))
                col = c * 256 + v * LANES
                out_ref[:, col:col + LANES] = (xv * cosf + rot * sinf).astype(out_ref.dtype)

    head_norm_rope(proj(0, GROUP), gq_ref[...], qa_ref)
    head_norm_rope(proj(GROUP, GROUP), gk_ref[...], ka_ref)
    va_ref[...] = proj(2 * GROUP, GROUP).astype(BF16)
    qh_ref[...] = proj(3 * GROUP, GROUP).astype(BF16)
    fh_ref[...] = proj(4 * GROUP, GROUP)
    ih_ref[...] = proj(5 * GROUP, GROUP).astype(BF16)
    qm_ref[...] = proj(6 * GROUP, GROUP).astype(BF16)
    for c in range(3):
        z_ref[:, c * GROUP:(c + 1) * GROUP] = proj((7 + c) * GROUP, GROUP).astype(BF16)


def _in_proj(xf, g, w, cosf, sinf, gq, gk, layer):
    n = xf.shape[0]
    tm = ROW_TILE
    seg = np.arange(256)[:, None] // MOBA_D == np.arange(256)[None, :] // MOBA_D
    seg = jnp.asarray(seg, BF16)
    const = lambda i: (0, 0)
    row = lambda i: (i, 0)
    of_layer = lambda i: (layer, 0, 0)
    outs = ((GROUP, F32), (GROUP, BF16), (GROUP, BF16), (GROUP, BF16), (GROUP, F32), (GROUP, BF16),
            (GROUP, BF16), (D_MIX, BF16))
    return pl.pallas_call(
        _in_proj_kernel,
        grid=(n // tm,),
        in_specs=[pl.BlockSpec((tm, D_MODEL), row),
                  pl.BlockSpec((None, 1, D_MODEL), of_layer),
                  pl.BlockSpec((None, D_MODEL, IN_COLS), of_layer, pipeline_mode=pl.Buffered(1)),
                  pl.BlockSpec((tm, LANES), row),
                  pl.BlockSpec((tm, LANES), row),
                  pl.BlockSpec((None, 1, GROUP), of_layer),
                  pl.BlockSpec((None, 1, GROUP), of_layer),
                  pl.BlockSpec((256, 256), const)],
        out_specs=[pl.BlockSpec((tm, w), row) for w, _ in outs],
        out_shape=[jax.ShapeDtypeStruct((n, w), dt) for w, dt in outs],
        compiler_params=pltpu.CompilerParams(dimension_semantics=("parallel",), vmem_limit_bytes=VMEM_LIMIT),
        name="in_proj",
    )(xf, g, w, cosf, sinf, gq, gk, seg)


def _moba_kernel(q_ref, k_ref, v_ref, o_ref, kmean_ref, kaug_ref, vt_ref, qaug_ref, s_ref, *, nb):
    blk = MOBA_BLOCK
    seq = nb * blk
    lane = lax.broadcasted_iota(jnp.int32, (1, LANES), 1)
    heads = [(lane >= h * MOBA_D) & (lane < (h + 1) * MOBA_D) for h in range(2)]
    spare = [(1 - h) * MOBA_D for h in range(2)]

    ones = jnp.ones((MOBA_VT_ROWS - MOBA_D, blk), F32)
    for j in range(nb):
        rows = slice(j * blk, (j + 1) * blk)
        kj = k_ref[rows, :]
        kmean_ref[j:j + 1, :] = jnp.mean(kj.astype(F32), axis=0, keepdims=True)
        v_t = v_ref[rows, :].astype(F32).T
        for h in range(2):
            onehot = jnp.where(lane == spare[h] + j, 1.0, 0.0).astype(BF16)
            kaug_ref[h, rows, :] = jnp.where(heads[h], kj, onehot)
            vt_ref[h, :, rows] = jnp.concatenate([v_t[h * MOBA_D:(h + 1) * MOBA_D], ones], axis=0).astype(BF16)

    km = kmean_ref[...]
    q_hi, q_lo = _split2(q_ref[...])
    km_parts = [_split2(jnp.where(heads[h], km, 0.0)) for h in range(2)]
    g_hi = _nt(jnp.concatenate([km_parts[0][0], km_parts[0][1], km_parts[1][0], km_parts[1][1]], axis=0), q_hi)
    g_lo = _nt(jnp.concatenate([km_parts[0][0], km_parts[1][0]], axis=0), q_lo)
    blk_row = lax.broadcasted_iota(jnp.int32, (nb, 1), 0)
    q_blk = lax.broadcasted_iota(jnp.int32, (1, seq), 1) // blk
    past = blk_row < q_blk
    for h in range(2):
        gate_t = g_hi[2 * h * nb:(2 * h + 1) * nb] + g_hi[(2 * h + 1) * nb:(2 * h + 2) * nb] + g_lo[h * nb:(h + 1) * nb]
        gate_t = jnp.where(past, gate_t, -jnp.inf)
        cnt = jnp.zeros((nb, seq), F32)
        for n in range(nb - 1):
            gn = gate_t[n:n + 1, :]
            before = (gn > gate_t) | ((gn == gate_t) & (n < blk_row))
            cnt = cnt + jnp.where(before, 1.0, 0.0)
        sel = (past & (cnt < float(MOBA_TOPK))) | (blk_row == q_blk)
        bias_t = jnp.where(sel, 0.0, MASKED)
        for c in range(nb):
            parts = []
            if spare[h]:
                parts.append(jnp.zeros((spare[h], blk), F32))
            parts.append(bias_t[:, c * blk:(c + 1) * blk])
            parts.append(jnp.zeros((LANES - spare[h] - nb, blk), F32))
            bias = jnp.concatenate(parts, axis=0).T
            qs = q_ref[c * blk:(c + 1) * blk, :] * (MOBA_D ** -0.5 * LOG2E)
            qaug_ref[c, h] = jnp.where(heads[h], qs, bias).astype(BF16)

    kpos = lax.broadcasted_iota(jnp.int32, (blk, blk), 0)
    qpos = lax.broadcasted_iota(jnp.int32, (blk, blk), 1)
    causal_t = kpos <= qpos

    token = None
    for t in range(nb // 2):
        buf = t % 2
        base = 0
        for c in (nb - 1 - t, t):
            outs = []
            for h in range(2):
                qa = qaug_ref[c, h]
                if token is not None:
                    qa = (qa.astype(F32) + token).astype(BF16)
                st = s_ref.at[buf, h]
                mx = None
                for j in range(c + 1):
                    s = _nt(kaug_ref[h, j * blk:(j + 1) * blk, :], qa)
                    if j == 0:
                        bits = pltpu.bitcast(s[0:1, 0:LANES], jnp.uint32)
                        zero = lax.shift_right_logical(lax.shift_right_logical(bits, jnp.uint32(16)), jnp.uint32(16))
                        token = zero.astype(F32)
                    if j == c:
                        s = jnp.where(causal_t, s, MASKED)
                    st[base + j * blk:base + (j + 1) * blk, :] = s
                    mx = s if mx is None else jnp.maximum(mx, s)
                m = jnp.max(mx, axis=0, keepdims=True)
                acc = None
                for j in range(c + 1):
                    rows = slice(j * blk, (j + 1) * blk)
                    p = jnp.exp2(st[base + j * blk:base + (j + 1) * blk, :] - m).astype(BF16)
                    pv = _nn(vt_ref[h, :, rows], p)
                    acc = pv if acc is None else acc + pv
                outs.append(acc[0:MOBA_D] / acc[MOBA_D:MOBA_D + 1])
            o_ref[c * blk:(c + 1) * blk, :] = jnp.concatenate(outs, axis=0).T.astype(BF16)
            base += (c + 1) * blk


def _moba(qa, ka, va, batch, seq):
    nb = seq // MOBA_BLOCK
    assert nb % 2 == 0 and nb <= MOBA_D
    pairs = GROUP // LANES
    whole = pl.BlockSpec((seq, LANES), lambda b, p: (b, p))
    return pl.pallas_call(
        functools.partial(_moba_kernel, nb=nb),
        grid=(batch, pairs),
        in_specs=[whole, whole, whole],
        out_specs=whole,
        out_shape=jax.ShapeDtypeStruct(qa.shape, BF16),
        scratch_shapes=[pltpu.VMEM((nb, LANES), F32),
                        pltpu.VMEM((2, seq, LANES), BF16),
                        pltpu.VMEM((2, MOBA_VT_ROWS, seq), BF16),
                        pltpu.VMEM((nb, 2, MOBA_BLOCK, LANES), BF16),
                        pltpu.VMEM((2, 2, (nb + 1) * MOBA_BLOCK, MOBA_BLOCK), F32)],
        compiler_params=pltpu.CompilerParams(dimension_semantics=("parallel", "parallel"),
                                             vmem_limit_bytes=VMEM_LIMIT),
        name="moba",
    )(qa, ka, va)


def _hgrn_consts():
    c = HGRN_CHUNK
    t = np.arange(c)
    w_rows, masks = [], []
    for lvl in range(HGRN_LEVELS):
        m = 1 << lvl
        blk = t // m
        odd = (blk % 2) == 1
        start = blk * m
        end = start + m - 1
        u = t[None, :]
        w_odd = (u >= start[:, None]) & (u <= t[:, None])
        w_even = (u > t[:, None]) & (u <= end[:, None])
        w_rows.append(np.where(odd[:, None], w_odd, w_even))
        masks.append(odd[:, None] & (blk[None, :] == blk[:, None] - 1))
    masks.append(t[None, :] == t[:, None])
    w_mm = w_rows[:HGRN_MM_LEVELS] + [t[None, :] <= t[:, None]]
    w_all = np.concatenate(w_mm, axis=0).astype(np.float32)
    w_all = np.concatenate([w_all, w_all], axis=1)
    mask_cat = np.concatenate([masks[-1]] + masks[:-1] + [np.zeros((c, c), bool)], axis=1).astype(np.float32)
    return jnp.asarray(w_all, BF16), jnp.asarray(mask_cat, F32)


def _hgrn_kernel(qh_ref, fh_ref, ih_ref, logit_ref, gain_ref, w_ref, mask_ref, o_ref, state_ref, *, layer):
    c = HGRN_CHUNK
    state_ref[...] = jnp.zeros_like(state_ref)

    if layer > 0:
        lg = logit_ref[...]
        e = jnp.exp(lg - jnp.max(lg, axis=0, keepdims=True))
        sm = e / jnp.sum(e, axis=0, keepdims=True)
        lb = jnp.sum(sm[1:layer + 1, :], axis=0, keepdims=True)
        log_lb = jnp.log(lb)
        log_1m_lb = jnp.log1p(-lb)
    w_all = w_ref[...]
    gain = gain_ref[...]

    def prepare(base, ci):
        rows = pl.ds(base + ci * c, c)
        fl = fh_ref[rows, :]
        vi = ih_ref[rows, :]
        qf = _silu(qh_ref[rows, :].astype(F32))
        e = jnp.exp(-jnp.abs(fl))
        r = 1.0 / (1.0 + e)
        log_sig = jnp.minimum(fl, 0.0) - jnp.log(1.0 + e)
        sig_neg = jnp.where(fl >= 0.0, e * r, r)
        if layer > 0:
            b = log_1m_lb + log_sig
            lf = jnp.maximum(log_lb, b) + jnp.log(1.0 + jnp.exp(-jnp.abs(log_lb - b)))
            kf = (1.0 - lb) * sig_neg
        else:
            lf = log_sig
            kf = sig_neg

        f1, f2 = _split2(lf * LOG2E)
        x_mm = _nn(w_all, jnp.concatenate([f1, f2], axis=0))
        a2 = x_mm[HGRN_MM_LEVELS * c:(HGRN_MM_LEVELS + 1) * c, :]
        e_lvl = [jnp.exp2(x_mm[lvl * c:(lvl + 1) * c, :]) for lvl in range(HGRN_MM_LEVELS)]
        for lvl in range(HGRN_MM_LEVELS, HGRN_LEVELS):
            m = 1 << lvl
            parts = []
            for g in range(c // (2 * m)):
                ref = a2[2 * m * g + m - 1:2 * m * g + m, :]
                parts.append(-jnp.abs(a2[2 * m * g:2 * m * (g + 1), :] - ref))
            e_lvl.append(jnp.exp2(jnp.concatenate(parts, axis=0)))
        e_cum = jnp.exp2(a2)
        e_rest = jnp.exp2(a2[c - 1:c, :] - a2)
        e_end = e_cum[c - 1:c, :]
        return qf, kf, vi, e_lvl, e_cum, e_rest, e_end

    def recur(base, ci, prepared):
        rows = pl.ds(base + ci * c, c)
        qf, kf, vi, e_lvl, e_cum, e_rest, e_end = prepared
        zero = jnp.zeros((c, HGRN_D), BF16)
        zsq = jnp.zeros((HGRN_D, HGRN_D), BF16)
        for hp in range(HGRN_HEADS // 2):
            p_pair, ql_pair, v_pair, st_pair = [], [], [], []
            for h in (2 * hp, 2 * hp + 1):
                cols = slice(h * HGRN_D, (h + 1) * HGRN_D)
                qfh, kfh = qf[:, cols], kf[:, cols]
                qb = [qfh.astype(BF16)] + [(qfh * e[:, cols]).astype(BF16) for e in e_lvl]
                kb = [kfh.astype(BF16)] + [(kfh * e[:, cols]).astype(BF16) for e in e_lvl]
                s_parts = []
                for a in range(0, HGRN_LEVELS, 2):
                    q_cat = jnp.concatenate([qb[a], qb[a + 1]], axis=1)
                    k_blk = jnp.concatenate([jnp.concatenate([kb[a], zero], axis=1),
                                             jnp.concatenate([zero, kb[a + 1]], axis=1)], axis=0)
                    s_parts.append(_nt(q_cat, k_blk))
                s_parts.append(_nt(qb[HGRN_LEVELS], jnp.concatenate([kb[HGRN_LEVELS], zero], axis=0)))
                p = None
                for a, part in enumerate(s_parts):
                    part = part * mask_ref[:, 2 * a * c:(2 * a + 2) * c]
                    p = part if p is None else p + part
                p_pair.append(p.astype(BF16))
                ql_pair.append((qfh * e_cum[:, cols]).astype(BF16))
                v_pair.append(vi[:, cols])
                st_pair.append(state_ref[:, cols])
            vv = [jnp.concatenate([v, v], axis=0) for v in v_pair]
            zv = jnp.zeros_like(vv[0])
            v_blk = jnp.concatenate([jnp.concatenate([vv[0], zv], axis=1),
                                     jnp.concatenate([zv, vv[1]], axis=1)], axis=0)
            o_pair = _nn(jnp.concatenate(p_pair, axis=1), v_blk)
            st_b = [st.astype(BF16) for st in st_pair]
            st_blk = jnp.concatenate([jnp.concatenate([st_b[0], zsq], axis=1),
                                      jnp.concatenate([zsq, st_b[1]], axis=1)], axis=0)
            o_pair = o_pair + _nt(jnp.concatenate(ql_pair, axis=1), st_blk)
            for i, h in enumerate((2 * hp, 2 * hp + 1)):
                cols = slice(h * HGRN_D, (h + 1) * HGRN_D)
                o = o_pair[:, i * HGRN_D:(i + 1) * HGRN_D]
                kdec = (kf[:, cols] * e_rest[:, cols]).astype(BF16)
                state_ref[:, cols] = st_pair[i] * e_end[:, cols] + _tn(v_pair[i], kdec)
                ms = jnp.mean(o * o, axis=-1, keepdims=True)
                o_ref[rows, cols] = (o * lax.rsqrt(ms + NORM_EPS) * gain[:, cols]).astype(BF16)

    n_chunks = HGRN_TILE // c

    def group(g, carry):
        base = pl.multiple_of(g * HGRN_TILE, HGRN_TILE)
        prepared = prepare(base, 0)
        for ci in range(n_chunks):
            nxt = prepare(base, ci + 1) if ci + 1 < n_chunks else None
            recur(base, ci, prepared)
            prepared = nxt
        return carry

    lax.fori_loop(0, qh_ref.shape[0] // HGRN_TILE, group, 0)


def _hgrn(qh, fh, ih, logits, gain, layer, batch, seq):
    w_all, mask_all = _hgrn_consts()
    assert seq % HGRN_TILE == 0
    depth = logits.shape[0]
    const = lambda b: (0, 0)
    return pl.pallas_call(
        functools.partial(_hgrn_kernel, layer=layer),
        grid=(batch,),
        in_specs=[pl.BlockSpec((seq, GROUP), lambda b: (b, 0))] * 3 + [
                  pl.BlockSpec((depth, GROUP), const),
                  pl.BlockSpec((None, 1, GROUP), lambda b: (layer, 0, 0)),
                  pl.BlockSpec(w_all.shape, const),
                  pl.BlockSpec(mask_all.shape, const)],
        out_specs=pl.BlockSpec((seq, GROUP), lambda b: (b, 0)),
        out_shape=jax.ShapeDtypeStruct((fh.shape[0], GROUP), BF16),
        scratch_shapes=[pltpu.VMEM((HGRN_D, GROUP), F32)],
        compiler_params=pltpu.CompilerParams(dimension_semantics=("parallel",),
                                             vmem_limit_bytes=VMEM_LIMIT),
        name="hgrn",
    )(qh, fh, ih, logits, gain, w_all, mask_all)


def _mem_kv_kernel(m_ref, g_ref, w_ref, gk_ref, seg_ref, k_ref, v_ref):
    x = m_ref[...]
    ms = jnp.mean(x * x, axis=-1, keepdims=True)
    hb = (x * lax.rsqrt(ms + NORM_EPS) * g_ref[...]).astype(BF16)
    gk = gk_ref[...]
    seg = seg_ref[...]
    for c in range(GROUP // 256):
        kc = _nn(hb, w_ref[:, c * 256:(c + 1) * 256].astype(BF16))
        ss = _nn((kc * kc).astype(BF16), seg) * (1.0 / MEM_D)
        kn = kc * lax.rsqrt(ss + NORM_EPS) * gk[:, c * 256:(c + 1) * 256]
        k_ref[:, c * 256:(c + 1) * 256] = kn.astype(BF16)
    v_ref[...] = _nn(hb, w_ref[:, GROUP:2 * GROUP].astype(BF16)).astype(BF16)


def _mem_kv(memf, g, w, gk, mem_len, layer):
    n = memf.shape[0]
    seg = np.arange(256)[:, None] // MEM_D == np.arange(256)[None, :] // MEM_D
    seg = jnp.asarray(seg, BF16)
    const = lambda i: (0, 0)
    row = lambda i: (i, 0)
    of_layer = lambda i: (layer, 0, 0)
    return pl.pallas_call(
        _mem_kv_kernel,
        grid=(n // mem_len,),
        in_specs=[pl.BlockSpec((mem_len, D_MODEL), row),
                  pl.BlockSpec((None, 1, D_MODEL), of_layer),
                  pl.BlockSpec((None, D_MODEL, 2 * GROUP), of_layer),
                  pl.BlockSpec((None, 1, GROUP), of_layer),
                  pl.BlockSpec((256, 256), const)],
        out_specs=[pl.BlockSpec((mem_len, GROUP), row)] * 2,
        out_shape=[jax.ShapeDtypeStruct((n, GROUP), BF16)] * 2,
        compiler_params=pltpu.CompilerParams(dimension_semantics=("parallel",), vmem_limit_bytes=VMEM_LIMIT),
        name="mem_kv",
    )(memf, g, w, gk, seg)


def _out_proj_kernel(x_ref, oa_ref, oh_ref, qm_ref, z_ref, km_ref, vm_ref, gq_ref, w_ref, o_ref, y_ref):
    def gate(o, c0, c1):
        y_ref[:, c0:c1] = (o * _silu(z_ref[:, c0:c1].astype(F32))).astype(BF16)

    gate(oa_ref[...].astype(F32), 0, GROUP)
    gate(oh_ref[...].astype(F32), GROUP, 2 * GROUP)
    gq = gq_ref[...]
    for h in range(MEM_HEADS):
        cols = slice(h * MEM_D, (h + 1) * MEM_D)
        q = qm_ref[:, cols].astype(F32)
        ms = jnp.mean(q * q, axis=-1, keepdims=True)
        qn = q * lax.rsqrt(ms + NORM_EPS) * gq * (MEM_D ** -0.5)
        s = _nt(qn.astype(BF16), km_ref[:, cols])
        p = jnp.exp(s - jnp.max(s, axis=1, keepdims=True))
        v_aug = jnp.concatenate([vm_ref[:, cols], jnp.ones((vm_ref.shape[0], MEM_D), BF16)], axis=1)
        pv = _nn(p.astype(BF16), v_aug)
        om = pv[:, 0:MEM_D] / pv[:, MEM_D:2 * MEM_D]
        c0 = 2 * GROUP + h * MEM_D
        gate(om, c0, c0 + MEM_D)
    o_ref[...] = x_ref[...] + _nn(y_ref[...], w_ref[...].astype(BF16))


def _out_proj(xf, oa, oh, qm, z, km, vm, gq, w, seq, mem_len, layer):
    n = xf.shape[0]
    tm = OUT_TILE
    steps = seq // tm
    const = lambda i: (0, 0)
    row = lambda i: (i, 0)
    per_batch = lambda i: (i // steps, 0)
    of_layer = lambda i: (layer, 0, 0)
    return pl.pallas_call(
        _out_proj_kernel,
        grid=(n // tm,),
        in_specs=[pl.BlockSpec((tm, D_MODEL), row),
                  pl.BlockSpec((tm, GROUP), row),
                  pl.BlockSpec((tm, GROUP), row),
                  pl.BlockSpec((tm, GROUP), row),
                  pl.BlockSpec((tm, D_MIX), row),
                  pl.BlockSpec((mem_len, GROUP), per_batch),
                  pl.BlockSpec((mem_len, GROUP), per_batch),
                  pl.BlockSpec((None, 1, MEM_D), of_layer),
                  pl.BlockSpec((None, D_MIX, D_MODEL), of_layer, pipeline_mode=pl.Buffered(1))],
        out_specs=pl.BlockSpec((tm, D_MODEL), row),
        out_shape=jax.ShapeDtypeStruct((n, D_MODEL), F32),
        scratch_shapes=[pltpu.VMEM((tm, D_MIX), BF16)],
        compiler_params=pltpu.CompilerParams(dimension_semantics=("parallel",), vmem_limit_bytes=VMEM_LIMIT),
        name="out_proj",
    )(xf, oa, oh, qm, z, km, vm, gq, w)


def kernel(x, mem, positions, norm_g, w_in, w_out, moba_q_norm, moba_k_norm, hgrn_lb_logits, hgrn_o_norm,
           mem_norm_g, w_mem_kv, mem_q_norm, mem_k_norm):
    batch, seq, d_model = x.shape
    mem_len = mem.shape[1]
    depth = w_in.shape[0]
    assert d_model == D_MODEL and seq % ROW_TILE == 0 and seq % OUT_TILE == 0 and mem_len % 8 == 0
    n = batch * seq
    xf = x.reshape(n, d_model)
    memf = mem.reshape(batch * mem_len, d_model)
    cosf, sinf = _rope_tables(positions.reshape(1, n))
    per_head = lambda p, reps: jnp.tile(p, (1, reps))[:, None, :]
    norm_g, mem_norm_g, mem_q_norm = norm_g[:, None, :], mem_norm_g[:, None, :], mem_q_norm[:, None, :]
    gq, gk = per_head(moba_q_norm, GROUP // MOBA_D), per_head(moba_k_norm, GROUP // MOBA_D)
    g_hgrn, g_memk = per_head(hgrn_o_norm, HGRN_HEADS), per_head(mem_k_norm, MEM_HEADS)
    for l in range(depth):
        qa, ka, va, qh, fh, ih, qm, z = _in_proj(xf, norm_g, w_in, cosf, sinf, gq, gk, l)
        oa = _moba(qa, ka, va, batch, seq)
        oh = _hgrn(qh, fh, ih, hgrn_lb_logits, g_hgrn, l, batch, seq)
        km, vm = _mem_kv(memf, mem_norm_g, w_mem_kv, g_memk, mem_len, l)
        xf = _out_proj(xf, oa, oh, qm, z, km, vm, mem_q_norm, w_out, seq, mem_len, l)
    return xf.reshape(batch, seq, d_model)
```

```python
import functools

import numpy as np
import jax
import jax.numpy as jnp
from jax import lax
from jax.experimental import pallas as pl
from jax.experimental.pallas import tpu as pltpu

F32 = jnp.float32
BF16 = jnp.bfloat16

D_MODEL = 1024
GROUP = D_MODEL // 2
D_MIX = 3 * GROUP
IN_COLS = 7 * GROUP + D_MIX
MOBA_D = 64
MOBA_BLOCK = 256
MOBA_TOPK = 3
MOBA_VT_ROWS = MOBA_D + 16
HGRN_D = 128
HGRN_HEADS = GROUP // HGRN_D
HGRN_CHUNK = 64
HGRN_LEVELS = 6
HGRN_MM_LEVELS = 3
LOG2E = 1.4426950408889634
MEM_D = 128
MEM_HEADS = GROUP // MEM_D
ROPE_THETA = 500000.0
ROPE_DIM = MOBA_D // 4
ROPE_HALF = ROPE_DIM // 2
NORM_EPS = 1e-6
LANES = 128
ROW_TILE = 512
OUT_TILE = 1024
HGRN_TILE = 1024
MASKED = -1e30
VMEM_LIMIT = 48 * 1024 * 1024


def _nt(a, b):
    return lax.dot_general(a, b, (((1,), (1,)), ((), ())), preferred_element_type=F32)


def _tn(a, b):
    return lax.dot_general(a, b, (((0,), (0,)), ((), ())), preferred_element_type=F32)


def _nn(a, b):
    return jnp.dot(a, b, preferred_element_type=F32)


def _split2(x):
    hi = x.astype(BF16)
    lo = (x - hi.astype(F32)).astype(BF16)
    return hi, lo


def _split3(x):
    hi = x.astype(BF16)
    r = x - hi.astype(F32)
    mid = r.astype(BF16)
    lo = (r - mid.astype(F32)).astype(BF16)
    return hi, mid, lo


def _sigmoid(x):
    return 1.0 / (1.0 + jnp.exp(-x))


def _silu(x):
    return x * _sigmoid(x)


def _rope_table_kernel(pos_ref, invf_ref, expand_ref, base_ref, cos_ref, sin_ref):
    ang = invf_ref[...] * pos_ref[...].astype(F32)
    tm = ang.shape[1]
    table = jnp.concatenate([jnp.cos(ang), jnp.sin(ang), jnp.zeros((LANES - ROPE_DIM, tm), F32)], axis=0)
    hi, mid, lo = _split3(table.T)
    e = expand_ref[...]
    both = _nn(hi, e) + _nn(mid, e) + _nn(lo, e)
    cos_ref[...] = both[:, 0:LANES] + base_ref[...]
    sin_ref[...] = both[:, LANES:2 * LANES]


def _rope_tables(pos):
    n = pos.shape[1]
    tm = min(n, 2048)
    lane = np.arange(LANES) % MOBA_D
    freq = lane % ROPE_HALF
    rope = lane < ROPE_DIM
    sgn = np.where(lane < ROPE_HALF, -1.0, 1.0)
    expand = np.zeros((LANES, 2 * LANES), np.float32)
    for l in np.nonzero(rope)[0]:
        expand[freq[l], l] = 1.0
        expand[ROPE_HALF + freq[l], LANES + l] = sgn[l]
    base = np.where(rope, 0.0, 1.0).astype(np.float32)[None, :]
    invf = (ROPE_THETA ** (-np.arange(ROPE_HALF, dtype=np.float64) / ROPE_HALF)).astype(np.float32)[:, None]
    const = lambda i: (0, 0)
    return pl.pallas_call(
        _rope_table_kernel,
        grid=(n // tm,),
        in_specs=[pl.BlockSpec((1, tm), lambda i: (0, i)),
                  pl.BlockSpec((ROPE_HALF, 1), const),
                  pl.BlockSpec((LANES, 2 * LANES), const),
                  pl.BlockSpec((1, LANES), const)],
        out_specs=[pl.BlockSpec((tm, LANES), lambda i: (i, 0))] * 2,
        out_shape=[jax.ShapeDtypeStruct((n, LANES), F32)] * 2,
        name="rope_tables",
    )(pos, jnp.asarray(invf), jnp.asarray(expand, BF16), jnp.asarray(base))


def _in_proj_kernel(x_ref, g_ref, w_ref, cos_ref, sin_ref, gq_ref, gk_ref, seg_ref,
                    qa_ref, ka_ref, va_ref, qh_ref, fh_ref, ih_ref, qm_ref, z_ref):
    x = x_ref[...]
    ms = jnp.mean(x * x, axis=-1, keepdims=True)
    hb = (x * lax.rsqrt(ms + NORM_EPS) * g_ref[...]).astype(BF16)

    def proj(c0, width):
        return _nn(hb, w_ref[:, c0:c0 + width].astype(BF16))

    cosf = cos_ref[...]
    sinf = sin_ref[...]
    lane = lax.broadcasted_iota(jnp.int32, (1, LANES), 1) % MOBA_D
    lo_lane = lane < ROPE_HALF
    seg = seg_ref[...]

    def head_norm_rope(p, gain, out_ref):
        for c in range(GROUP // 256):
            pc = p[:, c * 256:(c + 1) * 256]
            ss = _nn((pc * pc).astype(BF16), seg) * (1.0 / MOBA_D)
            pn = pc * lax.rsqrt(ss + NORM_EPS) * gain[:, c * 256:(c + 1) * 256]
            for v in range(2):
                xv = pn[:, v * LANES:(v + 1) * LANES]
                rot = jnp.where(lo_lane, pltpu.roll(xv, LANES - ROPE_HALF, 1), pltpu.roll(xv, ROPE_HALF, 1))
                col = c * 256 + v * LANES
                out_ref[:, col:col + LANES] = (xv * cosf + rot * sinf).astype(out_ref.dtype)

    head_norm_rope(proj(0, GROUP), gq_ref[...], qa_ref)
    head_norm_rope(proj(GROUP, GROUP), gk_ref[...], ka_ref)
    va_ref[...] = proj(2 * GROUP, GROUP).astype(BF16)
    qh_ref[...] = proj(3 * GROUP, GROUP).astype(BF16)
    fh_ref[...] = proj(4 * GROUP, GROUP)
    ih_ref[...] = proj(5 * GROUP, GROUP).astype(BF16)
    qm_ref[...] = proj(6 * GROUP, GROUP).astype(BF16)
    for c in range(3):
        z_ref[:, c * GROUP:(c + 1) * GROUP] = proj((7 + c) * GROUP, GROUP).astype(BF16)


def _in_proj(xf, g, w, cosf, sinf, gq, gk, layer):
    n = xf.shape[0]
    tm = ROW_TILE
    seg = np.arange(256)[:, None] // MOBA_D == np.arange(256)[None, :] // MOBA_D
    seg = jnp.asarray(seg, BF16)
    const = lambda i: (0, 0)
    row = lambda i: (i, 0)
    of_layer = lambda i: (layer, 0, 0)
    outs = ((GROUP, F32), (GROUP, BF16), (GROUP, BF16), (GROUP, BF16), (GROUP, F32), (GROUP, BF16),
            (GROUP, BF16), (D_MIX, BF16))
    return pl.pallas_call(
        _in_proj_kernel,
        grid=(n // tm,),
        in_specs=[pl.BlockSpec((tm, D_MODEL), row),
                  pl.BlockSpec((None, 1, D_MODEL), of_layer),
                  pl.BlockSpec((None, D_MODEL, IN_COLS), of_layer, pipeline_mode=pl.Buffered(1)),
                  pl.BlockSpec((tm, LANES), row),
                  pl.BlockSpec((tm, LANES), row),
                  pl.BlockSpec((None, 1, GROUP), of_layer),
                  pl.BlockSpec((None, 1, GROUP), of_layer),
                  pl.BlockSpec((256, 256), const)],
        out_specs=[pl.BlockSpec((tm, w), row) for w, _ in outs],
        out_shape=[jax.ShapeDtypeStruct((n, w), dt) for w, dt in outs],
        compiler_params=pltpu.CompilerParams(dimension_semantics=("parallel",), vmem_limit_bytes=VMEM_LIMIT),
        name="in_proj",
    )(xf, g, w, cosf, sinf, gq, gk, seg)


def _moba_kernel(q_ref, k_ref, v_ref, o_ref, kmean_ref, kaug_ref, vt_ref, qaug_ref, s_ref, *, nb):
    blk = MOBA_BLOCK
    seq = nb * blk
    lane = lax.broadcasted_iota(jnp.int32, (1, LANES), 1)
    heads = [(lane >= h * MOBA_D) & (lane < (h + 1) * MOBA_D) for h in range(2)]
    spare = [(1 - h) * MOBA_D for h in range(2)]

    ones = jnp.ones((MOBA_VT_ROWS - MOBA_D, blk), F32)
    for j in range(nb):
        rows = slice(j * blk, (j + 1) * blk)
        kj = k_ref[rows, :]
        kmean_ref[j:j + 1, :] = jnp.mean(kj.astype(F32), axis=0, keepdims=True)
        v_t = v_ref[rows, :].astype(F32).T
        for h in range(2):
            onehot = jnp.where(lane == spare[h] + j, 1.0, 0.0).astype(BF16)
            kaug_ref[h, rows, :] = jnp.where(heads[h], kj, onehot)
            vt_ref[h, :, rows] = jnp.concatenate([v_t[h * MOBA_D:(h + 1) * MOBA_D], ones], axis=0).astype(BF16)

    km = kmean_ref[...]
    q_hi, q_lo = _split2(q_ref[...])
    km_parts = [_split2(jnp.where(heads[h], km, 0.0)) for h in range(2)]
    g_hi = _nt(jnp.concatenate([km_parts[0][0], km_parts[0][1], km_parts[1][0], km_parts[1][1]], axis=0), q_hi)
    g_lo = _nt(jnp.concatenate([km_parts[0][0], km_parts[1][0]], axis=0), q_lo)
    blk_row = lax.broadcasted_iota(jnp.int32, (nb, 1), 0)
    q_blk = lax.broadcasted_iota(jnp.int32, (1, seq), 1) // blk
    past = blk_row < q_blk
    for h in range(2):
        gate_t = g_hi[2 * h * nb:(2 * h + 1) * nb] + g_hi[(2 * h + 1) * nb:(2 * h + 2) * nb] + g_lo[h * nb:(h + 1) * nb]
        gate_t = jnp.where(past, gate_t, -jnp.inf)
        cnt = jnp.zeros((nb, seq), F32)
        for n in range(nb - 1):
            gn = gate_t[n:n + 1, :]
            before = (gn > gate_t) | ((gn == gate_t) & (n < blk_row))
            cnt = cnt + jnp.where(before, 1.0, 0.0)
        sel = (past & (cnt < float(MOBA_TOPK))) | (blk_row == q_blk)
        bias_t = jnp.where(sel, 0.0, MASKED)
        for c in range(nb):
            parts = []
            if spare[h]:
                parts.append(jnp.zeros((spare[h], blk), F32))
            parts.append(bias_t[:, c * blk:(c + 1) * blk])
            parts.append(jnp.zeros((LANES - spare[h] - nb, blk), F32))
            bias = jnp.concatenate(parts, axis=0).T
            qs = q_ref[c * blk:(c + 1) * blk, :] * (MOBA_D ** -0.5 * LOG2E)
            qaug_ref[c, h] = jnp.where(heads[h], qs, bias).astype(BF16)

    kpos = lax.broadcasted_iota(jnp.int32, (blk, blk), 0)
    qpos = lax.broadcasted_iota(jnp.int32, (blk, blk), 1)
    causal_t = kpos <= qpos

    token = None
    for t in range(nb // 2):
        buf = t % 2
        base = 0
        for c in (nb - 1 - t, t):
            outs = []
            for h in range(2):
                qa = qaug_ref[c, h]
                if token is not None:
                    qa = (qa.astype(F32) + token).astype(BF16)
                st = s_ref.at[buf, h]
                mx = None
                for j in range(c + 1):
                    s = _nt(kaug_ref[h, j * blk:(j + 1) * blk, :], qa)
                    if j == 0:
                        bits = pltpu.bitcast(s[0:1, 0:LANES], jnp.uint32)
                        zero = lax.shift_right_logical(lax.shift_right_logical(bits, jnp.uint32(16)), jnp.uint32(16))
                        token = zero.astype(F32)
                    if j == c:
                        s = jnp.where(causal_t, s, MASKED)
                    st[base + j * blk:base + (j + 1) * blk, :] = s
                    mx = s if mx is None else jnp.maximum(mx, s)
                m = jnp.max(mx, axis=0, keepdims=True)
                acc = None
                for j in range(c + 1):
                    rows = slice(j * blk, (j + 1) * blk)
                    p = jnp.exp2(st[base + j * blk:base + (j + 1) * blk, :] - m).astype(BF16)
                    pv = _nn(vt_ref[h, :, rows], p)
                    acc = pv if acc is None else acc + pv
                outs.append(acc[0:MOBA_D] / acc[MOBA_D:MOBA_D + 1])
            o_ref[c * blk:(c + 1) * blk, :] = jnp.concatenate(outs, axis=0).T.astype(BF16)
            base += (c + 1) * blk


def _moba(qa, ka, va, batch, seq):
    nb = seq // MOBA_BLOCK
    assert nb % 2 == 0 and nb <= MOBA_D
    pairs = GROUP // LANES
    whole = pl.BlockSpec((seq, LANES), lambda b, p: (b, p))
    return pl.pallas_call(
        functools.partial(_moba_kernel, nb=nb),
        grid=(batch, pairs),
        in_specs=[whole, whole, whole],
        out_specs=whole,
        out_shape=jax.ShapeDtypeStruct(qa.shape, BF16),
        scratch_shapes=[pltpu.VMEM((nb, LANES), F32),
                        pltpu.VMEM((2, seq, LANES), BF16),
                        pltpu.VMEM((2, MOBA_VT_ROWS, seq), BF16),
                        pltpu.VMEM((nb, 2, MOBA_BLOCK, LANES), BF16),
                        pltpu.VMEM((2, 2, (nb + 1) * MOBA_BLOCK, MOBA_BLOCK), F32)],
        compiler_params=pltpu.CompilerParams(dimension_semantics=("parallel", "parallel"),
                                             vmem_limit_bytes=VMEM_LIMIT),
        name="moba",
    )(qa, ka, va)


def _hgrn_consts():
    c = HGRN_CHUNK
    t = np.arange(c)
    w_rows, masks = [], []
    for lvl in range(HGRN_LEVELS):
        m = 1 << lvl
        blk = t // m
        odd = (blk % 2) == 1
        start = blk * m
        end = start + m - 1
        u = t[None, :]
        w_odd = (u >= start[:, None]) & (u <= t[:, None])
        w_even = (u > t[:, None]) & (u <= end[:, None])
        w_rows.append(np.where(odd[:, None], w_odd, w_even))
        masks.append(odd[:, None] & (blk[None, :] == blk[:, None] - 1))
    masks.append(t[None, :] == t[:, None])
    w_mm = w_rows[:HGRN_MM_LEVELS] + [t[None, :] <= t[:, None]]
    w_all = np.concatenate(w_mm, axis=0).astype(np.float32)
    w_all = np.concatenate([w_all, w_all], axis=1)
    mask_cat = np.concatenate([masks[-1]] + masks[:-1] + [np.zeros((c, c), bool)], axis=1).astype(np.float32)
    return jnp.asarray(w_all, BF16), jnp.asarray(mask_cat, BF16)


def _hgrn_kernel(qh_ref, fh_ref, ih_ref, logit_ref, gain_ref, w_ref, mask_ref, o_ref, state_ref, *, layer):
    c = HGRN_CHUNK
    state_ref[...] = jnp.zeros_like(state_ref)

    if layer > 0:
        lg = logit_ref[...]
        e = jnp.exp(lg - jnp.max(lg, axis=0, keepdims=True))
        sm = e / jnp.sum(e, axis=0, keepdims=True)
        lb = jnp.sum(sm[1:layer + 1, :], axis=0, keepdims=True)
        log_lb = jnp.log(lb) * LOG2E
        log_1m_lb = jnp.log1p(-lb) * LOG2E
    w_all = w_ref[...]
    gain = gain_ref[...]

    def prepare(base, ci):
        rows = pl.ds(base + ci * c, c)
        fl = fh_ref[rows, :]
        vi = ih_ref[rows, :]
        qh = qh_ref[rows, :].astype(F32)
        qf = qh / (1.0 + jnp.exp2(qh * -LOG2E))
        e = jnp.exp2(jnp.abs(fl) * -LOG2E)
        r = 1.0 / (1.0 + e)
        log_sig = jnp.minimum(fl, 0.0) * LOG2E - jnp.log2(1.0 + e)
        sig_neg = jnp.where(fl >= 0.0, e * r, r)
        if layer > 0:
            b = log_1m_lb + log_sig
            lf = jnp.maximum(log_lb, b) + jnp.log2(1.0 + jnp.exp2(-jnp.abs(log_lb - b)))
            kf = (1.0 - lb) * sig_neg
        else:
            lf = log_sig
            kf = sig_neg

        f1, f2 = _split2(lf)
        x_mm = _nn(w_all, jnp.concatenate([f1, f2], axis=0))
        a2 = x_mm[HGRN_MM_LEVELS * c:(HGRN_MM_LEVELS + 1) * c, :]
        e_lvl = [jnp.exp2(x_mm[lvl * c:(lvl + 1) * c, :]) for lvl in range(HGRN_MM_LEVELS)]
        for lvl in range(HGRN_MM_LEVELS, HGRN_LEVELS):
            m = 1 << lvl
            parts = []
            for g in range(c // (2 * m)):
                ref = a2[2 * m * g + m - 1:2 * m * g + m, :]
                parts.append(-jnp.abs(a2[2 * m * g:2 * m * (g + 1), :] - ref))
            e_lvl.append(jnp.exp2(jnp.concatenate(parts, axis=0)))
        e_cum = jnp.exp2(a2)
        e_rest = jnp.exp2(a2[c - 1:c, :] - a2)
        e_end = e_cum[c - 1:c, :]
        qf_b, kf_b = qf.astype(BF16), kf.astype(BF16)
        e_b = [e.astype(BF16) for e in e_lvl]
        qb = [qf_b] + [qf_b * e for e in e_b]
        kb = [kf_b] + [kf_b * e for e in e_b]
        return qb, kb, qf_b * e_cum.astype(BF16), kf_b * e_rest.astype(BF16), vi, e_end

    def recur(base, ci, prepared):
        rows = pl.ds(base + ci * c, c)
        qb_all, kb_all, ql_all, kdec_all, vi, e_end = prepared
        zero = jnp.zeros((c, HGRN_D), BF16)
        zsq = jnp.zeros((HGRN_D, HGRN_D), BF16)
        for hp in range(HGRN_HEADS // 2):
            p_pair, ql_pair, v_pair, st_pair = [], [], [], []
            for h in (2 * hp, 2 * hp + 1):
                cols = slice(h * HGRN_D, (h + 1) * HGRN_D)
                qb = [q[:, cols] for q in qb_all]
                kb = [k[:, cols] for k in kb_all]
                s_parts = []
                for a in range(0, HGRN_LEVELS, 2):
                    q_cat = jnp.concatenate([qb[a], qb[a + 1]], axis=1)
                    k_blk = jnp.concatenate([jnp.concatenate([kb[a], zero], axis=1),
                                             jnp.concatenate([zero, kb[a + 1]], axis=1)], axis=0)
                    s_parts.append(_nt(q_cat, k_blk))
                s_parts.append(_nt(qb[HGRN_LEVELS], jnp.concatenate([kb[HGRN_LEVELS], zero], axis=0)))
                p = None
                for a, part in enumerate(s_parts):
                    part = part.astype(BF16) * mask_ref[:, 2 * a * c:(2 * a + 2) * c]
                    p = part if p is None else p + part
                p_pair.append(p)
                ql_pair.append(ql_all[:, cols])
                v_pair.append(vi[:, cols])
                st_pair.append(state_ref[:, cols])
            vv = [jnp.concatenate([v, v], axis=0) for v in v_pair]
            zv = jnp.zeros_like(vv[0])
            v_blk = jnp.concatenate([jnp.concatenate([vv[0], zv], axis=1),
                                     jnp.concatenate([zv, vv[1]], axis=1)], axis=0)
            o_pair = _nn(jnp.concatenate(p_pair, axis=1), v_blk)
            st_b = [st.astype(BF16) for st in st_pair]
            st_blk = jnp.concatenate([jnp.concatenate([st_b[0], zsq], axis=1),
                                      jnp.concatenate([zsq, st_b[1]], axis=1)], axis=0)
            o_pair = o_pair + _nt(jnp.concatenate(ql_pair, axis=1), st_blk)
            for i, h in enumerate((2 * hp, 2 * hp + 1)):
                cols = slice(h * HGRN_D, (h + 1) * HGRN_D)
                o = o_pair[:, i * HGRN_D:(i + 1) * HGRN_D]
                state_ref[:, cols] = st_pair[i] * e_end[:, cols] + _tn(v_pair[i], kdec_all[:, cols])
                ms = jnp.mean(o * o, axis=-1, keepdims=True)
                o_ref[rows, cols] = (o * lax.rsqrt(ms + NORM_EPS) * gain[:, cols]).astype(BF16)

    n_chunks = HGRN_TILE // c

    def group(g, carry):
        base = pl.multiple_of(g * HGRN_TILE, HGRN_TILE)
        prepared = prepare(base, 0)
        for ci in range(n_chunks):
            nxt = prepare(base, ci + 1) if ci + 1 < n_chunks else None
            recur(base, ci, prepared)
            prepared = nxt
        return carry

    lax.fori_loop(0, qh_ref.shape[0] // HGRN_TILE, group, 0)


def _hgrn(qh, fh, ih, logits, gain, layer, batch, seq):
    w_all, mask_all = _hgrn_consts()
    assert seq % HGRN_TILE == 0
    depth = logits.shape[0]
    const = lambda b: (0, 0)
    return pl.pallas_call(
        functools.partial(_hgrn_kernel, layer=layer),
        grid=(batch,),
        in_specs=[pl.BlockSpec((seq, GROUP), lambda b: (b, 0))] * 3 + [
                  pl.BlockSpec((depth, GROUP), const),
                  pl.BlockSpec((None, 1, GROUP), lambda b: (layer, 0, 0)),
                  pl.BlockSpec(w_all.shape, const),
                  pl.BlockSpec(mask_all.shape, const)],
        out_specs=pl.BlockSpec((seq, GROUP), lambda b: (b, 0)),
        out_shape=jax.ShapeDtypeStruct((fh.shape[0], GROUP), BF16),
        scratch_shapes=[pltpu.VMEM((HGRN_D, GROUP), F32)],
        compiler_params=pltpu.CompilerParams(dimension_semantics=("parallel",),
                                             vmem_limit_bytes=VMEM_LIMIT),
        name="hgrn",
    )(qh, fh, ih, logits, gain, w_all, mask_all)


def _mem_kv_kernel(m_ref, g_ref, w_ref, gk_ref, seg_ref, k_ref, v_ref):
    x = m_ref[...]
    ms = jnp.mean(x * x, axis=-1, keepdims=True)
    hb = (x * lax.rsqrt(ms + NORM_EPS) * g_ref[...]).astype(BF16)
    gk = gk_ref[...]
    seg = seg_ref[...]
    for c in range(GROUP // 256):
        kc = _nn(hb, w_ref[:, c * 256:(c + 1) * 256].astype(BF16))
        ss = _nn((kc * kc).astype(BF16), seg) * (1.0 / MEM_D)
        kn = kc * lax.rsqrt(ss + NORM_EPS) * gk[:, c * 256:(c + 1) * 256]
        k_ref[:, c * 256:(c + 1) * 256] = kn.astype(BF16)
    v_ref[...] = _nn(hb, w_ref[:, GROUP:2 * GROUP].astype(BF16)).astype(BF16)


def _mem_kv(memf, g, w, gk, mem_len, layer):
    n = memf.shape[0]
    seg = np.arange(256)[:, None] // MEM_D == np.arange(256)[None, :] // MEM_D
    seg = jnp.asarray(seg, BF16)
    const = lambda i: (0, 0)
    row = lambda i: (i, 0)
    of_layer = lambda i: (layer, 0, 0)
    return pl.pallas_call(
        _mem_kv_kernel,
        grid=(n // mem_len,),
        in_specs=[pl.BlockSpec((mem_len, D_MODEL), row),
                  pl.BlockSpec((None, 1, D_MODEL), of_layer),
                  pl.BlockSpec((None, D_MODEL, 2 * GROUP), of_layer),
                  pl.BlockSpec((None, 1, GROUP), of_layer),
                  pl.BlockSpec((256, 256), const)],
        out_specs=[pl.BlockSpec((mem_len, GROUP), row)] * 2,
        out_shape=[jax.ShapeDtypeStruct((n, GROUP), BF16)] * 2,
        compiler_params=pltpu.CompilerParams(dimension_semantics=("parallel",), vmem_limit_bytes=VMEM_LIMIT),
        name="mem_kv",
    )(memf, g, w, gk, seg)


def _out_proj_kernel(x_ref, oa_ref, oh_ref, qm_ref, z_ref, km_ref, vm_ref, gq_ref, w_ref, o_ref, y_ref):
    def gate(o, c0, c1):
        y_ref[:, c0:c1] = (o * _silu(z_ref[:, c0:c1].astype(F32))).astype(BF16)

    gate(oa_ref[...].astype(F32), 0, GROUP)
    gate(oh_ref[...].astype(F32), GROUP, 2 * GROUP)
    gq = gq_ref[...]
    for h in range(MEM_HEADS):
        cols = slice(h * MEM_D, (h + 1) * MEM_D)
        q = qm_ref[:, cols].astype(F32)
        ms = jnp.mean(q * q, axis=-1, keepdims=True)
        qn = q * lax.rsqrt(ms + NORM_EPS) * gq * (MEM_D ** -0.5)
        s = _nt(qn.astype(BF16), km_ref[:, cols])
        p = jnp.exp(s - jnp.max(s, axis=1, keepdims=True))
        v_aug = jnp.concatenate([vm_ref[:, cols], jnp.ones((vm_ref.shape[0], MEM_D), BF16)], axis=1)
        pv = _nn(p.astype(BF16), v_aug)
        om = pv[:, 0:MEM_D] / pv[:, MEM_D:2 * MEM_D]
        c0 = 2 * GROUP + h * MEM_D
        gate(om, c0, c0 + MEM_D)
    o_ref[...] = x_ref[...] + _nn(y_ref[...], w_ref[...].astype(BF16))


def _out_proj(xf, oa, oh, qm, z, km, vm, gq, w, seq, mem_len, layer):
    n = xf.shape[0]
    tm = OUT_TILE
    steps = seq // tm
    const = lambda i: (0, 0)
    row = lambda i: (i, 0)
    per_batch = lambda i: (i // steps, 0)
    of_layer = lambda i: (layer, 0, 0)
    return pl.pallas_call(
        _out_proj_kernel,
        grid=(n // tm,),
        in_specs=[pl.BlockSpec((tm, D_MODEL), row),
                  pl.BlockSpec((tm, GROUP), row),
                  pl.BlockSpec((tm, GROUP), row),
                  pl.BlockSpec((tm, GROUP), row),
                  pl.BlockSpec((tm, D_MIX), row),
                  pl.BlockSpec((mem_len, GROUP), per_batch),
                  pl.BlockSpec((mem_len, GROUP), per_batch),
                  pl.BlockSpec((None, 1, MEM_D), of_layer),
                  pl.BlockSpec((None, D_MIX, D_MODEL), of_layer, pipeline_mode=pl.Buffered(1))],
        out_specs=pl.BlockSpec((tm, D_MODEL), row),
        out_shape=jax.ShapeDtypeStruct((n, D_MODEL), F32),
        scratch_shapes=[pltpu.VMEM((tm, D_MIX), BF16)],
        compiler_params=pltpu.CompilerParams(dimension_semantics=("parallel",), vmem_limit_bytes=VMEM_LIMIT),
        name="out_proj",
    )(xf, oa, oh, qm, z, km, vm, gq, w)


def kernel(x, mem, positions, norm_g, w_in, w_out, moba_q_norm, moba_k_norm, hgrn_lb_logits, hgrn_o_norm,
           mem_norm_g, w_mem_kv, mem_q_norm, mem_k_norm):
    batch, seq, d_model = x.shape
    mem_len = mem.shape[1]
    depth = w_in.shape[0]
    assert d_model == D_MODEL and seq % ROW_TILE == 0 and seq % OUT_TILE == 0 and mem_len % 8 == 0
    n = batch * seq
    xf = x.reshape(n, d_model)
    memf = mem.reshape(batch * mem_len, d_model)
    cosf, sinf = _rope_tables(positions.reshape(1, n))
    per_head = lambda p, reps: jnp.tile(p, (1, reps))[:, None, :]
    norm_g, mem_norm_g, mem_q_norm = norm_g[:, None, :], mem_norm_g[:, None, :], mem_q_norm[:, None, :]
    gq, gk = per_head(moba_q_norm, GROUP // MOBA_D), per_head(moba_k_norm, GROUP // MOBA_D)
    g_hgrn, g_memk = per_head(hgrn_o_norm, HGRN_HEADS), per_head(mem_k_norm, MEM_HEADS)
    for l in range(depth):
        qa, ka, va, qh, fh, ih, qm, z = _in_proj(xf, norm_g, w_in, cosf, sinf, gq, gk, l)
        oa = _moba(qa, ka, va, batch, seq)
        oh = _hgrn(qh, fh, ih, hgrn_lb_logits, g_hgrn, l, batch, seq)
        km, vm = _mem_kv(memf, mem_norm_g, w_mem_kv, g_memk, mem_len, l)
        xf = _out_proj(xf, oa, oh, qm, z, km, vm, mem_q_norm, w_out, seq, mem_len, l)
    return xf.reshape(batch, seq, d_model)
```

```python
import functools

import numpy as np
import jax
import jax.numpy as jnp
from jax import lax
from jax.experimental import pallas as pl
from jax.experimental.pallas import tpu as pltpu

F32 = jnp.float32
BF16 = jnp.bfloat16

D_MODEL = 1024
GROUP = D_MODEL // 2
D_MIX = 3 * GROUP
IN_COLS = 7 * GROUP + D_MIX
MOBA_D = 64
MOBA_BLOCK = 256
MOBA_TOPK = 3
MOBA_VT_ROWS = MOBA_D + 16
HGRN_D = 128
HGRN_HEADS = GROUP // HGRN_D
HGRN_CHUNK = 64
HGRN_LEVELS = 6
HGRN_MM_LEVELS = 3
LOG2E = 1.4426950408889634
MEM_D = 128
MEM_HEADS = GROUP // MEM_D
ROPE_THETA = 500000.0
ROPE_DIM = MOBA_D // 4
ROPE_HALF = ROPE_DIM // 2
NORM_EPS = 1e-6
LANES = 128
ROW_TILE = 512
OUT_TILE = 1024
HGRN_TILE = 1024
MASKED = -1e30
VMEM_LIMIT = 48 * 1024 * 1024


def _nt(a, b):
    return lax.dot_general(a, b, (((1,), (1,)), ((), ())), preferred_element_type=F32)


def _tn(a, b):
    return lax.dot_general(a, b, (((0,), (0,)), ((), ())), preferred_element_type=F32)


def _nn(a, b):
    return jnp.dot(a, b, preferred_element_type=F32)


def _split2(x):
    hi = x.astype(BF16)
    lo = (x - hi.astype(F32)).astype(BF16)
    return hi, lo


def _split3(x):
    hi = x.astype(BF16)
    r = x - hi.astype(F32)
    mid = r.astype(BF16)
    lo = (r - mid.astype(F32)).astype(BF16)
    return hi, mid, lo


def _sigmoid(x):
    return 1.0 / (1.0 + jnp.exp(-x))


def _silu(x):
    return x * _sigmoid(x)


def _rope_table_kernel(pos_ref, invf_ref, expand_ref, base_ref, cos_ref, sin_ref):
    ang = invf_ref[...] * pos_ref[...].astype(F32)
    tm = ang.shape[1]
    table = jnp.concatenate([jnp.cos(ang), jnp.sin(ang), jnp.zeros((LANES - ROPE_DIM, tm), F32)], axis=0)
    hi, mid, lo = _split3(table.T)
    e = expand_ref[...]
    both = _nn(hi, e) + _nn(mid, e) + _nn(lo, e)
    cos_ref[...] = both[:, 0:LANES] + base_ref[...]
    sin_ref[...] = both[:, LANES:2 * LANES]


def _rope_tables(pos):
    n = pos.shape[1]
    tm = min(n, 2048)
    lane = np.arange(LANES) % MOBA_D
    freq = lane % ROPE_HALF
    rope = lane < ROPE_DIM
    sgn = np.where(lane < ROPE_HALF, -1.0, 1.0)
    expand = np.zeros((LANES, 2 * LANES), np.float32)
    for l in np.nonzero(rope)[0]:
        expand[freq[l], l] = 1.0
        expand[ROPE_HALF + freq[l], LANES + l] = sgn[l]
    base = np.where(rope, 0.0, 1.0).astype(np.float32)[None, :]
    invf = (ROPE_THETA ** (-np.arange(ROPE_HALF, dtype=np.float64) / ROPE_HALF)).astype(np.float32)[:, None]
    const = lambda i: (0, 0)
    return pl.pallas_call(
        _rope_table_kernel,
        grid=(n // tm,),
        in_specs=[pl.BlockSpec((1, tm), lambda i: (0, i)),
                  pl.BlockSpec((ROPE_HALF, 1), const),
                  pl.BlockSpec((LANES, 2 * LANES), const),
                  pl.BlockSpec((1, LANES), const)],
        out_specs=[pl.BlockSpec((tm, LANES), lambda i: (i, 0))] * 2,
        out_shape=[jax.ShapeDtypeStruct((n, LANES), F32)] * 2,
        name="rope_tables",
    )(pos, jnp.asarray(invf), jnp.asarray(expand, BF16), jnp.asarray(base))


def _in_proj_kernel(x_ref, g_ref, w_ref, cos_ref, sin_ref, gq_ref, gk_ref, seg_ref,
                    qa_ref, ka_ref, va_ref, qh_ref, fh_ref, ih_ref, qm_ref, z_ref):
    x = x_ref[...]
    ms = jnp.mean(x * x, axis=-1, keepdims=True)
    hb = (x * lax.rsqrt(ms + NORM_EPS) * g_ref[...]).astype(BF16)

    def proj(c0, width):
        return _nn(hb, w_ref[:, c0:c0 + width].astype(BF16))

    cosf = cos_ref[...]
    sinf = sin_ref[...]
    lane = lax.broadcasted_iota(jnp.int32, (1, LANES), 1) % MOBA_D
    lo_lane = lane < ROPE_HALF
    seg = seg_ref[...]

    def head_norm_rope(p, gain, out_ref):
        for c in range(GROUP // 256):
            pc = p[:, c * 256:(c + 1) * 256]
            ss = _nn((pc * pc).astype(BF16), seg) * (1.0 / MOBA_D)
            pn = pc * lax.rsqrt(ss + NORM_EPS) * gain[:, c * 256:(c + 1) * 256]
            for v in range(2):
                xv = pn[:, v * LANES:(v + 1) * LANES]
                rot = jnp.where(lo_lane, pltpu.roll(xv, LANES - ROPE_HALF, 1), pltpu.roll(xv, ROPE_HALF, 1))
                col = c * 256 + v * LANES
                out_ref[:, col:col + LANES] = (xv * cosf + rot * sinf).astype(out_ref.dtype)

    head_norm_rope(proj(0, GROUP), gq_ref[...], qa_ref)
    head_norm_rope(proj(GROUP, GROUP), gk_ref[...], ka_ref)
    va_ref[...] = proj(2 * GROUP, GROUP).astype(BF16)
    qh_ref[...] = proj(3 * GROUP, GROUP).astype(BF16)
    fh_ref[...] = proj(4 * GROUP, GROUP)
    ih_ref[...] = proj(5 * GROUP, GROUP).astype(BF16)
    qm_ref[...] = proj(6 * GROUP, GROUP).astype(BF16)
    for c in range(3):
        z_ref[:, c * GROUP:(c + 1) * GROUP] = proj((7 + c) * GROUP, GROUP).astype(BF16)


def _in_proj(xf, g, w, cosf, sinf, gq, gk, layer):
    n = xf.shape[0]
    tm = ROW_TILE
    seg = np.arange(256)[:, None] // MOBA_D == np.arange(256)[None, :] // MOBA_D
    seg = jnp.asarray(seg, BF16)
    const = lambda i: (0, 0)
    row = lambda i: (i, 0)
    of_layer = lambda i: (layer, 0, 0)
    outs = ((GROUP, F32), (GROUP, BF16), (GROUP, BF16), (GROUP, BF16), (GROUP, F32), (GROUP, BF16),
            (GROUP, BF16), (D_MIX, BF16))
    return pl.pallas_call(
        _in_proj_kernel,
        grid=(n // tm,),
        in_specs=[pl.BlockSpec((tm, D_MODEL), row),
                  pl.BlockSpec((None, 1, D_MODEL), of_layer),
                  pl.BlockSpec((None, D_MODEL, IN_COLS), of_layer, pipeline_mode=pl.Buffered(1)),
                  pl.BlockSpec((tm, LANES), row),
                  pl.BlockSpec((tm, LANES), row),
                  pl.BlockSpec((None, 1, GROUP), of_layer),
                  pl.BlockSpec((None, 1, GROUP), of_layer),
                  pl.BlockSpec((256, 256), const)],
        out_specs=[pl.BlockSpec((tm, w), row) for w, _ in outs],
        out_shape=[jax.ShapeDtypeStruct((n, w), dt) for w, dt in outs],
        compiler_params=pltpu.CompilerParams(dimension_semantics=("parallel",), vmem_limit_bytes=VMEM_LIMIT),
        name="in_proj",
    )(xf, g, w, cosf, sinf, gq, gk, seg)


def _moba_kernel(q_ref, k_ref, v_ref, o_ref, kmean_ref, kaug_ref, vt_ref, qaug_ref, s_ref, *, nb):
    blk = MOBA_BLOCK
    seq = nb * blk
    lane = lax.broadcasted_iota(jnp.int32, (1, LANES), 1)
    heads = [(lane >= h * MOBA_D) & (lane < (h + 1) * MOBA_D) for h in range(2)]
    spare = [(1 - h) * MOBA_D for h in range(2)]

    ones = jnp.ones((MOBA_VT_ROWS - MOBA_D, blk), F32)
    for j in range(nb):
        rows = slice(j * blk, (j + 1) * blk)
        kj = k_ref[rows, :]
        kmean_ref[j:j + 1, :] = jnp.mean(kj.astype(F32), axis=0, keepdims=True)
        v_t = v_ref[rows, :].astype(F32).T
        vth = [jnp.concatenate([v_t[h * MOBA_D:(h + 1) * MOBA_D], ones], axis=0).astype(BF16) for h in range(2)]
        zvt = jnp.zeros_like(vth[0])
        vt_ref[j] = jnp.concatenate([jnp.concatenate([vth[0], zvt], axis=1),
                                     jnp.concatenate([zvt, vth[1]], axis=1)], axis=0)
        for h in range(2):
            onehot = jnp.where(lane == spare[h] + j, 1.0, 0.0).astype(BF16)
            kaug_ref[h, rows, :] = jnp.where(heads[h], kj, onehot)

    km = kmean_ref[...]
    q_hi, q_lo = _split2(q_ref[...])
    km_parts = [_split2(jnp.where(heads[h], km, 0.0)) for h in range(2)]
    g_hi = _nt(jnp.concatenate([km_parts[0][0], km_parts[0][1], km_parts[1][0], km_parts[1][1]], axis=0), q_hi)
    g_lo = _nt(jnp.concatenate([km_parts[0][0], km_parts[1][0]], axis=0), q_lo)
    blk_row = lax.broadcasted_iota(jnp.int32, (nb, 1), 0)
    q_blk = lax.broadcasted_iota(jnp.int32, (1, seq), 1) // blk
    past = blk_row < q_blk
    for h in range(2):
        gate_t = g_hi[2 * h * nb:(2 * h + 1) * nb] + g_hi[(2 * h + 1) * nb:(2 * h + 2) * nb] + g_lo[h * nb:(h + 1) * nb]
        gate_t = jnp.where(past, gate_t, -jnp.inf)
        cnt = jnp.zeros((nb, seq), F32)
        for n in range(nb - 1):
            gn = gate_t[n:n + 1, :]
            before = (gn > gate_t) | ((gn == gate_t) & (n < blk_row))
            cnt = cnt + jnp.where(before, 1.0, 0.0)
        sel = (past & (cnt < float(MOBA_TOPK))) | (blk_row == q_blk)
        bias_t = jnp.where(sel, 0.0, MASKED)
        for c in range(nb):
            parts = []
            if spare[h]:
                parts.append(jnp.zeros((spare[h], blk), F32))
            parts.append(bias_t[:, c * blk:(c + 1) * blk])
            parts.append(jnp.zeros((LANES - spare[h] - nb, blk), F32))
            bias = jnp.concatenate(parts, axis=0).T
            qs = q_ref[c * blk:(c + 1) * blk, :] * (MOBA_D ** -0.5 * LOG2E)
            qaug_ref[c, h] = jnp.where(heads[h], qs, bias).astype(BF16)

    kpos = lax.broadcasted_iota(jnp.int32, (blk, blk), 0)
    qpos = lax.broadcasted_iota(jnp.int32, (blk, blk), 1)
    causal_t = kpos <= qpos

    token = None
    for t in range(nb // 2):
        buf = t % 2
        base = 0
        for c in (nb - 1 - t, t):
            ms = []
            for h in range(2):
                qa = qaug_ref[c, h]
                if token is not None:
                    qa = (qa.astype(F32) + token).astype(BF16)
                st = s_ref.at[buf, h]
                mx = None
                for j in range(c + 1):
                    s = _nt(kaug_ref[h, j * blk:(j + 1) * blk, :], qa)
                    if j == 0:
                        bits = pltpu.bitcast(s[0:1, 0:LANES], jnp.uint32)
                        zero = lax.shift_right_logical(lax.shift_right_logical(bits, jnp.uint32(16)), jnp.uint32(16))
                        token = zero.astype(F32)
                    if j == c:
                        s = jnp.where(causal_t, s, MASKED)
                    st[base + j * blk:base + (j + 1) * blk, :] = s
                    mx = s if mx is None else jnp.maximum(mx, s)
                ms.append(jnp.max(mx, axis=0, keepdims=True))
            acc = None
            for j in range(c + 1):
                tile = slice(base + j * blk, base + (j + 1) * blk)
                p = jnp.concatenate([jnp.exp2(s_ref[buf, h, tile, :] - ms[h]).astype(BF16) for h in range(2)], axis=0)
                pv = _nn(vt_ref[j], p)
                acc = pv if acc is None else acc + pv
            outs = [acc[h * MOBA_VT_ROWS:h * MOBA_VT_ROWS + MOBA_D]
                    / acc[h * MOBA_VT_ROWS + MOBA_D:h * MOBA_VT_ROWS + MOBA_D + 1] for h in range(2)]
            o_ref[c * blk:(c + 1) * blk, :] = jnp.concatenate(outs, axis=0).T.astype(BF16)
            base += (c + 1) * blk


def _moba(qa, ka, va, batch, seq):
    nb = seq // MOBA_BLOCK
    assert nb % 2 == 0 and nb <= MOBA_D
    pairs = GROUP // LANES
    whole = pl.BlockSpec((seq, LANES), lambda b, p: (b, p))
    return pl.pallas_call(
        functools.partial(_moba_kernel, nb=nb),
        grid=(batch, pairs),
        in_specs=[whole, whole, whole],
        out_specs=whole,
        out_shape=jax.ShapeDtypeStruct(qa.shape, BF16),
        scratch_shapes=[pltpu.VMEM((nb, LANES), F32),
                        pltpu.VMEM((2, seq, LANES), BF16),
                        pltpu.VMEM((nb, 2 * MOBA_VT_ROWS, 2 * MOBA_BLOCK), BF16),
                        pltpu.VMEM((nb, 2, MOBA_BLOCK, LANES), BF16),
                        pltpu.VMEM((2, 2, (nb + 1) * MOBA_BLOCK, MOBA_BLOCK), F32)],
        compiler_params=pltpu.CompilerParams(dimension_semantics=("parallel", "parallel"),
                                             vmem_limit_bytes=VMEM_LIMIT),
        name="moba",
    )(qa, ka, va)


def _hgrn_consts():
    c = HGRN_CHUNK
    t = np.arange(c)
    w_rows, masks = [], []
    for lvl in range(HGRN_LEVELS):
        m = 1 << lvl
        blk = t // m
        odd = (blk % 2) == 1
        start = blk * m
        end = start + m - 1
        u = t[None, :]
        w_odd = (u >= start[:, None]) & (u <= t[:, None])
        w_even = (u > t[:, None]) & (u <= end[:, None])
        w_rows.append(np.where(odd[:, None], w_odd, w_even))
        masks.append(odd[:, None] & (blk[None, :] == blk[:, None] - 1))
    masks.append(t[None, :] == t[:, None])
    w_mm = w_rows[:HGRN_MM_LEVELS] + [t[None, :] <= t[:, None]]
    w_all = np.concatenate(w_mm, axis=0).astype(np.float32)
    w_all = np.concatenate([w_all, w_all], axis=1)
    mask_cat = np.concatenate([masks[-1]] + masks[:-1] + [np.zeros((c, c), bool)], axis=1).astype(np.float32)
    return jnp.asarray(w_all, BF16), jnp.asarray(mask_cat, BF16)


def _hgrn_kernel(qh_ref, fh_ref, ih_ref, logit_ref, gain_ref, w_ref, mask_ref, o_ref, state_ref, *, layer):
    c = HGRN_CHUNK
    state_ref[...] = jnp.zeros_like(state_ref)

    if layer > 0:
        lg = logit_ref[...]
        e = jnp.exp(lg - jnp.max(lg, axis=0, keepdims=True))
        sm = e / jnp.sum(e, axis=0, keepdims=True)
        lb = jnp.sum(sm[1:layer + 1, :], axis=0, keepdims=True)
        log_lb = jnp.log(lb) * LOG2E
        log_1m_lb = jnp.log1p(-lb) * LOG2E
    w_all = w_ref[...]
    gain = gain_ref[...]

    def prepare(base, ci):
        rows = pl.ds(base + ci * c, c)
        fl = fh_ref[rows, :]
        vi = ih_ref[rows, :]
        qh = qh_ref[rows, :].astype(F32)
        qf = qh / (1.0 + jnp.exp2(qh * -LOG2E))
        e = jnp.exp2(jnp.abs(fl) * -LOG2E)
        r = 1.0 / (1.0 + e)
        log_sig = jnp.minimum(fl, 0.0) * LOG2E - jnp.log2(1.0 + e)
        sig_neg = jnp.where(fl >= 0.0, e * r, r)
        if layer > 0:
            b = log_1m_lb + log_sig
            lf = jnp.maximum(log_lb, b) + jnp.log2(1.0 + jnp.exp2(-jnp.abs(log_lb - b)))
            kf = (1.0 - lb) * sig_neg
        else:
            lf = log_sig
            kf = sig_neg

        f1, f2 = _split2(lf)
        x_mm = _nn(w_all, jnp.concatenate([f1, f2], axis=0))
        a2 = x_mm[HGRN_MM_LEVELS * c:(HGRN_MM_LEVELS + 1) * c, :]
        e_lvl = [jnp.exp2(x_mm[lvl * c:(lvl + 1) * c, :]) for lvl in range(HGRN_MM_LEVELS)]
        for lvl in range(HGRN_MM_LEVELS, HGRN_LEVELS):
            m = 1 << lvl
            parts = []
            for g in range(c // (2 * m)):
                ref = a2[2 * m * g + m - 1:2 * m * g + m, :]
                parts.append(-jnp.abs(a2[2 * m * g:2 * m * (g + 1), :] - ref))
            e_lvl.append(jnp.exp2(jnp.concatenate(parts, axis=0)))
        e_cum = jnp.exp2(a2)
        e_rest = jnp.exp2(a2[c - 1:c, :] - a2)
        e_end = e_cum[c - 1:c, :]
        qf_b, kf_b = qf.astype(BF16), kf.astype(BF16)
        e_b = [e.astype(BF16) for e in e_lvl]
        qb = [qf_b] + [qf_b * e for e in e_b]
        kb = [kf_b] + [kf_b * e for e in e_b]
        return qb, kb, qf_b * e_cum.astype(BF16), kf_b * e_rest.astype(BF16), vi, e_end

    def recur(base, ci, prepared):
        rows = pl.ds(base + ci * c, c)
        qb_all, kb_all, ql_all, kdec_all, vi, e_end = prepared
        zero = jnp.zeros((c, HGRN_D), BF16)
        zsq = jnp.zeros((HGRN_D, HGRN_D), BF16)
        for hp in range(HGRN_HEADS // 2):
            p_pair, ql_pair, v_pair, st_pair = [], [], [], []
            for h in (2 * hp, 2 * hp + 1):
                cols = slice(h * HGRN_D, (h + 1) * HGRN_D)
                qb = [q[:, cols] for q in qb_all]
                kb = [k[:, cols] for k in kb_all]
                s_parts = []
                for a in range(0, HGRN_LEVELS, 2):
                    q_cat = jnp.concatenate([qb[a], qb[a + 1]], axis=1)
                    k_blk = jnp.concatenate([jnp.concatenate([kb[a], zero], axis=1),
                                             jnp.concatenate([zero, kb[a + 1]], axis=1)], axis=0)
                    s_parts.append(_nt(q_cat, k_blk))
                s_parts.append(_nt(qb[HGRN_LEVELS], jnp.concatenate([kb[HGRN_LEVELS], zero], axis=0)))
                p = None
                for a, part in enumerate(s_parts):
                    part = part.astype(BF16) * mask_ref[:, 2 * a * c:(2 * a + 2) * c]
                    p = part if p is None else p + part
                p_pair.append(p)
                ql_pair.append(ql_all[:, cols])
                v_pair.append(vi[:, cols])
                st_pair.append(state_ref[:, cols])
            vv = [jnp.concatenate([v, v], axis=0) for v in v_pair]
            zv = jnp.zeros_like(vv[0])
            v_blk = jnp.concatenate([jnp.concatenate([vv[0], zv], axis=1),
                                     jnp.concatenate([zv, vv[1]], axis=1)], axis=0)
            o_pair = _nn(jnp.concatenate(p_pair, axis=1), v_blk)
            st_b = [st.astype(BF16) for st in st_pair]
            st_blk = jnp.concatenate([jnp.concatenate([st_b[0], zsq], axis=1),
                                      jnp.concatenate([zsq, st_b[1]], axis=1)], axis=0)
            o_pair = o_pair + _nt(jnp.concatenate(ql_pair, axis=1), st_blk)
            for i, h in enumerate((2 * hp, 2 * hp + 1)):
                cols = slice(h * HGRN_D, (h + 1) * HGRN_D)
                o = o_pair[:, i * HGRN_D:(i + 1) * HGRN_D]
                state_ref[:, cols] = st_pair[i] * e_end[:, cols] + _tn(v_pair[i], kdec_all[:, cols])
                ms = jnp.mean(o * o, axis=-1, keepdims=True)
                o_ref[rows, cols] = (o * lax.rsqrt(ms + NORM_EPS) * gain[:, cols]).astype(BF16)

    n_chunks = HGRN_TILE // c

    def group(g, carry):
        base = pl.multiple_of(g * HGRN_TILE, HGRN_TILE)
        prepared = prepare(base, 0)
        for ci in range(n_chunks):
            nxt = prepare(base, ci + 1) if ci + 1 < n_chunks else None
            recur(base, ci, prepared)
            prepared = nxt
        return carry

    lax.fori_loop(0, qh_ref.shape[0] // HGRN_TILE, group, 0)


def _hgrn(qh, fh, ih, logits, gain, layer, batch, seq):
    w_all, mask_all = _hgrn_consts()
    assert seq % HGRN_TILE == 0
    depth = logits.shape[0]
    const = lambda b: (0, 0)
    return pl.pallas_call(
        functools.partial(_hgrn_kernel, layer=layer),
        grid=(batch,),
        in_specs=[pl.BlockSpec((seq, GROUP), lambda b: (b, 0))] * 3 + [
                  pl.BlockSpec((depth, GROUP), const),
                  pl.BlockSpec((None, 1, GROUP), lambda b: (layer, 0, 0)),
                  pl.BlockSpec(w_all.shape, const),
                  pl.BlockSpec(mask_all.shape, const)],
        out_specs=pl.BlockSpec((seq, GROUP), lambda b: (b, 0)),
        out_shape=jax.ShapeDtypeStruct((fh.shape[0], GROUP), BF16),
        scratch_shapes=[pltpu.VMEM((HGRN_D, GROUP), F32)],
        compiler_params=pltpu.CompilerParams(dimension_semantics=("parallel",),
                                             vmem_limit_bytes=VMEM_LIMIT),
        name="hgrn",
    )(qh, fh, ih, logits, gain, w_all, mask_all)


def _mem_kv_kernel(m_ref, g_ref, w_ref, gk_ref, seg_ref, k_ref, v_ref):
    x = m_ref[...]
    ms = jnp.mean(x * x, axis=-1, keepdims=True)
    hb = (x * lax.rsqrt(ms + NORM_EPS) * g_ref[...]).astype(BF16)
    gk = gk_ref[...]
    seg = seg_ref[...]
    for c in range(GROUP // 256):
        kc = _nn(hb, w_ref[:, c * 256:(c + 1) * 256].astype(BF16))
        ss = _nn((kc * kc).astype(BF16), seg) * (1.0 / MEM_D)
        kn = kc * lax.rsqrt(ss + NORM_EPS) * gk[:, c * 256:(c + 1) * 256]
        k_ref[:, c * 256:(c + 1) * 256] = kn.astype(BF16)
    v_ref[...] = _nn(hb, w_ref[:, GROUP:2 * GROUP].astype(BF16)).astype(BF16)


def _mem_kv(memf, g, w, gk, mem_len, layer):
    n = memf.shape[0]
    seg = np.arange(256)[:, None] // MEM_D == np.arange(256)[None, :] // MEM_D
    seg = jnp.asarray(seg, BF16)
    const = lambda i: (0, 0)
    row = lambda i: (i, 0)
    of_layer = lambda i: (layer, 0, 0)
    return pl.pallas_call(
        _mem_kv_kernel,
        grid=(n // mem_len,),
        in_specs=[pl.BlockSpec((mem_len, D_MODEL), row),
                  pl.BlockSpec((None, 1, D_MODEL), of_layer),
                  pl.BlockSpec((None, D_MODEL, 2 * GROUP), of_layer),
                  pl.BlockSpec((None, 1, GROUP), of_layer),
                  pl.BlockSpec((256, 256), const)],
        out_specs=[pl.BlockSpec((mem_len, GROUP), row)] * 2,
        out_shape=[jax.ShapeDtypeStruct((n, GROUP), BF16)] * 2,
        compiler_params=pltpu.CompilerParams(dimension_semantics=("parallel",), vmem_limit_bytes=VMEM_LIMIT),
        name="mem_kv",
    )(memf, g, w, gk, seg)


def _out_proj_kernel(x_ref, oa_ref, oh_ref, qm_ref, z_ref, km_ref, vm_ref, gq_ref, w_ref, o_ref, y_ref):
    def gate(o, c0, c1):
        y_ref[:, c0:c1] = (o * _silu(z_ref[:, c0:c1].astype(F32))).astype(BF16)

    gate(oa_ref[...].astype(F32), 0, GROUP)
    gate(oh_ref[...].astype(F32), GROUP, 2 * GROUP)
    gq = gq_ref[...]
    for h in range(MEM_HEADS):
        cols = slice(h * MEM_D, (h + 1) * MEM_D)
        q = qm_ref[:, cols].astype(F32)
        ms = jnp.mean(q * q, axis=-1, keepdims=True)
        qn = q * lax.rsqrt(ms + NORM_EPS) * gq * (MEM_D ** -0.5)
        s = _nt(qn.astype(BF16), km_ref[:, cols])
        p = jnp.exp(s - jnp.max(s, axis=1, keepdims=True))
        v_aug = jnp.concatenate([vm_ref[:, cols], jnp.ones((vm_ref.shape[0], MEM_D), BF16)], axis=1)
        pv = _nn(p.astype(BF16), v_aug)
        om = pv[:, 0:MEM_D] / pv[:, MEM_D:2 * MEM_D]
        c0 = 2 * GROUP + h * MEM_D
        gate(om, c0, c0 + MEM_D)
    o_ref[...] = x_ref[...] + _nn(y_ref[...], w_ref[...].astype(BF16))


def _out_proj(xf, oa, oh, qm, z, km, vm, gq, w, seq, mem_len, layer):
    n = xf.shape[0]
    tm = OUT_TILE
    steps = seq // tm
    const = lambda i: (0, 0)
    row = lambda i: (i, 0)
    per_batch = lambda i: (i // steps, 0)
    of_layer = lambda i: (layer, 0, 0)
    return pl.pallas_call(
        _out_proj_kernel,
        grid=(n // tm,),
        in_specs=[pl.BlockSpec((tm, D_MODEL), row),
                  pl.BlockSpec((tm, GROUP), row),
                  pl.BlockSpec((tm, GROUP), row),
                  pl.BlockSpec((tm, GROUP), row),
                  pl.BlockSpec((tm, D_MIX), row),
                  pl.BlockSpec((mem_len, GROUP), per_batch),
                  pl.BlockSpec((mem_len, GROUP), per_batch),
                  pl.BlockSpec((None, 1, MEM_D), of_layer),
                  pl.BlockSpec((None, D_MIX, D_MODEL), of_layer, pipeline_mode=pl.Buffered(1))],
        out_specs=pl.BlockSpec((tm, D_MODEL), row),
        out_shape=jax.ShapeDtypeStruct((n, D_MODEL), F32),
        scratch_shapes=[pltpu.VMEM((tm, D_MIX), BF16)],
        compiler_params=pltpu.CompilerParams(dimension_semantics=("parallel",), vmem_limit_bytes=VMEM_LIMIT),
        name="out_proj",
    )(xf, oa, oh, qm, z, km, vm, gq, w)


def kernel(x, mem, positions, norm_g, w_in, w_out, moba_q_norm, moba_k_norm, hgrn_lb_logits, hgrn_o_norm,
           mem_norm_g, w_mem_kv, mem_q_norm, mem_k_norm):
    batch, seq, d_model = x.shape
    mem_len = mem.shape[1]
    depth = w_in.shape[0]
    assert d_model == D_MODEL and seq % ROW_TILE == 0 and seq % OUT_TILE == 0 and mem_len % 8 == 0
    n = batch * seq
    xf = x.reshape(n, d_model)
    memf = mem.reshape(batch * mem_len, d_model)
    cosf, sinf = _rope_tables(positions.reshape(1, n))
    per_head = lambda p, reps: jnp.tile(p, (1, reps))[:, None, :]
    norm_g, mem_norm_g, mem_q_norm = norm_g[:, None, :], mem_norm_g[:, None, :], mem_q_norm[:, None, :]
    gq, gk = per_head(moba_q_norm, GROUP // MOBA_D), per_head(moba_k_norm, GROUP // MOBA_D)
    g_hgrn, g_memk = per_head(hgrn_o_norm, HGRN_HEADS), per_head(mem_k_norm, MEM_HEADS)
    for l in range(depth):
        qa, ka, va, qh, fh, ih, qm, z = _in_proj(xf, norm_g, w_in, cosf, sinf, gq, gk, l)
        oa = _moba(qa, ka, va, batch, seq)
        oh = _hgrn(qh, fh, ih, hgrn_lb_logits, g_hgrn, l, batch, seq)
        km, vm = _mem_kv(memf, mem_norm_g, w_mem_kv, g_memk, mem_len, l)
        xf = _out_proj(xf, oa, oh, qm, z, km, vm, mem_q_norm, w_out, seq, mem_len, l)
    return xf.reshape(batch, seq, d_model)
```

```python
import functools

import numpy as np
import jax
import jax.numpy as jnp
from jax import lax
from jax.experimental import pallas as pl
from jax.experimental.pallas import tpu as pltpu

F32 = jnp.float32
BF16 = jnp.bfloat16

D_MODEL = 1024
GROUP = D_MODEL // 2
D_MIX = 3 * GROUP
IN_COLS = 7 * GROUP + D_MIX
MOBA_D = 64
MOBA_BLOCK = 256
MOBA_TOPK = 3
MOBA_VT_ROWS = MOBA_D + 16
HGRN_D = 128
HGRN_HEADS = GROUP // HGRN_D
HGRN_CHUNK = 64
HGRN_LEVELS = 6
HGRN_MM_LEVELS = 3
LOG2E = 1.4426950408889634
MEM_D = 128
MEM_HEADS = GROUP // MEM_D
ROPE_THETA = 500000.0
ROPE_DIM = MOBA_D // 4
ROPE_HALF = ROPE_DIM // 2
NORM_EPS = 1e-6
LANES = 128
ROW_TILE = 512
OUT_TILE = 1024
HGRN_TILE = 1024
MASKED = -1e30
VMEM_LIMIT = 48 * 1024 * 1024


def _nt(a, b):
    return lax.dot_general(a, b, (((1,), (1,)), ((), ())), preferred_element_type=F32)


def _tn(a, b):
    return lax.dot_general(a, b, (((0,), (0,)), ((), ())), preferred_element_type=F32)


def _nn(a, b):
    return jnp.dot(a, b, preferred_element_type=F32)


def _split2(x):
    hi = x.astype(BF16)
    lo = (x - hi.astype(F32)).astype(BF16)
    return hi, lo


def _split3(x):
    hi = x.astype(BF16)
    r = x - hi.astype(F32)
    mid = r.astype(BF16)
    lo = (r - mid.astype(F32)).astype(BF16)
    return hi, mid, lo


def _sigmoid(x):
    return 1.0 / (1.0 + jnp.exp(-x))


def _silu(x):
    return x * _sigmoid(x)


def _rope_table_kernel(pos_ref, invf_ref, expand_ref, base_ref, cos_ref, sin_ref):
    ang = invf_ref[...] * pos_ref[...].astype(F32)
    tm = ang.shape[1]
    table = jnp.concatenate([jnp.cos(ang), jnp.sin(ang), jnp.zeros((LANES - ROPE_DIM, tm), F32)], axis=0)
    hi, mid, lo = _split3(table.T)
    e = expand_ref[...]
    both = _nn(hi, e) + _nn(mid, e) + _nn(lo, e)
    cos_ref[...] = both[:, 0:LANES] + base_ref[...]
    sin_ref[...] = both[:, LANES:2 * LANES]


def _rope_tables(pos):
    n = pos.shape[1]
    tm = min(n, 2048)
    lane = np.arange(LANES) % MOBA_D
    freq = lane % ROPE_HALF
    rope = lane < ROPE_DIM
    sgn = np.where(lane < ROPE_HALF, -1.0, 1.0)
    expand = np.zeros((LANES, 2 * LANES), np.float32)
    for l in np.nonzero(rope)[0]:
        expand[freq[l], l] = 1.0
        expand[ROPE_HALF + freq[l], LANES + l] = sgn[l]
    base = np.where(rope, 0.0, 1.0).astype(np.float32)[None, :]
    invf = (ROPE_THETA ** (-np.arange(ROPE_HALF, dtype=np.float64) / ROPE_HALF)).astype(np.float32)[:, None]
    const = lambda i: (0, 0)
    return pl.pallas_call(
        _rope_table_kernel,
        grid=(n // tm,),
        in_specs=[pl.BlockSpec((1, tm), lambda i: (0, i)),
                  pl.BlockSpec((ROPE_HALF, 1), const),
                  pl.BlockSpec((LANES, 2 * LANES), const),
                  pl.BlockSpec((1, LANES), const)],
        out_specs=[pl.BlockSpec((tm, LANES), lambda i: (i, 0))] * 2,
        out_shape=[jax.ShapeDtypeStruct((n, LANES), F32)] * 2,
        name="rope_tables",
    )(pos, jnp.asarray(invf), jnp.asarray(expand, BF16), jnp.asarray(base))


def _in_proj_kernel(x_ref, g_ref, w_ref, cos_ref, sin_ref, gq_ref, gk_ref, seg_ref,
                    qa_ref, ka_ref, va_ref, qh_ref, fh_ref, ih_ref, qm_ref, z_ref):
    x = x_ref[...]
    ms = jnp.mean(x * x, axis=-1, keepdims=True)
    hb = (x * lax.rsqrt(ms + NORM_EPS) * g_ref[...]).astype(BF16)

    def proj(c0, width):
        return _nn(hb, w_ref[:, c0:c0 + width].astype(BF16))

    cosf = cos_ref[...]
    sinf = sin_ref[...]
    lane = lax.broadcasted_iota(jnp.int32, (1, LANES), 1) % MOBA_D
    lo_lane = lane < ROPE_HALF
    seg = seg_ref[...]

    def head_norm_rope(p, gain, out_ref):
        for c in range(GROUP // 256):
            pc = p[:, c * 256:(c + 1) * 256]
            ss = _nn((pc * pc).astype(BF16), seg) * (1.0 / MOBA_D)
            pn = pc * lax.rsqrt(ss + NORM_EPS) * gain[:, c * 256:(c + 1) * 256]
            for v in range(2):
                xv = pn[:, v * LANES:(v + 1) * LANES]
                rot = jnp.where(lo_lane, pltpu.roll(xv, LANES - ROPE_HALF, 1), pltpu.roll(xv, ROPE_HALF, 1))
                col = c * 256 + v * LANES
                out_ref[:, col:col + LANES] = (xv * cosf + rot * sinf).astype(out_ref.dtype)

    head_norm_rope(proj(0, GROUP), gq_ref[...], qa_ref)
    head_norm_rope(proj(GROUP, GROUP), gk_ref[...], ka_ref)
    va_ref[...] = proj(2 * GROUP, GROUP).astype(BF16)
    qh_ref[...] = proj(3 * GROUP, GROUP).astype(BF16)
    fh_ref[...] = proj(4 * GROUP, GROUP)
    ih_ref[...] = proj(5 * GROUP, GROUP).astype(BF16)
    qm_ref[...] = proj(6 * GROUP, GROUP).astype(BF16)
    for c in range(3):
        z_ref[:, c * GROUP:(c + 1) * GROUP] = proj((7 + c) * GROUP, GROUP).astype(BF16)


def _in_proj(xf, g, w, cosf, sinf, gq, gk, layer):
    n = xf.shape[0]
    tm = ROW_TILE
    seg = np.arange(256)[:, None] // MOBA_D == np.arange(256)[None, :] // MOBA_D
    seg = jnp.asarray(seg, BF16)
    const = lambda i: (0, 0)
    row = lambda i: (i, 0)
    of_layer = lambda i: (layer, 0, 0)
    outs = ((GROUP, F32), (GROUP, BF16), (GROUP, BF16), (GROUP, BF16), (GROUP, F32), (GROUP, BF16),
            (GROUP, BF16), (D_MIX, BF16))
    return pl.pallas_call(
        _in_proj_kernel,
        grid=(n // tm,),
        in_specs=[pl.BlockSpec((tm, D_MODEL), row),
                  pl.BlockSpec((None, 1, D_MODEL), of_layer),
                  pl.BlockSpec((None, D_MODEL, IN_COLS), of_layer, pipeline_mode=pl.Buffered(1)),
                  pl.BlockSpec((tm, LANES), row),
                  pl.BlockSpec((tm, LANES), row),
                  pl.BlockSpec((None, 1, GROUP), of_layer),
                  pl.BlockSpec((None, 1, GROUP), of_layer),
                  pl.BlockSpec((256, 256), const)],
        out_specs=[pl.BlockSpec((tm, w), row) for w, _ in outs],
        out_shape=[jax.ShapeDtypeStruct((n, w), dt) for w, dt in outs],
        compiler_params=pltpu.CompilerParams(dimension_semantics=("parallel",), vmem_limit_bytes=VMEM_LIMIT),
        name="in_proj",
    )(xf, g, w, cosf, sinf, gq, gk, seg)


def _moba_kernel(q_ref, k_ref, v_ref, o_ref, kmean_ref, kaug_ref, vt_ref, qaug_ref, s_ref, *, nb):
    blk = MOBA_BLOCK
    seq = nb * blk
    lane = lax.broadcasted_iota(jnp.int32, (1, LANES), 1)
    heads = [(lane >= h * MOBA_D) & (lane < (h + 1) * MOBA_D) for h in range(2)]
    spare = [(1 - h) * MOBA_D for h in range(2)]

    ones = jnp.ones((MOBA_VT_ROWS - MOBA_D, blk), F32)
    for j in range(nb):
        rows = slice(j * blk, (j + 1) * blk)
        kj = k_ref[rows, :]
        kmean_ref[j:j + 1, :] = jnp.mean(kj.astype(F32), axis=0, keepdims=True)
        v_t = v_ref[rows, :].astype(F32).T
        vth = [jnp.concatenate([v_t[h * MOBA_D:(h + 1) * MOBA_D], ones], axis=0).astype(BF16) for h in range(2)]
        zvt = jnp.zeros_like(vth[0])
        vt_ref[j] = jnp.concatenate([jnp.concatenate([vth[0], zvt], axis=1),
                                     jnp.concatenate([zvt, vth[1]], axis=1)], axis=0)
        for h in range(2):
            onehot = jnp.where(lane == spare[h] + j, 1.0, 0.0).astype(BF16)
            kaug_ref[h, rows, :] = jnp.where(heads[h], kj, onehot)

    km = kmean_ref[...]
    q_hi, q_lo = _split2(q_ref[...])
    km_parts = [_split2(jnp.where(heads[h], km, 0.0)) for h in range(2)]
    g_hi = _nt(jnp.concatenate([km_parts[0][0], km_parts[0][1], km_parts[1][0], km_parts[1][1]], axis=0), q_hi)
    g_lo = _nt(jnp.concatenate([km_parts[0][0], km_parts[1][0]], axis=0), q_lo)
    blk_row = lax.broadcasted_iota(jnp.int32, (nb, 1), 0)
    q_blk = lax.broadcasted_iota(jnp.int32, (1, seq), 1) // blk
    past = blk_row < q_blk
    for h in range(2):
        gate_t = g_hi[2 * h * nb:(2 * h + 1) * nb] + g_hi[(2 * h + 1) * nb:(2 * h + 2) * nb] + g_lo[h * nb:(h + 1) * nb]
        gate_t = jnp.where(past, gate_t, -jnp.inf)
        cnt = jnp.zeros((nb, seq), F32)
        for n in range(nb - 1):
            gn = gate_t[n:n + 1, :]
            before = (gn > gate_t) | ((gn == gate_t) & (n < blk_row))
            cnt = cnt + jnp.where(before, 1.0, 0.0)
        sel = (past & (cnt < float(MOBA_TOPK))) | (blk_row == q_blk)
        bias_t = jnp.where(sel, 0.0, MASKED)
        for c in range(nb):
            parts = []
            if spare[h]:
                parts.append(jnp.zeros((spare[h], blk), F32))
            parts.append(bias_t[:, c * blk:(c + 1) * blk])
            parts.append(jnp.zeros((LANES - spare[h] - nb, blk), F32))
            bias = jnp.concatenate(parts, axis=0).T
            qs = q_ref[c * blk:(c + 1) * blk, :] * (MOBA_D ** -0.5 * LOG2E)
            qaug_ref[c, h] = jnp.where(heads[h], qs, bias).astype(BF16)

    kpos = lax.broadcasted_iota(jnp.int32, (blk, blk), 0)
    qpos = lax.broadcasted_iota(jnp.int32, (blk, blk), 1)
    causal_t = kpos <= qpos

    for t in range(nb // 2):
        buf = t % 2
        base = 0
        for c in (nb - 1 - t, t):
            ms = []
            for h in range(2):
                qa = qaug_ref[c, h]
                st = s_ref.at[buf, h]
                mx = None
                for j in range(c + 1):
                    s = _nt(kaug_ref[h, j * blk:(j + 1) * blk, :], qa)
                    if j == c:
                        s = jnp.where(causal_t, s, MASKED)
                    st[base + j * blk:base + (j + 1) * blk, :] = s
                    mx = s if mx is None else jnp.maximum(mx, s)
                ms.append(jnp.max(mx, axis=0, keepdims=True))
            acc = None
            for j in range(c + 1):
                tile = slice(base + j * blk, base + (j + 1) * blk)
                p = jnp.concatenate([jnp.exp2(s_ref[buf, h, tile, :] - ms[h]).astype(BF16) for h in range(2)], axis=0)
                pv = _nn(vt_ref[j], p)
                acc = pv if acc is None else acc + pv
            outs = [acc[h * MOBA_VT_ROWS:h * MOBA_VT_ROWS + MOBA_D]
                    / acc[h * MOBA_VT_ROWS + MOBA_D:h * MOBA_VT_ROWS + MOBA_D + 1] for h in range(2)]
            o_ref[c * blk:(c + 1) * blk, :] = jnp.concatenate(outs, axis=0).T.astype(BF16)
            base += (c + 1) * blk


def _moba(qa, ka, va, batch, seq):
    nb = seq // MOBA_BLOCK
    assert nb % 2 == 0 and nb <= MOBA_D
    pairs = GROUP // LANES
    whole = pl.BlockSpec((seq, LANES), lambda b, p: (b, p))
    return pl.pallas_call(
        functools.partial(_moba_kernel, nb=nb),
        grid=(batch, pairs),
        in_specs=[whole, whole, whole],
        out_specs=whole,
        out_shape=jax.ShapeDtypeStruct(qa.shape, BF16),
        scratch_shapes=[pltpu.VMEM((nb, LANES), F32),
                        pltpu.VMEM((2, seq, LANES), BF16),
                        pltpu.VMEM((nb, 2 * MOBA_VT_ROWS, 2 * MOBA_BLOCK), BF16),
                        pltpu.VMEM((nb, 2, MOBA_BLOCK, LANES), BF16),
                        pltpu.VMEM((2, 2, (nb + 1) * MOBA_BLOCK, MOBA_BLOCK), F32)],
        compiler_params=pltpu.CompilerParams(dimension_semantics=("parallel", "parallel"),
                                             vmem_limit_bytes=VMEM_LIMIT),
        name="moba",
    )(qa, ka, va)


def _hgrn_consts():
    c = HGRN_CHUNK
    t = np.arange(c)
    w_rows, masks = [], []
    for lvl in range(HGRN_LEVELS):
        m = 1 << lvl
        blk = t // m
        odd = (blk % 2) == 1
        start = blk * m
        end = start + m - 1
        u = t[None, :]
        w_odd = (u >= start[:, None]) & (u <= t[:, None])
        w_even = (u > t[:, None]) & (u <= end[:, None])
        w_rows.append(np.where(odd[:, None], w_odd, w_even))
        masks.append(odd[:, None] & (blk[None, :] == blk[:, None] - 1))
    masks.append(t[None, :] == t[:, None])
    w_mm = w_rows[:HGRN_MM_LEVELS] + [t[None, :] <= t[:, None]]
    w_all = np.concatenate(w_mm, axis=0).astype(np.float32)
    w_all = np.concatenate([w_all, w_all], axis=1)
    mask_cat = np.concatenate([masks[-1]] + masks[:-1] + [np.zeros((c, c), bool)], axis=1).astype(np.float32)
    return jnp.asarray(w_all, BF16), jnp.asarray(mask_cat, BF16)


def _hgrn_kernel(qh_ref, fh_ref, ih_ref, logit_ref, gain_ref, w_ref, mask_ref, o_ref, state_ref, *, layer):
    c = HGRN_CHUNK
    state_ref[...] = jnp.zeros_like(state_ref)

    if layer > 0:
        lg = logit_ref[...]
        e = jnp.exp(lg - jnp.max(lg, axis=0, keepdims=True))
        sm = e / jnp.sum(e, axis=0, keepdims=True)
        lb = jnp.sum(sm[1:layer + 1, :], axis=0, keepdims=True)
        log_lb = jnp.log(lb) * LOG2E
        log_1m_lb = jnp.log1p(-lb) * LOG2E
    w_all = w_ref[...]
    gain = gain_ref[...]

    def prepare(base, ci):
        rows = pl.ds(base + ci * c, c)
        fl = fh_ref[rows, :]
        vi = ih_ref[rows, :]
        qh = qh_ref[rows, :].astype(F32)
        qf = qh / (1.0 + jnp.exp2(qh * -LOG2E))
        e = jnp.exp2(jnp.abs(fl) * -LOG2E)
        r = 1.0 / (1.0 + e)
        log_sig = jnp.minimum(fl, 0.0) * LOG2E - jnp.log2(1.0 + e)
        sig_neg = jnp.where(fl >= 0.0, e * r, r)
        if layer > 0:
            b = log_1m_lb + log_sig
            lf = jnp.maximum(log_lb, b) + jnp.log2(1.0 + jnp.exp2(-jnp.abs(log_lb - b)))
            kf = (1.0 - lb) * sig_neg
        else:
            lf = log_sig
            kf = sig_neg

        f1, f2 = _split2(lf)
        x_mm = _nn(w_all, jnp.concatenate([f1, f2], axis=0))
        a2 = x_mm[HGRN_MM_LEVELS * c:(HGRN_MM_LEVELS + 1) * c, :]
        e_lvl = [jnp.exp2(x_mm[lvl * c:(lvl + 1) * c, :]) for lvl in range(HGRN_MM_LEVELS)]
        for lvl in range(HGRN_MM_LEVELS, HGRN_LEVELS):
            m = 1 << lvl
            parts = []
            for g in range(c // (2 * m)):
                ref = a2[2 * m * g + m - 1:2 * m * g + m, :]
                parts.append(-jnp.abs(a2[2 * m * g:2 * m * (g + 1), :] - ref))
            e_lvl.append(jnp.exp2(jnp.concatenate(parts, axis=0)))
        e_cum = jnp.exp2(a2)
        e_rest = jnp.exp2(a2[c - 1:c, :] - a2)
        e_end = e_cum[c - 1:c, :]
        qf_b, kf_b = qf.astype(BF16), kf.astype(BF16)
        e_b = [e.astype(BF16) for e in e_lvl]
        qb = [qf_b] + [qf_b * e for e in e_b]
        kb = [kf_b] + [kf_b * e for e in e_b]
        return qb, kb, qf_b * e_cum.astype(BF16), kf_b * e_rest.astype(BF16), vi, e_end

    def recur(base, ci, prepared):
        rows = pl.ds(base + ci * c, c)
        qb_all, kb_all, ql_all, kdec_all, vi, e_end = prepared
        zero = jnp.zeros((c, HGRN_D), BF16)
        zsq = jnp.zeros((HGRN_D, HGRN_D), BF16)
        for hp in range(HGRN_HEADS // 2):
            p_pair, ql_pair, v_pair, st_pair = [], [], [], []
            for h in (2 * hp, 2 * hp + 1):
                cols = slice(h * HGRN_D, (h + 1) * HGRN_D)
                qb = [q[:, cols] for q in qb_all]
                kb = [k[:, cols] for k in kb_all]
                s_parts = []
                for a in range(0, HGRN_LEVELS, 2):
                    q_cat = jnp.concatenate([qb[a], qb[a + 1]], axis=1)
                    k_blk = jnp.concatenate([jnp.concatenate([kb[a], zero], axis=1),
                                             jnp.concatenate([zero, kb[a + 1]], axis=1)], axis=0)
                    s_parts.append(_nt(q_cat, k_blk))
                s_parts.append(_nt(qb[HGRN_LEVELS], jnp.concatenate([kb[HGRN_LEVELS], zero], axis=0)))
                p = None
                for a, part in enumerate(s_parts):
                    part = part.astype(BF16) * mask_ref[:, 2 * a * c:(2 * a + 2) * c]
                    p = part if p is None else p + part
                p_pair.append(p)
                ql_pair.append(ql_all[:, cols])
                v_pair.append(vi[:, cols])
                st_pair.append(state_ref[:, cols])
            vv = [jnp.concatenate([v, v], axis=0) for v in v_pair]
            zv = jnp.zeros_like(vv[0])
            v_blk = jnp.concatenate([jnp.concatenate([vv[0], zv], axis=1),
                                     jnp.concatenate([zv, vv[1]], axis=1)], axis=0)
            o_pair = _nn(jnp.concatenate(p_pair, axis=1), v_blk)
            st_b = [st.astype(BF16) for st in st_pair]
            st_blk = jnp.concatenate([jnp.concatenate([st_b[0], zsq], axis=1),
                                      jnp.concatenate([zsq, st_b[1]], axis=1)], axis=0)
            o_pair = o_pair + _nt(jnp.concatenate(ql_pair, axis=1), st_blk)
            for i, h in enumerate((2 * hp, 2 * hp + 1)):
                cols = slice(h * HGRN_D, (h + 1) * HGRN_D)
                o = o_pair[:, i * HGRN_D:(i + 1) * HGRN_D]
                state_ref[:, cols] = st_pair[i] * e_end[:, cols] + _tn(v_pair[i], kdec_all[:, cols])
                ms = jnp.mean(o * o, axis=-1, keepdims=True)
                o_ref[rows, cols] = (o * lax.rsqrt(ms + NORM_EPS) * gain[:, cols]).astype(BF16)

    n_chunks = HGRN_TILE // c

    def group(g, carry):
        base = pl.multiple_of(g * HGRN_TILE, HGRN_TILE)
        prepared = prepare(base, 0)
        for ci in range(n_chunks):
            nxt = prepare(base, ci + 1) if ci + 1 < n_chunks else None
            recur(base, ci, prepared)
            prepared = nxt
        return carry

    lax.fori_loop(0, qh_ref.shape[0] // HGRN_TILE, group, 0)


def _hgrn(qh, fh, ih, logits, gain, layer, batch, seq):
    w_all, mask_all = _hgrn_consts()
    assert seq % HGRN_TILE == 0
    depth = logits.shape[0]
    const = lambda b: (0, 0)
    return pl.pallas_call(
        functools.partial(_hgrn_kernel, layer=layer),
        grid=(batch,),
        in_specs=[pl.BlockSpec((seq, GROUP), lambda b: (b, 0))] * 3 + [
                  pl.BlockSpec((depth, GROUP), const),
                  pl.BlockSpec((None, 1, GROUP), lambda b: (layer, 0, 0)),
                  pl.BlockSpec(w_all.shape, const),
                  pl.BlockSpec(mask_all.shape, const)],
        out_specs=pl.BlockSpec((seq, GROUP), lambda b: (b, 0)),
        out_shape=jax.ShapeDtypeStruct((fh.shape[0], GROUP), BF16),
        scratch_shapes=[pltpu.VMEM((HGRN_D, GROUP), F32)],
        compiler_params=pltpu.CompilerParams(dimension_semantics=("parallel",),
                                             vmem_limit_bytes=VMEM_LIMIT),
        name="hgrn",
    )(qh, fh, ih, logits, gain, w_all, mask_all)


def _mem_kv_kernel(m_ref, g_ref, w_ref, gk_ref, seg_ref, k_ref, v_ref):
    x = m_ref[...]
    ms = jnp.mean(x * x, axis=-1, keepdims=True)
    hb = (x * lax.rsqrt(ms + NORM_EPS) * g_ref[...]).astype(BF16)
    gk = gk_ref[...]
    seg = seg_ref[...]
    for c in range(GROUP // 256):
        kc = _nn(hb, w_ref[:, c * 256:(c + 1) * 256].astype(BF16))
        ss = _nn((kc * kc).astype(BF16), seg) * (1.0 / MEM_D)
        kn = kc * lax.rsqrt(ss + NORM_EPS) * gk[:, c * 256:(c + 1) * 256]
        k_ref[:, c * 256:(c + 1) * 256] = kn.astype(BF16)
    v_ref[...] = _nn(hb, w_ref[:, GROUP:2 * GROUP].astype(BF16)).astype(BF16)


def _mem_kv(memf, g, w, gk, mem_len, layer):
    n = memf.shape[0]
    seg = np.arange(256)[:, None] // MEM_D == np.arange(256)[None, :] // MEM_D
    seg = jnp.asarray(seg, BF16)
    const = lambda i: (0, 0)
    row = lambda i: (i, 0)
    of_layer = lambda i: (layer, 0, 0)
    return pl.pallas_call(
        _mem_kv_kernel,
        grid=(n // mem_len,),
        in_specs=[pl.BlockSpec((mem_len, D_MODEL), row),
                  pl.BlockSpec((None, 1, D_MODEL), of_layer),
                  pl.BlockSpec((None, D_MODEL, 2 * GROUP), of_layer),
                  pl.BlockSpec((None, 1, GROUP), of_layer),
                  pl.BlockSpec((256, 256), const)],
        out_specs=[pl.BlockSpec((mem_len, GROUP), row)] * 2,
        out_shape=[jax.ShapeDtypeStruct((n, GROUP), BF16)] * 2,
        compiler_params=pltpu.CompilerParams(dimension_semantics=("parallel",), vmem_limit_bytes=VMEM_LIMIT),
        name="mem_kv",
    )(memf, g, w, gk, seg)


def _out_proj_kernel(x_ref, oa_ref, oh_ref, qm_ref, z_ref, km_ref, vm_ref, gq_ref, w_ref, o_ref, y_ref):
    def gate(o, c0, c1):
        y_ref[:, c0:c1] = (o * _silu(z_ref[:, c0:c1].astype(F32))).astype(BF16)

    gate(oa_ref[...].astype(F32), 0, GROUP)
    gate(oh_ref[...].astype(F32), GROUP, 2 * GROUP)
    gq = gq_ref[...]
    for h in range(MEM_HEADS):
        cols = slice(h * MEM_D, (h + 1) * MEM_D)
        q = qm_ref[:, cols].astype(F32)
        ms = jnp.mean(q * q, axis=-1, keepdims=True)
        qn = q * lax.rsqrt(ms + NORM_EPS) * gq * (MEM_D ** -0.5)
        s = _nt(qn.astype(BF16), km_ref[:, cols])
        p = jnp.exp(s - jnp.max(s, axis=1, keepdims=True))
        v_aug = jnp.concatenate([vm_ref[:, cols], jnp.ones((vm_ref.shape[0], MEM_D), BF16)], axis=1)
        pv = _nn(p.astype(BF16), v_aug)
        om = pv[:, 0:MEM_D] / pv[:, MEM_D:2 * MEM_D]
        c0 = 2 * GROUP + h * MEM_D
        gate(om, c0, c0 + MEM_D)
    o_ref[...] = x_ref[...] + _nn(y_ref[...], w_ref[...].astype(BF16))


def _out_proj(xf, oa, oh, qm, z, km, vm, gq, w, seq, mem_len, layer):
    n = xf.shape[0]
    tm = OUT_TILE
    steps = seq // tm
    const = lambda i: (0, 0)
    row = lambda i: (i, 0)
    per_batch = lambda i: (i // steps, 0)
    of_layer = lambda i: (layer, 0, 0)
    return pl.pallas_call(
        _out_proj_kernel,
        grid=(n // tm,),
        in_specs=[pl.BlockSpec((tm, D_MODEL), row),
                  pl.BlockSpec((tm, GROUP), row),
                  pl.BlockSpec((tm, GROUP), row),
                  pl.BlockSpec((tm, GROUP), row),
                  pl.BlockSpec((tm, D_MIX), row),
                  pl.BlockSpec((mem_len, GROUP), per_batch),
                  pl.BlockSpec((mem_len, GROUP), per_batch),
                  pl.BlockSpec((None, 1, MEM_D), of_layer),
                  pl.BlockSpec((None, D_MIX, D_MODEL), of_layer, pipeline_mode=pl.Buffered(1))],
        out_specs=pl.BlockSpec((tm, D_MODEL), row),
        out_shape=jax.ShapeDtypeStruct((n, D_MODEL), F32),
        scratch_shapes=[pltpu.VMEM((tm, D_MIX), BF16)],
        compiler_params=pltpu.CompilerParams(dimension_semantics=("parallel",), vmem_limit_bytes=VMEM_LIMIT),
        name="out_proj",
    )(xf, oa, oh, qm, z, km, vm, gq, w)


def kernel(x, mem, positions, norm_g, w_in, w_out, moba_q_norm, moba_k_norm, hgrn_lb_logits, hgrn_o_norm,
           mem_norm_g, w_mem_kv, mem_q_norm, mem_k_norm):
    batch, seq, d_model = x.shape
    mem_len = mem.shape[1]
    depth = w_in.shape[0]
    assert d_model == D_MODEL and seq % ROW_TILE == 0 and seq % OUT_TILE == 0 and mem_len % 8 == 0
    n = batch * seq
    xf = x.reshape(n, d_model)
    memf = mem.reshape(batch * mem_len, d_model)
    cosf, sinf = _rope_tables(positions.reshape(1, n))
    per_head = lambda p, reps: jnp.tile(p, (1, reps))[:, None, :]
    norm_g, mem_norm_g, mem_q_norm = norm_g[:, None, :], mem_norm_g[:, None, :], mem_q_norm[:, None, :]
    gq, gk = per_head(moba_q_norm, GROUP // MOBA_D), per_head(moba_k_norm, GROUP // MOBA_D)
    g_hgrn, g_memk = per_head(hgrn_o_norm, HGRN_HEADS), per_head(mem_k_norm, MEM_HEADS)
    for l in range(depth):
        qa, ka, va, qh, fh, ih, qm, z = _in_proj(xf, norm_g, w_in, cosf, sinf, gq, gk, l)
        oa = _moba(qa, ka, va, batch, seq)
        oh = _hgrn(qh, fh, ih, hgrn_lb_logits, g_hgrn, l, batch, seq)
        km, vm = _mem_kv(memf, mem_norm_g, w_mem_kv, g_memk, mem_len, l)
        xf = _out_proj(xf, oa, oh, qm, z, km, vm, mem_q_norm, w_out, seq, mem_len, l)
    return xf.reshape(batch, seq, d_model)
```

```python
import functools

import numpy as np
import jax
import jax.numpy as jnp
from jax import lax
from jax.experimental import pallas as pl
from jax.experimental.pallas import tpu as pltpu

F32 = jnp.float32
BF16 = jnp.bfloat16

D_MODEL = 1024
GROUP = D_MODEL // 2
D_MIX = 3 * GROUP
IN_COLS = 7 * GROUP + D_MIX
MOBA_D = 64
MOBA_BLOCK = 256
MOBA_TOPK = 3
MOBA_VT_ROWS = MOBA_D + 16
HGRN_D = 128
HGRN_HEADS = GROUP // HGRN_D
HGRN_CHUNK = 64
HGRN_LEVELS = 6
HGRN_MM_LEVELS = 3
LOG2E = 1.4426950408889634
MEM_D = 128
MEM_HEADS = GROUP // MEM_D
ROPE_THETA = 500000.0
ROPE_DIM = MOBA_D // 4
ROPE_HALF = ROPE_DIM // 2
NORM_EPS = 1e-6
LANES = 128
ROW_TILE = 512
OUT_TILE = 1024
HGRN_TILE = 1024
MASKED = -1e30
VMEM_LIMIT = 48 * 1024 * 1024


def _nt(a, b):
    return lax.dot_general(a, b, (((1,), (1,)), ((), ())), preferred_element_type=F32)


def _tn(a, b):
    return lax.dot_general(a, b, (((0,), (0,)), ((), ())), preferred_element_type=F32)


def _nn(a, b):
    return jnp.dot(a, b, preferred_element_type=F32)


def _split2(x):
    hi = x.astype(BF16)
    lo = (x - hi.astype(F32)).astype(BF16)
    return hi, lo


def _split3(x):
    hi = x.astype(BF16)
    r = x - hi.astype(F32)
    mid = r.astype(BF16)
    lo = (r - mid.astype(F32)).astype(BF16)
    return hi, mid, lo


def _sigmoid(x):
    return 1.0 / (1.0 + jnp.exp(-x))


def _silu(x):
    return x * _sigmoid(x)


def _rope_table_kernel(pos_ref, invf_ref, expand_ref, base_ref, cos_ref, sin_ref):
    ang = invf_ref[...] * pos_ref[...].astype(F32)
    tm = ang.shape[1]
    table = jnp.concatenate([jnp.cos(ang), jnp.sin(ang), jnp.zeros((LANES - ROPE_DIM, tm), F32)], axis=0)
    hi, mid, lo = _split3(table.T)
    e = expand_ref[...]
    both = _nn(hi, e) + _nn(mid, e) + _nn(lo, e)
    cos_ref[...] = both[:, 0:LANES] + base_ref[...]
    sin_ref[...] = both[:, LANES:2 * LANES]


def _rope_tables(pos):
    n = pos.shape[1]
    tm = min(n, 2048)
    lane = np.arange(LANES) % MOBA_D
    freq = lane % ROPE_HALF
    rope = lane < ROPE_DIM
    sgn = np.where(lane < ROPE_HALF, -1.0, 1.0)
    expand = np.zeros((LANES, 2 * LANES), np.float32)
    for l in np.nonzero(rope)[0]:
        expand[freq[l], l] = 1.0
        expand[ROPE_HALF + freq[l], LANES + l] = sgn[l]
    base = np.where(rope, 0.0, 1.0).astype(np.float32)[None, :]
    invf = (ROPE_THETA ** (-np.arange(ROPE_HALF, dtype=np.float64) / ROPE_HALF)).astype(np.float32)[:, None]
    const = lambda i: (0, 0)
    return pl.pallas_call(
        _rope_table_kernel,
        grid=(n // tm,),
        in_specs=[pl.BlockSpec((1, tm), lambda i: (0, i)),
                  pl.BlockSpec((ROPE_HALF, 1), const),
                  pl.BlockSpec((LANES, 2 * LANES), const),
                  pl.BlockSpec((1, LANES), const)],
        out_specs=[pl.BlockSpec((tm, LANES), lambda i: (i, 0))] * 2,
        out_shape=[jax.ShapeDtypeStruct((n, LANES), F32)] * 2,
        name="rope_tables",
    )(pos, jnp.asarray(invf), jnp.asarray(expand, BF16), jnp.asarray(base))


def _in_proj_kernel(x_ref, g_ref, w_ref, cos_ref, sin_ref, gq_ref, gk_ref,
                    qa_ref, ka_ref, va_ref, qh_ref, fh_ref, ih_ref, qm_ref, z_ref):
    x = x_ref[...]
    ms = jnp.mean(x * x, axis=-1, keepdims=True)
    hb = (x * lax.rsqrt(ms + NORM_EPS) * g_ref[...]).astype(BF16)

    def proj(c0, width):
        return _nn(hb, w_ref[:, c0:c0 + width].astype(BF16))

    cosf = cos_ref[...]
    sinf = sin_ref[...]
    lane = lax.broadcasted_iota(jnp.int32, (1, LANES), 1) % MOBA_D
    lo_lane = lane < ROPE_HALF
    first = lax.broadcasted_iota(jnp.int32, (1, LANES), 1) < MOBA_D

    def head_norm_rope(p, gain, out_ref):
        for v in range(GROUP // LANES):
            xv = p[:, v * LANES:(v + 1) * LANES]
            sq = xv * xv
            s_first = jnp.sum(jnp.where(first, sq, 0.0), axis=1, keepdims=True)
            s_both = jnp.sum(sq, axis=1, keepdims=True)
            ss = jnp.where(first, s_first, s_both - s_first) * (1.0 / MOBA_D)
            xv = xv * lax.rsqrt(ss + NORM_EPS) * gain[:, v * LANES:(v + 1) * LANES]
            rot = jnp.where(lo_lane, pltpu.roll(xv, LANES - ROPE_HALF, 1), pltpu.roll(xv, ROPE_HALF, 1))
            out_ref[:, v * LANES:(v + 1) * LANES] = (xv * cosf + rot * sinf).astype(out_ref.dtype)

    head_norm_rope(proj(0, GROUP), gq_ref[...], qa_ref)
    head_norm_rope(proj(GROUP, GROUP), gk_ref[...], ka_ref)
    va_ref[...] = proj(2 * GROUP, GROUP).astype(BF16)
    qh_ref[...] = proj(3 * GROUP, GROUP).astype(BF16)
    fh_ref[...] = proj(4 * GROUP, GROUP)
    ih_ref[...] = proj(5 * GROUP, GROUP).astype(BF16)
    qm_ref[...] = proj(6 * GROUP, GROUP).astype(BF16)
    for c in range(3):
        z_ref[:, c * GROUP:(c + 1) * GROUP] = proj((7 + c) * GROUP, GROUP).astype(BF16)


def _in_proj(xf, g, w, cosf, sinf, gq, gk, layer):
    n = xf.shape[0]
    tm = ROW_TILE
    row = lambda i: (i, 0)
    of_layer = lambda i: (layer, 0, 0)
    outs = ((GROUP, F32), (GROUP, BF16), (GROUP, BF16), (GROUP, BF16), (GROUP, F32), (GROUP, BF16),
            (GROUP, BF16), (D_MIX, BF16))
    return pl.pallas_call(
        _in_proj_kernel,
        grid=(n // tm,),
        in_specs=[pl.BlockSpec((tm, D_MODEL), row),
                  pl.BlockSpec((None, 1, D_MODEL), of_layer),
                  pl.BlockSpec((None, D_MODEL, IN_COLS), of_layer, pipeline_mode=pl.Buffered(1)),
                  pl.BlockSpec((tm, LANES), row),
                  pl.BlockSpec((tm, LANES), row),
                  pl.BlockSpec((None, 1, GROUP), of_layer),
                  pl.BlockSpec((None, 1, GROUP), of_layer)],
        out_specs=[pl.BlockSpec((tm, w), row) for w, _ in outs],
        out_shape=[jax.ShapeDtypeStruct((n, w), dt) for w, dt in outs],
        compiler_params=pltpu.CompilerParams(dimension_semantics=("parallel",), vmem_limit_bytes=VMEM_LIMIT),
        name="in_proj",
    )(xf, g, w, cosf, sinf, gq, gk)


def _moba_kernel(q_ref, k_ref, v_ref, o_ref, kmean_ref, kaug_ref, vt_ref, qaug_ref, s_ref, *, nb):
    blk = MOBA_BLOCK
    seq = nb * blk
    lane = lax.broadcasted_iota(jnp.int32, (1, LANES), 1)
    heads = [(lane >= h * MOBA_D) & (lane < (h + 1) * MOBA_D) for h in range(2)]
    spare = [(1 - h) * MOBA_D for h in range(2)]

    ones = jnp.ones((MOBA_VT_ROWS - MOBA_D, blk), F32)
    for j in range(nb):
        rows = slice(j * blk, (j + 1) * blk)
        kj = k_ref[rows, :]
        kmean_ref[j:j + 1, :] = jnp.mean(kj.astype(F32), axis=0, keepdims=True)
        v_t = v_ref[rows, :].astype(F32).T
        vth = [jnp.concatenate([v_t[h * MOBA_D:(h + 1) * MOBA_D], ones], axis=0).astype(BF16) for h in range(2)]
        zvt = jnp.zeros_like(vth[0])
        vt_ref[j] = jnp.concatenate([jnp.concatenate([vth[0], zvt], axis=1),
                                     jnp.concatenate([zvt, vth[1]], axis=1)], axis=0)
        for h in range(2):
            onehot = jnp.where(lane == spare[h] + j, 1.0, 0.0).astype(BF16)
            kaug_ref[h, rows, :] = jnp.where(heads[h], kj, onehot)

    km = kmean_ref[...]
    q_hi, q_lo = _split2(q_ref[...])
    km_parts = [_split2(jnp.where(heads[h], km, 0.0)) for h in range(2)]
    g_hi = _nt(jnp.concatenate([km_parts[0][0], km_parts[0][1], km_parts[1][0], km_parts[1][1]], axis=0), q_hi)
    g_lo = _nt(jnp.concatenate([km_parts[0][0], km_parts[1][0]], axis=0), q_lo)
    blk_row = lax.broadcasted_iota(jnp.int32, (nb, 1), 0)
    q_blk = lax.broadcasted_iota(jnp.int32, (1, seq), 1) // blk
    past = blk_row < q_blk
    for h in range(2):
        gate_t = g_hi[2 * h * nb:(2 * h + 1) * nb] + g_hi[(2 * h + 1) * nb:(2 * h + 2) * nb] + g_lo[h * nb:(h + 1) * nb]
        gate_t = jnp.where(past, gate_t, -jnp.inf)
        cnt = jnp.zeros((nb, seq), F32)
        for n in range(nb - 1):
            gn = gate_t[n:n + 1, :]
            before = (gn > gate_t) | ((gn == gate_t) & (n < blk_row))
            cnt = cnt + jnp.where(before, 1.0, 0.0)
        sel = (past & (cnt < float(MOBA_TOPK))) | (blk_row == q_blk)
        bias_t = jnp.where(sel, 0.0, MASKED)
        for c in range(nb):
            parts = []
            if spare[h]:
                parts.append(jnp.zeros((spare[h], blk), F32))
            parts.append(bias_t[:, c * blk:(c + 1) * blk])
            parts.append(jnp.zeros((LANES - spare[h] - nb, blk), F32))
            bias = jnp.concatenate(parts, axis=0).T
            qs = q_ref[c * blk:(c + 1) * blk, :] * (MOBA_D ** -0.5 * LOG2E)
            qaug_ref[c, h] = jnp.where(heads[h], qs, bias).astype(BF16)

    kpos = lax.broadcasted_iota(jnp.int32, (blk, blk), 0)
    qpos = lax.broadcasted_iota(jnp.int32, (blk, blk), 1)
    causal_t = kpos <= qpos

    for t in range(nb // 2):
        buf = t % 2
        base = 0
        for c in (nb - 1 - t, t):
            ms = []
            for h in range(2):
                qa = qaug_ref[c, h]
                st = s_ref.at[buf, h]
                mx = None
                for j in range(c + 1):
                    s = _nt(kaug_ref[h, j * blk:(j + 1) * blk, :], qa)
                    if j == c:
                        s = jnp.where(causal_t, s, MASKED)
                    st[base + j * blk:base + (j + 1) * blk, :] = s
                    mx = s if mx is None else jnp.maximum(mx, s)
                ms.append(jnp.max(mx, axis=0, keepdims=True))
            acc = None
            for j in range(c + 1):
                tile = slice(base + j * blk, base + (j + 1) * blk)
                p = jnp.concatenate([jnp.exp2(s_ref[buf, h, tile, :] - ms[h]).astype(BF16) for h in range(2)], axis=0)
                pv = _nn(vt_ref[j], p)
                acc = pv if acc is None else acc + pv
            outs = [acc[h * MOBA_VT_ROWS:h * MOBA_VT_ROWS + MOBA_D]
                    / acc[h * MOBA_VT_ROWS + MOBA_D:h * MOBA_VT_ROWS + MOBA_D + 1] for h in range(2)]
            o_ref[c * blk:(c + 1) * blk, :] = jnp.concatenate(outs, axis=0).T.astype(BF16)
            base += (c + 1) * blk


def _moba(qa, ka, va, batch, seq):
    nb = seq // MOBA_BLOCK
    assert nb % 2 == 0 and nb <= MOBA_D
    pairs = GROUP // LANES
    whole = pl.BlockSpec((seq, LANES), lambda b, p: (b, p))
    return pl.pallas_call(
        functools.partial(_moba_kernel, nb=nb),
        grid=(batch, pairs),
        in_specs=[whole, whole, whole],
        out_specs=whole,
        out_shape=jax.ShapeDtypeStruct(qa.shape, BF16),
        scratch_shapes=[pltpu.VMEM((nb, LANES), F32),
                        pltpu.VMEM((2, seq, LANES), BF16),
                        pltpu.VMEM((nb, 2 * MOBA_VT_ROWS, 2 * MOBA_BLOCK), BF16),
                        pltpu.VMEM((nb, 2, MOBA_BLOCK, LANES), BF16),
                        pltpu.VMEM((2, 2, (nb + 1) * MOBA_BLOCK, MOBA_BLOCK), F32)],
        compiler_params=pltpu.CompilerParams(dimension_semantics=("parallel", "parallel"),
                                             vmem_limit_bytes=VMEM_LIMIT),
        name="moba",
    )(qa, ka, va)


def _hgrn_consts():
    c = HGRN_CHUNK
    t = np.arange(c)
    w_rows, masks = [], []
    for lvl in range(HGRN_LEVELS):
        m = 1 << lvl
        blk = t // m
        odd = (blk % 2) == 1
        start = blk * m
        end = start + m - 1
        u = t[None, :]
        w_odd = (u >= start[:, None]) & (u <= t[:, None])
        w_even = (u > t[:, None]) & (u <= end[:, None])
        w_rows.append(np.where(odd[:, None], w_odd, w_even))
        masks.append(odd[:, None] & (blk[None, :] == blk[:, None] - 1))
    masks.append(t[None, :] == t[:, None])
    w_mm = w_rows[:HGRN_MM_LEVELS] + [t[None, :] <= t[:, None]]
    w_all = np.concatenate(w_mm, axis=0).astype(np.float32)
    w_all = np.concatenate([w_all, w_all], axis=1)
    mask_cat = np.concatenate([masks[-1]] + masks[:-1] + [np.zeros((c, c), bool)], axis=1).astype(np.float32)
    return jnp.asarray(w_all, BF16), jnp.asarray(mask_cat, BF16)


def _hgrn_kernel(qh_ref, fh_ref, ih_ref, logit_ref, gain_ref, w_ref, mask_ref, o_ref, state_ref, *, layer):
    c = HGRN_CHUNK
    state_ref[...] = jnp.zeros_like(state_ref)

    if layer > 0:
        lg = logit_ref[...]
        e = jnp.exp(lg - jnp.max(lg, axis=0, keepdims=True))
        sm = e / jnp.sum(e, axis=0, keepdims=True)
        lb = jnp.sum(sm[1:layer + 1, :], axis=0, keepdims=True)
        log_lb = jnp.log(lb) * LOG2E
        log_1m_lb = jnp.log1p(-lb) * LOG2E
    w_all = w_ref[...]
    gain = gain_ref[...]

    def prepare(base, ci):
        rows = pl.ds(base + ci * c, c)
        fl = fh_ref[rows, :]
        vi = ih_ref[rows, :]
        qh = qh_ref[rows, :].astype(F32)
        qf = qh / (1.0 + jnp.exp2(qh * -LOG2E))
        e = jnp.exp2(jnp.abs(fl) * -LOG2E)
        r = 1.0 / (1.0 + e)
        log_sig = jnp.minimum(fl, 0.0) * LOG2E - jnp.log2(1.0 + e)
        sig_neg = jnp.where(fl >= 0.0, e * r, r)
        if layer > 0:
            b = log_1m_lb + log_sig
            lf = jnp.maximum(log_lb, b) + jnp.log2(1.0 + jnp.exp2(-jnp.abs(log_lb - b)))
            kf = (1.0 - lb) * sig_neg
        else:
            lf = log_sig
            kf = sig_neg

        f1, f2 = _split2(lf)
        x_mm = _nn(w_all, jnp.concatenate([f1, f2], axis=0))
        a2 = x_mm[HGRN_MM_LEVELS * c:(HGRN_MM_LEVELS + 1) * c, :]
        e_lvl = [jnp.exp2(x_mm[lvl * c:(lvl + 1) * c, :]) for lvl in range(HGRN_MM_LEVELS)]
        for lvl in range(HGRN_MM_LEVELS, HGRN_LEVELS):
            m = 1 << lvl
            parts = []
            for g in range(c // (2 * m)):
                ref = a2[2 * m * g + m - 1:2 * m * g + m, :]
                parts.append(-jnp.abs(a2[2 * m * g:2 * m * (g + 1), :] - ref))
            e_lvl.append(jnp.exp2(jnp.concatenate(parts, axis=0)))
        e_cum = jnp.exp2(a2)
        e_rest = jnp.exp2(a2[c - 1:c, :] - a2)
        e_end = e_cum[c - 1:c, :]
        qf_b, kf_b = qf.astype(BF16), kf.astype(BF16)
        e_b = [e.astype(BF16) for e in e_lvl]
        qb = [qf_b] + [qf_b * e for e in e_b]
        kb = [kf_b] + [kf_b * e for e in e_b]
        return qb, kb, qf_b * e_cum.astype(BF16), kf_b * e_rest.astype(BF16), vi, e_end

    def recur(base, ci, prepared):
        rows = pl.ds(base + ci * c, c)
        qb_all, kb_all, ql_all, kdec_all, vi, e_end = prepared
        zero = jnp.zeros((c, HGRN_D), BF16)
        zsq = jnp.zeros((HGRN_D, HGRN_D), BF16)
        for hp in range(HGRN_HEADS // 2):
            p_pair, ql_pair, v_pair, st_pair = [], [], [], []
            for h in (2 * hp, 2 * hp + 1):
                cols = slice(h * HGRN_D, (h + 1) * HGRN_D)
                qb = [q[:, cols] for q in qb_all]
                kb = [k[:, cols] for k in kb_all]
                s_parts = []
                for a in range(0, HGRN_LEVELS, 2):
                    q_cat = jnp.concatenate([qb[a], qb[a + 1]], axis=1)
                    k_blk = jnp.concatenate([jnp.concatenate([kb[a], zero], axis=1),
                                             jnp.concatenate([zero, kb[a + 1]], axis=1)], axis=0)
                    s_parts.append(_nt(q_cat, k_blk))
                s_parts.append(_nt(qb[HGRN_LEVELS], jnp.concatenate([kb[HGRN_LEVELS], zero], axis=0)))
                p = None
                for a, part in enumerate(s_parts):
                    part = part.astype(BF16) * mask_ref[:, 2 * a * c:(2 * a + 2) * c]
                    p = part if p is None else p + part
                p_pair.append(p)
                ql_pair.append(ql_all[:, cols])
                v_pair.append(vi[:, cols])
                st_pair.append(state_ref[:, cols])
            vv = [jnp.concatenate([v, v], axis=0) for v in v_pair]
            zv = jnp.zeros_like(vv[0])
            v_blk = jnp.concatenate([jnp.concatenate([vv[0], zv], axis=1),
                                     jnp.concatenate([zv, vv[1]], axis=1)], axis=0)
            o_pair = _nn(jnp.concatenate(p_pair, axis=1), v_blk)
            st_b = [st.astype(BF16) for st in st_pair]
            st_blk = jnp.concatenate([jnp.concatenate([st_b[0], zsq], axis=1),
                                      jnp.concatenate([zsq, st_b[1]], axis=1)], axis=0)
            o_pair = o_pair + _nt(jnp.concatenate(ql_pair, axis=1), st_blk)
            for i, h in enumerate((2 * hp, 2 * hp + 1)):
                cols = slice(h * HGRN_D, (h + 1) * HGRN_D)
                o = o_pair[:, i * HGRN_D:(i + 1) * HGRN_D]
                state_ref[:, cols] = st_pair[i] * e_end[:, cols] + _tn(v_pair[i], kdec_all[:, cols])
                ms = jnp.mean(o * o, axis=-1, keepdims=True)
                o_ref[rows, cols] = (o * lax.rsqrt(ms + NORM_EPS) * gain[:, cols]).astype(BF16)

    n_chunks = HGRN_TILE // c

    def group(g, carry):
        base = pl.multiple_of(g * HGRN_TILE, HGRN_TILE)
        prepared = prepare(base, 0)
        for ci in range(n_chunks):
            nxt = prepare(base, ci + 1) if ci + 1 < n_chunks else None
            recur(base, ci, prepared)
            prepared = nxt
        return carry

    lax.fori_loop(0, qh_ref.shape[0] // HGRN_TILE, group, 0)


def _hgrn(qh, fh, ih, logits, gain, layer, batch, seq):
    w_all, mask_all = _hgrn_consts()
    assert seq % HGRN_TILE == 0
    depth = logits.shape[0]
    const = lambda b: (0, 0)
    return pl.pallas_call(
        functools.partial(_hgrn_kernel, layer=layer),
        grid=(batch,),
        in_specs=[pl.BlockSpec((seq, GROUP), lambda b: (b, 0))] * 3 + [
                  pl.BlockSpec((depth, GROUP), const),
                  pl.BlockSpec((None, 1, GROUP), lambda b: (layer, 0, 0)),
                  pl.BlockSpec(w_all.shape, const),
                  pl.BlockSpec(mask_all.shape, const)],
        out_specs=pl.BlockSpec((seq, GROUP), lambda b: (b, 0)),
        out_shape=jax.ShapeDtypeStruct((fh.shape[0], GROUP), BF16),
        scratch_shapes=[pltpu.VMEM((HGRN_D, GROUP), F32)],
        compiler_params=pltpu.CompilerParams(dimension_semantics=("parallel",),
                                             vmem_limit_bytes=VMEM_LIMIT),
        name="hgrn",
    )(qh, fh, ih, logits, gain, w_all, mask_all)


def _mem_kv_kernel(m_ref, g_ref, w_ref, gk_ref, seg_ref, k_ref, v_ref):
    x = m_ref[...]
    ms = jnp.mean(x * x, axis=-1, keepdims=True)
    hb = (x * lax.rsqrt(ms + NORM_EPS) * g_ref[...]).astype(BF16)
    gk = gk_ref[...]
    seg = seg_ref[...]
    for c in range(GROUP // 256):
        kc = _nn(hb, w_ref[:, c * 256:(c + 1) * 256].astype(BF16))
        ss = _nn((kc * kc).astype(BF16), seg) * (1.0 / MEM_D)
        kn = kc * lax.rsqrt(ss + NORM_EPS) * gk[:, c * 256:(c + 1) * 256]
        k_ref[:, c * 256:(c + 1) * 256] = kn.astype(BF16)
    v_ref[...] = _nn(hb, w_ref[:, GROUP:2 * GROUP].astype(BF16)).astype(BF16)


def _mem_kv(memf, g, w, gk, mem_len, layer):
    n = memf.shape[0]
    seg = np.arange(256)[:, None] // MEM_D == np.arange(256)[None, :] // MEM_D
    seg = jnp.asarray(seg, BF16)
    const = lambda i: (0, 0)
    row = lambda i: (i, 0)
    of_layer = lambda i: (layer, 0, 0)
    return pl.pallas_call(
        _mem_kv_kernel,
        grid=(n // mem_len,),
        in_specs=[pl.BlockSpec((mem_len, D_MODEL), row),
                  pl.BlockSpec((None, 1, D_MODEL), of_layer),
                  pl.BlockSpec((None, D_MODEL, 2 * GROUP), of_layer),
                  pl.BlockSpec((None, 1, GROUP), of_layer),
                  pl.BlockSpec((256, 256), const)],
        out_specs=[pl.BlockSpec((mem_len, GROUP), row)] * 2,
        out_shape=[jax.ShapeDtypeStruct((n, GROUP), BF16)] * 2,
        compiler_params=pltpu.CompilerParams(dimension_semantics=("parallel",), vmem_limit_bytes=VMEM_LIMIT),
        name="mem_kv",
    )(memf, g, w, gk, seg)


def _out_proj_kernel(x_ref, oa_ref, oh_ref, qm_ref, z_ref, km_ref, vm_ref, gq_ref, w_ref, o_ref, y_ref):
    def gate(o, c0, c1):
        y_ref[:, c0:c1] = (o * _silu(z_ref[:, c0:c1].astype(F32))).astype(BF16)

    gate(oa_ref[...].astype(F32), 0, GROUP)
    gate(oh_ref[...].astype(F32), GROUP, 2 * GROUP)
    gq = gq_ref[...]
    for h in range(MEM_HEADS):
        cols = slice(h * MEM_D, (h + 1) * MEM_D)
        q = qm_ref[:, cols].astype(F32)
        ms = jnp.mean(q * q, axis=-1, keepdims=True)
        qn = q * lax.rsqrt(ms + NORM_EPS) * gq * (MEM_D ** -0.5)
        s = _nt(qn.astype(BF16), km_ref[:, cols])
        p = jnp.exp(s - jnp.max(s, axis=1, keepdims=True))
        v_aug = jnp.concatenate([vm_ref[:, cols], jnp.ones((vm_ref.shape[0], MEM_D), BF16)], axis=1)
        pv = _nn(p.astype(BF16), v_aug)
        om = pv[:, 0:MEM_D] / pv[:, MEM_D:2 * MEM_D]
        c0 = 2 * GROUP + h * MEM_D
        gate(om, c0, c0 + MEM_D)
    o_ref[...] = x_ref[...] + _nn(y_ref[...], w_ref[...].astype(BF16))


def _out_proj(xf, oa, oh, qm, z, km, vm, gq, w, seq, mem_len, layer):
    n = xf.shape[0]
    tm = OUT_TILE
    steps = seq // tm
    const = lambda i: (0, 0)
    row = lambda i: (i, 0)
    per_batch = lambda i: (i // steps, 0)
    of_layer = lambda i: (layer, 0, 0)
    return pl.pallas_call(
        _out_proj_kernel,
        grid=(n // tm,),
        in_specs=[pl.BlockSpec((tm, D_MODEL), row),
                  pl.BlockSpec((tm, GROUP), row),
                  pl.BlockSpec((tm, GROUP), row),
                  pl.BlockSpec((tm, GROUP), row),
                  pl.BlockSpec((tm, D_MIX), row),
                  pl.BlockSpec((mem_len, GROUP), per_batch),
                  pl.BlockSpec((mem_len, GROUP), per_batch),
                  pl.BlockSpec((None, 1, MEM_D), of_layer),
                  pl.BlockSpec((None, D_MIX, D_MODEL), of_layer, pipeline_mode=pl.Buffered(1))],
        out_specs=pl.BlockSpec((tm, D_MODEL), row),
        out_shape=jax.ShapeDtypeStruct((n, D_MODEL), F32),
        scratch_shapes=[pltpu.VMEM((tm, D_MIX), BF16)],
        compiler_params=pltpu.CompilerParams(dimension_semantics=("parallel",), vmem_limit_bytes=VMEM_LIMIT),
        name="out_proj",
    )(xf, oa, oh, qm, z, km, vm, gq, w)


def kernel(x, mem, positions, norm_g, w_in, w_out, moba_q_norm, moba_k_norm, hgrn_lb_logits, hgrn_o_norm,
           mem_norm_g, w_mem_kv, mem_q_norm, mem_k_norm):
    batch, seq, d_model = x.shape
    mem_len = mem.shape[1]
    depth = w_in.shape[0]
    assert d_model == D_MODEL and seq % ROW_TILE == 0 and seq % OUT_TILE == 0 and mem_len % 8 == 0
    n = batch * seq
    xf = x.reshape(n, d_model)
    memf = mem.reshape(batch * mem_len, d_model)
    cosf, sinf = _rope_tables(positions.reshape(1, n))
    per_head = lambda p, reps: jnp.tile(p, (1, reps))[:, None, :]
    norm_g, mem_norm_g, mem_q_norm = norm_g[:, None, :], mem_norm_g[:, None, :], mem_q_norm[:, None, :]
    gq, gk = per_head(moba_q_norm, GROUP // MOBA_D), per_head(moba_k_norm, GROUP // MOBA_D)
    g_hgrn, g_memk = per_head(hgrn_o_norm, HGRN_HEADS), per_head(mem_k_norm, MEM_HEADS)
    for l in range(depth):
        qa, ka, va, qh, fh, ih, qm, z = _in_proj(xf, norm_g, w_in, cosf, sinf, gq, gk, l)
        oa = _moba(qa, ka, va, batch, seq)
        oh = _hgrn(qh, fh, ih, hgrn_lb_logits, g_hgrn, l, batch, seq)
        km, vm = _mem_kv(memf, mem_norm_g, w_mem_kv, g_memk, mem_len, l)
        xf = _out_proj(xf, oa, oh, qm, z, km, vm, mem_q_norm, w_out, seq, mem_len, l)
    return xf.reshape(batch, seq, d_model)
```

```python
import functools

import numpy as np
import jax
import jax.numpy as jnp
from jax import lax
from jax.experimental import pallas as pl
from jax.experimental.pallas import tpu as pltpu

F32 = jnp.float32
BF16 = jnp.bfloat16

D_MODEL = 1024
GROUP = D_MODEL // 2
D_MIX = 3 * GROUP
IN_COLS = 7 * GROUP + D_MIX
MOBA_D = 64
MOBA_BLOCK = 256
MOBA_TOPK = 3
MOBA_VT_ROWS = MOBA_D + 16
HGRN_D = 128
HGRN_HEADS = GROUP // HGRN_D
HGRN_CHUNK = 64
HGRN_LEVELS = 6
HGRN_MM_LEVELS = 3
LOG2E = 1.4426950408889634
MEM_D = 128
MEM_HEADS = GROUP // MEM_D
ROPE_THETA = 500000.0
ROPE_DIM = MOBA_D // 4
ROPE_HALF = ROPE_DIM // 2
NORM_EPS = 1e-6
LANES = 128
ROW_TILE = 512
OUT_TILE = 1024
HGRN_TILE = 1024
MASKED = -1e30
VMEM_LIMIT = 48 * 1024 * 1024


def _nt(a, b):
    return lax.dot_general(a, b, (((1,), (1,)), ((), ())), preferred_element_type=F32)


def _tn(a, b):
    return lax.dot_general(a, b, (((0,), (0,)), ((), ())), preferred_element_type=F32)


def _nn(a, b):
    return jnp.dot(a, b, preferred_element_type=F32)


def _split2(x):
    hi = x.astype(BF16)
    lo = (x - hi.astype(F32)).astype(BF16)
    return hi, lo


def _split3(x):
    hi = x.astype(BF16)
    r = x - hi.astype(F32)
    mid = r.astype(BF16)
    lo = (r - mid.astype(F32)).astype(BF16)
    return hi, mid, lo


def _sigmoid(x):
    return 1.0 / (1.0 + jnp.exp(-x))


def _silu(x):
    return x * _sigmoid(x)


def _rope_table_kernel(pos_ref, invf_ref, expand_ref, base_ref, cos_ref, sin_ref):
    ang = invf_ref[...] * pos_ref[...].astype(F32)
    tm = ang.shape[1]
    table = jnp.concatenate([jnp.cos(ang), jnp.sin(ang), jnp.zeros((LANES - ROPE_DIM, tm), F32)], axis=0)
    hi, mid, lo = _split3(table.T)
    e = expand_ref[...]
    both = _nn(hi, e) + _nn(mid, e) + _nn(lo, e)
    cos_ref[...] = both[:, 0:LANES] + base_ref[...]
    sin_ref[...] = both[:, LANES:2 * LANES]


def _rope_tables(pos):
    n = pos.shape[1]
    tm = min(n, 2048)
    lane = np.arange(LANES) % MOBA_D
    freq = lane % ROPE_HALF
    rope = lane < ROPE_DIM
    sgn = np.where(lane < ROPE_HALF, -1.0, 1.0)
    expand = np.zeros((LANES, 2 * LANES), np.float32)
    for l in np.nonzero(rope)[0]:
        expand[freq[l], l] = 1.0
        expand[ROPE_HALF + freq[l], LANES + l] = sgn[l]
    base = np.where(rope, 0.0, 1.0).astype(np.float32)[None, :]
    invf = (ROPE_THETA ** (-np.arange(ROPE_HALF, dtype=np.float64) / ROPE_HALF)).astype(np.float32)[:, None]
    const = lambda i: (0, 0)
    return pl.pallas_call(
        _rope_table_kernel,
        grid=(n // tm,),
        in_specs=[pl.BlockSpec((1, tm), lambda i: (0, i)),
                  pl.BlockSpec((ROPE_HALF, 1), const),
                  pl.BlockSpec((LANES, 2 * LANES), const),
                  pl.BlockSpec((1, LANES), const)],
        out_specs=[pl.BlockSpec((tm, LANES), lambda i: (i, 0))] * 2,
        out_shape=[jax.ShapeDtypeStruct((n, LANES), F32)] * 2,
        name="rope_tables",
    )(pos, jnp.asarray(invf), jnp.asarray(expand, BF16), jnp.asarray(base))


def _in_proj_kernel(x_ref, g_ref, w_ref, cos_ref, sin_ref, gq_ref, gk_ref,
                    qa_ref, ka_ref, va_ref, qh_ref, fh_ref, ih_ref, qm_ref, z_ref):
    x = x_ref[...]
    ms = jnp.mean(x * x, axis=-1, keepdims=True)
    hb = (x * lax.rsqrt(ms + NORM_EPS) * g_ref[...]).astype(BF16)

    def proj(c0, width):
        return _nn(hb, w_ref[:, c0:c0 + width].astype(BF16))

    cosf = cos_ref[...]
    sinf = sin_ref[...]
    lane = lax.broadcasted_iota(jnp.int32, (1, LANES), 1) % MOBA_D
    lo_lane = lane < ROPE_HALF
    first = lax.broadcasted_iota(jnp.int32, (1, LANES), 1) < MOBA_D

    def head_norm_rope(p, gain, out_ref):
        for v in range(GROUP // LANES):
            xv = p[:, v * LANES:(v + 1) * LANES]
            sq = xv * xv
            s_first = jnp.sum(jnp.where(first, sq, 0.0), axis=1, keepdims=True)
            s_both = jnp.sum(sq, axis=1, keepdims=True)
            ss = jnp.where(first, s_first, s_both - s_first) * (1.0 / MOBA_D)
            xv = xv * lax.rsqrt(ss + NORM_EPS) * gain[:, v * LANES:(v + 1) * LANES]
            rot = jnp.where(lo_lane, pltpu.roll(xv, LANES - ROPE_HALF, 1), pltpu.roll(xv, ROPE_HALF, 1))
            out_ref[:, v * LANES:(v + 1) * LANES] = (xv * cosf + rot * sinf).astype(out_ref.dtype)

    head_norm_rope(proj(0, GROUP), gq_ref[...], qa_ref)
    head_norm_rope(proj(GROUP, GROUP), gk_ref[...], ka_ref)
    va_ref[...] = proj(2 * GROUP, GROUP).astype(BF16)
    qh_ref[...] = proj(3 * GROUP, GROUP).astype(BF16)
    fh_ref[...] = proj(4 * GROUP, GROUP)
    ih_ref[...] = proj(5 * GROUP, GROUP).astype(BF16)
    qm_ref[...] = proj(6 * GROUP, GROUP).astype(BF16)
    for c in range(3):
        z_ref[:, c * GROUP:(c + 1) * GROUP] = proj((7 + c) * GROUP, GROUP).astype(BF16)


def _in_proj(xf, g, w, cosf, sinf, gq, gk, layer):
    n = xf.shape[0]
    tm = ROW_TILE
    row = lambda i: (i, 0)
    of_layer = lambda i: (layer, 0, 0)
    outs = ((GROUP, F32), (GROUP, BF16), (GROUP, BF16), (GROUP, BF16), (GROUP, F32), (GROUP, BF16),
            (GROUP, BF16), (D_MIX, BF16))
    return pl.pallas_call(
        _in_proj_kernel,
        grid=(n // tm,),
        in_specs=[pl.BlockSpec((tm, D_MODEL), row),
                  pl.BlockSpec((None, 1, D_MODEL), of_layer),
                  pl.BlockSpec((None, D_MODEL, IN_COLS), of_layer, pipeline_mode=pl.Buffered(1)),
                  pl.BlockSpec((tm, LANES), row),
                  pl.BlockSpec((tm, LANES), row),
                  pl.BlockSpec((None, 1, GROUP), of_layer),
                  pl.BlockSpec((None, 1, GROUP), of_layer)],
        out_specs=[pl.BlockSpec((tm, w), row) for w, _ in outs],
        out_shape=[jax.ShapeDtypeStruct((n, w), dt) for w, dt in outs],
        compiler_params=pltpu.CompilerParams(dimension_semantics=("parallel",), vmem_limit_bytes=VMEM_LIMIT),
        name="in_proj",
    )(xf, g, w, cosf, sinf, gq, gk)


def _moba_kernel(q_ref, k_ref, v_ref, o_ref, kmean_ref, kaug_ref, vt_ref, qaug_ref, s_ref, *, nb):
    blk = MOBA_BLOCK
    seq = nb * blk
    lane = lax.broadcasted_iota(jnp.int32, (1, LANES), 1)
    heads = [(lane >= h * MOBA_D) & (lane < (h + 1) * MOBA_D) for h in range(2)]
    spare = [(1 - h) * MOBA_D for h in range(2)]

    ones = jnp.ones((MOBA_VT_ROWS - MOBA_D, blk), F32)
    for j in range(nb):
        rows = slice(j * blk, (j + 1) * blk)
        kj = k_ref[rows, :]
        kmean_ref[j:j + 1, :] = jnp.mean(kj.astype(F32), axis=0, keepdims=True)
        v_t = v_ref[rows, :].astype(F32).T
        vth = [jnp.concatenate([v_t[h * MOBA_D:(h + 1) * MOBA_D], ones], axis=0).astype(BF16) for h in range(2)]
        zvt = jnp.zeros_like(vth[0])
        vt_ref[j] = jnp.concatenate([jnp.concatenate([vth[0], zvt], axis=1),
                                     jnp.concatenate([zvt, vth[1]], axis=1)], axis=0)
        for h in range(2):
            onehot = jnp.where(lane == spare[h] + j, 1.0, 0.0).astype(BF16)
            kaug_ref[h, rows, :] = jnp.where(heads[h], kj, onehot)

    km = kmean_ref[...]
    q_hi, q_lo = _split2(q_ref[...])
    km_parts = [_split2(jnp.where(heads[h], km, 0.0)) for h in range(2)]
    g_hi = _nt(jnp.concatenate([km_parts[0][0], km_parts[0][1], km_parts[1][0], km_parts[1][1]], axis=0), q_hi)
    g_lo = _nt(jnp.concatenate([km_parts[0][0], km_parts[1][0]], axis=0), q_lo)
    blk_row = lax.broadcasted_iota(jnp.int32, (nb, 1), 0)
    q_blk = lax.broadcasted_iota(jnp.int32, (1, seq), 1) // blk
    past = blk_row < q_blk
    for h in range(2):
        gate_t = g_hi[2 * h * nb:(2 * h + 1) * nb] + g_hi[(2 * h + 1) * nb:(2 * h + 2) * nb] + g_lo[h * nb:(h + 1) * nb]
        gate_t = jnp.where(past, gate_t, -jnp.inf)
        cnt = jnp.zeros((nb, seq), F32)
        for n in range(nb - 1):
            gn = gate_t[n:n + 1, :]
            before = (gn > gate_t) | ((gn == gate_t) & (n < blk_row))
            cnt = cnt + jnp.where(before, 1.0, 0.0)
        sel = (past & (cnt < float(MOBA_TOPK))) | (blk_row == q_blk)
        bias_t = jnp.where(sel, 0.0, MASKED)
        for c in range(nb):
            parts = []
            if spare[h]:
                parts.append(jnp.zeros((spare[h], blk), F32))
            parts.append(bias_t[:, c * blk:(c + 1) * blk])
            parts.append(jnp.zeros((LANES - spare[h] - nb, blk), F32))
            bias = jnp.concatenate(parts, axis=0).T
            qs = q_ref[c * blk:(c + 1) * blk, :] * (MOBA_D ** -0.5 * LOG2E)
            qaug_ref[c, h] = jnp.where(heads[h], qs, bias).astype(BF16)

    kpos = lax.broadcasted_iota(jnp.int32, (blk, blk), 0)
    qpos = lax.broadcasted_iota(jnp.int32, (blk, blk), 1)
    causal_t = kpos <= qpos

    for t in range(nb // 2):
        buf = t % 2
        base = 0
        for c in (nb - 1 - t, t):
            ms = []
            for h in range(2):
                qa = qaug_ref[c, h]
                st = s_ref.at[buf, h]
                mx = None
                for j in range(c + 1):
                    s = _nt(kaug_ref[h, j * blk:(j + 1) * blk, :], qa)
                    if j == c:
                        s = jnp.where(causal_t, s, MASKED)
                    st[base + j * blk:base + (j + 1) * blk, :] = s
                    mx = s if mx is None else jnp.maximum(mx, s)
                ms.append(jnp.max(mx, axis=0, keepdims=True))
            acc = None
            for j in range(c + 1):
                tile = slice(base + j * blk, base + (j + 1) * blk)
                p = jnp.concatenate([jnp.exp2(s_ref[buf, h, tile, :] - ms[h]).astype(BF16) for h in range(2)], axis=0)
                pv = _nn(vt_ref[j], p)
                acc = pv if acc is None else acc + pv
            outs = [acc[h * MOBA_VT_ROWS:h * MOBA_VT_ROWS + MOBA_D]
                    / acc[h * MOBA_VT_ROWS + MOBA_D:h * MOBA_VT_ROWS + MOBA_D + 1] for h in range(2)]
            o_ref[c * blk:(c + 1) * blk, :] = jnp.concatenate(outs, axis=0).T.astype(BF16)
            base += (c + 1) * blk


def _moba(qa, ka, va, batch, seq):
    nb = seq // MOBA_BLOCK
    assert nb % 2 == 0 and nb <= MOBA_D
    pairs = GROUP // LANES
    whole = pl.BlockSpec((seq, LANES), lambda b, p: (b, p))
    return pl.pallas_call(
        functools.partial(_moba_kernel, nb=nb),
        grid=(batch, pairs),
        in_specs=[whole, whole, whole],
        out_specs=whole,
        out_shape=jax.ShapeDtypeStruct(qa.shape, BF16),
        scratch_shapes=[pltpu.VMEM((nb, LANES), F32),
                        pltpu.VMEM((2, seq, LANES), BF16),
                        pltpu.VMEM((nb, 2 * MOBA_VT_ROWS, 2 * MOBA_BLOCK), BF16),
                        pltpu.VMEM((nb, 2, MOBA_BLOCK, LANES), BF16),
                        pltpu.VMEM((2, 2, (nb + 1) * MOBA_BLOCK, MOBA_BLOCK), F32)],
        compiler_params=pltpu.CompilerParams(dimension_semantics=("parallel", "parallel"),
                                             vmem_limit_bytes=VMEM_LIMIT),
        name="moba",
    )(qa, ka, va)


def _hgrn_consts():
    c = HGRN_CHUNK
    t = np.arange(c)
    w_rows, masks = [], []
    for lvl in range(HGRN_LEVELS):
        m = 1 << lvl
        blk = t // m
        odd = (blk % 2) == 1
        start = blk * m
        end = start + m - 1
        u = t[None, :]
        w_odd = (u >= start[:, None]) & (u <= t[:, None])
        w_even = (u > t[:, None]) & (u <= end[:, None])
        w_rows.append(np.where(odd[:, None], w_odd, w_even))
        masks.append(odd[:, None] & (blk[None, :] == blk[:, None] - 1))
    masks.append(t[None, :] == t[:, None])
    w_mm = w_rows[:HGRN_MM_LEVELS] + [t[None, :] <= t[:, None]]
    w_all = np.concatenate(w_mm, axis=0).astype(np.float32)
    w_all = np.concatenate([w_all, w_all], axis=1)
    mask_cat = np.concatenate([masks[-1]] + masks[:-1] + [np.zeros((c, c), bool)], axis=1).astype(np.float32)
    return jnp.asarray(w_all, BF16), jnp.asarray(mask_cat, BF16)


def _hgrn_kernel(qh_ref, fh_ref, ih_ref, logit_ref, gain_ref, w_ref, mask_ref, o_ref, state_ref, *, layer):
    c = HGRN_CHUNK
    state_ref[...] = jnp.zeros_like(state_ref)

    if layer > 0:
        lg = logit_ref[...]
        e = jnp.exp(lg - jnp.max(lg, axis=0, keepdims=True))
        sm = e / jnp.sum(e, axis=0, keepdims=True)
        lb = jnp.sum(sm[1:layer + 1, :], axis=0, keepdims=True)
        log_lb = jnp.log(lb) * LOG2E
        log_1m_lb = jnp.log1p(-lb) * LOG2E
    w_all = w_ref[...]
    gain = gain_ref[...]

    def prepare(base, ci):
        rows = pl.ds(base + ci * c, c)
        fl = fh_ref[rows, :]
        vi = ih_ref[rows, :]
        qh = qh_ref[rows, :].astype(F32)
        qf = qh / (1.0 + jnp.exp2(qh * -LOG2E))
        e = jnp.exp2(jnp.abs(fl) * -LOG2E)
        r = 1.0 / (1.0 + e)
        log_sig = jnp.minimum(fl, 0.0) * LOG2E - jnp.log2(1.0 + e)
        sig_neg = jnp.where(fl >= 0.0, e * r, r)
        if layer > 0:
            b = log_1m_lb + log_sig
            lf = jnp.maximum(log_lb, b) + jnp.log2(1.0 + jnp.exp2(-jnp.abs(log_lb - b)))
            kf = (1.0 - lb) * sig_neg
        else:
            lf = log_sig
            kf = sig_neg

        f1, f2 = _split2(lf)
        x_mm = _nn(w_all, jnp.concatenate([f1, f2], axis=0))
        a2 = x_mm[HGRN_MM_LEVELS * c:(HGRN_MM_LEVELS + 1) * c, :]
        e_lvl = [jnp.exp2(x_mm[lvl * c:(lvl + 1) * c, :]) for lvl in range(HGRN_MM_LEVELS)]
        for lvl in range(HGRN_MM_LEVELS, HGRN_LEVELS):
            m = 1 << lvl
            parts = []
            for g in range(c // (2 * m)):
                ref = a2[2 * m * g + m - 1:2 * m * g + m, :]
                parts.append(-jnp.abs(a2[2 * m * g:2 * m * (g + 1), :] - ref))
            e_lvl.append(jnp.exp2(jnp.concatenate(parts, axis=0)))
        e_cum = jnp.exp2(a2)
        e_rest = jnp.exp2(a2[c - 1:c, :] - a2)
        e_end = e_cum[c - 1:c, :]
        qf_b, kf_b = qf.astype(BF16), kf.astype(BF16)
        e_b = [e.astype(BF16) for e in e_lvl]
        qb = [qf_b] + [qf_b * e for e in e_b]
        kb = [kf_b] + [kf_b * e for e in e_b]
        return qb, kb, qf_b * e_cum.astype(BF16), kf_b * e_rest.astype(BF16), vi, e_end

    def recur(base, ci, prepared):
        rows = pl.ds(base + ci * c, c)
        qb_all, kb_all, ql_all, kdec_all, vi, e_end = prepared
        zero = jnp.zeros((c, HGRN_D), BF16)
        zsq = jnp.zeros((HGRN_D, HGRN_D), BF16)
        for hp in range(HGRN_HEADS // 2):
            p_pair, ql_pair, v_pair, st_pair = [], [], [], []
            for h in (2 * hp, 2 * hp + 1):
                cols = slice(h * HGRN_D, (h + 1) * HGRN_D)
                qb = [q[:, cols] for q in qb_all]
                kb = [k[:, cols] for k in kb_all]
                s_parts = []
                for a in range(0, HGRN_LEVELS, 2):
                    q_cat = jnp.concatenate([qb[a], qb[a + 1]], axis=1)
                    k_blk = jnp.concatenate([jnp.concatenate([kb[a], zero], axis=1),
                                             jnp.concatenate([zero, kb[a + 1]], axis=1)], axis=0)
                    s_parts.append(_nt(q_cat, k_blk))
                s_parts.append(_nt(qb[HGRN_LEVELS], jnp.concatenate([kb[HGRN_LEVELS], zero], axis=0)))
                p = None
                for a, part in enumerate(s_parts):
                    part = part.astype(BF16) * mask_ref[:, 2 * a * c:(2 * a + 2) * c]
                    p = part if p is None else p + part
                p_pair.append(p)
                ql_pair.append(ql_all[:, cols])
                v_pair.append(vi[:, cols])
                st_pair.append(state_ref[:, cols])
            vv = [jnp.concatenate([v, v], axis=0) for v in v_pair]
            zv = jnp.zeros_like(vv[0])
            v_blk = jnp.concatenate([jnp.concatenate([vv[0], zv], axis=1),
                                     jnp.concatenate([zv, vv[1]], axis=1)], axis=0)
            o_pair = _nn(jnp.concatenate(p_pair, axis=1), v_blk)
            st_b = [st.astype(BF16) for st in st_pair]
            st_blk = jnp.concatenate([jnp.concatenate([st_b[0], zsq], axis=1),
                                      jnp.concatenate([zsq, st_b[1]], axis=1)], axis=0)
            o_pair = o_pair + _nt(jnp.concatenate(ql_pair, axis=1), st_blk)
            for i, h in enumerate((2 * hp, 2 * hp + 1)):
                cols = slice(h * HGRN_D, (h + 1) * HGRN_D)
                o = o_pair[:, i * HGRN_D:(i + 1) * HGRN_D]
                state_ref[:, cols] = st_pair[i] * e_end[:, cols] + _tn(v_pair[i], kdec_all[:, cols])
                ms = jnp.mean(o * o, axis=-1, keepdims=True)
                o_ref[rows, cols] = (o * lax.rsqrt(ms + NORM_EPS) * gain[:, cols]).astype(BF16)

    n_chunks = HGRN_TILE // c

    def group(g, carry):
        base = pl.multiple_of(g * HGRN_TILE, HGRN_TILE)
        prepared = prepare(base, 0)
        for ci in range(n_chunks):
            nxt = prepare(base, ci + 1) if ci + 1 < n_chunks else None
            recur(base, ci, prepared)
            prepared = nxt
        return carry

    lax.fori_loop(0, qh_ref.shape[0] // HGRN_TILE, group, 0)


def _hgrn(qh, fh, ih, logits, gain, layer, batch, seq):
    w_all, mask_all = _hgrn_consts()
    assert seq % HGRN_TILE == 0
    depth = logits.shape[0]
    const = lambda b: (0, 0)
    return pl.pallas_call(
        functools.partial(_hgrn_kernel, layer=layer),
        grid=(batch,),
        in_specs=[pl.BlockSpec((seq, GROUP), lambda b: (b, 0))] * 3 + [
                  pl.BlockSpec((depth, GROUP), const),
                  pl.BlockSpec((None, 1, GROUP), lambda b: (layer, 0, 0)),
                  pl.BlockSpec(w_all.shape, const),
                  pl.BlockSpec(mask_all.shape, const)],
        out_specs=pl.BlockSpec((seq, GROUP), lambda b: (b, 0)),
        out_shape=jax.ShapeDtypeStruct((fh.shape[0], GROUP), BF16),
        scratch_shapes=[pltpu.VMEM((HGRN_D, GROUP), F32)],
        compiler_params=pltpu.CompilerParams(dimension_semantics=("parallel",),
                                             vmem_limit_bytes=VMEM_LIMIT),
        name="hgrn",
    )(qh, fh, ih, logits, gain, w_all, mask_all)


def _mem_kv_kernel(m_ref, g_ref, w_ref, gk_ref, seg_ref, k_ref, v_ref):
    x = m_ref[...]
    ms = jnp.mean(x * x, axis=-1, keepdims=True)
    hb = (x * lax.rsqrt(ms + NORM_EPS) * g_ref[...]).astype(BF16)
    gk = gk_ref[...]
    seg = seg_ref[...]
    for c in range(GROUP // 256):
        kc = _nn(hb, w_ref[:, c * 256:(c + 1) * 256].astype(BF16))
        ss = _nn((kc * kc).astype(BF16), seg) * (1.0 / MEM_D)
        kn = kc * lax.rsqrt(ss + NORM_EPS) * gk[:, c * 256:(c + 1) * 256]
        k_ref[:, c * 256:(c + 1) * 256] = kn.astype(BF16)
    v_ref[...] = _nn(hb, w_ref[:, GROUP:2 * GROUP].astype(BF16)).astype(BF16)


def _mem_kv(memf, g, w, gk, mem_len, layer):
    n = memf.shape[0]
    seg = np.arange(256)[:, None] // MEM_D == np.arange(256)[None, :] // MEM_D
    seg = jnp.asarray(seg, BF16)
    const = lambda i: (0, 0)
    row = lambda i: (i, 0)
    of_layer = lambda i: (layer, 0, 0)
    return pl.pallas_call(
        _mem_kv_kernel,
        grid=(n // mem_len,),
        in_specs=[pl.BlockSpec((mem_len, D_MODEL), row),
                  pl.BlockSpec((None, 1, D_MODEL), of_layer),
                  pl.BlockSpec((None, D_MODEL, 2 * GROUP), of_layer),
                  pl.BlockSpec((None, 1, GROUP), of_layer),
                  pl.BlockSpec((256, 256), const)],
        out_specs=[pl.BlockSpec((mem_len, GROUP), row)] * 2,
        out_shape=[jax.ShapeDtypeStruct((n, GROUP), BF16)] * 2,
        compiler_params=pltpu.CompilerParams(dimension_semantics=("parallel",), vmem_limit_bytes=VMEM_LIMIT),
        name="mem_kv",
    )(memf, g, w, gk, seg)


def _out_proj_kernel(x_ref, oa_ref, oh_ref, qm_ref, z_ref, km_ref, vm_ref, gq_ref, w_ref, o_ref, y_ref):
    def gate(o, c0, c1):
        y_ref[:, c0:c1] = (o * _silu(z_ref[:, c0:c1].astype(F32))).astype(BF16)

    gate(oa_ref[...].astype(F32), 0, GROUP)
    gate(oh_ref[...].astype(F32), GROUP, 2 * GROUP)
    gq = gq_ref[...]
    mem_len = km_ref.shape[0]
    zk = jnp.zeros((mem_len, MEM_D), BF16)
    ones = jnp.ones((mem_len, MEM_D), BF16)
    for hp in range(MEM_HEADS // 2):
        qns = []
        for h in (2 * hp, 2 * hp + 1):
            q = qm_ref[:, h * MEM_D:(h + 1) * MEM_D].astype(F32)
            ms = jnp.mean(q * q, axis=-1, keepdims=True)
            qns.append((q * lax.rsqrt(ms + NORM_EPS) * gq * (MEM_D ** -0.5)).astype(BF16))
        c0, c1 = 2 * hp * MEM_D, (2 * hp + 1) * MEM_D
        k_blk = jnp.concatenate([jnp.concatenate([km_ref[:, c0:c0 + MEM_D], zk], axis=1),
                                 jnp.concatenate([zk, km_ref[:, c1:c1 + MEM_D]], axis=1)], axis=0)
        s2 = _nt(jnp.concatenate(qns, axis=1), k_blk)
        for i, h in enumerate((2 * hp, 2 * hp + 1)):
            s = s2[:, i * mem_len:(i + 1) * mem_len]
            p = jnp.exp(s - jnp.max(s, axis=1, keepdims=True))
            v_aug = jnp.concatenate([vm_ref[:, h * MEM_D:(h + 1) * MEM_D], ones], axis=1)
            pv = _nn(p.astype(BF16), v_aug)
            om = pv[:, 0:MEM_D] / pv[:, MEM_D:2 * MEM_D]
            col = 2 * GROUP + h * MEM_D
            gate(om, col, col + MEM_D)
    o_ref[...] = x_ref[...] + _nn(y_ref[...], w_ref[...].astype(BF16))


def _out_proj(xf, oa, oh, qm, z, km, vm, gq, w, seq, mem_len, layer):
    n = xf.shape[0]
    tm = OUT_TILE
    steps = seq // tm
    const = lambda i: (0, 0)
    row = lambda i: (i, 0)
    per_batch = lambda i: (i // steps, 0)
    of_layer = lambda i: (layer, 0, 0)
    return pl.pallas_call(
        _out_proj_kernel,
        grid=(n // tm,),
        in_specs=[pl.BlockSpec((tm, D_MODEL), row),
                  pl.BlockSpec((tm, GROUP), row),
                  pl.BlockSpec((tm, GROUP), row),
                  pl.BlockSpec((tm, GROUP), row),
                  pl.BlockSpec((tm, D_MIX), row),
                  pl.BlockSpec((mem_len, GROUP), per_batch),
                  pl.BlockSpec((mem_len, GROUP), per_batch),
                  pl.BlockSpec((None, 1, MEM_D), of_layer),
                  pl.BlockSpec((None, D_MIX, D_MODEL), of_layer, pipeline_mode=pl.Buffered(1))],
        out_specs=pl.BlockSpec((tm, D_MODEL), row),
        out_shape=jax.ShapeDtypeStruct((n, D_MODEL), F32),
        scratch_shapes=[pltpu.VMEM((tm, D_MIX), BF16)],
        compiler_params=pltpu.CompilerParams(dimension_semantics=("parallel",), vmem_limit_bytes=VMEM_LIMIT),
        name="out_proj",
    )(xf, oa, oh, qm, z, km, vm, gq, w)


def kernel(x, mem, positions, norm_g, w_in, w_out, moba_q_norm, moba_k_norm, hgrn_lb_logits, hgrn_o_norm,
           mem_norm_g, w_mem_kv, mem_q_norm, mem_k_norm):
    batch, seq, d_model = x.shape
    mem_len = mem.shape[1]
    depth = w_in.shape[0]
    assert d_model == D_MODEL and seq % ROW_TILE == 0 and seq % OUT_TILE == 0 and mem_len % 8 == 0
    n = batch * seq
    xf = x.reshape(n, d_model)
    memf = mem.reshape(batch * mem_len, d_model)
    cosf, sinf = _rope_tables(positions.reshape(1, n))
    per_head = lambda p, reps: jnp.tile(p, (1, reps))[:, None, :]
    norm_g, mem_norm_g, mem_q_norm = norm_g[:, None, :], mem_norm_g[:, None, :], mem_q_norm[:, None, :]
    gq, gk = per_head(moba_q_norm, GROUP // MOBA_D), per_head(moba_k_norm, GROUP // MOBA_D)
    g_hgrn, g_memk = per_head(hgrn_o_norm, HGRN_HEADS), per_head(mem_k_norm, MEM_HEADS)
    for l in range(depth):
        qa, ka, va, qh, fh, ih, qm, z = _in_proj(xf, norm_g, w_in, cosf, sinf, gq, gk, l)
        oa = _moba(qa, ka, va, batch, seq)
        oh = _hgrn(qh, fh, ih, hgrn_lb_logits, g_hgrn, l, batch, seq)
        km, vm = _mem_kv(memf, mem_norm_g, w_mem_kv, g_memk, mem_len, l)
        xf = _out_proj(xf, oa, oh, qm, z, km, vm, mem_q_norm, w_out, seq, mem_len, l)
    return xf.reshape(batch, seq, d_model)
```

```python
import functools

import numpy as np
import jax
import jax.numpy as jnp
from jax import lax
from jax.experimental import pallas as pl
from jax.experimental.pallas import tpu as pltpu

F32 = jnp.float32
BF16 = jnp.bfloat16

D_MODEL = 1024
GROUP = D_MODEL // 2
D_MIX = 3 * GROUP
IN_COLS = 7 * GROUP + D_MIX
MOBA_D = 64
MOBA_BLOCK = 256
MOBA_TOPK = 3
MOBA_VT_ROWS = MOBA_D + 16
HGRN_D = 128
HGRN_HEADS = GROUP // HGRN_D
HGRN_CHUNK = 64
HGRN_LEVELS = 6
HGRN_MM_LEVELS = 3
LOG2E = 1.4426950408889634
MEM_D = 128
MEM_HEADS = GROUP // MEM_D
ROPE_THETA = 500000.0
ROPE_DIM = MOBA_D // 4
ROPE_HALF = ROPE_DIM // 2
NORM_EPS = 1e-6
LANES = 128
ROW_TILE = 512
OUT_TILE = 1024
HGRN_TILE = 1024
MASKED = -1e30
VMEM_LIMIT = 48 * 1024 * 1024


def _nt(a, b):
    return lax.dot_general(a, b, (((1,), (1,)), ((), ())), preferred_element_type=F32)


def _tn(a, b):
    return lax.dot_general(a, b, (((0,), (0,)), ((), ())), preferred_element_type=F32)


def _nn(a, b):
    return jnp.dot(a, b, preferred_element_type=F32)


def _split2(x):
    hi = x.astype(BF16)
    lo = (x - hi.astype(F32)).astype(BF16)
    return hi, lo


def _split3(x):
    hi = x.astype(BF16)
    r = x - hi.astype(F32)
    mid = r.astype(BF16)
    lo = (r - mid.astype(F32)).astype(BF16)
    return hi, mid, lo


def _sigmoid(x):
    return 1.0 / (1.0 + jnp.exp(-x))


def _silu(x):
    return x * _sigmoid(x)


def _rope_table_kernel(pos_ref, invf_ref, expand_ref, base_ref, cos_ref, sin_ref):
    ang = invf_ref[...] * pos_ref[...].astype(F32)
    tm = ang.shape[1]
    table = jnp.concatenate([jnp.cos(ang), jnp.sin(ang), jnp.zeros((LANES - ROPE_DIM, tm), F32)], axis=0)
    hi, mid, lo = _split3(table.T)
    e = expand_ref[...]
    both = _nn(hi, e) + _nn(mid, e) + _nn(lo, e)
    cos_ref[...] = both[:, 0:LANES] + base_ref[...]
    sin_ref[...] = both[:, LANES:2 * LANES]


def _rope_tables(pos):
    n = pos.shape[1]
    tm = min(n, 2048)
    lane = np.arange(LANES) % MOBA_D
    freq = lane % ROPE_HALF
    rope = lane < ROPE_DIM
    sgn = np.where(lane < ROPE_HALF, -1.0, 1.0)
    expand = np.zeros((LANES, 2 * LANES), np.float32)
    for l in np.nonzero(rope)[0]:
        expand[freq[l], l] = 1.0
        expand[ROPE_HALF + freq[l], LANES + l] = sgn[l]
    base = np.where(rope, 0.0, 1.0).astype(np.float32)[None, :]
    invf = (ROPE_THETA ** (-np.arange(ROPE_HALF, dtype=np.float64) / ROPE_HALF)).astype(np.float32)[:, None]
    const = lambda i: (0, 0)
    return pl.pallas_call(
        _rope_table_kernel,
        grid=(n // tm,),
        in_specs=[pl.BlockSpec((1, tm), lambda i: (0, i)),
                  pl.BlockSpec((ROPE_HALF, 1), const),
                  pl.BlockSpec((LANES, 2 * LANES), const),
                  pl.BlockSpec((1, LANES), const)],
        out_specs=[pl.BlockSpec((tm, LANES), lambda i: (i, 0))] * 2,
        out_shape=[jax.ShapeDtypeStruct((n, LANES), F32)] * 2,
        name="rope_tables",
    )(pos, jnp.asarray(invf), jnp.asarray(expand, BF16), jnp.asarray(base))


def _in_proj_kernel(x_ref, g_ref, w_ref, cos_ref, sin_ref, gq_ref, gk_ref,
                    qa_ref, ka_ref, va_ref, qh_ref, fh_ref, ih_ref, qm_ref, z_ref):
    x = x_ref[...]
    ms = jnp.mean(x * x, axis=-1, keepdims=True)
    hb = (x * lax.rsqrt(ms + NORM_EPS) * g_ref[...]).astype(BF16)

    def proj(c0, width):
        return _nn(hb, w_ref[:, c0:c0 + width].astype(BF16))

    cosf = cos_ref[...]
    sinf = sin_ref[...]
    lane = lax.broadcasted_iota(jnp.int32, (1, LANES), 1) % MOBA_D
    lo_lane = lane < ROPE_HALF
    first = lax.broadcasted_iota(jnp.int32, (1, LANES), 1) < MOBA_D

    def head_norm_rope(p, gain, out_ref):
        for v in range(GROUP // LANES):
            xv = p[:, v * LANES:(v + 1) * LANES]
            sq = xv * xv
            s_first = jnp.sum(jnp.where(first, sq, 0.0), axis=1, keepdims=True)
            s_both = jnp.sum(sq, axis=1, keepdims=True)
            ss = jnp.where(first, s_first, s_both - s_first) * (1.0 / MOBA_D)
            xv = xv * lax.rsqrt(ss + NORM_EPS) * gain[:, v * LANES:(v + 1) * LANES]
            rot = jnp.where(lo_lane, pltpu.roll(xv, LANES - ROPE_HALF, 1), pltpu.roll(xv, ROPE_HALF, 1))
            out_ref[:, v * LANES:(v + 1) * LANES] = (xv * cosf + rot * sinf).astype(out_ref.dtype)

    head_norm_rope(proj(0, GROUP), gq_ref[...], qa_ref)
    head_norm_rope(proj(GROUP, GROUP), gk_ref[...], ka_ref)
    va_ref[...] = proj(2 * GROUP, GROUP).astype(BF16)
    qh_ref[...] = proj(3 * GROUP, GROUP).astype(BF16)
    fh_ref[...] = proj(4 * GROUP, GROUP)
    ih_ref[...] = proj(5 * GROUP, GROUP).astype(BF16)
    qm_ref[...] = proj(6 * GROUP, GROUP).astype(BF16)
    for c in range(3):
        z_ref[:, c * GROUP:(c + 1) * GROUP] = proj((7 + c) * GROUP, GROUP).astype(BF16)


def _in_proj(xf, g, w, cosf, sinf, gq, gk, layer):
    n = xf.shape[0]
    tm = ROW_TILE
    row = lambda i: (i, 0)
    of_layer = lambda i: (layer, 0, 0)
    outs = ((GROUP, F32), (GROUP, BF16), (GROUP, BF16), (GROUP, BF16), (GROUP, F32), (GROUP, BF16),
            (GROUP, BF16), (D_MIX, BF16))
    return pl.pallas_call(
        _in_proj_kernel,
        grid=(n // tm,),
        in_specs=[pl.BlockSpec((tm, D_MODEL), row),
                  pl.BlockSpec((None, 1, D_MODEL), of_layer),
                  pl.BlockSpec((None, D_MODEL, IN_COLS), of_layer, pipeline_mode=pl.Buffered(1)),
                  pl.BlockSpec((tm, LANES), row),
                  pl.BlockSpec((tm, LANES), row),
                  pl.BlockSpec((None, 1, GROUP), of_layer),
                  pl.BlockSpec((None, 1, GROUP), of_layer)],
        out_specs=[pl.BlockSpec((tm, w), row) for w, _ in outs],
        out_shape=[jax.ShapeDtypeStruct((n, w), dt) for w, dt in outs],
        compiler_params=pltpu.CompilerParams(dimension_semantics=("parallel",), vmem_limit_bytes=VMEM_LIMIT),
        name="in_proj",
    )(xf, g, w, cosf, sinf, gq, gk)


def _moba_kernel(q_ref, k_ref, v_ref, o_ref, kmean_ref, bias_ref, vt_ref, qw_ref, s_ref, *, nb):
    blk = MOBA_BLOCK
    seq = nb * blk
    lane = lax.broadcasted_iota(jnp.int32, (1, LANES), 1)
    heads = [(lane >= h * MOBA_D) & (lane < (h + 1) * MOBA_D) for h in range(2)]

    ones = jnp.ones((MOBA_VT_ROWS - MOBA_D, blk), F32)
    for j in range(nb):
        rows = slice(j * blk, (j + 1) * blk)
        kmean_ref[j:j + 1, :] = jnp.mean(k_ref[rows, :].astype(F32), axis=0, keepdims=True)
        v_t = v_ref[rows, :].astype(F32).T
        vth = [jnp.concatenate([v_t[h * MOBA_D:(h + 1) * MOBA_D], ones], axis=0).astype(BF16) for h in range(2)]
        zvt = jnp.zeros_like(vth[0])
        vt_ref[j] = jnp.concatenate([jnp.concatenate([vth[0], zvt], axis=1),
                                     jnp.concatenate([zvt, vth[1]], axis=1)], axis=0)

    km = kmean_ref[...]
    q_hi, q_lo = _split2(q_ref[...])
    km_parts = [_split2(jnp.where(heads[h], km, 0.0)) for h in range(2)]
    g_hi = _nt(jnp.concatenate([km_parts[0][0], km_parts[0][1], km_parts[1][0], km_parts[1][1]], axis=0), q_hi)
    g_lo = _nt(jnp.concatenate([km_parts[0][0], km_parts[1][0]], axis=0), q_lo)
    blk_row = lax.broadcasted_iota(jnp.int32, (nb, 1), 0)
    q_blk = lax.broadcasted_iota(jnp.int32, (1, seq), 1) // blk
    past = blk_row < q_blk
    for h in range(2):
        gate_t = g_hi[2 * h * nb:(2 * h + 1) * nb] + g_hi[(2 * h + 1) * nb:(2 * h + 2) * nb] + g_lo[h * nb:(h + 1) * nb]
        gate_t = jnp.where(past, gate_t, -jnp.inf)
        cnt = jnp.zeros((nb, seq), F32)
        for n in range(nb - 1):
            gn = gate_t[n:n + 1, :]
            before = (gn > gate_t) | ((gn == gate_t) & (n < blk_row))
            cnt = cnt + jnp.where(before, 1.0, 0.0)
        sel = (past & (cnt < float(MOBA_TOPK))) | (blk_row == q_blk)
        bias_ref[h] = jnp.where(sel, 0.0, MASKED)
    for c in range(nb):
        qs = q_ref[c * blk:(c + 1) * blk, :] * (MOBA_D ** -0.5 * LOG2E)
        qw_ref[c] = jnp.concatenate([jnp.where(heads[h], qs, 0.0) for h in range(2)], axis=0).astype(BF16)

    kpos = lax.broadcasted_iota(jnp.int32, (blk, blk), 0)
    qpos = lax.broadcasted_iota(jnp.int32, (blk, blk), 1)
    causal_t = kpos <= qpos

    for t in range(nb // 2):
        buf = t % 2
        base = 0
        for c in (nb - 1 - t, t):
            qw = qw_ref[c]
            mx = [None, None]
            for j in range(c + 1):
                s2 = _nt(k_ref[j * blk:(j + 1) * blk, :], qw)
                for h in range(2):
                    s = s2[:, h * blk:(h + 1) * blk] + bias_ref[h, j:j + 1, c * blk:(c + 1) * blk]
                    if j == c:
                        s = jnp.where(causal_t, s, MASKED)
                    s_ref[buf, h, base + j * blk:base + (j + 1) * blk, :] = s
                    mx[h] = s if mx[h] is None else jnp.maximum(mx[h], s)
            ms = [jnp.max(mx[h], axis=0, keepdims=True) for h in range(2)]
            acc = None
            for j in range(c + 1):
                tile = slice(base + j * blk, base + (j + 1) * blk)
                p = jnp.concatenate([jnp.exp2(s_ref[buf, h, tile, :] - ms[h]).astype(BF16) for h in range(2)], axis=0)
                pv = _nn(vt_ref[j], p)
                acc = pv if acc is None else acc + pv
            outs = [acc[h * MOBA_VT_ROWS:h * MOBA_VT_ROWS + MOBA_D]
                    / acc[h * MOBA_VT_ROWS + MOBA_D:h * MOBA_VT_ROWS + MOBA_D + 1] for h in range(2)]
            o_ref[c * blk:(c + 1) * blk, :] = jnp.concatenate(outs, axis=0).T.astype(BF16)
            base += (c + 1) * blk


def _moba(qa, ka, va, batch, seq):
    nb = seq // MOBA_BLOCK
    assert nb % 2 == 0
    pairs = GROUP // LANES
    whole = pl.BlockSpec((seq, LANES), lambda b, p: (b, p))
    return pl.pallas_call(
        functools.partial(_moba_kernel, nb=nb),
        grid=(batch, pairs),
        in_specs=[whole, whole, whole],
        out_specs=whole,
        out_shape=jax.ShapeDtypeStruct(qa.shape, BF16),
        scratch_shapes=[pltpu.VMEM((nb, LANES), F32),
                        pltpu.VMEM((2, nb, seq), F32),
                        pltpu.VMEM((nb, 2 * MOBA_VT_ROWS, 2 * MOBA_BLOCK), BF16),
                        pltpu.VMEM((nb, 2 * MOBA_BLOCK, LANES), BF16),
                        pltpu.VMEM((2, 2, (nb + 1) * MOBA_BLOCK, MOBA_BLOCK), F32)],
        compiler_params=pltpu.CompilerParams(dimension_semantics=("parallel", "parallel"),
                                             vmem_limit_bytes=VMEM_LIMIT),
        name="moba",
    )(qa, ka, va)


def _hgrn_consts():
    c = HGRN_CHUNK
    t = np.arange(c)
    w_rows, masks = [], []
    for lvl in range(HGRN_LEVELS):
        m = 1 << lvl
        blk = t // m
        odd = (blk % 2) == 1
        start = blk * m
        end = start + m - 1
        u = t[None, :]
        w_odd = (u >= start[:, None]) & (u <= t[:, None])
        w_even = (u > t[:, None]) & (u <= end[:, None])
        w_rows.append(np.where(odd[:, None], w_odd, w_even))
        masks.append(odd[:, None] & (blk[None, :] == blk[:, None] - 1))
    masks.append(t[None, :] == t[:, None])
    w_mm = w_rows[:HGRN_MM_LEVELS] + [t[None, :] <= t[:, None]]
    w_all = np.concatenate(w_mm, axis=0).astype(np.float32)
    w_all = np.concatenate([w_all, w_all], axis=1)
    mask_cat = np.concatenate([masks[-1]] + masks[:-1] + [np.zeros((c, c), bool)], axis=1).astype(np.float32)
    return jnp.asarray(w_all, BF16), jnp.asarray(mask_cat, BF16)


def _hgrn_kernel(qh_ref, fh_ref, ih_ref, logit_ref, gain_ref, w_ref, mask_ref, o_ref, state_ref, *, layer):
    c = HGRN_CHUNK
    state_ref[...] = jnp.zeros_like(state_ref)

    if layer > 0:
        lg = logit_ref[...]
        e = jnp.exp(lg - jnp.max(lg, axis=0, keepdims=True))
        sm = e / jnp.sum(e, axis=0, keepdims=True)
        lb = jnp.sum(sm[1:layer + 1, :], axis=0, keepdims=True)
        log_lb = jnp.log(lb) * LOG2E
        log_1m_lb = jnp.log1p(-lb) * LOG2E
    w_all = w_ref[...]
    gain = gain_ref[...]

    def prepare(base, ci):
        rows = pl.ds(base + ci * c, c)
        fl = fh_ref[rows, :]
        vi = ih_ref[rows, :]
        qh = qh_ref[rows, :].astype(F32)
        qf = qh / (1.0 + jnp.exp2(qh * -LOG2E))
        e = jnp.exp2(jnp.abs(fl) * -LOG2E)
        r = 1.0 / (1.0 + e)
        log_sig = jnp.minimum(fl, 0.0) * LOG2E - jnp.log2(1.0 + e)
        sig_neg = jnp.where(fl >= 0.0, e * r, r)
        if layer > 0:
            b = log_1m_lb + log_sig
            lf = jnp.maximum(log_lb, b) + jnp.log2(1.0 + jnp.exp2(-jnp.abs(log_lb - b)))
            kf = (1.0 - lb) * sig_neg
        else:
            lf = log_sig
            kf = sig_neg

        f1, f2 = _split2(lf)
        x_mm = _nn(w_all, jnp.concatenate([f1, f2], axis=0))
        a2 = x_mm[HGRN_MM_LEVELS * c:(HGRN_MM_LEVELS + 1) * c, :]
        e_lvl = [jnp.exp2(x_mm[lvl * c:(lvl + 1) * c, :]) for lvl in range(HGRN_MM_LEVELS)]
        for lvl in range(HGRN_MM_LEVELS, HGRN_LEVELS):
            m = 1 << lvl
            parts = []
            for g in range(c // (2 * m)):
                ref = a2[2 * m * g + m - 1:2 * m * g + m, :]
                parts.append(-jnp.abs(a2[2 * m * g:2 * m * (g + 1), :] - ref))
            e_lvl.append(jnp.exp2(jnp.concatenate(parts, axis=0)))
        e_cum = jnp.exp2(a2)
        e_rest = jnp.exp2(a2[c - 1:c, :] - a2)
        e_end = e_cum[c - 1:c, :]
        qf_b, kf_b = qf.astype(BF16), kf.astype(BF16)
        e_b = [e.astype(BF16) for e in e_lvl]
        qb = [qf_b] + [qf_b * e for e in e_b]
        kb = [kf_b] + [kf_b * e for e in e_b]
        return qb, kb, qf_b * e_cum.astype(BF16), kf_b * e_rest.astype(BF16), vi, e_end

    def recur(base, ci, prepared):
        rows = pl.ds(base + ci * c, c)
        qb_all, kb_all, ql_all, kdec_all, vi, e_end = prepared
        zero = jnp.zeros((c, HGRN_D), BF16)
        zsq = jnp.zeros((HGRN_D, HGRN_D), BF16)
        for hp in range(HGRN_HEADS // 2):
            p_pair, ql_pair, v_pair, st_pair = [], [], [], []
            for h in (2 * hp, 2 * hp + 1):
                cols = slice(h * HGRN_D, (h + 1) * HGRN_D)
                qb = [q[:, cols] for q in qb_all]
                kb = [k[:, cols] for k in kb_all]
                s_parts = []
                for a in range(0, HGRN_LEVELS, 2):
                    q_cat = jnp.concatenate([qb[a], qb[a + 1]], axis=1)
                    k_blk = jnp.concatenate([jnp.concatenate([kb[a], zero], axis=1),
                                             jnp.concatenate([zero, kb[a + 1]], axis=1)], axis=0)
                    s_parts.append(_nt(q_cat, k_blk))
                s_parts.append(_nt(qb[HGRN_LEVELS], jnp.concatenate([kb[HGRN_LEVELS], zero], axis=0)))
                p = None
                for a, part in enumerate(s_parts):
                    part = part.astype(BF16) * mask_ref[:, 2 * a * c:(2 * a + 2) * c]
                    p = part if p is None else p + part
                p_pair.append(p)
                ql_pair.append(ql_all[:, cols])
                v_pair.append(vi[:, cols])
                st_pair.append(state_ref[:, cols])
            vv = [jnp.concatenate([v, v], axis=0) for v in v_pair]
            zv = jnp.zeros_like(vv[0])
            v_blk = jnp.concatenate([jnp.concatenate([vv[0], zv], axis=1),
                                     jnp.concatenate([zv, vv[1]], axis=1)], axis=0)
            o_pair = _nn(jnp.concatenate(p_pair, axis=1), v_blk)
            st_b = [st.astype(BF16) for st in st_pair]
            st_blk = jnp.concatenate([jnp.concatenate([st_b[0], zsq], axis=1),
                                      jnp.concatenate([zsq, st_b[1]], axis=1)], axis=0)
            o_pair = o_pair + _nt(jnp.concatenate(ql_pair, axis=1), st_blk)
            for i, h in enumerate((2 * hp, 2 * hp + 1)):
                cols = slice(h * HGRN_D, (h + 1) * HGRN_D)
                o = o_pair[:, i * HGRN_D:(i + 1) * HGRN_D]
                state_ref[:, cols] = st_pair[i] * e_end[:, cols] + _tn(v_pair[i], kdec_all[:, cols])
                ms = jnp.mean(o * o, axis=-1, keepdims=True)
                o_ref[rows, cols] = (o * lax.rsqrt(ms + NORM_EPS) * gain[:, cols]).astype(BF16)

    n_chunks = HGRN_TILE // c

    def group(g, carry):
        base = pl.multiple_of(g * HGRN_TILE, HGRN_TILE)
        prepared = prepare(base, 0)
        for ci in range(n_chunks):
            nxt = prepare(base, ci + 1) if ci + 1 < n_chunks else None
            recur(base, ci, prepared)
            prepared = nxt
        return carry

    lax.fori_loop(0, qh_ref.shape[0] // HGRN_TILE, group, 0)


def _hgrn(qh, fh, ih, logits, gain, layer, batch, seq):
    w_all, mask_all = _hgrn_consts()
    assert seq % HGRN_TILE == 0
    depth = logits.shape[0]
    const = lambda b: (0, 0)
    return pl.pallas_call(
        functools.partial(_hgrn_kernel, layer=layer),
        grid=(batch,),
        in_specs=[pl.BlockSpec((seq, GROUP), lambda b: (b, 0))] * 3 + [
                  pl.BlockSpec((depth, GROUP), const),
                  pl.BlockSpec((None, 1, GROUP), lambda b: (layer, 0, 0)),
                  pl.BlockSpec(w_all.shape, const),
                  pl.BlockSpec(mask_all.shape, const)],
        out_specs=pl.BlockSpec((seq, GROUP), lambda b: (b, 0)),
        out_shape=jax.ShapeDtypeStruct((fh.shape[0], GROUP), BF16),
        scratch_shapes=[pltpu.VMEM((HGRN_D, GROUP), F32)],
        compiler_params=pltpu.CompilerParams(dimension_semantics=("parallel",),
                                             vmem_limit_bytes=VMEM_LIMIT),
        name="hgrn",
    )(qh, fh, ih, logits, gain, w_all, mask_all)


def _mem_kv_kernel(m_ref, g_ref, w_ref, gk_ref, seg_ref, k_ref, v_ref):
    x = m_ref[...]
    ms = jnp.mean(x * x, axis=-1, keepdims=True)
    hb = (x * lax.rsqrt(ms + NORM_EPS) * g_ref[...]).astype(BF16)
    gk = gk_ref[...]
    seg = seg_ref[...]
    for c in range(GROUP // 256):
        kc = _nn(hb, w_ref[:, c * 256:(c + 1) * 256].astype(BF16))
        ss = _nn((kc * kc).astype(BF16), seg) * (1.0 / MEM_D)
        kn = kc * lax.rsqrt(ss + NORM_EPS) * gk[:, c * 256:(c + 1) * 256]
        k_ref[:, c * 256:(c + 1) * 256] = kn.astype(BF16)
    v_ref[...] = _nn(hb, w_ref[:, GROUP:2 * GROUP].astype(BF16)).astype(BF16)


def _mem_kv(memf, g, w, gk, mem_len, layer):
    n = memf.shape[0]
    seg = np.arange(256)[:, None] // MEM_D == np.arange(256)[None, :] // MEM_D
    seg = jnp.asarray(seg, BF16)
    const = lambda i: (0, 0)
    row = lambda i: (i, 0)
    of_layer = lambda i: (layer, 0, 0)
    return pl.pallas_call(
        _mem_kv_kernel,
        grid=(n // mem_len,),
        in_specs=[pl.BlockSpec((mem_len, D_MODEL), row),
                  pl.BlockSpec((None, 1, D_MODEL), of_layer),
                  pl.BlockSpec((None, D_MODEL, 2 * GROUP), of_layer),
                  pl.BlockSpec((None, 1, GROUP), of_layer),
                  pl.BlockSpec((256, 256), const)],
        out_specs=[pl.BlockSpec((mem_len, GROUP), row)] * 2,
        out_shape=[jax.ShapeDtypeStruct((n, GROUP), BF16)] * 2,
        compiler_params=pltpu.CompilerParams(dimension_semantics=("parallel",), vmem_limit_bytes=VMEM_LIMIT),
        name="mem_kv",
    )(memf, g, w, gk, seg)


def _out_proj_kernel(x_ref, oa_ref, oh_ref, qm_ref, z_ref, km_ref, vm_ref, gq_ref, w_ref, o_ref, y_ref):
    def gate(o, c0, c1):
        y_ref[:, c0:c1] = (o * _silu(z_ref[:, c0:c1].astype(F32))).astype(BF16)

    gate(oa_ref[...].astype(F32), 0, GROUP)
    gate(oh_ref[...].astype(F32), GROUP, 2 * GROUP)
    gq = gq_ref[...]
    for h in range(MEM_HEADS):
        cols = slice(h * MEM_D, (h + 1) * MEM_D)
        q = qm_ref[:, cols].astype(F32)
        ms = jnp.mean(q * q, axis=-1, keepdims=True)
        qn = q * lax.rsqrt(ms + NORM_EPS) * gq * (MEM_D ** -0.5)
        s = _nt(qn.astype(BF16), km_ref[:, cols])
        p = jnp.exp(s - jnp.max(s, axis=1, keepdims=True))
        v_aug = jnp.concatenate([vm_ref[:, cols], jnp.ones((vm_ref.shape[0], MEM_D), BF16)], axis=1)
        pv = _nn(p.astype(BF16), v_aug)
        om = pv[:, 0:MEM_D] / pv[:, MEM_D:2 * MEM_D]
        c0 = 2 * GROUP + h * MEM_D
        gate(om, c0, c0 + MEM_D)
    o_ref[...] = x_ref[...] + _nn(y_ref[...], w_ref[...].astype(BF16))


def _out_proj(xf, oa, oh, qm, z, km, vm, gq, w, seq, mem_len, layer):
    n = xf.shape[0]
    tm = OUT_TILE
    steps = seq // tm
    const = lambda i: (0, 0)
    row = lambda i: (i, 0)
    per_batch = lambda i: (i // steps, 0)
    of_layer = lambda i: (layer, 0, 0)
    return pl.pallas_call(
        _out_proj_kernel,
        grid=(n // tm,),
        in_specs=[pl.BlockSpec((tm, D_MODEL), row),
                  pl.BlockSpec((tm, GROUP), row),
                  pl.BlockSpec((tm, GROUP), row),
                  pl.BlockSpec((tm, GROUP), row),
                  pl.BlockSpec((tm, D_MIX), row),
                  pl.BlockSpec((mem_len, GROUP), per_batch),
                  pl.BlockSpec((mem_len, GROUP), per_batch),
                  pl.BlockSpec((None, 1, MEM_D), of_layer),
                  pl.BlockSpec((None, D_MIX, D_MODEL), of_layer, pipeline_mode=pl.Buffered(1))],
        out_specs=pl.BlockSpec((tm, D_MODEL), row),
        out_shape=jax.ShapeDtypeStruct((n, D_MODEL), F32),
        scratch_shapes=[pltpu.VMEM((tm, D_MIX), BF16)],
        compiler_params=pltpu.CompilerParams(dimension_semantics=("parallel",), vmem_limit_bytes=VMEM_LIMIT),
        name="out_proj",
    )(xf, oa, oh, qm, z, km, vm, gq, w)


def kernel(x, mem, positions, norm_g, w_in, w_out, moba_q_norm, moba_k_norm, hgrn_lb_logits, hgrn_o_norm,
           mem_norm_g, w_mem_kv, mem_q_norm, mem_k_norm):
    batch, seq, d_model = x.shape
    mem_len = mem.shape[1]
    depth = w_in.shape[0]
    assert d_model == D_MODEL and seq % ROW_TILE == 0 and seq % OUT_TILE == 0 and mem_len % 8 == 0
    n = batch * seq
    xf = x.reshape(n, d_model)
    memf = mem.reshape(batch * mem_len, d_model)
    cosf, sinf = _rope_tables(positions.reshape(1, n))
    per_head = lambda p, reps: jnp.tile(p, (1, reps))[:, None, :]
    norm_g, mem_norm_g, mem_q_norm = norm_g[:, None, :], mem_norm_g[:, None, :], mem_q_norm[:, None, :]
    gq, gk = per_head(moba_q_norm, GROUP // MOBA_D), per_head(moba_k_norm, GROUP // MOBA_D)
    g_hgrn, g_memk = per_head(hgrn_o_norm, HGRN_HEADS), per_head(mem_k_norm, MEM_HEADS)
    for l in range(depth):
        qa, ka, va, qh, fh, ih, qm, z = _in_proj(xf, norm_g, w_in, cosf, sinf, gq, gk, l)
        oa = _moba(qa, ka, va, batch, seq)
        oh = _hgrn(qh, fh, ih, hgrn_lb_logits, g_hgrn, l, batch, seq)
        km, vm = _mem_kv(memf, mem_norm_g, w_mem_kv, g_memk, mem_len, l)
        xf = _out_proj(xf, oa, oh, qm, z, km, vm, mem_q_norm, w_out, seq, mem_len, l)
    return xf.reshape(batch, seq, d_model)
```

```python
import functools

import numpy as np
import jax
import jax.numpy as jnp
from jax import lax
from jax.experimental import pallas as pl
from jax.experimental.pallas import tpu as pltpu

F32 = jnp.float32
BF16 = jnp.bfloat16

D_MODEL = 1024
GROUP = D_MODEL // 2
D_MIX = 3 * GROUP
IN_COLS = 7 * GROUP + D_MIX
MOBA_D = 64
MOBA_BLOCK = 256
MOBA_TOPK = 3
MOBA_VT_ROWS = MOBA_D + 16
HGRN_D = 128
HGRN_HEADS = GROUP // HGRN_D
HGRN_CHUNK = 64
HGRN_LEVELS = 6
HGRN_MM_LEVELS = 3
LOG2E = 1.4426950408889634
MEM_D = 128
MEM_HEADS = GROUP // MEM_D
ROPE_THETA = 500000.0
ROPE_DIM = MOBA_D // 4
ROPE_HALF = ROPE_DIM // 2
NORM_EPS = 1e-6
LANES = 128
ROW_TILE = 512
OUT_TILE = 1024
HGRN_TILE = 1024
MASKED = -1e30
VMEM_LIMIT = 48 * 1024 * 1024


def _nt(a, b):
    return lax.dot_general(a, b, (((1,), (1,)), ((), ())), preferred_element_type=F32)


def _tn(a, b):
    return lax.dot_general(a, b, (((0,), (0,)), ((), ())), preferred_element_type=F32)


def _nn(a, b):
    return jnp.dot(a, b, preferred_element_type=F32)


def _split2(x):
    hi = x.astype(BF16)
    lo = (x - hi.astype(F32)).astype(BF16)
    return hi, lo


def _split3(x):
    hi = x.astype(BF16)
    r = x - hi.astype(F32)
    mid = r.astype(BF16)
    lo = (r - mid.astype(F32)).astype(BF16)
    return hi, mid, lo


def _sigmoid(x):
    return 1.0 / (1.0 + jnp.exp(-x))


def _silu(x):
    return x * _sigmoid(x)


def _rope_table_kernel(pos_ref, invf_ref, expand_ref, base_ref, cos_ref, sin_ref):
    ang = invf_ref[...] * pos_ref[...].astype(F32)
    tm = ang.shape[1]
    table = jnp.concatenate([jnp.cos(ang), jnp.sin(ang), jnp.zeros((LANES - ROPE_DIM, tm), F32)], axis=0)
    hi, mid, lo = _split3(table.T)
    e = expand_ref[...]
    both = _nn(hi, e) + _nn(mid, e) + _nn(lo, e)
    cos_ref[...] = both[:, 0:LANES] + base_ref[...]
    sin_ref[...] = both[:, LANES:2 * LANES]


def _rope_tables(pos):
    n = pos.shape[1]
    tm = min(n, 2048)
    lane = np.arange(LANES) % MOBA_D
    freq = lane % ROPE_HALF
    rope = lane < ROPE_DIM
    sgn = np.where(lane < ROPE_HALF, -1.0, 1.0)
    expand = np.zeros((LANES, 2 * LANES), np.float32)
    for l in np.nonzero(rope)[0]:
        expand[freq[l], l] = 1.0
        expand[ROPE_HALF + freq[l], LANES + l] = sgn[l]
    base = np.where(rope, 0.0, 1.0).astype(np.float32)[None, :]
    invf = (ROPE_THETA ** (-np.arange(ROPE_HALF, dtype=np.float64) / ROPE_HALF)).astype(np.float32)[:, None]
    const = lambda i: (0, 0)
    return pl.pallas_call(
        _rope_table_kernel,
        grid=(n // tm,),
        in_specs=[pl.BlockSpec((1, tm), lambda i: (0, i)),
                  pl.BlockSpec((ROPE_HALF, 1), const),
                  pl.BlockSpec((LANES, 2 * LANES), const),
                  pl.BlockSpec((1, LANES), const)],
        out_specs=[pl.BlockSpec((tm, LANES), lambda i: (i, 0))] * 2,
        out_shape=[jax.ShapeDtypeStruct((n, LANES), F32)] * 2,
        name="rope_tables",
    )(pos, jnp.asarray(invf), jnp.asarray(expand, BF16), jnp.asarray(base))


def _in_proj_kernel(x_ref, g_ref, w_ref, cos_ref, sin_ref, gq_ref, gk_ref,
                    qa_ref, ka_ref, va_ref, qh_ref, fh_ref, ih_ref, qm_ref, z_ref):
    x = x_ref[...]
    ms = jnp.mean(x * x, axis=-1, keepdims=True)
    hb = (x * lax.rsqrt(ms + NORM_EPS) * g_ref[...]).astype(BF16)

    def proj(c0, width):
        return _nn(hb, w_ref[:, c0:c0 + width].astype(BF16))

    cosf = cos_ref[...]
    sinf = sin_ref[...]
    lane = lax.broadcasted_iota(jnp.int32, (1, LANES), 1) % MOBA_D
    lo_lane = lane < ROPE_HALF
    first = lax.broadcasted_iota(jnp.int32, (1, LANES), 1) < MOBA_D

    def head_norm_rope(p, gain, out_ref):
        for v in range(GROUP // LANES):
            xv = p[:, v * LANES:(v + 1) * LANES]
            sq = xv * xv
            s_first = jnp.sum(jnp.where(first, sq, 0.0), axis=1, keepdims=True)
            s_both = jnp.sum(sq, axis=1, keepdims=True)
            ss = jnp.where(first, s_first, s_both - s_first) * (1.0 / MOBA_D)
            xv = xv * lax.rsqrt(ss + NORM_EPS) * gain[:, v * LANES:(v + 1) * LANES]
            rot = jnp.where(lo_lane, pltpu.roll(xv, LANES - ROPE_HALF, 1), pltpu.roll(xv, ROPE_HALF, 1))
            out_ref[:, v * LANES:(v + 1) * LANES] = (xv * cosf + rot * sinf).astype(out_ref.dtype)

    head_norm_rope(proj(0, GROUP), gq_ref[...], qa_ref)
    head_norm_rope(proj(GROUP, GROUP), gk_ref[...], ka_ref)
    va_ref[...] = proj(2 * GROUP, GROUP).astype(BF16)
    qh_ref[...] = proj(3 * GROUP, GROUP).astype(BF16)
    fh_ref[...] = proj(4 * GROUP, GROUP)
    ih_ref[...] = proj(5 * GROUP, GROUP).astype(BF16)
    qm_ref[...] = proj(6 * GROUP, GROUP).astype(BF16)
    for c in range(3):
        z_ref[:, c * GROUP:(c + 1) * GROUP] = proj((7 + c) * GROUP, GROUP).astype(BF16)


def _in_proj(xf, g, w, cosf, sinf, gq, gk, layer):
    n = xf.shape[0]
    tm = ROW_TILE
    row = lambda i: (i, 0)
    of_layer = lambda i: (layer, 0, 0)
    outs = ((GROUP, F32), (GROUP, BF16), (GROUP, BF16), (GROUP, BF16), (GROUP, F32), (GROUP, BF16),
            (GROUP, BF16), (D_MIX, BF16))
    return pl.pallas_call(
        _in_proj_kernel,
        grid=(n // tm,),
        in_specs=[pl.BlockSpec((tm, D_MODEL), row),
                  pl.BlockSpec((None, 1, D_MODEL), of_layer),
                  pl.BlockSpec((None, D_MODEL, IN_COLS), of_layer, pipeline_mode=pl.Buffered(1)),
                  pl.BlockSpec((tm, LANES), row),
                  pl.BlockSpec((tm, LANES), row),
                  pl.BlockSpec((None, 1, GROUP), of_layer),
                  pl.BlockSpec((None, 1, GROUP), of_layer)],
        out_specs=[pl.BlockSpec((tm, w), row) for w, _ in outs],
        out_shape=[jax.ShapeDtypeStruct((n, w), dt) for w, dt in outs],
        compiler_params=pltpu.CompilerParams(dimension_semantics=("parallel",), vmem_limit_bytes=VMEM_LIMIT),
        name="in_proj",
    )(xf, g, w, cosf, sinf, gq, gk)


def _moba_kernel(q_ref, k_ref, v_ref, o_ref, kmean_ref, bias_ref, vt_ref, qw_ref, s_ref, *, nb):
    blk = MOBA_BLOCK
    seq = nb * blk
    lane = lax.broadcasted_iota(jnp.int32, (1, LANES), 1)
    heads = [(lane >= h * MOBA_D) & (lane < (h + 1) * MOBA_D) for h in range(2)]

    ones = jnp.ones((MOBA_VT_ROWS - MOBA_D, blk), F32)
    for j in range(nb):
        rows = slice(j * blk, (j + 1) * blk)
        kmean_ref[j:j + 1, :] = jnp.mean(k_ref[rows, :].astype(F32), axis=0, keepdims=True)
        v_t = v_ref[rows, :].astype(F32).T
        vth = [jnp.concatenate([v_t[h * MOBA_D:(h + 1) * MOBA_D], ones], axis=0).astype(BF16) for h in range(2)]
        zvt = jnp.zeros_like(vth[0])
        vt_ref[j] = jnp.concatenate([jnp.concatenate([vth[0], zvt], axis=1),
                                     jnp.concatenate([zvt, vth[1]], axis=1)], axis=0)

    km = kmean_ref[...]
    q_hi, q_lo = _split2(q_ref[...])
    km_parts = [_split2(jnp.where(heads[h], km, 0.0)) for h in range(2)]
    g_hi = _nt(jnp.concatenate([km_parts[0][0], km_parts[0][1], km_parts[1][0], km_parts[1][1]], axis=0), q_hi)
    g_lo = _nt(jnp.concatenate([km_parts[0][0], km_parts[1][0]], axis=0), q_lo)
    blk_row = lax.broadcasted_iota(jnp.int32, (nb, 1), 0)
    q_blk = lax.broadcasted_iota(jnp.int32, (1, seq), 1) // blk
    past = blk_row < q_blk
    for h in range(2):
        gate_t = g_hi[2 * h * nb:(2 * h + 1) * nb] + g_hi[(2 * h + 1) * nb:(2 * h + 2) * nb] + g_lo[h * nb:(h + 1) * nb]
        gate_t = jnp.where(past, gate_t, -jnp.inf)
        cnt = jnp.zeros((nb, seq), F32)
        for n in range(nb - 1):
            gn = gate_t[n:n + 1, :]
            before = (gn > gate_t) | ((gn == gate_t) & (n < blk_row))
            cnt = cnt + jnp.where(before, 1.0, 0.0)
        sel = (past & (cnt < float(MOBA_TOPK))) | (blk_row == q_blk)
        bias_ref[h] = jnp.where(sel, 0.0, MASKED)
    for c in range(nb):
        qs = q_ref[c * blk:(c + 1) * blk, :] * (MOBA_D ** -0.5 * LOG2E)
        qw_ref[c] = jnp.concatenate([jnp.where(heads[h], qs, 0.0) for h in range(2)], axis=0).astype(BF16)

    kpos = lax.broadcasted_iota(jnp.int32, (blk, blk), 0)
    qpos = lax.broadcasted_iota(jnp.int32, (blk, blk), 1)
    causal_t = kpos <= qpos

    for t in range(nb // 2):
        buf = t % 2
        c_hi, c_lo = nb - 1 - t, t
        blocks = (c_hi, c_lo)
        bases = (0, (c_hi + 1) * blk)
        qw_pair = jnp.concatenate([qw_ref[c_hi], qw_ref[c_lo]], axis=0)
        mx = [[None, None], [None, None]]
        for j in range(c_hi + 1):
            shared = j <= c_lo
            s4 = _nt(k_ref[j * blk:(j + 1) * blk, :], qw_pair if shared else qw_ref[c_hi])
            for i, c in enumerate(blocks if shared else blocks[:1]):
                for h in range(2):
                    col = (2 * i + h) * blk
                    s = s4[:, col:col + blk] + bias_ref[h, j:j + 1, c * blk:(c + 1) * blk]
                    if j == c:
                        s = jnp.where(causal_t, s, MASKED)
                    s_ref[buf, h, bases[i] + j * blk:bases[i] + (j + 1) * blk, :] = s
                    mx[i][h] = s if mx[i][h] is None else jnp.maximum(mx[i][h], s)
        for i, c in enumerate(blocks):
            base = bases[i]
            ms = [jnp.max(mx[i][h], axis=0, keepdims=True) for h in range(2)]
            acc = None
            for j in range(c + 1):
                tile = slice(base + j * blk, base + (j + 1) * blk)
                p = jnp.concatenate([jnp.exp2(s_ref[buf, h, tile, :] - ms[h]).astype(BF16) for h in range(2)], axis=0)
                pv = _nn(vt_ref[j], p)
                acc = pv if acc is None else acc + pv
            outs = [acc[h * MOBA_VT_ROWS:h * MOBA_VT_ROWS + MOBA_D]
                    / acc[h * MOBA_VT_ROWS + MOBA_D:h * MOBA_VT_ROWS + MOBA_D + 1] for h in range(2)]
            o_ref[c * blk:(c + 1) * blk, :] = jnp.concatenate(outs, axis=0).T.astype(BF16)


def _moba(qa, ka, va, batch, seq):
    nb = seq // MOBA_BLOCK
    assert nb % 2 == 0
    pairs = GROUP // LANES
    whole = pl.BlockSpec((seq, LANES), lambda b, p: (b, p))
    return pl.pallas_call(
        functools.partial(_moba_kernel, nb=nb),
        grid=(batch, pairs),
        in_specs=[whole, whole, whole],
        out_specs=whole,
        out_shape=jax.ShapeDtypeStruct(qa.shape, BF16),
        scratch_shapes=[pltpu.VMEM((nb, LANES), F32),
                        pltpu.VMEM((2, nb, seq), F32),
                        pltpu.VMEM((nb, 2 * MOBA_VT_ROWS, 2 * MOBA_BLOCK), BF16),
                        pltpu.VMEM((nb, 2 * MOBA_BLOCK, LANES), BF16),
                        pltpu.VMEM((2, 2, (nb + 1) * MOBA_BLOCK, MOBA_BLOCK), F32)],
        compiler_params=pltpu.CompilerParams(dimension_semantics=("parallel", "parallel"),
                                             vmem_limit_bytes=VMEM_LIMIT),
        name="moba",
    )(qa, ka, va)


def _hgrn_consts():
    c = HGRN_CHUNK
    t = np.arange(c)
    w_rows, masks = [], []
    for lvl in range(HGRN_LEVELS):
        m = 1 << lvl
        blk = t // m
        odd = (blk % 2) == 1
        start = blk * m
        end = start + m - 1
        u = t[None, :]
        w_odd = (u >= start[:, None]) & (u <= t[:, None])
        w_even = (u > t[:, None]) & (u <= end[:, None])
        w_rows.append(np.where(odd[:, None], w_odd, w_even))
        masks.append(odd[:, None] & (blk[None, :] == blk[:, None] - 1))
    masks.append(t[None, :] == t[:, None])
    w_mm = w_rows[:HGRN_MM_LEVELS] + [t[None, :] <= t[:, None]]
    w_all = np.concatenate(w_mm, axis=0).astype(np.float32)
    w_all = np.concatenate([w_all, w_all], axis=1)
    mask_cat = np.concatenate([masks[-1]] + masks[:-1] + [np.zeros((c, c), bool)], axis=1).astype(np.float32)
    return jnp.asarray(w_all, BF16), jnp.asarray(mask_cat, BF16)


def _hgrn_kernel(qh_ref, fh_ref, ih_ref, logit_ref, gain_ref, w_ref, mask_ref, o_ref, state_ref, *, layer):
    c = HGRN_CHUNK
    state_ref[...] = jnp.zeros_like(state_ref)

    if layer > 0:
        lg = logit_ref[...]
        e = jnp.exp(lg - jnp.max(lg, axis=0, keepdims=True))
        sm = e / jnp.sum(e, axis=0, keepdims=True)
        lb = jnp.sum(sm[1:layer + 1, :], axis=0, keepdims=True)
        log_lb = jnp.log(lb) * LOG2E
        log_1m_lb = jnp.log1p(-lb) * LOG2E
    w_all = w_ref[...]
    gain = gain_ref[...]

    def prepare(base, ci):
        rows = pl.ds(base + ci * c, c)
        fl = fh_ref[rows, :]
        vi = ih_ref[rows, :]
        qh = qh_ref[rows, :].astype(F32)
        qf = qh / (1.0 + jnp.exp2(qh * -LOG2E))
        e = jnp.exp2(jnp.abs(fl) * -LOG2E)
        r = 1.0 / (1.0 + e)
        log_sig = jnp.minimum(fl, 0.0) * LOG2E - jnp.log2(1.0 + e)
        sig_neg = jnp.where(fl >= 0.0, e * r, r)
        if layer > 0:
            b = log_1m_lb + log_sig
            lf = jnp.maximum(log_lb, b) + jnp.log2(1.0 + jnp.exp2(-jnp.abs(log_lb - b)))
            kf = (1.0 - lb) * sig_neg
        else:
            lf = log_sig
            kf = sig_neg

        f1, f2 = _split2(lf)
        x_mm = _nn(w_all, jnp.concatenate([f1, f2], axis=0))
        a2 = x_mm[HGRN_MM_LEVELS * c:(HGRN_MM_LEVELS + 1) * c, :]
        e_lvl = [jnp.exp2(x_mm[lvl * c:(lvl + 1) * c, :]) for lvl in range(HGRN_MM_LEVELS)]
        for lvl in range(HGRN_MM_LEVELS, HGRN_LEVELS):
            m = 1 << lvl
            parts = []
            for g in range(c // (2 * m)):
                ref = a2[2 * m * g + m - 1:2 * m * g + m, :]
                parts.append(-jnp.abs(a2[2 * m * g:2 * m * (g + 1), :] - ref))
            e_lvl.append(jnp.exp2(jnp.concatenate(parts, axis=0)))
        e_cum = jnp.exp2(a2)
        e_rest = jnp.exp2(a2[c - 1:c, :] - a2)
        e_end = e_cum[c - 1:c, :]
        qf_b, kf_b = qf.astype(BF16), kf.astype(BF16)
        e_b = [e.astype(BF16) for e in e_lvl]
        qb = [qf_b] + [qf_b * e for e in e_b]
        kb = [kf_b] + [kf_b * e for e in e_b]
        return qb, kb, qf_b * e_cum.astype(BF16), kf_b * e_rest.astype(BF16), vi, e_end

    def recur(base, ci, prepared):
        rows = pl.ds(base + ci * c, c)
        qb_all, kb_all, ql_all, kdec_all, vi, e_end = prepared
        zero = jnp.zeros((c, HGRN_D), BF16)
        zsq = jnp.zeros((HGRN_D, HGRN_D), BF16)
        for hp in range(HGRN_HEADS // 2):
            p_pair, ql_pair, v_pair, st_pair = [], [], [], []
            for h in (2 * hp, 2 * hp + 1):
                cols = slice(h * HGRN_D, (h + 1) * HGRN_D)
                qb = [q[:, cols] for q in qb_all]
                kb = [k[:, cols] for k in kb_all]
                s_parts = []
                for a in range(0, HGRN_LEVELS, 2):
                    q_cat = jnp.concatenate([qb[a], qb[a + 1]], axis=1)
                    k_blk = jnp.concatenate([jnp.concatenate([kb[a], zero], axis=1),
                                             jnp.concatenate([zero, kb[a + 1]], axis=1)], axis=0)
                    s_parts.append(_nt(q_cat, k_blk))
                s_parts.append(_nt(qb[HGRN_LEVELS], jnp.concatenate([kb[HGRN_LEVELS], zero], axis=0)))
                p = None
                for a, part in enumerate(s_parts):
                    part = part.astype(BF16) * mask_ref[:, 2 * a * c:(2 * a + 2) * c]
                    p = part if p is None else p + part
                p_pair.append(p)
                ql_pair.append(ql_all[:, cols])
                v_pair.append(vi[:, cols])
                st_pair.append(state_ref[:, cols])
            vv = [jnp.concatenate([v, v], axis=0) for v in v_pair]
            zv = jnp.zeros_like(vv[0])
            v_blk = jnp.concatenate([jnp.concatenate([vv[0], zv], axis=1),
                                     jnp.concatenate([zv, vv[1]], axis=1)], axis=0)
            o_pair = _nn(jnp.concatenate(p_pair, axis=1), v_blk)
            st_b = [st.astype(BF16) for st in st_pair]
            st_blk = jnp.concatenate([jnp.concatenate([st_b[0], zsq], axis=1),
                                      jnp.concatenate([zsq, st_b[1]], axis=1)], axis=0)
            o_pair = o_pair + _nt(jnp.concatenate(ql_pair, axis=1), st_blk)
            for i, h in enumerate((2 * hp, 2 * hp + 1)):
                cols = slice(h * HGRN_D, (h + 1) * HGRN_D)
                o = o_pair[:, i * HGRN_D:(i + 1) * HGRN_D]
                state_ref[:, cols] = st_pair[i] * e_end[:, cols] + _tn(v_pair[i], kdec_all[:, cols])
                ms = jnp.mean(o * o, axis=-1, keepdims=True)
                o_ref[rows, cols] = (o * lax.rsqrt(ms + NORM_EPS) * gain[:, cols]).astype(BF16)

    n_chunks = HGRN_TILE // c

    def group(g, carry):
        base = pl.multiple_of(g * HGRN_TILE, HGRN_TILE)
        prepared = prepare(base, 0)
        for ci in range(n_chunks):
            nxt = prepare(base, ci + 1) if ci + 1 < n_chunks else None
            recur(base, ci, prepared)
            prepared = nxt
        return carry

    lax.fori_loop(0, qh_ref.shape[0] // HGRN_TILE, group, 0)


def _hgrn(qh, fh, ih, logits, gain, layer, batch, seq):
    w_all, mask_all = _hgrn_consts()
    assert seq % HGRN_TILE == 0
    depth = logits.shape[0]
    const = lambda b: (0, 0)
    return pl.pallas_call(
        functools.partial(_hgrn_kernel, layer=layer),
        grid=(batch,),
        in_specs=[pl.BlockSpec((seq, GROUP), lambda b: (b, 0))] * 3 + [
                  pl.BlockSpec((depth, GROUP), const),
                  pl.BlockSpec((None, 1, GROUP), lambda b: (layer, 0, 0)),
                  pl.BlockSpec(w_all.shape, const),
                  pl.BlockSpec(mask_all.shape, const)],
        out_specs=pl.BlockSpec((seq, GROUP), lambda b: (b, 0)),
        out_shape=jax.ShapeDtypeStruct((fh.shape[0], GROUP), BF16),
        scratch_shapes=[pltpu.VMEM((HGRN_D, GROUP), F32)],
        compiler_params=pltpu.CompilerParams(dimension_semantics=("parallel",),
                                             vmem_limit_bytes=VMEM_LIMIT),
        name="hgrn",
    )(qh, fh, ih, logits, gain, w_all, mask_all)


def _mem_kv_kernel(m_ref, g_ref, w_ref, gk_ref, seg_ref, k_ref, v_ref):
    x = m_ref[...]
    ms = jnp.mean(x * x, axis=-1, keepdims=True)
    hb = (x * lax.rsqrt(ms + NORM_EPS) * g_ref[...]).astype(BF16)
    gk = gk_ref[...]
    seg = seg_ref[...]
    for c in range(GROUP // 256):
        kc = _nn(hb, w_ref[:, c * 256:(c + 1) * 256].astype(BF16))
        ss = _nn((kc * kc).astype(BF16), seg) * (1.0 / MEM_D)
        kn = kc * lax.rsqrt(ss + NORM_EPS) * gk[:, c * 256:(c + 1) * 256]
        k_ref[:, c * 256:(c + 1) * 256] = kn.astype(BF16)
    v_ref[...] = _nn(hb, w_ref[:, GROUP:2 * GROUP].astype(BF16)).astype(BF16)


def _mem_kv(memf, g, w, gk, mem_len, layer):
    n = memf.shape[0]
    seg = np.arange(256)[:, None] // MEM_D == np.arange(256)[None, :] // MEM_D
    seg = jnp.asarray(seg, BF16)
    const = lambda i: (0, 0)
    row = lambda i: (i, 0)
    of_layer = lambda i: (layer, 0, 0)
    return pl.pallas_call(
        _mem_kv_kernel,
        grid=(n // mem_len,),
        in_specs=[pl.BlockSpec((mem_len, D_MODEL), row),
                  pl.BlockSpec((None, 1, D_MODEL), of_layer),
                  pl.BlockSpec((None, D_MODEL, 2 * GROUP), of_layer),
                  pl.BlockSpec((None, 1, GROUP), of_layer),
                  pl.BlockSpec((256, 256), const)],
        out_specs=[pl.BlockSpec((mem_len, GROUP), row)] * 2,
        out_shape=[jax.ShapeDtypeStruct((n, GROUP), BF16)] * 2,
        compiler_params=pltpu.CompilerParams(dimension_semantics=("parallel",), vmem_limit_bytes=VMEM_LIMIT),
        name="mem_kv",
    )(memf, g, w, gk, seg)


def _out_proj_kernel(x_ref, oa_ref, oh_ref, qm_ref, z_ref, km_ref, vm_ref, gq_ref, w_ref, o_ref, y_ref):
    def gate(o, c0, c1):
        y_ref[:, c0:c1] = (o * _silu(z_ref[:, c0:c1].astype(F32))).astype(BF16)

    gate(oa_ref[...].astype(F32), 0, GROUP)
    gate(oh_ref[...].astype(F32), GROUP, 2 * GROUP)
    gq = gq_ref[...]
    for h in range(MEM_HEADS):
        cols = slice(h * MEM_D, (h + 1) * MEM_D)
        q = qm_ref[:, cols].astype(F32)
        ms = jnp.mean(q * q, axis=-1, keepdims=True)
        qn = q * lax.rsqrt(ms + NORM_EPS) * gq * (MEM_D ** -0.5)
        s = _nt(qn.astype(BF16), km_ref[:, cols])
        p = jnp.exp(s - jnp.max(s, axis=1, keepdims=True))
        v_aug = jnp.concatenate([vm_ref[:, cols], jnp.ones((vm_ref.shape[0], MEM_D), BF16)], axis=1)
        pv = _nn(p.astype(BF16), v_aug)
        om = pv[:, 0:MEM_D] / pv[:, MEM_D:2 * MEM_D]
        c0 = 2 * GROUP + h * MEM_D
        gate(om, c0, c0 + MEM_D)
    o_ref[...] = x_ref[...] + _nn(y_ref[...], w_ref[...].astype(BF16))


def _out_proj(xf, oa, oh, qm, z, km, vm, gq, w, seq, mem_len, layer):
    n = xf.shape[0]
    tm = OUT_TILE
    steps = seq // tm
    const = lambda i: (0, 0)
    row = lambda i: (i, 0)
    per_batch = lambda i: (i // steps, 0)
    of_layer = lambda i: (layer, 0, 0)
    return pl.pallas_call(
        _out_proj_kernel,
        grid=(n // tm,),
        in_specs=[pl.BlockSpec((tm, D_MODEL), row),
                  pl.BlockSpec((tm, GROUP), row),
                  pl.BlockSpec((tm, GROUP), row),
                  pl.BlockSpec((tm, GROUP), row),
                  pl.BlockSpec((tm, D_MIX), row),
                  pl.BlockSpec((mem_len, GROUP), per_batch),
                  pl.BlockSpec((mem_len, GROUP), per_batch),
                  pl.BlockSpec((None, 1, MEM_D), of_layer),
                  pl.BlockSpec((None, D_MIX, D_MODEL), of_layer, pipeline_mode=pl.Buffered(1))],
        out_specs=pl.BlockSpec((tm, D_MODEL), row),
        out_shape=jax.ShapeDtypeStruct((n, D_MODEL), F32),
        scratch_shapes=[pltpu.VMEM((tm, D_MIX), BF16)],
        compiler_params=pltpu.CompilerParams(dimension_semantics=("parallel",), vmem_limit_bytes=VMEM_LIMIT),
        name="out_proj",
    )(xf, oa, oh, qm, z, km, vm, gq, w)


def kernel(x, mem, positions, norm_g, w_in, w_out, moba_q_norm, moba_k_norm, hgrn_lb_logits, hgrn_o_norm,
           mem_norm_g, w_mem_kv, mem_q_norm, mem_k_norm):
    batch, seq, d_model = x.shape
    mem_len = mem.shape[1]
    depth = w_in.shape[0]
    assert d_model == D_MODEL and seq % ROW_TILE == 0 and seq % OUT_TILE == 0 and mem_len % 8 == 0
    n = batch * seq
    xf = x.reshape(n, d_model)
    memf = mem.reshape(batch * mem_len, d_model)
    cosf, sinf = _rope_tables(positions.reshape(1, n))
    per_head = lambda p, reps: jnp.tile(p, (1, reps))[:, None, :]
    norm_g, mem_norm_g, mem_q_norm = norm_g[:, None, :], mem_norm_g[:, None, :], mem_q_norm[:, None, :]
    gq, gk = per_head(moba_q_norm, GROUP // MOBA_D), per_head(moba_k_norm, GROUP // MOBA_D)
    g_hgrn, g_memk = per_head(hgrn_o_norm, HGRN_HEADS), per_head(mem_k_norm, MEM_HEADS)
    for l in range(depth):
        qa, ka, va, qh, fh, ih, qm, z = _in_proj(xf, norm_g, w_in, cosf, sinf, gq, gk, l)
        oa = _moba(qa, ka, va, batch, seq)
        oh = _hgrn(qh, fh, ih, hgrn_lb_logits, g_hgrn, l, batch, seq)
        km, vm = _mem_kv(memf, mem_norm_g, w_mem_kv, g_memk, mem_len, l)
        xf = _out_proj(xf, oa, oh, qm, z, km, vm, mem_q_norm, w_out, seq, mem_len, l)
    return xf.reshape(batch, seq, d_model)
```

```python
import functools

import numpy as np
import jax
import jax.numpy as jnp
from jax import lax
from jax.experimental import pallas as pl
from jax.experimental.pallas import tpu as pltpu

F32 = jnp.float32
BF16 = jnp.bfloat16

D_MODEL = 1024
GROUP = D_MODEL // 2
D_MIX = 3 * GROUP
IN_COLS = 7 * GROUP + D_MIX
MOBA_D = 64
MOBA_BLOCK = 256
MOBA_TOPK = 3
MOBA_VT_ROWS = MOBA_D + 16
HGRN_D = 128
HGRN_HEADS = GROUP // HGRN_D
HGRN_CHUNK = 64
HGRN_LEVELS = 6
HGRN_MM_LEVELS = 3
LOG2E = 1.4426950408889634
MEM_D = 128
MEM_HEADS = GROUP // MEM_D
ROPE_THETA = 500000.0
ROPE_DIM = MOBA_D // 4
ROPE_HALF = ROPE_DIM // 2
NORM_EPS = 1e-6
LANES = 128
ROW_TILE = 512
OUT_TILE = 1024
HGRN_TILE = 1024
MASKED = -1e30
VMEM_LIMIT = 48 * 1024 * 1024


def _nt(a, b):
    return lax.dot_general(a, b, (((1,), (1,)), ((), ())), preferred_element_type=F32)


def _tn(a, b):
    return lax.dot_general(a, b, (((0,), (0,)), ((), ())), preferred_element_type=F32)


def _nn(a, b):
    return jnp.dot(a, b, preferred_element_type=F32)


def _split2(x):
    hi = x.astype(BF16)
    lo = (x - hi.astype(F32)).astype(BF16)
    return hi, lo


def _split3(x):
    hi = x.astype(BF16)
    r = x - hi.astype(F32)
    mid = r.astype(BF16)
    lo = (r - mid.astype(F32)).astype(BF16)
    return hi, mid, lo


def _sigmoid(x):
    return 1.0 / (1.0 + jnp.exp(-x))


def _silu(x):
    return x * _sigmoid(x)


def _rope_table_kernel(pos_ref, invf_ref, expand_ref, base_ref, cos_ref, sin_ref):
    ang = invf_ref[...] * pos_ref[...].astype(F32)
    tm = ang.shape[1]
    table = jnp.concatenate([jnp.cos(ang), jnp.sin(ang), jnp.zeros((LANES - ROPE_DIM, tm), F32)], axis=0)
    hi, mid, lo = _split3(table.T)
    e = expand_ref[...]
    both = _nn(hi, e) + _nn(mid, e) + _nn(lo, e)
    cos_ref[...] = both[:, 0:LANES] + base_ref[...]
    sin_ref[...] = both[:, LANES:2 * LANES]


def _rope_tables(pos):
    n = pos.shape[1]
    tm = min(n, 2048)
    lane = np.arange(LANES) % MOBA_D
    freq = lane % ROPE_HALF
    rope = lane < ROPE_DIM
    sgn = np.where(lane < ROPE_HALF, -1.0, 1.0)
    expand = np.zeros((LANES, 2 * LANES), np.float32)
    for l in np.nonzero(rope)[0]:
        expand[freq[l], l] = 1.0
        expand[ROPE_HALF + freq[l], LANES + l] = sgn[l]
    base = np.where(rope, 0.0, 1.0).astype(np.float32)[None, :]
    invf = (ROPE_THETA ** (-np.arange(ROPE_HALF, dtype=np.float64) / ROPE_HALF)).astype(np.float32)[:, None]
    const = lambda i: (0, 0)
    return pl.pallas_call(
        _rope_table_kernel,
        grid=(n // tm,),
        in_specs=[pl.BlockSpec((1, tm), lambda i: (0, i)),
                  pl.BlockSpec((ROPE_HALF, 1), const),
                  pl.BlockSpec((LANES, 2 * LANES), const),
                  pl.BlockSpec((1, LANES), const)],
        out_specs=[pl.BlockSpec((tm, LANES), lambda i: (i, 0))] * 2,
        out_shape=[jax.ShapeDtypeStruct((n, LANES), F32)] * 2,
        name="rope_tables",
    )(pos, jnp.asarray(invf), jnp.asarray(expand, BF16), jnp.asarray(base))


def _in_proj_kernel(x_ref, g_ref, w_ref, cos_ref, sin_ref, gq_ref, gk_ref,
                    qa_ref, ka_ref, va_ref, qh_ref, fh_ref, ih_ref, qm_ref, z_ref):
    x = x_ref[...]
    ms = jnp.mean(x * x, axis=-1, keepdims=True)
    hb = (x * lax.rsqrt(ms + NORM_EPS) * g_ref[...]).astype(BF16)

    def proj(c0, width):
        return _nn(hb, w_ref[:, c0:c0 + width].astype(BF16))

    cosf = cos_ref[...]
    sinf = sin_ref[...]
    lane = lax.broadcasted_iota(jnp.int32, (1, LANES), 1) % MOBA_D
    lo_lane = lane < ROPE_HALF
    first = lax.broadcasted_iota(jnp.int32, (1, LANES), 1) < MOBA_D

    def head_norm_rope(p, gain, out_ref):
        for v in range(GROUP // LANES):
            xv = p[:, v * LANES:(v + 1) * LANES]
            sq = xv * xv
            s_first = jnp.sum(jnp.where(first, sq, 0.0), axis=1, keepdims=True)
            s_both = jnp.sum(sq, axis=1, keepdims=True)
            ss = jnp.where(first, s_first, s_both - s_first) * (1.0 / MOBA_D)
            xv = xv * lax.rsqrt(ss + NORM_EPS) * gain[:, v * LANES:(v + 1) * LANES]
            rot = jnp.where(lo_lane, pltpu.roll(xv, LANES - ROPE_HALF, 1), pltpu.roll(xv, ROPE_HALF, 1))
            out_ref[:, v * LANES:(v + 1) * LANES] = (xv * cosf + rot * sinf).astype(out_ref.dtype)

    head_norm_rope(proj(0, GROUP), gq_ref[...], qa_ref)
    head_norm_rope(proj(GROUP, GROUP), gk_ref[...], ka_ref)
    va_ref[...] = proj(2 * GROUP, GROUP).astype(BF16)
    qh_ref[...] = proj(3 * GROUP, GROUP).astype(BF16)
    fh_ref[...] = proj(4 * GROUP, GROUP)
    ih_ref[...] = proj(5 * GROUP, GROUP).astype(BF16)
    qm_ref[...] = proj(6 * GROUP, GROUP).astype(BF16)
    for c in range(3):
        z_ref[:, c * GROUP:(c + 1) * GROUP] = proj((7 + c) * GROUP, GROUP).astype(BF16)


def _in_proj(xf, g, w, cosf, sinf, gq, gk, layer):
    n = xf.shape[0]
    tm = ROW_TILE
    row = lambda i: (i, 0)
    of_layer = lambda i: (layer, 0, 0)
    outs = ((GROUP, F32), (GROUP, BF16), (GROUP, BF16), (GROUP, BF16), (GROUP, F32), (GROUP, BF16),
            (GROUP, BF16), (D_MIX, BF16))
    return pl.pallas_call(
        _in_proj_kernel,
        grid=(n // tm,),
        in_specs=[pl.BlockSpec((tm, D_MODEL), row),
                  pl.BlockSpec((None, 1, D_MODEL), of_layer),
                  pl.BlockSpec((None, D_MODEL, IN_COLS), of_layer, pipeline_mode=pl.Buffered(1)),
                  pl.BlockSpec((tm, LANES), row),
                  pl.BlockSpec((tm, LANES), row),
                  pl.BlockSpec((None, 1, GROUP), of_layer),
                  pl.BlockSpec((None, 1, GROUP), of_layer)],
        out_specs=[pl.BlockSpec((tm, w), row) for w, _ in outs],
        out_shape=[jax.ShapeDtypeStruct((n, w), dt) for w, dt in outs],
        compiler_params=pltpu.CompilerParams(dimension_semantics=("parallel",), vmem_limit_bytes=VMEM_LIMIT),
        name="in_proj",
    )(xf, g, w, cosf, sinf, gq, gk)


def _moba_kernel(q_ref, k_ref, v_ref, o_ref, kmean_ref, bias_ref, vt_ref, qw_ref, s_ref, *, nb):
    blk = MOBA_BLOCK
    seq = nb * blk
    lane = lax.broadcasted_iota(jnp.int32, (1, LANES), 1)
    heads = [(lane >= h * MOBA_D) & (lane < (h + 1) * MOBA_D) for h in range(2)]

    ones = jnp.ones((MOBA_VT_ROWS - MOBA_D, blk), F32)
    for j in range(nb):
        rows = slice(j * blk, (j + 1) * blk)
        kmean_ref[j:j + 1, :] = jnp.mean(k_ref[rows, :].astype(F32), axis=0, keepdims=True)
        v_t = v_ref[rows, :].astype(F32).T
        vth = [jnp.concatenate([v_t[h * MOBA_D:(h + 1) * MOBA_D], ones], axis=0).astype(BF16) for h in range(2)]
        zvt = jnp.zeros_like(vth[0])
        vt_ref[j] = jnp.concatenate([jnp.concatenate([vth[0], zvt], axis=1),
                                     jnp.concatenate([zvt, vth[1]], axis=1)], axis=0)

    km = kmean_ref[...]
    q_hi, q_lo = _split2(q_ref[...])
    km_parts = [_split2(jnp.where(heads[h], km, 0.0)) for h in range(2)]
    g_hi = _nt(jnp.concatenate([km_parts[0][0], km_parts[0][1], km_parts[1][0], km_parts[1][1]], axis=0), q_hi)
    g_lo = _nt(jnp.concatenate([km_parts[0][0], km_parts[1][0]], axis=0), q_lo)
    blk_row = lax.broadcasted_iota(jnp.int32, (nb, 1), 0)
    q_blk = lax.broadcasted_iota(jnp.int32, (1, seq), 1) // blk
    past = blk_row < q_blk
    for h in range(2):
        gate_t = g_hi[2 * h * nb:(2 * h + 1) * nb] + g_hi[(2 * h + 1) * nb:(2 * h + 2) * nb] + g_lo[h * nb:(h + 1) * nb]
        gate_t = jnp.where(past, gate_t, -jnp.inf)
        cnt = jnp.zeros((nb, seq), F32)
        for n in range(nb - 1):
            gn = gate_t[n:n + 1, :]
            before = (gn > gate_t) | ((gn == gate_t) & (n < blk_row))
            cnt = cnt + jnp.where(before, 1.0, 0.0)
        sel = (past & (cnt < float(MOBA_TOPK))) | (blk_row == q_blk)
        bias_ref[h] = jnp.where(sel, 0.0, MASKED)
    for c in range(nb):
        qs = q_ref[c * blk:(c + 1) * blk, :] * (MOBA_D ** -0.5 * LOG2E)
        qw_ref[c] = jnp.concatenate([jnp.where(heads[h], qs, 0.0) for h in range(2)], axis=0).astype(BF16)

    kpos = lax.broadcasted_iota(jnp.int32, (blk, blk), 0)
    qpos = lax.broadcasted_iota(jnp.int32, (blk, blk), 1)
    causal_t = kpos <= qpos

    starts = [sum(cc + 1 for cc in range(c)) * blk for c in range(nb)]
    for j in range(nb):
        qw_all = jnp.concatenate([qw_ref[c] for c in range(j, nb)], axis=0)
        s_all = _nt(k_ref[j * blk:(j + 1) * blk, :], qw_all)
        for c in range(j, nb):
            for h in range(2):
                col = (2 * (c - j) + h) * blk
                s = s_all[:, col:col + blk] + bias_ref[h, j:j + 1, c * blk:(c + 1) * blk]
                if j == c:
                    s = jnp.where(causal_t, s, MASKED)
                s_ref[h, starts[c] + j * blk:starts[c] + (j + 1) * blk, :] = s
    for c in range(nb):
        base = starts[c]
        ms = []
        for h in range(2):
            mx = None
            for j in range(c + 1):
                s = s_ref[h, base + j * blk:base + (j + 1) * blk, :]
                mx = s if mx is None else jnp.maximum(mx, s)
            ms.append(jnp.max(mx, axis=0, keepdims=True))
        acc = None
        for j in range(c + 1):
            tile = slice(base + j * blk, base + (j + 1) * blk)
            p = jnp.concatenate([jnp.exp2(s_ref[h, tile, :] - ms[h]).astype(BF16) for h in range(2)], axis=0)
            pv = _nn(vt_ref[j], p)
            acc = pv if acc is None else acc + pv
        outs = [acc[h * MOBA_VT_ROWS:h * MOBA_VT_ROWS + MOBA_D]
                / acc[h * MOBA_VT_ROWS + MOBA_D:h * MOBA_VT_ROWS + MOBA_D + 1] for h in range(2)]
        o_ref[c * blk:(c + 1) * blk, :] = jnp.concatenate(outs, axis=0).T.astype(BF16)


def _moba(qa, ka, va, batch, seq):
    nb = seq // MOBA_BLOCK
    assert nb % 2 == 0
    pairs = GROUP // LANES
    whole = pl.BlockSpec((seq, LANES), lambda b, p: (b, p))
    return pl.pallas_call(
        functools.partial(_moba_kernel, nb=nb),
        grid=(batch, pairs),
        in_specs=[whole, whole, whole],
        out_specs=whole,
        out_shape=jax.ShapeDtypeStruct(qa.shape, BF16),
        scratch_shapes=[pltpu.VMEM((nb, LANES), F32),
                        pltpu.VMEM((2, nb, seq), F32),
                        pltpu.VMEM((nb, 2 * MOBA_VT_ROWS, 2 * MOBA_BLOCK), BF16),
                        pltpu.VMEM((nb, 2 * MOBA_BLOCK, LANES), BF16),
                        pltpu.VMEM((2, nb * (nb + 1) // 2 * MOBA_BLOCK, MOBA_BLOCK), F32)],
        compiler_params=pltpu.CompilerParams(dimension_semantics=("parallel", "parallel"),
                                             vmem_limit_bytes=VMEM_LIMIT),
        name="moba",
    )(qa, ka, va)


def _hgrn_consts():
    c = HGRN_CHUNK
    t = np.arange(c)
    w_rows, masks = [], []
    for lvl in range(HGRN_LEVELS):
        m = 1 << lvl
        blk = t // m
        odd = (blk % 2) == 1
        start = blk * m
        end = start + m - 1
        u = t[None, :]
        w_odd = (u >= start[:, None]) & (u <= t[:, None])
        w_even = (u > t[:, None]) & (u <= end[:, None])
        w_rows.append(np.where(odd[:, None], w_odd, w_even))
        masks.append(odd[:, None] & (blk[None, :] == blk[:, None] - 1))
    masks.append(t[None, :] == t[:, None])
    w_mm = w_rows[:HGRN_MM_LEVELS] + [t[None, :] <= t[:, None]]
    w_all = np.concatenate(w_mm, axis=0).astype(np.float32)
    w_all = np.concatenate([w_all, w_all], axis=1)
    mask_cat = np.concatenate([masks[-1]] + masks[:-1] + [np.zeros((c, c), bool)], axis=1).astype(np.float32)
    return jnp.asarray(w_all, BF16), jnp.asarray(mask_cat, BF16)


def _hgrn_kernel(qh_ref, fh_ref, ih_ref, logit_ref, gain_ref, w_ref, mask_ref, o_ref, state_ref, *, layer):
    c = HGRN_CHUNK
    state_ref[...] = jnp.zeros_like(state_ref)

    if layer > 0:
        lg = logit_ref[...]
        e = jnp.exp(lg - jnp.max(lg, axis=0, keepdims=True))
        sm = e / jnp.sum(e, axis=0, keepdims=True)
        lb = jnp.sum(sm[1:layer + 1, :], axis=0, keepdims=True)
        log_lb = jnp.log(lb) * LOG2E
        log_1m_lb = jnp.log1p(-lb) * LOG2E
    w_all = w_ref[...]
    gain = gain_ref[...]

    def prepare(base, ci):
        rows = pl.ds(base + ci * c, c)
        fl = fh_ref[rows, :]
        vi = ih_ref[rows, :]
        qh = qh_ref[rows, :].astype(F32)
        qf = qh / (1.0 + jnp.exp2(qh * -LOG2E))
        e = jnp.exp2(jnp.abs(fl) * -LOG2E)
        r = 1.0 / (1.0 + e)
        log_sig = jnp.minimum(fl, 0.0) * LOG2E - jnp.log2(1.0 + e)
        sig_neg = jnp.where(fl >= 0.0, e * r, r)
        if layer > 0:
            b = log_1m_lb + log_sig
            lf = jnp.maximum(log_lb, b) + jnp.log2(1.0 + jnp.exp2(-jnp.abs(log_lb - b)))
            kf = (1.0 - lb) * sig_neg
        else:
            lf = log_sig
            kf = sig_neg

        f1, f2 = _split2(lf)
        x_mm = _nn(w_all, jnp.concatenate([f1, f2], axis=0))
        a2 = x_mm[HGRN_MM_LEVELS * c:(HGRN_MM_LEVELS + 1) * c, :]
        e_lvl = [jnp.exp2(x_mm[lvl * c:(lvl + 1) * c, :]) for lvl in range(HGRN_MM_LEVELS)]
        for lvl in range(HGRN_MM_LEVELS, HGRN_LEVELS):
            m = 1 << lvl
            parts = []
            for g in range(c // (2 * m)):
                ref = a2[2 * m * g + m - 1:2 * m * g + m, :]
                parts.append(-jnp.abs(a2[2 * m * g:2 * m * (g + 1), :] - ref))
            e_lvl.append(jnp.exp2(jnp.concatenate(parts, axis=0)))
        e_cum = jnp.exp2(a2)
        e_rest = jnp.exp2(a2[c - 1:c, :] - a2)
        e_end = e_cum[c - 1:c, :]
        qf_b, kf_b = qf.astype(BF16), kf.astype(BF16)
        e_b = [e.astype(BF16) for e in e_lvl]
        qb = [qf_b] + [qf_b * e for e in e_b]
        kb = [kf_b] + [kf_b * e for e in e_b]
        return qb, kb, qf_b * e_cum.astype(BF16), kf_b * e_rest.astype(BF16), vi, e_end

    def recur(base, ci, prepared):
        rows = pl.ds(base + ci * c, c)
        qb_all, kb_all, ql_all, kdec_all, vi, e_end = prepared
        zero = jnp.zeros((c, HGRN_D), BF16)
        zsq = jnp.zeros((HGRN_D, HGRN_D), BF16)
        for hp in range(HGRN_HEADS // 2):
            p_pair, ql_pair, v_pair, st_pair = [], [], [], []
            for h in (2 * hp, 2 * hp + 1):
                cols = slice(h * HGRN_D, (h + 1) * HGRN_D)
                qb = [q[:, cols] for q in qb_all]
                kb = [k[:, cols] for k in kb_all]
                s_parts = []
                for a in range(0, HGRN_LEVELS, 2):
                    q_cat = jnp.concatenate([qb[a], qb[a + 1]], axis=1)
                    k_blk = jnp.concatenate([jnp.concatenate([kb[a], zero], axis=1),
                                             jnp.concatenate([zero, kb[a + 1]], axis=1)], axis=0)
                    s_parts.append(_nt(q_cat, k_blk))
                s_parts.append(_nt(qb[HGRN_LEVELS], jnp.concatenate([kb[HGRN_LEVELS], zero], axis=0)))
                p = None
                for a, part in enumerate(s_parts):
                    part = part.astype(BF16) * mask_ref[:, 2 * a * c:(2 * a + 2) * c]
                    p = part if p is None else p + part
                p_pair.append(p)
                ql_pair.append(ql_all[:, cols])
                v_pair.append(vi[:, cols])
                st_pair.append(state_ref[:, cols])
            vv = [jnp.concatenate([v, v], axis=0) for v in v_pair]
            zv = jnp.zeros_like(vv[0])
            v_blk = jnp.concatenate([jnp.concatenate([vv[0], zv], axis=1),
                                     jnp.concatenate([zv, vv[1]], axis=1)], axis=0)
            o_pair = _nn(jnp.concatenate(p_pair, axis=1), v_blk)
            st_b = [st.astype(BF16) for st in st_pair]
            st_blk = jnp.concatenate([jnp.concatenate([st_b[0], zsq], axis=1),
                                      jnp.concatenate([zsq, st_b[1]], axis=1)], axis=0)
            o_pair = o_pair + _nt(jnp.concatenate(ql_pair, axis=1), st_blk)
            for i, h in enumerate((2 * hp, 2 * hp + 1)):
                cols = slice(h * HGRN_D, (h + 1) * HGRN_D)
                o = o_pair[:, i * HGRN_D:(i + 1) * HGRN_D]
                state_ref[:, cols] = st_pair[i] * e_end[:, cols] + _tn(v_pair[i], kdec_all[:, cols])
                ms = jnp.mean(o * o, axis=-1, keepdims=True)
                o_ref[rows, cols] = (o * lax.rsqrt(ms + NORM_EPS) * gain[:, cols]).astype(BF16)

    n_chunks = HGRN_TILE // c

    def group(g, carry):
        base = pl.multiple_of(g * HGRN_TILE, HGRN_TILE)
        prepared = prepare(base, 0)
        for ci in range(n_chunks):
            nxt = prepare(base, ci + 1) if ci + 1 < n_chunks else None
            recur(base, ci, prepared)
            prepared = nxt
        return carry

    lax.fori_loop(0, qh_ref.shape[0] // HGRN_TILE, group, 0)


def _hgrn(qh, fh, ih, logits, gain, layer, batch, seq):
    w_all, mask_all = _hgrn_consts()
    assert seq % HGRN_TILE == 0
    depth = logits.shape[0]
    const = lambda b: (0, 0)
    return pl.pallas_call(
        functools.partial(_hgrn_kernel, layer=layer),
        grid=(batch,),
        in_specs=[pl.BlockSpec((seq, GROUP), lambda b: (b, 0))] * 3 + [
                  pl.BlockSpec((depth, GROUP), const),
                  pl.BlockSpec((None, 1, GROUP), lambda b: (layer, 0, 0)),
                  pl.BlockSpec(w_all.shape, const),
                  pl.BlockSpec(mask_all.shape, const)],
        out_specs=pl.BlockSpec((seq, GROUP), lambda b: (b, 0)),
        out_shape=jax.ShapeDtypeStruct((fh.shape[0], GROUP), BF16),
        scratch_shapes=[pltpu.VMEM((HGRN_D, GROUP), F32)],
        compiler_params=pltpu.CompilerParams(dimension_semantics=("parallel",),
                                             vmem_limit_bytes=VMEM_LIMIT),
        name="hgrn",
    )(qh, fh, ih, logits, gain, w_all, mask_all)


def _mem_kv_kernel(m_ref, g_ref, w_ref, gk_ref, seg_ref, k_ref, v_ref):
    x = m_ref[...]
    ms = jnp.mean(x * x, axis=-1, keepdims=True)
    hb = (x * lax.rsqrt(ms + NORM_EPS) * g_ref[...]).astype(BF16)
    gk = gk_ref[...]
    seg = seg_ref[...]
    for c in range(GROUP // 256):
        kc = _nn(hb, w_ref[:, c * 256:(c + 1) * 256].astype(BF16))
        ss = _nn((kc * kc).astype(BF16), seg) * (1.0 / MEM_D)
        kn = kc * lax.rsqrt(ss + NORM_EPS) * gk[:, c * 256:(c + 1) * 256]
        k_ref[:, c * 256:(c + 1) * 256] = kn.astype(BF16)
    v_ref[...] = _nn(hb, w_ref[:, GROUP:2 * GROUP].astype(BF16)).astype(BF16)


def _mem_kv(memf, g, w, gk, mem_len, layer):
    n = memf.shape[0]
    seg = np.arange(256)[:, None] // MEM_D == np.arange(256)[None, :] // MEM_D
    seg = jnp.asarray(seg, BF16)
    const = lambda i: (0, 0)
    row = lambda i: (i, 0)
    of_layer = lambda i: (layer, 0, 0)
    return pl.pallas_call(
        _mem_kv_kernel,
        grid=(n // mem_len,),
        in_specs=[pl.BlockSpec((mem_len, D_MODEL), row),
                  pl.BlockSpec((None, 1, D_MODEL), of_layer),
                  pl.BlockSpec((None, D_MODEL, 2 * GROUP), of_layer),
                  pl.BlockSpec((None, 1, GROUP), of_layer),
                  pl.BlockSpec((256, 256), const)],
        out_specs=[pl.BlockSpec((mem_len, GROUP), row)] * 2,
        out_shape=[jax.ShapeDtypeStruct((n, GROUP), BF16)] * 2,
        compiler_params=pltpu.CompilerParams(dimension_semantics=("parallel",), vmem_limit_bytes=VMEM_LIMIT),
        name="mem_kv",
    )(memf, g, w, gk, seg)


def _out_proj_kernel(x_ref, oa_ref, oh_ref, qm_ref, z_ref, km_ref, vm_ref, gq_ref, w_ref, o_ref, y_ref):
    def gate(o, c0, c1):
        y_ref[:, c0:c1] = (o * _silu(z_ref[:, c0:c1].astype(F32))).astype(BF16)

    gate(oa_ref[...].astype(F32), 0, GROUP)
    gate(oh_ref[...].astype(F32), GROUP, 2 * GROUP)
    gq = gq_ref[...]
    for h in range(MEM_HEADS):
        cols = slice(h * MEM_D, (h + 1) * MEM_D)
        q = qm_ref[:, cols].astype(F32)
        ms = jnp.mean(q * q, axis=-1, keepdims=True)
        qn = q * lax.rsqrt(ms + NORM_EPS) * gq * (MEM_D ** -0.5)
        s = _nt(qn.astype(BF16), km_ref[:, cols])
        p = jnp.exp(s - jnp.max(s, axis=1, keepdims=True))
        v_aug = jnp.concatenate([vm_ref[:, cols], jnp.ones((vm_ref.shape[0], MEM_D), BF16)], axis=1)
        pv = _nn(p.astype(BF16), v_aug)
        om = pv[:, 0:MEM_D] / pv[:, MEM_D:2 * MEM_D]
        c0 = 2 * GROUP + h * MEM_D
        gate(om, c0, c0 + MEM_D)
    o_ref[...] = x_ref[...] + _nn(y_ref[...], w_ref[...].astype(BF16))


def _out_proj(xf, oa, oh, qm, z, km, vm, gq, w, seq, mem_len, layer):
    n = xf.shape[0]
    tm = OUT_TILE
    steps = seq // tm
    const = lambda i: (0, 0)
    row = lambda i: (i, 0)
    per_batch = lambda i: (i // steps, 0)
    of_layer = lambda i: (layer, 0, 0)
    return pl.pallas_call(
        _out_proj_kernel,
        grid=(n // tm,),
        in_specs=[pl.BlockSpec((tm, D_MODEL), row),
                  pl.BlockSpec((tm, GROUP), row),
                  pl.BlockSpec((tm, GROUP), row),
                  pl.BlockSpec((tm, GROUP), row),
                  pl.BlockSpec((tm, D_MIX), row),
                  pl.BlockSpec((mem_len, GROUP), per_batch),
                  pl.BlockSpec((mem_len, GROUP), per_batch),
                  pl.BlockSpec((None, 1, MEM_D), of_layer),
                  pl.BlockSpec((None, D_MIX, D_MODEL), of_layer, pipeline_mode=pl.Buffered(1))],
        out_specs=pl.BlockSpec((tm, D_MODEL), row),
        out_shape=jax.ShapeDtypeStruct((n, D_MODEL), F32),
        scratch_shapes=[pltpu.VMEM((tm, D_MIX), BF16)],
        compiler_params=pltpu.CompilerParams(dimension_semantics=("parallel",), vmem_limit_bytes=VMEM_LIMIT),
        name="out_proj",
    )(xf, oa, oh, qm, z, km, vm, gq, w)


def kernel(x, mem, positions, norm_g, w_in, w_out, moba_q_norm, moba_k_norm, hgrn_lb_logits, hgrn_o_norm,
           mem_norm_g, w_mem_kv, mem_q_norm, mem_k_norm):
    batch, seq, d_model = x.shape
    mem_len = mem.shape[1]
    depth = w_in.shape[0]
    assert d_model == D_MODEL and seq % ROW_TILE == 0 and seq % OUT_TILE == 0 and mem_len % 8 == 0
    n = batch * seq
    xf = x.reshape(n, d_model)
    memf = mem.reshape(batch * mem_len, d_model)
    cosf, sinf = _rope_tables(positions.reshape(1, n))
    per_head = lambda p, reps: jnp.tile(p, (1, reps))[:, None, :]
    norm_g, mem_norm_g, mem_q_norm = norm_g[:, None, :], mem_norm_g[:, None, :], mem_q_norm[:, None, :]
    gq, gk = per_head(moba_q_norm, GROUP // MOBA_D), per_head(moba_k_norm, GROUP // MOBA_D)
    g_hgrn, g_memk = per_head(hgrn_o_norm, HGRN_HEADS), per_head(mem_k_norm, MEM_HEADS)
    for l in range(depth):
        qa, ka, va, qh, fh, ih, qm, z = _in_proj(xf, norm_g, w_in, cosf, sinf, gq, gk, l)
        oa = _moba(qa, ka, va, batch, seq)
        oh = _hgrn(qh, fh, ih, hgrn_lb_logits, g_hgrn, l, batch, seq)
        km, vm = _mem_kv(memf, mem_norm_g, w_mem_kv, g_memk, mem_len, l)
        xf = _out_proj(xf, oa, oh, qm, z, km, vm, mem_q_norm, w_out, seq, mem_len, l)
    return xf.reshape(batch, seq, d_model)
```

```python
import functools

import numpy as np
import jax
import jax.numpy as jnp
from jax import lax
from jax.experimental import pallas as pl
from jax.experimental.pallas import tpu as pltpu

F32 = jnp.float32
BF16 = jnp.bfloat16

D_MODEL = 1024
GROUP = D_MODEL // 2
D_MIX = 3 * GROUP
IN_COLS = 7 * GROUP + D_MIX
MOBA_D = 64
MOBA_BLOCK = 256
MOBA_TOPK = 3
MOBA_VT_ROWS = MOBA_D + 16
HGRN_D = 128
HGRN_HEADS = GROUP // HGRN_D
HGRN_CHUNK = 64
HGRN_LEVELS = 6
HGRN_MM_LEVELS = 3
LOG2E = 1.4426950408889634
MEM_D = 128
MEM_HEADS = GROUP // MEM_D
ROPE_THETA = 500000.0
ROPE_DIM = MOBA_D // 4
ROPE_HALF = ROPE_DIM // 2
NORM_EPS = 1e-6
LANES = 128
ROW_TILE = 512
OUT_TILE = 1024
HGRN_TILE = 1024
MASKED = -1e30
VMEM_LIMIT = 48 * 1024 * 1024


def _nt(a, b):
    return lax.dot_general(a, b, (((1,), (1,)), ((), ())), preferred_element_type=F32)


def _tn(a, b):
    return lax.dot_general(a, b, (((0,), (0,)), ((), ())), preferred_element_type=F32)


def _nn(a, b):
    return jnp.dot(a, b, preferred_element_type=F32)


def _split2(x):
    hi = x.astype(BF16)
    lo = (x - hi.astype(F32)).astype(BF16)
    return hi, lo


def _split3(x):
    hi = x.astype(BF16)
    r = x - hi.astype(F32)
    mid = r.astype(BF16)
    lo = (r - mid.astype(F32)).astype(BF16)
    return hi, mid, lo


def _sigmoid(x):
    return 1.0 / (1.0 + jnp.exp(-x))


def _silu(x):
    return x * _sigmoid(x)


def _rope_table_kernel(pos_ref, invf_ref, expand_ref, base_ref, cos_ref, sin_ref):
    ang = invf_ref[...] * pos_ref[...].astype(F32)
    tm = ang.shape[1]
    table = jnp.concatenate([jnp.cos(ang), jnp.sin(ang), jnp.zeros((LANES - ROPE_DIM, tm), F32)], axis=0)
    hi, mid, lo = _split3(table.T)
    e = expand_ref[...]
    both = _nn(hi, e) + _nn(mid, e) + _nn(lo, e)
    cos_ref[...] = both[:, 0:LANES] + base_ref[...]
    sin_ref[...] = both[:, LANES:2 * LANES]


def _rope_tables(pos):
    n = pos.shape[1]
    tm = min(n, 2048)
    lane = np.arange(LANES) % MOBA_D
    freq = lane % ROPE_HALF
    rope = lane < ROPE_DIM
    sgn = np.where(lane < ROPE_HALF, -1.0, 1.0)
    expand = np.zeros((LANES, 2 * LANES), np.float32)
    for l in np.nonzero(rope)[0]:
        expand[freq[l], l] = 1.0
        expand[ROPE_HALF + freq[l], LANES + l] = sgn[l]
    base = np.where(rope, 0.0, 1.0).astype(np.float32)[None, :]
    invf = (ROPE_THETA ** (-np.arange(ROPE_HALF, dtype=np.float64) / ROPE_HALF)).astype(np.float32)[:, None]
    const = lambda i: (0, 0)
    return pl.pallas_call(
        _rope_table_kernel,
        grid=(n // tm,),
        in_specs=[pl.BlockSpec((1, tm), lambda i: (0, i)),
                  pl.BlockSpec((ROPE_HALF, 1), const),
                  pl.BlockSpec((LANES, 2 * LANES), const),
                  pl.BlockSpec((1, LANES), const)],
        out_specs=[pl.BlockSpec((tm, LANES), lambda i: (i, 0))] * 2,
        out_shape=[jax.ShapeDtypeStruct((n, LANES), F32)] * 2,
        name="rope_tables",
    )(pos, jnp.asarray(invf), jnp.asarray(expand, BF16), jnp.asarray(base))


def _in_proj_kernel(x_ref, g_ref, w_ref, cos_ref, sin_ref, gq_ref, gk_ref,
                    qa_ref, ka_ref, va_ref, qh_ref, fh_ref, ih_ref, qm_ref, z_ref):
    x = x_ref[...]
    ms = jnp.mean(x * x, axis=-1, keepdims=True)
    hb = (x * lax.rsqrt(ms + NORM_EPS) * g_ref[...]).astype(BF16)

    def proj(c0, width):
        return _nn(hb, w_ref[:, c0:c0 + width].astype(BF16))

    cosf = cos_ref[...]
    sinf = sin_ref[...]
    lane = lax.broadcasted_iota(jnp.int32, (1, LANES), 1) % MOBA_D
    lo_lane = lane < ROPE_HALF
    first = lax.broadcasted_iota(jnp.int32, (1, LANES), 1) < MOBA_D

    def head_norm_rope(p, gain, out_ref):
        for v in range(GROUP // LANES):
            xv = p[:, v * LANES:(v + 1) * LANES]
            sq = xv * xv
            s_first = jnp.sum(jnp.where(first, sq, 0.0), axis=1, keepdims=True)
            s_both = jnp.sum(sq, axis=1, keepdims=True)
            ss = jnp.where(first, s_first, s_both - s_first) * (1.0 / MOBA_D)
            xv = xv * lax.rsqrt(ss + NORM_EPS) * gain[:, v * LANES:(v + 1) * LANES]
            rot = jnp.where(lo_lane, pltpu.roll(xv, LANES - ROPE_HALF, 1), pltpu.roll(xv, ROPE_HALF, 1))
            out_ref[:, v * LANES:(v + 1) * LANES] = (xv * cosf + rot * sinf).astype(out_ref.dtype)

    head_norm_rope(proj(0, GROUP), gq_ref[...], qa_ref)
    head_norm_rope(proj(GROUP, GROUP), gk_ref[...], ka_ref)
    va_ref[...] = proj(2 * GROUP, GROUP).astype(BF16)
    qh_ref[...] = proj(3 * GROUP, GROUP).astype(BF16)
    fh_ref[...] = proj(4 * GROUP, GROUP)
    ih_ref[...] = proj(5 * GROUP, GROUP).astype(BF16)
    qm_ref[...] = proj(6 * GROUP, GROUP).astype(BF16)
    for c in range(3):
        z_ref[:, c * GROUP:(c + 1) * GROUP] = proj((7 + c) * GROUP, GROUP).astype(BF16)


def _in_proj(xf, g, w, cosf, sinf, gq, gk, layer):
    n = xf.shape[0]
    tm = ROW_TILE
    row = lambda i: (i, 0)
    of_layer = lambda i: (layer, 0, 0)
    outs = ((GROUP, F32), (GROUP, BF16), (GROUP, BF16), (GROUP, BF16), (GROUP, F32), (GROUP, BF16),
            (GROUP, BF16), (D_MIX, BF16))
    return pl.pallas_call(
        _in_proj_kernel,
        grid=(n // tm,),
        in_specs=[pl.BlockSpec((tm, D_MODEL), row),
                  pl.BlockSpec((None, 1, D_MODEL), of_layer),
                  pl.BlockSpec((None, D_MODEL, IN_COLS), of_layer, pipeline_mode=pl.Buffered(1)),
                  pl.BlockSpec((tm, LANES), row),
                  pl.BlockSpec((tm, LANES), row),
                  pl.BlockSpec((None, 1, GROUP), of_layer),
                  pl.BlockSpec((None, 1, GROUP), of_layer)],
        out_specs=[pl.BlockSpec((tm, w), row) for w, _ in outs],
        out_shape=[jax.ShapeDtypeStruct((n, w), dt) for w, dt in outs],
        compiler_params=pltpu.CompilerParams(dimension_semantics=("parallel",), vmem_limit_bytes=VMEM_LIMIT),
        name="in_proj",
    )(xf, g, w, cosf, sinf, gq, gk)


def _moba_kernel(q_ref, k_ref, v_ref, o_ref, kmean_ref, bias_ref, vt_ref, qw_ref, s_ref, *, nb):
    blk = MOBA_BLOCK
    seq = nb * blk
    lane = lax.broadcasted_iota(jnp.int32, (1, LANES), 1)
    heads = [(lane >= h * MOBA_D) & (lane < (h + 1) * MOBA_D) for h in range(2)]

    ones = jnp.ones((MOBA_VT_ROWS - MOBA_D, blk), F32)
    for j in range(nb):
        rows = slice(j * blk, (j + 1) * blk)
        kmean_ref[j:j + 1, :] = jnp.mean(k_ref[rows, :].astype(F32), axis=0, keepdims=True)
        v_t = v_ref[rows, :].astype(F32).T
        vth = [jnp.concatenate([v_t[h * MOBA_D:(h + 1) * MOBA_D], ones], axis=0).astype(BF16) for h in range(2)]
        zvt = jnp.zeros_like(vth[0])
        vt_ref[j] = jnp.concatenate([jnp.concatenate([vth[0], zvt], axis=1),
                                     jnp.concatenate([zvt, vth[1]], axis=1)], axis=0)

    km = kmean_ref[...]
    q_hi, q_lo = _split2(q_ref[...])
    km_parts = [_split2(jnp.where(heads[h], km, 0.0)) for h in range(2)]
    g_hi = _nt(jnp.concatenate([km_parts[0][0], km_parts[0][1], km_parts[1][0], km_parts[1][1]], axis=0), q_hi)
    g_lo = _nt(jnp.concatenate([km_parts[0][0], km_parts[1][0]], axis=0), q_lo)
    blk_row = lax.broadcasted_iota(jnp.int32, (nb, 1), 0)
    q_blk = lax.broadcasted_iota(jnp.int32, (1, seq), 1) // blk
    past = blk_row < q_blk
    for h in range(2):
        gate_t = g_hi[2 * h * nb:(2 * h + 1) * nb] + g_hi[(2 * h + 1) * nb:(2 * h + 2) * nb] + g_lo[h * nb:(h + 1) * nb]
        gate_t = jnp.where(past, gate_t, -jnp.inf)
        cnt = jnp.zeros((nb, seq), F32)
        for n in range(nb - 1):
            gn = gate_t[n:n + 1, :]
            before = (gn > gate_t) | ((gn == gate_t) & (n < blk_row))
            cnt = cnt + jnp.where(before, 1.0, 0.0)
        sel = (past & (cnt < float(MOBA_TOPK))) | (blk_row == q_blk)
        bias_ref[h] = jnp.where(sel, 0.0, MASKED)
    for c in range(nb):
        qs = q_ref[c * blk:(c + 1) * blk, :] * (MOBA_D ** -0.5 * LOG2E)
        qw_ref[c] = jnp.concatenate([jnp.where(heads[h], qs, 0.0) for h in range(2)], axis=0).astype(BF16)

    kpos = lax.broadcasted_iota(jnp.int32, (blk, blk), 0)
    qpos = lax.broadcasted_iota(jnp.int32, (blk, blk), 1)
    causal_t = kpos <= qpos

    starts = [sum(cc + 1 for cc in range(c)) * blk for c in range(nb)]
    ms = [[None, None] for _ in range(nb)]
    for j in range(nb):
        qw_all = jnp.concatenate([qw_ref[c] for c in range(j, nb)], axis=0)
        s_all = _nt(k_ref[j * blk:(j + 1) * blk, :], qw_all)
        for c in range(j, nb):
            for h in range(2):
                col = (2 * (c - j) + h) * blk
                s = s_all[:, col:col + blk] + bias_ref[h, j:j + 1, c * blk:(c + 1) * blk]
                if j == c:
                    s = jnp.where(causal_t, s, MASKED)
                s_ref[h, starts[c] + j * blk:starts[c] + (j + 1) * blk, :] = s
                tile_max = jnp.max(s, axis=0, keepdims=True)
                ms[c][h] = tile_max if ms[c][h] is None else jnp.maximum(ms[c][h], tile_max)
    for c in range(nb):
        base = starts[c]
        acc = None
        for j in range(c + 1):
            tile = slice(base + j * blk, base + (j + 1) * blk)
            p = jnp.concatenate([jnp.exp2(s_ref[h, tile, :] - ms[c][h]).astype(BF16) for h in range(2)], axis=0)
            pv = _nn(vt_ref[j], p)
            acc = pv if acc is None else acc + pv
        outs = [acc[h * MOBA_VT_ROWS:h * MOBA_VT_ROWS + MOBA_D]
                / acc[h * MOBA_VT_ROWS + MOBA_D:h * MOBA_VT_ROWS + MOBA_D + 1] for h in range(2)]
        o_ref[c * blk:(c + 1) * blk, :] = jnp.concatenate(outs, axis=0).T.astype(BF16)


def _moba(qa, ka, va, batch, seq):
    nb = seq // MOBA_BLOCK
    assert nb % 2 == 0
    pairs = GROUP // LANES
    whole = pl.BlockSpec((seq, LANES), lambda b, p: (b, p))
    return pl.pallas_call(
        functools.partial(_moba_kernel, nb=nb),
        grid=(batch, pairs),
        in_specs=[whole, whole, whole],
        out_specs=whole,
        out_shape=jax.ShapeDtypeStruct(qa.shape, BF16),
        scratch_shapes=[pltpu.VMEM((nb, LANES), F32),
                        pltpu.VMEM((2, nb, seq), F32),
                        pltpu.VMEM((nb, 2 * MOBA_VT_ROWS, 2 * MOBA_BLOCK), BF16),
                        pltpu.VMEM((nb, 2 * MOBA_BLOCK, LANES), BF16),
                        pltpu.VMEM((2, nb * (nb + 1) // 2 * MOBA_BLOCK, MOBA_BLOCK), F32)],
        compiler_params=pltpu.CompilerParams(dimension_semantics=("parallel", "parallel"),
                                             vmem_limit_bytes=VMEM_LIMIT),
        name="moba",
    )(qa, ka, va)


def _hgrn_consts():
    c = HGRN_CHUNK
    t = np.arange(c)
    w_rows, masks = [], []
    for lvl in range(HGRN_LEVELS):
        m = 1 << lvl
        blk = t // m
        odd = (blk % 2) == 1
        start = blk * m
        end = start + m - 1
        u = t[None, :]
        w_odd = (u >= start[:, None]) & (u <= t[:, None])
        w_even = (u > t[:, None]) & (u <= end[:, None])
        w_rows.append(np.where(odd[:, None], w_odd, w_even))
        masks.append(odd[:, None] & (blk[None, :] == blk[:, None] - 1))
    masks.append(t[None, :] == t[:, None])
    w_mm = w_rows[:HGRN_MM_LEVELS] + [t[None, :] <= t[:, None]]
    w_all = np.concatenate(w_mm, axis=0).astype(np.float32)
    w_all = np.concatenate([w_all, w_all], axis=1)
    mask_cat = np.concatenate([masks[-1]] + masks[:-1] + [np.zeros((c, c), bool)], axis=1).astype(np.float32)
    return jnp.asarray(w_all, BF16), jnp.asarray(mask_cat, BF16)


def _hgrn_kernel(qh_ref, fh_ref, ih_ref, logit_ref, gain_ref, w_ref, mask_ref, o_ref, state_ref, *, layer):
    c = HGRN_CHUNK
    state_ref[...] = jnp.zeros_like(state_ref)

    if layer > 0:
        lg = logit_ref[...]
        e = jnp.exp(lg - jnp.max(lg, axis=0, keepdims=True))
        sm = e / jnp.sum(e, axis=0, keepdims=True)
        lb = jnp.sum(sm[1:layer + 1, :], axis=0, keepdims=True)
        log_lb = jnp.log(lb) * LOG2E
        log_1m_lb = jnp.log1p(-lb) * LOG2E
    w_all = w_ref[...]
    gain = gain_ref[...]

    def prepare(base, ci):
        rows = pl.ds(base + ci * c, c)
        fl = fh_ref[rows, :]
        vi = ih_ref[rows, :]
        qh = qh_ref[rows, :].astype(F32)
        qf = qh / (1.0 + jnp.exp2(qh * -LOG2E))
        e = jnp.exp2(jnp.abs(fl) * -LOG2E)
        r = 1.0 / (1.0 + e)
        log_sig = jnp.minimum(fl, 0.0) * LOG2E - jnp.log2(1.0 + e)
        sig_neg = jnp.where(fl >= 0.0, e * r, r)
        if layer > 0:
            b = log_1m_lb + log_sig
            lf = jnp.maximum(log_lb, b) + jnp.log2(1.0 + jnp.exp2(-jnp.abs(log_lb - b)))
            kf = (1.0 - lb) * sig_neg
        else:
            lf = log_sig
            kf = sig_neg

        f1, f2 = _split2(lf)
        x_mm = _nn(w_all, jnp.concatenate([f1, f2], axis=0))
        a2 = x_mm[HGRN_MM_LEVELS * c:(HGRN_MM_LEVELS + 1) * c, :]
        e_lvl = [jnp.exp2(x_mm[lvl * c:(lvl + 1) * c, :]) for lvl in range(HGRN_MM_LEVELS)]
        for lvl in range(HGRN_MM_LEVELS, HGRN_LEVELS):
            m = 1 << lvl
            parts = []
            for g in range(c // (2 * m)):
                ref = a2[2 * m * g + m - 1:2 * m * g + m, :]
                parts.append(-jnp.abs(a2[2 * m * g:2 * m * (g + 1), :] - ref))
            e_lvl.append(jnp.exp2(jnp.concatenate(parts, axis=0)))
        e_cum = jnp.exp2(a2)
        e_rest = jnp.exp2(a2[c - 1:c, :] - a2)
        e_end = e_cum[c - 1:c, :]
        qf_b, kf_b = qf.astype(BF16), kf.astype(BF16)
        e_b = [e.astype(BF16) for e in e_lvl]
        qb = [qf_b] + [qf_b * e for e in e_b]
        kb = [kf_b] + [kf_b * e for e in e_b]
        return qb, kb, qf_b * e_cum.astype(BF16), kf_b * e_rest.astype(BF16), vi, e_end

    def recur(base, ci, prepared):
        rows = pl.ds(base + ci * c, c)
        qb_all, kb_all, ql_all, kdec_all, vi, e_end = prepared
        zero = jnp.zeros((c, HGRN_D), BF16)
        zsq = jnp.zeros((HGRN_D, HGRN_D), BF16)
        for hp in range(HGRN_HEADS // 2):
            p_pair, ql_pair, v_pair, st_pair = [], [], [], []
            for h in (2 * hp, 2 * hp + 1):
                cols = slice(h * HGRN_D, (h + 1) * HGRN_D)
                qb = [q[:, cols] for q in qb_all]
                kb = [k[:, cols] for k in kb_all]
                s_parts = []
                for a in range(0, HGRN_LEVELS, 2):
                    q_cat = jnp.concatenate([qb[a], qb[a + 1]], axis=1)
                    k_blk = jnp.concatenate([jnp.concatenate([kb[a], zero], axis=1),
                                             jnp.concatenate([zero, kb[a + 1]], axis=1)], axis=0)
                    s_parts.append(_nt(q_cat, k_blk))
                s_parts.append(_nt(qb[HGRN_LEVELS], jnp.concatenate([kb[HGRN_LEVELS], zero], axis=0)))
                p = None
                for a, part in enumerate(s_parts):
                    part = part.astype(BF16) * mask_ref[:, 2 * a * c:(2 * a + 2) * c]
                    p = part if p is None else p + part
                p_pair.append(p)
                ql_pair.append(ql_all[:, cols])
                v_pair.append(vi[:, cols])
                st_pair.append(state_ref[:, cols])
            vv = [jnp.concatenate([v, v], axis=0) for v in v_pair]
            zv = jnp.zeros_like(vv[0])
            v_blk = jnp.concatenate([jnp.concatenate([vv[0], zv], axis=1),
                                     jnp.concatenate([zv, vv[1]], axis=1)], axis=0)
            o_pair = _nn(jnp.concatenate(p_pair, axis=1), v_blk)
            st_b = [st.astype(BF16) for st in st_pair]
            st_blk = jnp.concatenate([jnp.concatenate([st_b[0], zsq], axis=1),
                                      jnp.concatenate([zsq, st_b[1]], axis=1)], axis=0)
            o_pair = o_pair + _nt(jnp.concatenate(ql_pair, axis=1), st_blk)
            for i, h in enumerate((2 * hp, 2 * hp + 1)):
                cols = slice(h * HGRN_D, (h + 1) * HGRN_D)
                o = o_pair[:, i * HGRN_D:(i + 1) * HGRN_D]
                state_ref[:, cols] = st_pair[i] * e_end[:, cols] + _tn(v_pair[i], kdec_all[:, cols])
                ms = jnp.mean(o * o, axis=-1, keepdims=True)
                o_ref[rows, cols] = (o * lax.rsqrt(ms + NORM_EPS) * gain[:, cols]).astype(BF16)

    n_chunks = HGRN_TILE // c

    def group(g, carry):
        base = pl.multiple_of(g * HGRN_TILE, HGRN_TILE)
        prepared = prepare(base, 0)
        for ci in range(n_chunks):
            nxt = prepare(base, ci + 1) if ci + 1 < n_chunks else None
            recur(base, ci, prepared)
            prepared = nxt
        return carry

    lax.fori_loop(0, qh_ref.shape[0] // HGRN_TILE, group, 0)


def _hgrn(qh, fh, ih, logits, gain, layer, batch, seq):
    w_all, mask_all = _hgrn_consts()
    assert seq % HGRN_TILE == 0
    depth = logits.shape[0]
    const = lambda b: (0, 0)
    return pl.pallas_call(
        functools.partial(_hgrn_kernel, layer=layer),
        grid=(batch,),
        in_specs=[pl.BlockSpec((seq, GROUP), lambda b: (b, 0))] * 3 + [
                  pl.BlockSpec((depth, GROUP), const),
                  pl.BlockSpec((None, 1, GROUP), lambda b: (layer, 0, 0)),
                  pl.BlockSpec(w_all.shape, const),
                  pl.BlockSpec(mask_all.shape, const)],
        out_specs=pl.BlockSpec((seq, GROUP), lambda b: (b, 0)),
        out_shape=jax.ShapeDtypeStruct((fh.shape[0], GROUP), BF16),
        scratch_shapes=[pltpu.VMEM((HGRN_D, GROUP), F32)],
        compiler_params=pltpu.CompilerParams(dimension_semantics=("parallel",),
                                             vmem_limit_bytes=VMEM_LIMIT),
        name="hgrn",
    )(qh, fh, ih, logits, gain, w_all, mask_all)


def _mem_kv_kernel(m_ref, g_ref, w_ref, gk_ref, seg_ref, k_ref, v_ref):
    x = m_ref[...]
    ms = jnp.mean(x * x, axis=-1, keepdims=True)
    hb = (x * lax.rsqrt(ms + NORM_EPS) * g_ref[...]).astype(BF16)
    gk = gk_ref[...]
    seg = seg_ref[...]
    for c in range(GROUP // 256):
        kc = _nn(hb, w_ref[:, c * 256:(c + 1) * 256].astype(BF16))
        ss = _nn((kc * kc).astype(BF16), seg) * (1.0 / MEM_D)
        kn = kc * lax.rsqrt(ss + NORM_EPS) * gk[:, c * 256:(c + 1) * 256]
        k_ref[:, c * 256:(c + 1) * 256] = kn.astype(BF16)
    v_ref[...] = _nn(hb, w_ref[:, GROUP:2 * GROUP].astype(BF16)).astype(BF16)


def _mem_kv(memf, g, w, gk, mem_len, layer):
    n = memf.shape[0]
    seg = np.arange(256)[:, None] // MEM_D == np.arange(256)[None, :] // MEM_D
    seg = jnp.asarray(seg, BF16)
    const = lambda i: (0, 0)
    row = lambda i: (i, 0)
    of_layer = lambda i: (layer, 0, 0)
    return pl.pallas_call(
        _mem_kv_kernel,
        grid=(n // mem_len,),
        in_specs=[pl.BlockSpec((mem_len, D_MODEL), row),
                  pl.BlockSpec((None, 1, D_MODEL), of_layer),
                  pl.BlockSpec((None, D_MODEL, 2 * GROUP), of_layer),
                  pl.BlockSpec((None, 1, GROUP), of_layer),
                  pl.BlockSpec((256, 256), const)],
        out_specs=[pl.BlockSpec((mem_len, GROUP), row)] * 2,
        out_shape=[jax.ShapeDtypeStruct((n, GROUP), BF16)] * 2,
        compiler_params=pltpu.CompilerParams(dimension_semantics=("parallel",), vmem_limit_bytes=VMEM_LIMIT),
        name="mem_kv",
    )(memf, g, w, gk, seg)


def _out_proj_kernel(x_ref, oa_ref, oh_ref, qm_ref, z_ref, km_ref, vm_ref, gq_ref, w_ref, o_ref, y_ref):
    def gate(o, c0, c1):
        y_ref[:, c0:c1] = (o * _silu(z_ref[:, c0:c1].astype(F32))).astype(BF16)

    gate(oa_ref[...].astype(F32), 0, GROUP)
    gate(oh_ref[...].astype(F32), GROUP, 2 * GROUP)
    gq = gq_ref[...]
    for h in range(MEM_HEADS):
        cols = slice(h * MEM_D, (h + 1) * MEM_D)
        q = qm_ref[:, cols].astype(F32)
        ms = jnp.mean(q * q, axis=-1, keepdims=True)
        qn = q * lax.rsqrt(ms + NORM_EPS) * gq * (MEM_D ** -0.5)
        s = _nt(qn.astype(BF16), km_ref[:, cols])
        p = jnp.exp(s - jnp.max(s, axis=1, keepdims=True))
        v_aug = jnp.concatenate([vm_ref[:, cols], jnp.ones((vm_ref.shape[0], MEM_D), BF16)], axis=1)
        pv = _nn(p.astype(BF16), v_aug)
        om = pv[:, 0:MEM_D] / pv[:, MEM_D:2 * MEM_D]
        c0 = 2 * GROUP + h * MEM_D
        gate(om, c0, c0 + MEM_D)
    o_ref[...] = x_ref[...] + _nn(y_ref[...], w_ref[...].astype(BF16))


def _out_proj(xf, oa, oh, qm, z, km, vm, gq, w, seq, mem_len, layer):
    n = xf.shape[0]
    tm = OUT_TILE
    steps = seq // tm
    const = lambda i: (0, 0)
    row = lambda i: (i, 0)
    per_batch = lambda i: (i // steps, 0)
    of_layer = lambda i: (layer, 0, 0)
    return pl.pallas_call(
        _out_proj_kernel,
        grid=(n // tm,),
        in_specs=[pl.BlockSpec((tm, D_MODEL), row),
                  pl.BlockSpec((tm, GROUP), row),
                  pl.BlockSpec((tm, GROUP), row),
                  pl.BlockSpec((tm, GROUP), row),
                  pl.BlockSpec((tm, D_MIX), row),
                  pl.BlockSpec((mem_len, GROUP), per_batch),
                  pl.BlockSpec((mem_len, GROUP), per_batch),
                  pl.BlockSpec((None, 1, MEM_D), of_layer),
                  pl.BlockSpec((None, D_MIX, D_MODEL), of_layer, pipeline_mode=pl.Buffered(1))],
        out_specs=pl.BlockSpec((tm, D_MODEL), row),
        out_shape=jax.ShapeDtypeStruct((n, D_MODEL), F32),
        scratch_shapes=[pltpu.VMEM((tm, D_MIX), BF16)],
        compiler_params=pltpu.CompilerParams(dimension_semantics=("parallel",), vmem_limit_bytes=VMEM_LIMIT),
        name="out_proj",
    )(xf, oa, oh, qm, z, km, vm, gq, w)


def kernel(x, mem, positions, norm_g, w_in, w_out, moba_q_norm, moba_k_norm, hgrn_lb_logits, hgrn_o_norm,
           mem_norm_g, w_mem_kv, mem_q_norm, mem_k_norm):
    batch, seq, d_model = x.shape
    mem_len = mem.shape[1]
    depth = w_in.shape[0]
    assert d_model == D_MODEL and seq % ROW_TILE == 0 and seq % OUT_TILE == 0 and mem_len % 8 == 0
    n = batch * seq
    xf = x.reshape(n, d_model)
    memf = mem.reshape(batch * mem_len, d_model)
    cosf, sinf = _rope_tables(positions.reshape(1, n))
    per_head = lambda p, reps: jnp.tile(p, (1, reps))[:, None, :]
    norm_g, mem_norm_g, mem_q_norm = norm_g[:, None, :], mem_norm_g[:, None, :], mem_q_norm[:, None, :]
    gq, gk = per_head(moba_q_norm, GROUP // MOBA_D), per_head(moba_k_norm, GROUP // MOBA_D)
    g_hgrn, g_memk = per_head(hgrn_o_norm, HGRN_HEADS), per_head(mem_k_norm, MEM_HEADS)
    for l in range(depth):
        qa, ka, va, qh, fh, ih, qm, z = _in_proj(xf, norm_g, w_in, cosf, sinf, gq, gk, l)
        oa = _moba(qa, ka, va, batch, seq)
        oh = _hgrn(qh, fh, ih, hgrn_lb_logits, g_hgrn, l, batch, seq)
        km, vm = _mem_kv(memf, mem_norm_g, w_mem_kv, g_memk, mem_len, l)
        xf = _out_proj(xf, oa, oh, qm, z, km, vm, mem_q_norm, w_out, seq, mem_len, l)
    return xf.reshape(batch, seq, d_model)
```
